```python
import jax
import jax.numpy as jnp
from jax import lax
import numpy as np

D_MODEL = 1024
BATCH = 16
SEQ = 2048
DEPTH = 2

GRID_W = 64
CTX_LEN = 256
HEAD_DIM = 64
D_MIX = 1024
CONV_CH = 256
CONV_WIDTH = 31
NA_HEADS = 6
NA_WIN_H = 8
NA_WIN_W = 16
GQA_HEADS = 6
GQA_KV_HEADS = 2
ROPE_THETA = 10000.0
Q_BLOCK = 128
PEER_HEADS = 8
PEER_N_KEYS = 128
PEER_N_EXPERTS = PEER_N_KEYS * PEER_N_KEYS
PEER_D_KEY = 128
PEER_TOPK = 16
PEER_CHUNK = 128
EPS = 1e-6

NA_W = NA_HEADS * HEAD_DIM
GQA_W = GQA_HEADS * HEAD_DIM
GQA_KV_W = GQA_KV_HEADS * HEAD_DIM
OFF_A_VAL = 0
OFF_A_GATE = OFF_A_VAL + CONV_CH
OFF_NA_Q = OFF_A_GATE + CONV_CH
OFF_G_Q = OFF_NA_Q + NA_W
KV_START = OFF_G_Q + GQA_W
OFF_NA_K = KV_START
OFF_NA_V = OFF_NA_K + NA_W
OFF_G_K = OFF_NA_V + NA_W
OFF_G_V = OFF_G_K + GQA_KV_W
IN_COLS = OFF_G_V + GQA_KV_W

kernel_name = 'hymba_conformer_natten_gqa_peer_dit'


def _cols(z, off, width, base=0):
    return z[..., off - base: off - base + width]


def rms_norm(x, g):
    xf = x.astype(jnp.float32)
    y = xf * lax.rsqrt(jnp.mean(xf * xf, axis=-1, keepdims=True) + EPS)
    return (y * g.astype(jnp.float32)).astype(x.dtype)


def layer_norm(x, g, b):
    xf = x.astype(jnp.float32)
    mu = jnp.mean(xf, axis=-1, keepdims=True)
    var = jnp.mean(jnp.square(xf - mu), axis=-1, keepdims=True)
    y = (xf - mu) * lax.rsqrt(var + EPS)
    return (y * g.astype(jnp.float32) + b.astype(jnp.float32)).astype(x.dtype)


def modulate(h, shift, scale):
    return h * (1 + scale) + shift


def split_heads(z, n):
    B, L, _ = z.shape
    return z.reshape(B, L, n, HEAD_DIM).transpose(0, 2, 1, 3)


def merge_heads(o):
    B, H, L, d = o.shape
    return o.transpose(0, 2, 1, 3).reshape(B, L, H * d)


def axial_rope_tables(T, dtype):
    t = jnp.arange(T, dtype=jnp.int32)
    row = (t // GRID_W).astype(jnp.float32)
    col = (t % GRID_W).astype(jnp.float32)
    n_freq = HEAD_DIM // 4
    inv_freq = ROPE_THETA ** (-jnp.arange(n_freq, dtype=jnp.float32) / n_freq)
    ang = jnp.concatenate([row[:, None] * inv_freq, col[:, None] * inv_freq], axis=-1)
    return jnp.cos(ang).astype(dtype), jnp.sin(ang).astype(dtype)


def apply_rope(x, cos, sin):
    half = HEAD_DIM // 2
    x1, x2 = x[..., :half], x[..., half:]
    return jnp.concatenate([x1 * cos - x2 * sin, x1 * sin + x2 * cos], axis=-1)


def dense_attention(qh, kh, vh):
    B, Hq, L, d = qh.shape
    Hk = kh.shape[1]
    qg = qh.reshape(B, Hk, Hq // Hk, L, d) * (d ** -0.5)
    s = jnp.einsum('bkgqd,bksd->bkgqs', qg, kh).astype(jnp.float32)
    p = jax.nn.softmax(s, axis=-1).astype(vh.dtype)
    o = jnp.einsum('bkgqs,bksd->bkgqd', p, vh)
    return merge_heads(o.reshape(B, Hq, L, d))


def conformer_conv(val, gate, w_dw, b_dw, ln_g, ln_b):
    u = val * jax.nn.sigmoid(gate)
    pad = CONV_WIDTH // 2
    y = lax.conv_general_dilated(u, w_dw[:, None, :], window_strides=(1,), padding=[(pad, pad)],
                                 dimension_numbers=('NWC', 'WIO', 'NWC'),
                                 feature_group_count=CONV_CH) + b_dw
    return jax.nn.silu(layer_norm(y, ln_g, ln_b))


def neighbourhood_attention(q, k, v, kc, vc, rel_bias):
    B, T, _ = q.shape
    rows = T // GRID_W
    kh = min(NA_WIN_H, rows)
    n_loc = kh * NA_WIN_W

    def grid(z):
        return z.reshape(B, rows, GRID_W, NA_HEADS, HEAD_DIM).transpose(0, 3, 1, 2, 4)

    kg, vg = grid(k), grid(v)
    q_rows = grid(q * (HEAD_DIM ** -0.5)).transpose(2, 0, 1, 3, 4)
    kc_h, vc_h = split_heads(kc, NA_HEADS), split_heads(vc, NA_HEADS)
    cols = jnp.arange(GRID_W, dtype=jnp.int32)
    c0 = jnp.clip(cols - NA_WIN_W // 2, 0, GRID_W - NA_WIN_W)
    col_idx = c0[:, None] + jnp.arange(NA_WIN_W, dtype=jnp.int32)[None, :]
    col_bias_idx = col_idx - cols[:, None] + (NA_WIN_W - 1)

    def row_step(args):
        r, qr = args
        r0 = jnp.clip(r - kh // 2, 0, rows - kh)
        band_k = lax.dynamic_slice_in_dim(kg, r0, kh, axis=2)
        band_v = lax.dynamic_slice_in_dim(vg, r0, kh, axis=2)
        k_win = jnp.take(band_k, col_idx, axis=3)
        v_win = jnp.take(band_v, col_idx, axis=3)
        row_bias_idx = r0 + jnp.arange(kh, dtype=jnp.int32) - r + (NA_WIN_H - 1)
        bias = jnp.take(jnp.take(rel_bias, row_bias_idx, axis=1), col_bias_idx, axis=2)
        s_loc = jnp.einsum('bhqd,bhiqjd->bhqij', qr, k_win).astype(jnp.float32)
        s_loc = s_loc + bias.transpose(0, 2, 1, 3)[None].astype(jnp.float32)
        s_loc = s_loc.reshape(B, NA_HEADS, GRID_W, n_loc)
        s_ctx = jnp.einsum('bhqd,bhcd->bhqc', qr, kc_h).astype(jnp.float32)
        p = jax.nn.softmax(jnp.concatenate([s_loc, s_ctx], axis=-1), axis=-1).astype(qr.dtype)
        p_loc = p[..., :n_loc].reshape(B, NA_HEADS, GRID_W, kh, NA_WIN_W)
        p_ctx = p[..., n_loc:]
        return (jnp.einsum('bhqij,bhiqjd->bhqd', p_loc, v_win)
                + jnp.einsum('bhqc,bhcd->bhqd', p_ctx, vc_h))

    out = lax.map(row_step, (jnp.arange(rows, dtype=jnp.int32), q_rows))
    return out.transpose(1, 0, 3, 2, 4).reshape(B, T, NA_W)


def gqa_latent(q, k, v, kc_h, vc_h, q_g, k_g, cos, sin):
    B, T, _ = q.shape
    G = GQA_HEADS // GQA_KV_HEADS
    n_blk = T // Q_BLOCK
    qh = apply_rope(rms_norm(split_heads(q, GQA_HEADS), q_g), cos, sin) * (HEAD_DIM ** -0.5)
    kh = apply_rope(rms_norm(split_heads(k, GQA_KV_HEADS), k_g), cos, sin)
    k_all = jnp.concatenate([kh, kc_h], axis=2)
    v_all = jnp.concatenate([split_heads(v, GQA_KV_HEADS), vc_h], axis=2)
    qb = qh.reshape(B, GQA_KV_HEADS, G, n_blk, Q_BLOCK, HEAD_DIM).transpose(3, 0, 1, 2, 4, 5)

    def block_step(qq):
        s = jnp.einsum('bkgqd,bksd->bkgqs', qq, k_all).astype(jnp.float32)
        p = jax.nn.softmax(s, axis=-1).astype(v_all.dtype)
        return jnp.einsum('bkgqs,bksd->bkgqd', p, v_all)

    out = lax.map(block_step, qb)
    return out.transpose(1, 0, 4, 2, 3, 5).reshape(B, T, GQA_W)


def peer_ffn(h, w_q, sub_keys, u, v):
    B, L, D = h.shape
    chunks = h.reshape(-1, PEER_CHUNK, D)
    n_cand = PEER_TOPK * PEER_TOPK

    def chunk_step(xc):
        q = (xc @ w_q).reshape(PEER_CHUNK, PEER_HEADS, 2, PEER_D_KEY)
        s = jnp.einsum('chpk,hpnk->chpn', q, sub_keys).astype(jnp.float32)
        top_s, top_i = lax.top_k(s, PEER_TOPK)
        cand_s = top_s[:, :, 0, :, None] + top_s[:, :, 1, None, :]
        cand_i = top_i[:, :, 0, :, None] * PEER_N_KEYS + top_i[:, :, 1, None, :]
        best_s, best_pos = lax.top_k(cand_s.reshape(PEER_CHUNK, PEER_HEADS, n_cand), PEER_TOPK)
        idx = jnp.take_along_axis(cand_i.reshape(PEER_CHUNK, PEER_HEADS, n_cand), best_pos, axis=-1)
        g = jax.nn.softmax(best_s, axis=-1).astype(xc.dtype)
        act = jax.nn.gelu(jnp.einsum('cd,chkd->chk', xc, jnp.take(u, idx, axis=0)))
        return jnp.einsum('chk,chkd->cd', g * act, jnp.take(v, idx, axis=0))

    return lax.map(chunk_step, chunks).reshape(B, L, D)


def setup_inputs(seed: int = 0) -> dict:
    key = jax.random.key(seed)
    ks = jax.random.split(key, 24)
    f32 = jnp.float32

    def nrm(k, shape, s):
        return jax.random.normal(k, shape, f32) * s

    L, D = DEPTH, D_MODEL
    return {
        'x': nrm(ks[0], (BATCH, SEQ, D), 1.0),
        'c': nrm(ks[1], (BATCH, D), 1.0),
        'ctx': nrm(ks[2], (BATCH, CTX_LEN, D), 1.0),
        'c_ctx': nrm(ks[3], (D,), 1.0),
        'norm1_g': 1.0 + nrm(ks[4], (L, D), 0.05),
        'norm2_g': 1.0 + nrm(ks[5], (L, D), 0.05),
        'w_ada': nrm(ks[6], (L, D, 6 * D), 0.5 * D ** -0.5),
        'b_ada': nrm(ks[7], (L, 6 * D), 0.02),
        'w_in': nrm(ks[8], (L, D, IN_COLS), D ** -0.5),
        'conv_w': nrm(ks[9], (L, CONV_WIDTH, CONV_CH), CONV_WIDTH ** -0.5),
        'conv_b': nrm(ks[10], (L, CONV_CH), 0.02),
        'conv_ln_g': 1.0 + nrm(ks[11], (L, CONV_CH), 0.05),
        'conv_ln_b': nrm(ks[12], (L, CONV_CH), 0.02),
        'na_rel_bias': nrm(ks[13], (L, NA_HEADS, 2 * NA_WIN_H - 1, 2 * NA_WIN_W - 1), 0.1),
        'gqa_q_norm': 1.0 + nrm(ks[14], (L, HEAD_DIM), 0.05),
        'gqa_k_norm': 1.0 + nrm(ks[15], (L, HEAD_DIM), 0.05),
        'w_out': nrm(ks[16], (L, D_MIX, D), D_MIX ** -0.5),
        'peer_w_q': nrm(ks[17], (L, D, PEER_HEADS * 2 * PEER_D_KEY), D ** -0.5),
        'peer_keys': nrm(ks[18], (L, PEER_HEADS, 2, PEER_N_KEYS, PEER_D_KEY), PEER_D_KEY ** -0.5),
        'peer_u': nrm(ks[19], (L, PEER_N_EXPERTS, D), D ** -0.5),
        'peer_v': nrm(ks[20], (L, PEER_N_EXPERTS, D), 0.5),
        'final_norm_g': 1.0 + nrm(ks[21], (D,), 0.05),
    }


def reference(x, c, ctx, c_ctx, norm1_g, norm2_g, w_ada, b_ada, w_in, conv_w, conv_b, conv_ln_g,
              conv_ln_b, na_rel_bias, gqa_q_norm, gqa_k_norm, w_out, peer_w_q, peer_keys, peer_u,
              peer_v, final_norm_g):
    T = x.shape[1]
    cos, sin = axial_rope_tables(T, x.dtype)
    silu_c = jax.nn.silu(c)
    silu_cc = jax.nn.silu(c_ctx)
    for l in range(DEPTH):
        last = l == DEPTH - 1
        mod = (silu_c @ w_ada[l] + b_ada[l])[:, None, :]
        sh1, sc1, g1, sh2, sc2, g2 = jnp.split(mod, 6, axis=-1)
        mod_c = silu_cc @ w_ada[l] + b_ada[l]
        csh1, csc1, cg1, csh2, csc2, cg2 = jnp.split(mod_c, 6, axis=-1)

        h = modulate(rms_norm(x, norm1_g[l]), sh1, sc1)
        hc = modulate(rms_norm(ctx, norm1_g[l]), csh1, csc1)
        z = h @ w_in[l]
        zc_kv = hc @ w_in[l][:, KV_START:]
        kc_na = _cols(zc_kv, OFF_NA_K, NA_W, KV_START)
        vc_na = _cols(zc_kv, OFF_NA_V, NA_W, KV_START)
        kc_g = rms_norm(split_heads(_cols(zc_kv, OFF_G_K, GQA_KV_W, KV_START), GQA_KV_HEADS), gqa_k_norm[l])
        vc_g = split_heads(_cols(zc_kv, OFF_G_V, GQA_KV_W, KV_START), GQA_KV_HEADS)

        a = conformer_conv(_cols(z, OFF_A_VAL, CONV_CH), _cols(z, OFF_A_GATE, CONV_CH),
                           conv_w[l], conv_b[l], conv_ln_g[l], conv_ln_b[l])
        b = neighbourhood_attention(_cols(z, OFF_NA_Q, NA_W), _cols(z, OFF_NA_K, NA_W),
                                    _cols(z, OFF_NA_V, NA_W), kc_na, vc_na, na_rel_bias[l])
        g = gqa_latent(_cols(z, OFF_G_Q, GQA_W), _cols(z, OFF_G_K, GQA_KV_W), _cols(z, OFF_G_V, GQA_KV_W),
                       kc_g, vc_g, gqa_q_norm[l], gqa_k_norm[l], cos, sin)
        x = x + g1 * (jnp.concatenate([a, b, g], axis=-1) @ w_out[l])

        x = x + g2 * peer_ffn(modulate(rms_norm(x, norm2_g[l]), sh2, sc2),
                              peer_w_q[l], peer_keys[l], peer_u[l], peer_v[l])

        if not last:
            zc_q = hc @ w_in[l][:, :KV_START]
            ac = conformer_conv(_cols(zc_q, OFF_A_VAL, CONV_CH), _cols(zc_q, OFF_A_GATE, CONV_CH),
                                conv_w[l], conv_b[l], conv_ln_g[l], conv_ln_b[l])
            bc = dense_attention(split_heads(_cols(zc_q, OFF_NA_Q, NA_W), NA_HEADS),
                                 split_heads(kc_na, NA_HEADS), split_heads(vc_na, NA_HEADS))
            qc_g = rms_norm(split_heads(_cols(zc_q, OFF_G_Q, GQA_W), GQA_HEADS), gqa_q_norm[l])
            gc = dense_attention(qc_g, kc_g, vc_g)
            ctx = ctx + cg1 * (jnp.concatenate([ac, bc, gc], axis=-1) @ w_out[l])
            ctx = ctx + cg2 * peer_ffn(modulate(rms_norm(ctx, norm2_g[l]), csh2, csc2),
                                       peer_w_q[l], peer_keys[l], peer_u[l], peer_v[l])
    return rms_norm(x, final_norm_g)
```

```python
import functools

import jax
import jax.numpy as jnp
from jax import lax
from jax.experimental import pallas as pl
from jax.experimental.pallas import tpu as pltpu

F32 = jnp.float32
BF16 = jnp.bfloat16

GRID_W = 64
HEAD_DIM = 64
CONV_CH = 256
CONV_WIDTH = 31
NA_HEADS = 6
NA_WIN_H = 8
NA_WIN_W = 16
GQA_HEADS = 6
GQA_KV_HEADS = 2
ROPE_THETA = 10000.0
PEER_HEADS = 8
PEER_N_KEYS = 128
PEER_D_KEY = 128
PEER_TOPK = 16
EPS = 1e-6

NA_W = NA_HEADS * HEAD_DIM
GQA_W = GQA_HEADS * HEAD_DIM
GQA_KV_W = GQA_KV_HEADS * HEAD_DIM
IN_SPLITS = (2 * CONV_CH, NA_W, GQA_W, NA_W, NA_W, GQA_KV_W, GQA_KV_W)
PEER_PAIRS = PEER_HEADS * PEER_TOPK

LANES = 128
SUBLANES = 8
VMEM_LIMIT = 48 * 1024 * 1024
MASK_VALUE = -1e30


def _params(*sem):
    return pltpu.CompilerParams(dimension_semantics=sem, vmem_limit_bytes=VMEM_LIMIT)


def _ada_kernel(a_ref, w_ref, b_ref, o_ref):
    a = a_ref[...]
    s = a * jax.nn.sigmoid(a)
    o_ref[...] = jnp.dot(s.astype(BF16), w_ref[...].astype(BF16), preferred_element_type=F32) + b_ref[...]


def ada_mod(a, w, b):
    R, D = a.shape
    N = w.shape[1]
    tn = 1024
    return pl.pallas_call(
        _ada_kernel,
        grid=(N // tn,),
        in_specs=[pl.BlockSpec((R, D), lambda j: (0, 0)),
                  pl.BlockSpec((D, tn), lambda j: (0, j)),
                  pl.BlockSpec((1, tn), lambda j: (0, j))],
        out_specs=pl.BlockSpec((R, tn), lambda j: (0, j)),
        out_shape=jax.ShapeDtypeStruct((R, N), F32),
        compiler_params=_params("parallel"),
        name="ada_mod",
    )(a, w, b.reshape(1, N))


def _norm_mod_proj_kernel(x_ref, g_ref, sc_ref, sh_ref, w_ref, *out_refs, splits, emit_h):
    x = x_ref[0]
    y = x * lax.rsqrt(jnp.mean(x * x, axis=-1, keepdims=True) + EPS) * g_ref[...]
    h = y * (1.0 + sc_ref[0]) + sh_ref[0]
    z = jnp.dot(h.astype(BF16), w_ref[...], preferred_element_type=F32)
    off = 0
    for o_ref, width in zip(out_refs, splits):
        o_ref[0] = z[:, off:off + width]
        off += width
    if emit_h:
        out_refs[len(splits)][0] = h


def norm_mod_proj(x, g, sc, sh, w, splits, emit_h=False):
    B, L, D = x.shape
    N = w.shape[1]
    assert sum(splits) == N
    tm = min(256, L)
    per_batch = sc.shape[0] == B
    mod_map = (lambda b, i: (b, 0, 0)) if per_batch else (lambda b, i: (0, 0, 0))
    out_shape = [jax.ShapeDtypeStruct((B, L, s), F32) for s in splits]
    out_specs = [pl.BlockSpec((1, tm, s), lambda b, i: (b, i, 0)) for s in splits]
    if emit_h:
        out_shape.append(jax.ShapeDtypeStruct((B, L, D), F32))
        out_specs.append(pl.BlockSpec((1, tm, D), lambda b, i: (b, i, 0)))
    return pl.pallas_call(
        functools.partial(_norm_mod_proj_kernel, splits=splits, emit_h=emit_h),
        grid=(B, L // tm),
        in_specs=[pl.BlockSpec((1, tm, D), lambda b, i: (b, i, 0)),
                  pl.BlockSpec((1, D), lambda b, i: (0, 0)),
                  pl.BlockSpec((1, 1, D), mod_map),
                  pl.BlockSpec((1, 1, D), mod_map),
                  pl.BlockSpec((D, N), lambda b, i: (0, 0))],
        out_specs=out_specs,
        out_shape=out_shape,
        compiler_params=_params("parallel", "parallel"),
        name="norm_mod_proj",
    )(x, g.reshape(1, D), sc, sh, w)


_CONV_PAD = 16


def _conv_kernel(za_ref, w_ref, b_ref, lg_ref, lb_ref, o_ref, upad_ref, *, L, tc):
    c = pl.program_id(1)

    @pl.when(c == 0)
    def _():
        val = za_ref[0, :, :CONV_CH]
        gate = za_ref[0, :, CONV_CH:]
        zeros = jnp.zeros((_CONV_PAD, CONV_CH), F32)
        upad_ref[pl.ds(0, _CONV_PAD), :] = zeros
        upad_ref[pl.ds(_CONV_PAD + L, _CONV_PAD), :] = zeros
        upad_ref[pl.ds(_CONV_PAD, L), :] = val * jax.nn.sigmoid(gate)

    start = pl.multiple_of(c * tc, SUBLANES)
    win = upad_ref[pl.ds(start, tc + 2 * _CONV_PAD), :]
    acc = jnp.zeros((tc, CONV_CH), F32)
    first = _CONV_PAD - CONV_WIDTH // 2
    for j in range(CONV_WIDTH):
        acc = acc + win[first + j:first + j + tc, :] * w_ref[j:j + 1, :]
    y = acc + b_ref[...]
    mu = jnp.mean(y, axis=-1, keepdims=True)
    d = y - mu
    var = jnp.mean(d * d, axis=-1, keepdims=True)
    yn = d * lax.rsqrt(var + EPS) * lg_ref[...] + lb_ref[...]
    o_ref[0] = yn * jax.nn.sigmoid(yn)


def conformer_conv(za, w_dw, b_dw, ln_g, ln_b):
    B, L, _ = za.shape
    tc = min(256, L)
    vec = lambda v: v.reshape(1, CONV_CH)
    return pl.pallas_call(
        functools.partial(_conv_kernel, L=L, tc=tc),
        grid=(B, L // tc),
        in_specs=[pl.BlockSpec((1, L, 2 * CONV_CH), lambda b, c: (b, 0, 0)),
                  pl.BlockSpec((CONV_WIDTH, CONV_CH), lambda b, c: (0, 0)),
                  pl.BlockSpec((1, CONV_CH), lambda b, c: (0, 0)),
                  pl.BlockSpec((1, CONV_CH), lambda b, c: (0, 0)),
                  pl.BlockSpec((1, CONV_CH), lambda b, c: (0, 0))],
        out_specs=pl.BlockSpec((1, tc, CONV_CH), lambda b, c: (b, c, 0)),
        out_shape=jax.ShapeDtypeStruct((B, L, CONV_CH), F32),
        scratch_shapes=[pltpu.VMEM((L + 2 * _CONV_PAD, CONV_CH), F32)],
        compiler_params=_params("parallel", "arbitrary"),
        name="conformer_conv",
    )(za, w_dw, vec(b_dw), vec(ln_g), vec(ln_b))


def _na_row_offset(r, rows):
    return r - jnp.clip(r - NA_WIN_H // 2, 0, rows - NA_WIN_H)


def _na_kernel(q_ref, k_ref, v_ref, kc_ref, vc_ref, bias_ref, o_ref, *, rows):
    r = pl.program_id(1)
    r0 = r - _na_row_offset(r, rows)
    band = pl.ds(pl.multiple_of(r0 * GRID_W, GRID_W), NA_WIN_H * GRID_W)
    nt = (((1,), (1,)), ((), ()))
    for h in range(NA_HEADS):
        q = (q_ref[0, h] * (HEAD_DIM ** -0.5)).astype(BF16)
        kb = k_ref[0, h, band, :]
        vb = v_ref[0, h, band, :]
        s_loc = lax.dot_general(q, kb, nt, preferred_element_type=F32) + bias_ref[0, h]
        s_ctx = lax.dot_general(q, kc_ref[0, h], nt, preferred_element_type=F32)
        m = jnp.maximum(jnp.max(s_loc, axis=-1, keepdims=True), jnp.max(s_ctx, axis=-1, keepdims=True))
        p_loc = jnp.exp(s_loc - m)
        p_ctx = jnp.exp(s_ctx - m)
        den = jnp.sum(p_loc, axis=-1, keepdims=True) + jnp.sum(p_ctx, axis=-1, keepdims=True)
        o = (jnp.dot(p_loc.astype(BF16), vb, preferred_element_type=F32)
             + jnp.dot(p_ctx.astype(BF16), vc_ref[0, h], preferred_element_type=F32))
        o_ref[0, h] = o / den


def na_bias_table(rel_bias):
    cols = jnp.arange(GRID_W, dtype=jnp.int32)
    c0 = jnp.clip(cols - NA_WIN_W // 2, 0, GRID_W - NA_WIN_W)
    in_win = (cols[None, :] >= c0[:, None]) & (cols[None, :] < c0[:, None] + NA_WIN_W)
    col_idx = jnp.clip(cols[None, :] - cols[:, None] + (NA_WIN_W - 1), 0, 2 * NA_WIN_W - 2)
    off = jnp.arange(NA_WIN_H, dtype=jnp.int32)
    row_idx = off[None, :] - off[:, None] + (NA_WIN_H - 1)
    t = rel_bias[:, row_idx]
    t = t[..., col_idx]
    t = jnp.where(in_win[None, None, None], t, MASK_VALUE)
    return t.transpose(1, 0, 3, 2, 4).reshape(NA_WIN_H, NA_HEADS, GRID_W, NA_WIN_H * GRID_W)


def neighbourhood_attention(q, k, v, kc, vc, bias):
    B, H, T, d = q.shape
    C = kc.shape[2]
    rows = T // GRID_W
    assert rows >= NA_WIN_H
    return pl.pallas_call(
        functools.partial(_na_kernel, rows=rows),
        grid=(B, rows),
        in_specs=[pl.BlockSpec((1, H, GRID_W, d), lambda b, r: (b, 0, r, 0)),
                  pl.BlockSpec((1, H, T, d), lambda b, r: (b, 0, 0, 0)),
                  pl.BlockSpec((1, H, T, d), lambda b, r: (b, 0, 0, 0)),
                  pl.BlockSpec((1, H, C, d), lambda b, r: (b, 0, 0, 0)),
                  pl.BlockSpec((1, H, C, d), lambda b, r: (b, 0, 0, 0)),
                  pl.BlockSpec((1, H, GRID_W, NA_WIN_H * GRID_W),
                               lambda b, r: (_na_row_offset(r, rows), 0, 0, 0))],
        out_specs=pl.BlockSpec((1, H, GRID_W, d), lambda b, r: (b, 0, r, 0)),
        out_shape=jax.ShapeDtypeStruct((B, H, T, d), F32),
        compiler_params=_params("parallel", "arbitrary"),
        name="neighbourhood_attention",
    )(q, k, v, kc, vc, bias)


def _qk_prep_kernel(x_ref, g_ref, cos_ref, sin_ref, o_ref, *, rope):
    x = x_ref[0, 0]
    y = x * lax.rsqrt(jnp.mean(x * x, axis=-1, keepdims=True) + EPS) * g_ref[...]
    if rope:
        half = HEAD_DIM // 2
        x1, x2 = y[:, :half], y[:, half:]
        cos, sin = cos_ref[...], sin_ref[...]
        y = jnp.concatenate([x1 * cos - x2 * sin, x1 * sin + x2 * cos], axis=-1)
    o_ref[0, 0] = y.astype(o_ref.dtype)


def qk_prep(x, g, cos, sin, rope, out_dtype):
    B, H, L, d = x.shape
    tl = min(512, L)
    return pl.pallas_call(
        functools.partial(_qk_prep_kernel, rope=rope),
        grid=(B, H, L // tl),
        in_specs=[pl.BlockSpec((1, 1, tl, d), lambda b, h, i: (b, h, i, 0)),
                  pl.BlockSpec((1, d), lambda b, h, i: (0, 0)),
                  pl.BlockSpec((tl, d // 2), lambda b, h, i: (i, 0)),
                  pl.BlockSpec((tl, d // 2), lambda b, h, i: (i, 0))],
        out_specs=pl.BlockSpec((1, 1, tl, d), lambda b, h, i: (b, h, i, 0)),
        out_shape=jax.ShapeDtypeStruct((B, H, L, d), out_dtype),
        compiler_params=_params("parallel", "parallel", "parallel"),
        name="qk_prep",
    )(x, g.reshape(1, d), cos, sin)


def _attn_kernel(q_ref, k_ref, v_ref, o_ref, *, G, bq):
    q = (q_ref[0, 0].reshape(G * bq, HEAD_DIM) * (HEAD_DIM ** -0.5)).astype(BF16)
    s = lax.dot_general(q, k_ref[0, 0], (((1,), (1,)), ((), ())), preferred_element_type=F32)
    m = jnp.max(s, axis=-1, keepdims=True)
    p = jnp.exp(s - m)
    den = jnp.sum(p, axis=-1, keepdims=True)
    o = jnp.dot(p.astype(BF16), v_ref[0, 0], preferred_element_type=F32) / den
    o_ref[0, 0] = o.reshape(G, bq, HEAD_DIM)


def grouped_attention(q, k, v):
    B, Hk, G, L, d = q.shape
    S = k.shape[2]
    bq = min(128, L)
    return pl.pallas_call(
        functools.partial(_attn_kernel, G=G, bq=bq),
        grid=(B, Hk, L // bq),
        in_specs=[pl.BlockSpec((1, 1, G, bq, d), lambda b, h, i: (b, h, 0, i, 0)),
                  pl.BlockSpec((1, 1, S, d), lambda b, h, i: (b, h, 0, 0)),
                  pl.BlockSpec((1, 1, S, d), lambda b, h, i: (b, h, 0, 0))],
        out_specs=pl.BlockSpec((1, 1, G, bq, d), lambda b, h, i: (b, h, 0, i, 0)),
        out_shape=jax.ShapeDtypeStruct((B, Hk, G, L, d), F32),
        compiler_params=_params("parallel", "parallel", "arbitrary"),
        name="grouped_attention",
    )(q, k, v)


def _proj_residual_kernel(m_ref, w_ref, x_ref, gate_ref, o_ref):
    y = jnp.dot(m_ref[0].astype(BF16), w_ref[...], preferred_element_type=F32)
    o_ref[0] = x_ref[0] + gate_ref[0] * y


def proj_residual(mix, w, x, gate):
    B, L, K = mix.shape
    D = w.shape[1]
    tm = min(512, L)
    gate_map = (lambda b, i: (b, 0, 0)) if gate.shape[0] == B else (lambda b, i: (0, 0, 0))
    return pl.pallas_call(
        _proj_residual_kernel,
        grid=(B, L // tm),
        in_specs=[pl.BlockSpec((1, tm, K), lambda b, i: (b, i, 0)),
                  pl.BlockSpec((K, D), lambda b, i: (0, 0)),
                  pl.BlockSpec((1, tm, D), lambda b, i: (b, i, 0)),
                  pl.BlockSpec((1, 1, D), gate_map)],
        out_specs=pl.BlockSpec((1, tm, D), lambda b, i: (b, i, 0)),
        out_shape=jax.ShapeDtypeStruct((B, L, D), F32),
        compiler_params=_params("parallel", "parallel"),
        name="proj_residual",
    )(mix, w, x, gate)


def _top_rows(s, k, payload=None):
    n = s.shape[0]
    rows = lax.broadcasted_iota(jnp.int32, s.shape, 0)
    vals, picks = [], []
    for _ in range(k):
        m = jnp.max(s, axis=0, keepdims=True)
        first = jnp.min(jnp.where(s == m, rows, n), axis=0, keepdims=True)
        hit = rows == first
        vals.append(m)
        if payload is None:
            picks.append(first)
        else:
            picks.append(jnp.max(jnp.where(hit, payload, -1), axis=0, keepdims=True))
        s = jnp.where(hit, -jnp.inf, s)
    return jnp.concatenate(vals, axis=0), jnp.concatenate(picks, axis=0)


def _peer_topk_kernel(q_ref, keys_ref, idx_ref, g_ref):
    K = PEER_TOPK
    nt = (((1,), (1,)), ((), ()))
    for h in range(PEER_HEADS):
        tops = []
        for p in range(2):
            col = (2 * h + p) * PEER_D_KEY
            qhp = q_ref[:, col:col + PEER_D_KEY].astype(BF16)
            s = lax.dot_general(keys_ref[h, p], qhp, nt, preferred_element_type=F32)
            tops.append(_top_rows(s, K))
        (s0, i0), (s1, i1) = tops
        t = s0.shape[1]
        cand_s = (s0[:, None, :] + s1[None, :, :]).reshape(K * K, t)
        cand_i = (i0[:, None, :] * PEER_N_KEYS + i1[None, :, :]).reshape(K * K, t)
        best_s, best_i = _top_rows(cand_s, K, payload=cand_i)
        e = jnp.exp(best_s - jnp.max(best_s, axis=0, keepdims=True))
        g_ref[h] = e / jnp.sum(e, axis=0, keepdims=True)
        idx_ref[h] = best_i


def peer_topk(q, keys):
    N = q.shape[0]
    tm = LANES
    return pl.pallas_call(
        _peer_topk_kernel,
        grid=(N // tm,),
        in_specs=[pl.BlockSpec((tm, q.shape[1]), lambda i: (i, 0)),
                  pl.BlockSpec(keys.shape, lambda i: (0, 0, 0, 0))],
        out_specs=[pl.BlockSpec((PEER_HEADS, PEER_TOPK, tm), lambda i: (0, 0, i)),
                   pl.BlockSpec((PEER_HEADS, PEER_TOPK, tm), lambda i: (0, 0, i))],
        out_shape=[jax.ShapeDtypeStruct((PEER_HEADS, PEER_TOPK, N), jnp.int32),
                   jax.ShapeDtypeStruct((PEER_HEADS, PEER_TOPK, N), F32)],
        compiler_params=_params("parallel"),
        name="peer_topk",
    )(q, keys)


_HALF = SUBLANES // 2
_PEER_TOKENS = 64
_MERGE_ORDER = (0, 4, 2, 6, 1, 5, 3, 7)


def pack_expert_table(w):
    E, D = w.shape
    assert D == 2 * _HALF * LANES
    bits = lax.bitcast_convert_type(w.astype(BF16), jnp.uint16).astype(jnp.uint32).reshape(E, 2 * _HALF, LANES)
    return bits[:, :_HALF] | (bits[:, _HALF:] << 16)


def _unpack_row(w):
    lo = pltpu.bitcast(w << 16, F32)
    hi = pltpu.bitcast(w & jnp.uint32(0xFFFF0000), F32)
    return lo, hi


def _merge8(ps):
    sub = lax.broadcasted_iota(jnp.int32, (SUBLANES, LANES), 0)
    xs = [jnp.concatenate([ps[2 * i], ps[2 * i + 1]], axis=0) for i in range(4)]
    xs = [x + pltpu.roll(x, SUBLANES - 2, axis=0) for x in xs]
    keep2 = (sub % 4) < 2
    zs = [jnp.where(keep2, xs[2 * i], pltpu.roll(xs[2 * i + 1], 2, axis=0)) for i in range(2)]
    zs = [z + pltpu.roll(z, SUBLANES - 1, axis=0) for z in zs]
    return jnp.where((sub % 2) == 0, zs[0], pltpu.roll(zs[1], 1, axis=0))


def _peer_act_kernel(idx_ref, h_ref, g_ref, tab_ref, o_ref, part_ref):
    tn = h_ref.shape[0]

    def token(t, carry):
        xlo = h_ref[t, 0:_HALF, :]
        xhi = h_ref[t, _HALF:SUBLANES, :]

        def group(gi, c):
            base = pl.multiple_of(gi * SUBLANES, SUBLANES)
            ps = []
            for j in _MERGE_ORDER:
                lo, hi = _unpack_row(tab_ref[idx_ref[t, base + j]])
                ps.append(lo * xlo + hi * xhi)
            part_ref[pl.ds(base, SUBLANES), :] = _merge8(ps)
            return c

        lax.fori_loop(0, PEER_PAIRS // SUBLANES, group, 0)
        o_ref[pl.ds(t, 1), :] = jnp.sum(part_ref[...].T, axis=0, keepdims=True)
        return carry

    lax.fori_loop(0, tn, token, 0)
    o_ref[...] = g_ref[...] * jax.nn.gelu(o_ref[...], approximate=True)


def peer_act(idx, h3, g, table):
    N = idx.shape[0]
    tn = _PEER_TOKENS
    return pl.pallas_call(
        _peer_act_kernel,
        grid=(N // tn,),
        in_specs=[pl.BlockSpec((tn, PEER_PAIRS), lambda i: (i, 0), memory_space=pltpu.SMEM),
                  pl.BlockSpec((tn, SUBLANES, LANES), lambda i: (i, 0, 0)),
                  pl.BlockSpec((tn, PEER_PAIRS), lambda i: (i, 0)),
                  pl.BlockSpec(memory_space=pltpu.VMEM)],
        out_specs=pl.BlockSpec((tn, PEER_PAIRS), lambda i: (i, 0)),
        out_shape=jax.ShapeDtypeStruct((N, PEER_PAIRS), F32),
        scratch_shapes=[pltpu.VMEM((PEER_PAIRS, LANES), F32)],
        compiler_params=_params("arbitrary"),
        name="peer_act",
    )(idx, h3, g, table)


def _peer_mix_kernel(idx_ref, coef_ref, tab_ref, x_ref, gate_ref, o_ref):
    tn = x_ref.shape[0]
    zero = jnp.zeros((_HALF, LANES), F32)

    def token(t, carry):
        def group(gi, accs):
            accs = list(accs)
            base = gi * SUBLANES
            for j in range(SUBLANES):
                lo, hi = _unpack_row(tab_ref[idx_ref[t, base + j]])
                c = coef_ref[t, base + j]
                a = 2 * (j % 2)
                accs[a] = accs[a] + c * lo
                accs[a + 1] = accs[a + 1] + c * hi
            return tuple(accs)

        a = lax.fori_loop(0, PEER_PAIRS // SUBLANES, group, (zero, zero, zero, zero))
        y = jnp.concatenate([a[0] + a[2], a[1] + a[3]], axis=0)
        o_ref[t] = x_ref[t] + gate_ref[0] * y
        return carry

    lax.fori_loop(0, tn, token, 0)


def peer_mix(idx, coef, table, x3, gate3, tokens_per_batch):
    N = idx.shape[0]
    tn = _PEER_TOKENS
    assert tokens_per_batch % tn == 0
    if gate3.shape[0] == 1:
        gate_map = lambda i: (0, 0, 0)
    else:
        gate_map = lambda i: ((i * tn) // tokens_per_batch, 0, 0)
    return pl.pallas_call(
        _peer_mix_kernel,
        grid=(N // tn,),
        in_specs=[pl.BlockSpec((tn, PEER_PAIRS), lambda i: (i, 0), memory_space=pltpu.SMEM),
                  pl.BlockSpec((tn, PEER_PAIRS), lambda i: (i, 0), memory_space=pltpu.SMEM),
                  pl.BlockSpec(memory_space=pltpu.VMEM),
                  pl.BlockSpec((tn, SUBLANES, LANES), lambda i: (i, 0, 0)),
                  pl.BlockSpec((1, SUBLANES, LANES), gate_map)],
        out_specs=pl.BlockSpec((tn, SUBLANES, LANES), lambda i: (i, 0, 0)),
        out_shape=jax.ShapeDtypeStruct((N, SUBLANES, LANES), F32),
        compiler_params=_params("arbitrary"),
        name="peer_mix",
    )(idx, coef, table, x3, gate3)


def _rmsnorm_kernel(x_ref, g_ref, o_ref):
    x = x_ref[0]
    o_ref[0] = x * lax.rsqrt(jnp.mean(x * x, axis=-1, keepdims=True) + EPS) * g_ref[...]


def rmsnorm(x, g):
    B, L, D = x.shape
    tm = min(512, L)
    return pl.pallas_call(
        _rmsnorm_kernel,
        grid=(B, L // tm),
        in_specs=[pl.BlockSpec((1, tm, D), lambda b, i: (b, i, 0)),
                  pl.BlockSpec((1, D), lambda b, i: (0, 0))],
        out_specs=pl.BlockSpec((1, tm, D), lambda b, i: (b, i, 0)),
        out_shape=jax.ShapeDtypeStruct((B, L, D), F32),
        compiler_params=_params("parallel", "parallel"),
        name="final_rmsnorm",
    )(x, g.reshape(1, D))


def _split_heads(z, n, dtype=None):
    B, L, _ = z.shape
    z = z.reshape(B, L, n, HEAD_DIM).transpose(0, 2, 1, 3)
    return z if dtype is None else z.astype(dtype)


def _merge_heads(o):
    B, H, L, d = o.shape
    return o.transpose(0, 2, 1, 3).reshape(B, L, H * d)


def _rope_tables(T):
    t = jnp.arange(T, dtype=jnp.int32)
    row = (t // GRID_W).astype(F32)
    col = (t % GRID_W).astype(F32)
    n_freq = HEAD_DIM // 4
    inv_freq = ROPE_THETA ** (-jnp.arange(n_freq, dtype=F32) / n_freq)
    ang = jnp.concatenate([row[:, None] * inv_freq, col[:, None] * inv_freq], axis=-1)
    return jnp.cos(ang), jnp.sin(ang)


def _peer_ffn_residual(x, g2, sc2, sh2, gate, w_q, keys, u_tab, v_tab):
    B, L, D = x.shape
    N = B * L
    q, h = norm_mod_proj(x, g2, sc2, sh2, w_q, (w_q.shape[1],), emit_h=True)
    idx_t, g_t = peer_topk(q.reshape(N, -1), keys)
    idx = idx_t.reshape(PEER_PAIRS, N).T
    gates = g_t.reshape(PEER_PAIRS, N).T
    coef = peer_act(idx, h.reshape(N, SUBLANES, LANES), gates, u_tab)
    gate3 = gate.reshape(gate.shape[0], SUBLANES, LANES)
    out = peer_mix(idx, coef, v_tab, x.reshape(N, SUBLANES, LANES), gate3, L)
    return out.reshape(B, L, D)


def kernel(x, c, ctx, c_ctx, norm1_g, norm2_g, w_ada, b_ada, w_in, conv_w, conv_b, conv_ln_g, conv_ln_b,
           na_rel_bias, gqa_q_norm, gqa_k_norm, w_out, peer_w_q, peer_keys, peer_u, peer_v, final_norm_g):
    B, T, D = x.shape
    depth = w_in.shape[0]
    G = GQA_HEADS // GQA_KV_HEADS
    cos, sin = _rope_tables(T)
    ones_c = jnp.ones((ctx.shape[1], HEAD_DIM // 2), F32)
    ada_rows = -(-(B + 1) // SUBLANES) * SUBLANES
    ada_in = jnp.zeros((ada_rows, D), F32).at[:B].set(c).at[B].set(c_ctx)

    for l in range(depth):
        last = l == depth - 1
        mod = ada_mod(ada_in, w_ada[l], b_ada[l])
        sh1, sc1, g1, sh2, sc2, g2 = [m.reshape(B, 1, D) for m in jnp.split(mod[:B], 6, axis=-1)]
        csh1, csc1, cg1, csh2, csc2, cg2 = [m.reshape(1, 1, D) for m in jnp.split(mod[B:B + 1], 6, axis=-1)]
        w_in_b = w_in[l].astype(BF16)
        w_out_b = w_out[l].astype(BF16)
        w_q_b = peer_w_q[l].astype(BF16)
        keys_b = peer_keys[l].astype(BF16)
        u_tab = pack_expert_table(peer_u[l])
        v_tab = pack_expert_table(peer_v[l])

        za, na_q, g_q, na_k, na_v, g_k, g_v = norm_mod_proj(x, norm1_g[l], sc1, sh1, w_in_b, IN_SPLITS)
        zca, cna_q, cg_q, cna_k, cna_v, cg_k, cg_v = norm_mod_proj(ctx, norm1_g[l], csc1, csh1, w_in_b, IN_SPLITS)

        kc_na = _split_heads(cna_k, NA_HEADS, BF16)
        vc_na = _split_heads(cna_v, NA_HEADS, BF16)
        kc_g = qk_prep(_split_heads(cg_k, GQA_KV_HEADS), gqa_k_norm[l], ones_c, ones_c, False, BF16)
        vc_g = _split_heads(cg_v, GQA_KV_HEADS, BF16)

        a = conformer_conv(za, conv_w[l], conv_b[l], conv_ln_g[l], conv_ln_b[l])
        bm = neighbourhood_attention(_split_heads(na_q, NA_HEADS), _split_heads(na_k, NA_HEADS, BF16),
                                     _split_heads(na_v, NA_HEADS, BF16), kc_na, vc_na,
                                     na_bias_table(na_rel_bias[l]))
        qh = qk_prep(_split_heads(g_q, GQA_HEADS), gqa_q_norm[l], cos, sin, True, F32)
        kh = qk_prep(_split_heads(g_k, GQA_KV_HEADS), gqa_k_norm[l], cos, sin, True, BF16)
        k_all = jnp.concatenate([kh, kc_g], axis=2)
        v_all = jnp.concatenate([_split_heads(g_v, GQA_KV_HEADS, BF16), vc_g], axis=2)
        gm = grouped_attention(qh.reshape(B, GQA_KV_HEADS, G, T, HEAD_DIM), k_all, v_all)
        mix = jnp.concatenate([a, _merge_heads(bm), _merge_heads(gm.reshape(B, GQA_HEADS, T, HEAD_DIM))], axis=-1)
        x = proj_residual(mix, w_out_b, x, g1)

        x = _peer_ffn_residual(x, norm2_g[l], sc2, sh2, g2, w_q_b, keys_b, u_tab, v_tab)

        if not last:
            C = ctx.shape[1]
            ac = conformer_conv(zca, conv_w[l], conv_b[l], conv_ln_g[l], conv_ln_b[l])
            bc = grouped_attention(_split_heads(cna_q, NA_HEADS).reshape(B, NA_HEADS, 1, C, HEAD_DIM), kc_na, vc_na)
            qc_g = qk_prep(_split_heads(cg_q, GQA_HEADS), gqa_q_norm[l], ones_c, ones_c, False, F32)
            gc = grouped_attention(qc_g.reshape(B, GQA_KV_HEADS, G, C, HEAD_DIM), kc_g, vc_g)
            mixc = jnp.concatenate([ac, _merge_heads(bc.reshape(B, NA_HEADS, C, HEAD_DIM)),
                                    _merge_heads(gc.reshape(B, GQA_HEADS, C, HEAD_DIM))], axis=-1)
            ctx = proj_residual(mixc, w_out_b, ctx, cg1)
            ctx = _peer_ffn_residual(ctx, norm2_g[l], csc2, csh2, cg2, w_q_b, keys_b, u_tab, v_tab)
    return rmsnorm(x, final_norm_g)
```

```python
import functools

import jax
import jax.numpy as jnp
from jax import lax
from jax.experimental import pallas as pl
from jax.experimental.pallas import tpu as pltpu

F32 = jnp.float32
BF16 = jnp.bfloat16

GRID_W = 64
HEAD_DIM = 64
CONV_CH = 256
CONV_WIDTH = 31
NA_HEADS = 6
NA_WIN_H = 8
NA_WIN_W = 16
GQA_HEADS = 6
GQA_KV_HEADS = 2
ROPE_THETA = 10000.0
PEER_HEADS = 8
PEER_N_KEYS = 128
PEER_D_KEY = 128
PEER_TOPK = 16
EPS = 1e-6

NA_W = NA_HEADS * HEAD_DIM
GQA_W = GQA_HEADS * HEAD_DIM
GQA_KV_W = GQA_KV_HEADS * HEAD_DIM
IN_SPLITS = (2 * CONV_CH, NA_W, GQA_W, NA_W, NA_W, GQA_KV_W, GQA_KV_W)
PEER_PAIRS = PEER_HEADS * PEER_TOPK

LANES = 128
SUBLANES = 8
VMEM_LIMIT = 48 * 1024 * 1024
MASK_VALUE = -1e30


def _params(*sem):
    return pltpu.CompilerParams(dimension_semantics=sem, vmem_limit_bytes=VMEM_LIMIT)


def _ada_kernel(a_ref, w_ref, b_ref, o_ref):
    a = a_ref[...]
    s = a * jax.nn.sigmoid(a)
    o_ref[...] = jnp.dot(s.astype(BF16), w_ref[...].astype(BF16), preferred_element_type=F32) + b_ref[...]


def ada_mod(a, w, b):
    R, D = a.shape
    N = w.shape[1]
    tn = 1024
    return pl.pallas_call(
        _ada_kernel,
        grid=(N // tn,),
        in_specs=[pl.BlockSpec((R, D), lambda j: (0, 0)),
                  pl.BlockSpec((D, tn), lambda j: (0, j)),
                  pl.BlockSpec((1, tn), lambda j: (0, j))],
        out_specs=pl.BlockSpec((R, tn), lambda j: (0, j)),
        out_shape=jax.ShapeDtypeStruct((R, N), F32),
        compiler_params=_params("parallel"),
        name="ada_mod",
    )(a, w, b.reshape(1, N))


def _norm_mod_proj_kernel(x_ref, g_ref, sc_ref, sh_ref, w_ref, *out_refs, splits, emit_h):
    x = x_ref[0]
    y = x * lax.rsqrt(jnp.mean(x * x, axis=-1, keepdims=True) + EPS) * g_ref[...]
    h = y * (1.0 + sc_ref[0]) + sh_ref[0]
    z = jnp.dot(h.astype(BF16), w_ref[...], preferred_element_type=F32)
    off = 0
    for o_ref, width in zip(out_refs, splits):
        o_ref[0] = z[:, off:off + width]
        off += width
    if emit_h:
        out_refs[len(splits)][0] = h


def norm_mod_proj(x, g, sc, sh, w, splits, emit_h=False):
    B, L, D = x.shape
    N = w.shape[1]
    assert sum(splits) == N
    tm = min(256, L)
    per_batch = sc.shape[0] == B
    mod_map = (lambda b, i: (b, 0, 0)) if per_batch else (lambda b, i: (0, 0, 0))
    out_shape = [jax.ShapeDtypeStruct((B, L, s), F32) for s in splits]
    out_specs = [pl.BlockSpec((1, tm, s), lambda b, i: (b, i, 0)) for s in splits]
    if emit_h:
        out_shape.append(jax.ShapeDtypeStruct((B, L, D), F32))
        out_specs.append(pl.BlockSpec((1, tm, D), lambda b, i: (b, i, 0)))
    return pl.pallas_call(
        functools.partial(_norm_mod_proj_kernel, splits=splits, emit_h=emit_h),
        grid=(B, L // tm),
        in_specs=[pl.BlockSpec((1, tm, D), lambda b, i: (b, i, 0)),
                  pl.BlockSpec((1, D), lambda b, i: (0, 0)),
                  pl.BlockSpec((1, 1, D), mod_map),
                  pl.BlockSpec((1, 1, D), mod_map),
                  pl.BlockSpec((D, N), lambda b, i: (0, 0))],
        out_specs=out_specs,
        out_shape=out_shape,
        compiler_params=_params("parallel", "parallel"),
        name="norm_mod_proj",
    )(x, g.reshape(1, D), sc, sh, w)


_CONV_PAD = 16


def _conv_kernel(za_ref, w_ref, b_ref, lg_ref, lb_ref, o_ref, upad_ref, *, L, tc):
    c = pl.program_id(1)

    @pl.when(c == 0)
    def _():
        val = za_ref[0, :, :CONV_CH]
        gate = za_ref[0, :, CONV_CH:]
        zeros = jnp.zeros((_CONV_PAD, CONV_CH), F32)
        upad_ref[pl.ds(0, _CONV_PAD), :] = zeros
        upad_ref[pl.ds(_CONV_PAD + L, _CONV_PAD), :] = zeros
        upad_ref[pl.ds(_CONV_PAD, L), :] = val * jax.nn.sigmoid(gate)

    start = pl.multiple_of(c * tc, SUBLANES)
    win = upad_ref[pl.ds(start, tc + 2 * _CONV_PAD), :]
    acc = jnp.zeros((tc, CONV_CH), F32)
    first = _CONV_PAD - CONV_WIDTH // 2
    for j in range(CONV_WIDTH):
        acc = acc + win[first + j:first + j + tc, :] * w_ref[j:j + 1, :]
    y = acc + b_ref[...]
    mu = jnp.mean(y, axis=-1, keepdims=True)
    d = y - mu
    var = jnp.mean(d * d, axis=-1, keepdims=True)
    yn = d * lax.rsqrt(var + EPS) * lg_ref[...] + lb_ref[...]
    o_ref[0] = yn * jax.nn.sigmoid(yn)


def conformer_conv(za, w_dw, b_dw, ln_g, ln_b):
    B, L, _ = za.shape
    tc = min(256, L)
    vec = lambda v: v.reshape(1, CONV_CH)
    return pl.pallas_call(
        functools.partial(_conv_kernel, L=L, tc=tc),
        grid=(B, L // tc),
        in_specs=[pl.BlockSpec((1, L, 2 * CONV_CH), lambda b, c: (b, 0, 0)),
                  pl.BlockSpec((CONV_WIDTH, CONV_CH), lambda b, c: (0, 0)),
                  pl.BlockSpec((1, CONV_CH), lambda b, c: (0, 0)),
                  pl.BlockSpec((1, CONV_CH), lambda b, c: (0, 0)),
                  pl.BlockSpec((1, CONV_CH), lambda b, c: (0, 0))],
        out_specs=pl.BlockSpec((1, tc, CONV_CH), lambda b, c: (b, c, 0)),
        out_shape=jax.ShapeDtypeStruct((B, L, CONV_CH), F32),
        scratch_shapes=[pltpu.VMEM((L + 2 * _CONV_PAD, CONV_CH), F32)],
        compiler_params=_params("parallel", "arbitrary"),
        name="conformer_conv",
    )(za, w_dw, vec(b_dw), vec(ln_g), vec(ln_b))


def _na_row_offset(r, rows):
    return r - jnp.clip(r - NA_WIN_H // 2, 0, rows - NA_WIN_H)


def _na_kernel(q_ref, k_ref, v_ref, kc_ref, vc_ref, bias_ref, o_ref, *, rows):
    r = pl.program_id(1)
    r0 = r - _na_row_offset(r, rows)
    band = pl.ds(pl.multiple_of(r0 * GRID_W, GRID_W), NA_WIN_H * GRID_W)
    nt = (((1,), (1,)), ((), ()))
    for h in range(NA_HEADS):
        q = (q_ref[0, h] * (HEAD_DIM ** -0.5)).astype(BF16)
        kb = k_ref[0, h, band, :]
        vb = v_ref[0, h, band, :]
        s_loc = lax.dot_general(q, kb, nt, preferred_element_type=F32) + bias_ref[0, h]
        s_ctx = lax.dot_general(q, kc_ref[0, h], nt, preferred_element_type=F32)
        m = jnp.maximum(jnp.max(s_loc, axis=-1, keepdims=True), jnp.max(s_ctx, axis=-1, keepdims=True))
        p_loc = jnp.exp(s_loc - m)
        p_ctx = jnp.exp(s_ctx - m)
        den = jnp.sum(p_loc, axis=-1, keepdims=True) + jnp.sum(p_ctx, axis=-1, keepdims=True)
        o = (jnp.dot(p_loc.astype(BF16), vb, preferred_element_type=F32)
             + jnp.dot(p_ctx.astype(BF16), vc_ref[0, h], preferred_element_type=F32))
        o_ref[0, h] = o / den


def na_bias_table(rel_bias):
    cols = jnp.arange(GRID_W, dtype=jnp.int32)
    c0 = jnp.clip(cols - NA_WIN_W // 2, 0, GRID_W - NA_WIN_W)
    in_win = (cols[None, :] >= c0[:, None]) & (cols[None, :] < c0[:, None] + NA_WIN_W)
    col_idx = jnp.clip(cols[None, :] - cols[:, None] + (NA_WIN_W - 1), 0, 2 * NA_WIN_W - 2)
    off = jnp.arange(NA_WIN_H, dtype=jnp.int32)
    row_idx = off[None, :] - off[:, None] + (NA_WIN_H - 1)
    t = rel_bias[:, row_idx]
    t = t[..., col_idx]
    t = jnp.where(in_win[None, None, None], t, MASK_VALUE)
    return t.transpose(1, 0, 3, 2, 4).reshape(NA_WIN_H, NA_HEADS, GRID_W, NA_WIN_H * GRID_W)


def neighbourhood_attention(q, k, v, kc, vc, bias):
    B, H, T, d = q.shape
    C = kc.shape[2]
    rows = T // GRID_W
    assert rows >= NA_WIN_H
    return pl.pallas_call(
        functools.partial(_na_kernel, rows=rows),
        grid=(B, rows),
        in_specs=[pl.BlockSpec((1, H, GRID_W, d), lambda b, r: (b, 0, r, 0)),
                  pl.BlockSpec((1, H, T, d), lambda b, r: (b, 0, 0, 0)),
                  pl.BlockSpec((1, H, T, d), lambda b, r: (b, 0, 0, 0)),
                  pl.BlockSpec((1, H, C, d), lambda b, r: (b, 0, 0, 0)),
                  pl.BlockSpec((1, H, C, d), lambda b, r: (b, 0, 0, 0)),
                  pl.BlockSpec((1, H, GRID_W, NA_WIN_H * GRID_W),
                               lambda b, r: (_na_row_offset(r, rows), 0, 0, 0))],
        out_specs=pl.BlockSpec((1, H, GRID_W, d), lambda b, r: (b, 0, r, 0)),
        out_shape=jax.ShapeDtypeStruct((B, H, T, d), F32),
        compiler_params=_params("parallel", "arbitrary"),
        name="neighbourhood_attention",
    )(q, k, v, kc, vc, bias)


def _qk_prep_kernel(x_ref, g_ref, cos_ref, sin_ref, o_ref, *, rope):
    x = x_ref[0, 0]
    y = x * lax.rsqrt(jnp.mean(x * x, axis=-1, keepdims=True) + EPS) * g_ref[...]
    if rope:
        half = HEAD_DIM // 2
        x1, x2 = y[:, :half], y[:, half:]
        cos, sin = cos_ref[...], sin_ref[...]
        y = jnp.concatenate([x1 * cos - x2 * sin, x1 * sin + x2 * cos], axis=-1)
    o_ref[0, 0] = y.astype(o_ref.dtype)


def qk_prep(x, g, cos, sin, rope, out_dtype):
    B, H, L, d = x.shape
    tl = min(512, L)
    return pl.pallas_call(
        functools.partial(_qk_prep_kernel, rope=rope),
        grid=(B, H, L // tl),
        in_specs=[pl.BlockSpec((1, 1, tl, d), lambda b, h, i: (b, h, i, 0)),
                  pl.BlockSpec((1, d), lambda b, h, i: (0, 0)),
                  pl.BlockSpec((tl, d // 2), lambda b, h, i: (i, 0)),
                  pl.BlockSpec((tl, d // 2), lambda b, h, i: (i, 0))],
        out_specs=pl.BlockSpec((1, 1, tl, d), lambda b, h, i: (b, h, i, 0)),
        out_shape=jax.ShapeDtypeStruct((B, H, L, d), out_dtype),
        compiler_params=_params("parallel", "parallel", "parallel"),
        name="qk_prep",
    )(x, g.reshape(1, d), cos, sin)


def _attn_kernel(q_ref, k_ref, v_ref, o_ref, *, G, bq):
    q = (q_ref[0, 0].reshape(G * bq, HEAD_DIM) * (HEAD_DIM ** -0.5)).astype(BF16)
    s = lax.dot_general(q, k_ref[0, 0], (((1,), (1,)), ((), ())), preferred_element_type=F32)
    m = jnp.max(s, axis=-1, keepdims=True)
    p = jnp.exp(s - m)
    den = jnp.sum(p, axis=-1, keepdims=True)
    o = jnp.dot(p.astype(BF16), v_ref[0, 0], preferred_element_type=F32) / den
    o_ref[0, 0] = o.reshape(G, bq, HEAD_DIM)


def grouped_attention(q, k, v):
    B, Hk, G, L, d = q.shape
    S = k.shape[2]
    bq = min(128, L)
    return pl.pallas_call(
        functools.partial(_attn_kernel, G=G, bq=bq),
        grid=(B, Hk, L // bq),
        in_specs=[pl.BlockSpec((1, 1, G, bq, d), lambda b, h, i: (b, h, 0, i, 0)),
                  pl.BlockSpec((1, 1, S, d), lambda b, h, i: (b, h, 0, 0)),
                  pl.BlockSpec((1, 1, S, d), lambda b, h, i: (b, h, 0, 0))],
        out_specs=pl.BlockSpec((1, 1, G, bq, d), lambda b, h, i: (b, h, 0, i, 0)),
        out_shape=jax.ShapeDtypeStruct((B, Hk, G, L, d), F32),
        compiler_params=_params("parallel", "parallel", "arbitrary"),
        name="grouped_attention",
    )(q, k, v)


def _proj_residual_kernel(m_ref, w_ref, x_ref, gate_ref, o_ref):
    y = jnp.dot(m_ref[0].astype(BF16), w_ref[...], preferred_element_type=F32)
    o_ref[0] = x_ref[0] + gate_ref[0] * y


def proj_residual(mix, w, x, gate):
    B, L, K = mix.shape
    D = w.shape[1]
    tm = min(512, L)
    gate_map = (lambda b, i: (b, 0, 0)) if gate.shape[0] == B else (lambda b, i: (0, 0, 0))
    return pl.pallas_call(
        _proj_residual_kernel,
        grid=(B, L // tm),
        in_specs=[pl.BlockSpec((1, tm, K), lambda b, i: (b, i, 0)),
                  pl.BlockSpec((K, D), lambda b, i: (0, 0)),
                  pl.BlockSpec((1, tm, D), lambda b, i: (b, i, 0)),
                  pl.BlockSpec((1, 1, D), gate_map)],
        out_specs=pl.BlockSpec((1, tm, D), lambda b, i: (b, i, 0)),
        out_shape=jax.ShapeDtypeStruct((B, L, D), F32),
        compiler_params=_params("parallel", "parallel"),
        name="proj_residual",
    )(mix, w, x, gate)


def _top_rows(s, k, payload=None):
    n = s.shape[0]
    rows = lax.broadcasted_iota(jnp.int32, s.shape, 0)
    vals, picks = [], []
    for _ in range(k):
        m = jnp.max(s, axis=0, keepdims=True)
        first = jnp.min(jnp.where(s == m, rows, n), axis=0, keepdims=True)
        hit = rows == first
        vals.append(m)
        if payload is None:
            picks.append(first)
        else:
            picks.append(jnp.max(jnp.where(hit, payload, -1), axis=0, keepdims=True))
        s = jnp.where(hit, -jnp.inf, s)
    return jnp.concatenate(vals, axis=0), jnp.concatenate(picks, axis=0)


def _peer_topk_kernel(q_ref, keys_ref, idx_ref, g_ref):
    K = PEER_TOPK
    nt = (((1,), (1,)), ((), ()))
    for h in range(PEER_HEADS):
        tops = []
        for p in range(2):
            col = (2 * h + p) * PEER_D_KEY
            qhp = q_ref[:, col:col + PEER_D_KEY].astype(BF16)
            s = lax.dot_general(keys_ref[h, p], qhp, nt, preferred_element_type=F32)
            tops.append(_top_rows(s, K))
        (s0, i0), (s1, i1) = tops
        t = s0.shape[1]
        cand_s = (s0[:, None, :] + s1[None, :, :]).reshape(K * K, t)
        cand_i = (i0[:, None, :] * PEER_N_KEYS + i1[None, :, :]).reshape(K * K, t)
        best_s, best_i = _top_rows(cand_s, K, payload=cand_i)
        e = jnp.exp(best_s - jnp.max(best_s, axis=0, keepdims=True))
        g_ref[h] = e / jnp.sum(e, axis=0, keepdims=True)
        idx_ref[h] = best_i


def peer_topk(q, keys):
    N = q.shape[0]
    tm = LANES
    return pl.pallas_call(
        _peer_topk_kernel,
        grid=(N // tm,),
        in_specs=[pl.BlockSpec((tm, q.shape[1]), lambda i: (i, 0)),
                  pl.BlockSpec(keys.shape, lambda i: (0, 0, 0, 0))],
        out_specs=[pl.BlockSpec((PEER_HEADS, PEER_TOPK, tm), lambda i: (0, 0, i)),
                   pl.BlockSpec((PEER_HEADS, PEER_TOPK, tm), lambda i: (0, 0, i))],
        out_shape=[jax.ShapeDtypeStruct((PEER_HEADS, PEER_TOPK, N), jnp.int32),
                   jax.ShapeDtypeStruct((PEER_HEADS, PEER_TOPK, N), F32)],
        compiler_params=_params("parallel"),
        name="peer_topk",
    )(q, keys)


_HALF = SUBLANES // 2
_PEER_TOKENS = 64
_GROUPS = PEER_PAIRS // SUBLANES


def pack_expert_table(w):
    E, D = w.shape
    assert D == SUBLANES * LANES
    bits = lax.bitcast_convert_type(w.astype(BF16), jnp.uint16).astype(jnp.uint32).reshape(E, _HALF, 2, LANES)
    return bits[:, :, 0] | (bits[:, :, 1] << 16)


def _expert_row(tab_ref, e):
    return pltpu.bitcast(tab_ref[e], BF16).astype(F32)


def _merge_pair(a, b, shift, first):
    if shift == _HALF:
        return jnp.where(first, a, b) + pltpu.roll(jnp.where(first, b, a), shift, axis=0)
    bs = pltpu.roll(b, shift, axis=0)
    return jnp.where(first, a, bs) + pltpu.roll(jnp.where(first, bs, a), SUBLANES - shift, axis=0)


_MERGE_ORDER = (0, 4, 2, 6, 1, 5, 3, 7)


def _merge8(ps):
    sub = lax.broadcasted_iota(jnp.int32, (SUBLANES, LANES), 0)
    shift = _HALF
    while len(ps) > 1:
        first = (sub % (2 * shift)) < shift
        ps = [_merge_pair(ps[2 * i], ps[2 * i + 1], shift, first) for i in range(len(ps) // 2)]
        shift //= 2
    return ps[0]


def _peer_act_kernel(idx_ref, h_ref, g_ref, tab_ref, o_ref, part_ref):
    tn = h_ref.shape[0]

    def products(t, slot):
        x = h_ref[t]
        for gi in range(_GROUPS):
            ps = [_expert_row(tab_ref, idx_ref[t, gi * SUBLANES + j]) * x for j in _MERGE_ORDER]
            part_ref[slot, pl.ds(gi * SUBLANES, SUBLANES), :] = _merge8(ps)

    def reduce(t, slot):
        o_ref[pl.ds(t, 1), :] = jnp.sum(part_ref[slot].T, axis=0, keepdims=True)

    part_ref[1] = jnp.zeros((PEER_PAIRS, LANES), F32)

    def two_tokens(i, carry):
        t = 2 * i
        reduce(jnp.maximum(t - 1, 0), 1)
        products(t, 0)
        reduce(t, 0)
        products(t + 1, 1)
        return carry

    lax.fori_loop(0, tn // 2, two_tokens, 0)
    reduce(tn - 1, 1)
    o_ref[...] = g_ref[...] * jax.nn.gelu(o_ref[...], approximate=True)


def peer_act(idx, h3, g, table):
    N = idx.shape[0]
    tn = _PEER_TOKENS
    return pl.pallas_call(
        _peer_act_kernel,
        grid=(N // tn,),
        in_specs=[pl.BlockSpec((tn, PEER_PAIRS), lambda i: (i, 0), memory_space=pltpu.SMEM),
                  pl.BlockSpec((tn, SUBLANES, LANES), lambda i: (i, 0, 0)),
                  pl.BlockSpec((tn, PEER_PAIRS), lambda i: (i, 0)),
                  pl.BlockSpec(memory_space=pltpu.VMEM)],
        out_specs=pl.BlockSpec((tn, PEER_PAIRS), lambda i: (i, 0)),
        out_shape=jax.ShapeDtypeStruct((N, PEER_PAIRS), F32),
        scratch_shapes=[pltpu.VMEM((2, PEER_PAIRS, LANES), F32)],
        compiler_params=_params("arbitrary"),
        name="peer_act",
    )(idx, h3, g, table)


_MIX_CHAINS = 4


def _peer_mix_kernel(idx_ref, coef_ref, tab_ref, x_ref, gate_ref, o_ref, cb_ref):
    tn = x_ref.shape[0]

    def spread(t, slot):
        row = coef_ref[pl.ds(t, 1), :]
        cb_ref[slot] = jnp.broadcast_to(row, (PEER_PAIRS, PEER_PAIRS)).T

    def mix(t, slot):
        accs = [None] * _MIX_CHAINS
        for j in range(PEER_PAIRS):
            c = jnp.broadcast_to(cb_ref[slot, pl.ds(j, 1), :], (SUBLANES, LANES))
            term = c * _expert_row(tab_ref, idx_ref[t, j])
            a = j % _MIX_CHAINS
            accs[a] = term if accs[a] is None else accs[a] + term
        y = (accs[0] + accs[1]) + (accs[2] + accs[3])
        o_ref[t] = x_ref[t] + gate_ref[0] * y

    spread(0, 0)

    def two_tokens(i, carry):
        t = 2 * i
        spread(t + 1, 1)
        mix(t, 0)
        spread(jnp.minimum(t + 2, tn - 1), 0)
        mix(t + 1, 1)
        return carry

    lax.fori_loop(0, tn // 2, two_tokens, 0)


def peer_mix(idx, coef, table, x3, gate3, tokens_per_batch):
    N = idx.shape[0]
    tn = _PEER_TOKENS
    assert tokens_per_batch % tn == 0
    if gate3.shape[0] == 1:
        gate_map = lambda i: (0, 0, 0)
    else:
        gate_map = lambda i: ((i * tn) // tokens_per_batch, 0, 0)
    return pl.pallas_call(
        _peer_mix_kernel,
        grid=(N // tn,),
        in_specs=[pl.BlockSpec((tn, PEER_PAIRS), lambda i: (i, 0), memory_space=pltpu.SMEM),
                  pl.BlockSpec((tn, PEER_PAIRS), lambda i: (i, 0)),
                  pl.BlockSpec(memory_space=pltpu.VMEM),
                  pl.BlockSpec((tn, SUBLANES, LANES), lambda i: (i, 0, 0)),
                  pl.BlockSpec((1, SUBLANES, LANES), gate_map)],
        out_specs=pl.BlockSpec((tn, SUBLANES, LANES), lambda i: (i, 0, 0)),
        out_shape=jax.ShapeDtypeStruct((N, SUBLANES, LANES), F32),
        scratch_shapes=[pltpu.VMEM((2, PEER_PAIRS, LANES), F32)],
        compiler_params=_params("arbitrary"),
        name="peer_mix",
    )(idx, coef, table, x3, gate3)


def _rmsnorm_kernel(x_ref, g_ref, o_ref):
    x = x_ref[0]
    o_ref[0] = x * lax.rsqrt(jnp.mean(x * x, axis=-1, keepdims=True) + EPS) * g_ref[...]


def rmsnorm(x, g):
    B, L, D = x.shape
    tm = min(512, L)
    return pl.pallas_call(
        _rmsnorm_kernel,
        grid=(B, L // tm),
        in_specs=[pl.BlockSpec((1, tm, D), lambda b, i: (b, i, 0)),
                  pl.BlockSpec((1, D), lambda b, i: (0, 0))],
        out_specs=pl.BlockSpec((1, tm, D), lambda b, i: (b, i, 0)),
        out_shape=jax.ShapeDtypeStruct((B, L, D), F32),
        compiler_params=_params("parallel", "parallel"),
        name="final_rmsnorm",
    )(x, g.reshape(1, D))


def _split_heads(z, n, dtype=None):
    B, L, _ = z.shape
    z = z.reshape(B, L, n, HEAD_DIM).transpose(0, 2, 1, 3)
    return z if dtype is None else z.astype(dtype)


def _merge_heads(o):
    B, H, L, d = o.shape
    return o.transpose(0, 2, 1, 3).reshape(B, L, H * d)


def _rope_tables(T):
    t = jnp.arange(T, dtype=jnp.int32)
    row = (t // GRID_W).astype(F32)
    col = (t % GRID_W).astype(F32)
    n_freq = HEAD_DIM // 4
    inv_freq = ROPE_THETA ** (-jnp.arange(n_freq, dtype=F32) / n_freq)
    ang = jnp.concatenate([row[:, None] * inv_freq, col[:, None] * inv_freq], axis=-1)
    return jnp.cos(ang), jnp.sin(ang)


def _peer_ffn_residual(x, g2, sc2, sh2, gate, w_q, keys, u_tab, v_tab):
    B, L, D = x.shape
    N = B * L
    q, h = norm_mod_proj(x, g2, sc2, sh2, w_q, (w_q.shape[1],), emit_h=True)
    idx_t, g_t = peer_topk(q.reshape(N, -1), keys)
    idx = idx_t.reshape(PEER_PAIRS, N).T
    gates = g_t.reshape(PEER_PAIRS, N).T
    coef = peer_act(idx, h.reshape(N, SUBLANES, LANES), gates, u_tab)
    gate3 = gate.reshape(gate.shape[0], SUBLANES, LANES)
    out = peer_mix(idx, coef, v_tab, x.reshape(N, SUBLANES, LANES), gate3, L)
    return out.reshape(B, L, D)


def kernel(x, c, ctx, c_ctx, norm1_g, norm2_g, w_ada, b_ada, w_in, conv_w, conv_b, conv_ln_g, conv_ln_b,
           na_rel_bias, gqa_q_norm, gqa_k_norm, w_out, peer_w_q, peer_keys, peer_u, peer_v, final_norm_g):
    B, T, D = x.shape
    depth = w_in.shape[0]
    G = GQA_HEADS // GQA_KV_HEADS
    cos, sin = _rope_tables(T)
    ones_c = jnp.ones((ctx.shape[1], HEAD_DIM // 2), F32)
    ada_rows = -(-(B + 1) // SUBLANES) * SUBLANES
    ada_in = jnp.zeros((ada_rows, D), F32).at[:B].set(c).at[B].set(c_ctx)

    for l in range(depth):
        last = l == depth - 1
        mod = ada_mod(ada_in, w_ada[l], b_ada[l])
        sh1, sc1, g1, sh2, sc2, g2 = [m.reshape(B, 1, D) for m in jnp.split(mod[:B], 6, axis=-1)]
        csh1, csc1, cg1, csh2, csc2, cg2 = [m.reshape(1, 1, D) for m in jnp.split(mod[B:B + 1], 6, axis=-1)]
        w_in_b = w_in[l].astype(BF16)
        w_out_b = w_out[l].astype(BF16)
        w_q_b = peer_w_q[l].astype(BF16)
        keys_b = peer_keys[l].astype(BF16)
        u_tab = pack_expert_table(peer_u[l])
        v_tab = pack_expert_table(peer_v[l])

        za, na_q, g_q, na_k, na_v, g_k, g_v = norm_mod_proj(x, norm1_g[l], sc1, sh1, w_in_b, IN_SPLITS)
        zca, cna_q, cg_q, cna_k, cna_v, cg_k, cg_v = norm_mod_proj(ctx, norm1_g[l], csc1, csh1, w_in_b, IN_SPLITS)

        kc_na = _split_heads(cna_k, NA_HEADS, BF16)
        vc_na = _split_heads(cna_v, NA_HEADS, BF16)
        kc_g = qk_prep(_split_heads(cg_k, GQA_KV_HEADS), gqa_k_norm[l], ones_c, ones_c, False, BF16)
        vc_g = _split_heads(cg_v, GQA_KV_HEADS, BF16)

        a = conformer_conv(za, conv_w[l], conv_b[l], conv_ln_g[l], conv_ln_b[l])
        bm = neighbourhood_attention(_split_heads(na_q, NA_HEADS), _split_heads(na_k, NA_HEADS, BF16),
                                     _split_heads(na_v, NA_HEADS, BF16), kc_na, vc_na,
                                     na_bias_table(na_rel_bias[l]))
        qh = qk_prep(_split_heads(g_q, GQA_HEADS), gqa_q_norm[l], cos, sin, True, F32)
        kh = qk_prep(_split_heads(g_k, GQA_KV_HEADS), gqa_k_norm[l], cos, sin, True, BF16)
        k_all = jnp.concatenate([kh, kc_g], axis=2)
        v_all = jnp.concatenate([_split_heads(g_v, GQA_KV_HEADS, BF16), vc_g], axis=2)
        gm = grouped_attention(qh.reshape(B, GQA_KV_HEADS, G, T, HEAD_DIM), k_all, v_all)
        mix = jnp.concatenate([a, _merge_heads(bm), _merge_heads(gm.reshape(B, GQA_HEADS, T, HEAD_DIM))], axis=-1)
        x = proj_residual(mix, w_out_b, x, g1)

        x = _peer_ffn_residual(x, norm2_g[l], sc2, sh2, g2, w_q_b, keys_b, u_tab, v_tab)

        if not last:
            C = ctx.shape[1]
            ac = conformer_conv(zca, conv_w[l], conv_b[l], conv_ln_g[l], conv_ln_b[l])
            bc = grouped_attention(_split_heads(cna_q, NA_HEADS).reshape(B, NA_HEADS, 1, C, HEAD_DIM), kc_na, vc_na)
            qc_g = qk_prep(_split_heads(cg_q, GQA_HEADS), gqa_q_norm[l], ones_c, ones_c, False, F32)
            gc = grouped_attention(qc_g.reshape(B, GQA_KV_HEADS, G, C, HEAD_DIM), kc_g, vc_g)
            mixc = jnp.concatenate([ac, _merge_heads(bc.reshape(B, NA_HEADS, C, HEAD_DIM)),
                                    _merge_heads(gc.reshape(B, GQA_HEADS, C, HEAD_DIM))], axis=-1)
            ctx = proj_residual(mixc, w_out_b, ctx, cg1)
            ctx = _peer_ffn_residual(ctx, norm2_g[l], csc2, csh2, cg2, w_q_b, keys_b, u_tab, v_tab)
    return rmsnorm(x, final_norm_g)
```

```python
import functools

import jax
import jax.numpy as jnp
from jax import lax
from jax.experimental import pallas as pl
from jax.experimental.pallas import tpu as pltpu

F32 = jnp.float32
BF16 = jnp.bfloat16

GRID_W = 64
HEAD_DIM = 64
CONV_CH = 256
CONV_WIDTH = 31
NA_HEADS = 6
NA_WIN_H = 8
NA_WIN_W = 16
GQA_HEADS = 6
GQA_KV_HEADS = 2
ROPE_THETA = 10000.0
PEER_HEADS = 8
PEER_N_KEYS = 128
PEER_D_KEY = 128
PEER_TOPK = 16
EPS = 1e-6

NA_W = NA_HEADS * HEAD_DIM
GQA_W = GQA_HEADS * HEAD_DIM
GQA_KV_W = GQA_KV_HEADS * HEAD_DIM
IN_SPLITS = (2 * CONV_CH, NA_W, GQA_W, NA_W, NA_W, GQA_KV_W, GQA_KV_W)
PEER_PAIRS = PEER_HEADS * PEER_TOPK

LANES = 128
SUBLANES = 8
VMEM_LIMIT = 48 * 1024 * 1024
MASK_VALUE = -1e30


def _params(*sem):
    return pltpu.CompilerParams(dimension_semantics=sem, vmem_limit_bytes=VMEM_LIMIT)


def _ada_kernel(a_ref, w_ref, b_ref, o_ref):
    a = a_ref[...]
    s = a * jax.nn.sigmoid(a)
    o_ref[...] = jnp.dot(s.astype(BF16), w_ref[...].astype(BF16), preferred_element_type=F32) + b_ref[...]


def ada_mod(a, w, b):
    R, D = a.shape
    N = w.shape[1]
    tn = 1024
    return pl.pallas_call(
        _ada_kernel,
        grid=(N // tn,),
        in_specs=[pl.BlockSpec((R, D), lambda j: (0, 0)),
                  pl.BlockSpec((D, tn), lambda j: (0, j)),
                  pl.BlockSpec((1, tn), lambda j: (0, j))],
        out_specs=pl.BlockSpec((R, tn), lambda j: (0, j)),
        out_shape=jax.ShapeDtypeStruct((R, N), F32),
        compiler_params=_params("parallel"),
        name="ada_mod",
    )(a, w, b.reshape(1, N))


def _norm_mod_proj_kernel(x_ref, g_ref, sc_ref, sh_ref, w_ref, *out_refs, splits, emit_h):
    x = x_ref[0]
    y = x * lax.rsqrt(jnp.mean(x * x, axis=-1, keepdims=True) + EPS) * g_ref[...]
    h = y * (1.0 + sc_ref[0]) + sh_ref[0]
    z = jnp.dot(h.astype(BF16), w_ref[...], preferred_element_type=F32)
    off = 0
    for o_ref, width in zip(out_refs, splits):
        o_ref[0] = z[:, off:off + width]
        off += width
    if emit_h:
        out_refs[len(splits)][0] = h


def norm_mod_proj(x, g, sc, sh, w, splits, emit_h=False):
    B, L, D = x.shape
    N = w.shape[1]
    assert sum(splits) == N
    tm = min(256, L)
    per_batch = sc.shape[0] == B
    mod_map = (lambda b, i: (b, 0, 0)) if per_batch else (lambda b, i: (0, 0, 0))
    out_shape = [jax.ShapeDtypeStruct((B, L, s), F32) for s in splits]
    out_specs = [pl.BlockSpec((1, tm, s), lambda b, i: (b, i, 0)) for s in splits]
    if emit_h:
        out_shape.append(jax.ShapeDtypeStruct((B, L, D), F32))
        out_specs.append(pl.BlockSpec((1, tm, D), lambda b, i: (b, i, 0)))
    return pl.pallas_call(
        functools.partial(_norm_mod_proj_kernel, splits=splits, emit_h=emit_h),
        grid=(B, L // tm),
        in_specs=[pl.BlockSpec((1, tm, D), lambda b, i: (b, i, 0)),
                  pl.BlockSpec((1, D), lambda b, i: (0, 0)),
                  pl.BlockSpec((1, 1, D), mod_map),
                  pl.BlockSpec((1, 1, D), mod_map),
                  pl.BlockSpec((D, N), lambda b, i: (0, 0))],
        out_specs=out_specs,
        out_shape=out_shape,
        compiler_params=_params("parallel", "parallel"),
        name="norm_mod_proj",
    )(x, g.reshape(1, D), sc, sh, w)


_CONV_PAD = 16


def _conv_kernel(za_ref, w_ref, b_ref, lg_ref, lb_ref, o_ref, upad_ref, *, L, tc):
    c = pl.program_id(1)

    @pl.when(c == 0)
    def _():
        val = za_ref[0, :, :CONV_CH]
        gate = za_ref[0, :, CONV_CH:]
        zeros = jnp.zeros((_CONV_PAD, CONV_CH), F32)
        upad_ref[pl.ds(0, _CONV_PAD), :] = zeros
        upad_ref[pl.ds(_CONV_PAD + L, _CONV_PAD), :] = zeros
        upad_ref[pl.ds(_CONV_PAD, L), :] = val * jax.nn.sigmoid(gate)

    start = pl.multiple_of(c * tc, SUBLANES)
    win = upad_ref[pl.ds(start, tc + 2 * _CONV_PAD), :]
    acc = jnp.zeros((tc, CONV_CH), F32)
    first = _CONV_PAD - CONV_WIDTH // 2
    for j in range(CONV_WIDTH):
        acc = acc + win[first + j:first + j + tc, :] * w_ref[j:j + 1, :]
    y = acc + b_ref[...]
    mu = jnp.mean(y, axis=-1, keepdims=True)
    d = y - mu
    var = jnp.mean(d * d, axis=-1, keepdims=True)
    yn = d * lax.rsqrt(var + EPS) * lg_ref[...] + lb_ref[...]
    o_ref[0] = yn * jax.nn.sigmoid(yn)


def conformer_conv(za, w_dw, b_dw, ln_g, ln_b):
    B, L, _ = za.shape
    tc = min(256, L)
    vec = lambda v: v.reshape(1, CONV_CH)
    return pl.pallas_call(
        functools.partial(_conv_kernel, L=L, tc=tc),
        grid=(B, L // tc),
        in_specs=[pl.BlockSpec((1, L, 2 * CONV_CH), lambda b, c: (b, 0, 0)),
                  pl.BlockSpec((CONV_WIDTH, CONV_CH), lambda b, c: (0, 0)),
                  pl.BlockSpec((1, CONV_CH), lambda b, c: (0, 0)),
                  pl.BlockSpec((1, CONV_CH), lambda b, c: (0, 0)),
                  pl.BlockSpec((1, CONV_CH), lambda b, c: (0, 0))],
        out_specs=pl.BlockSpec((1, tc, CONV_CH), lambda b, c: (b, c, 0)),
        out_shape=jax.ShapeDtypeStruct((B, L, CONV_CH), F32),
        scratch_shapes=[pltpu.VMEM((L + 2 * _CONV_PAD, CONV_CH), F32)],
        compiler_params=_params("parallel", "arbitrary"),
        name="conformer_conv",
    )(za, w_dw, vec(b_dw), vec(ln_g), vec(ln_b))


def _na_row_offset(r, rows):
    return r - jnp.clip(r - NA_WIN_H // 2, 0, rows - NA_WIN_H)


def _na_kernel(q_ref, k_ref, v_ref, kc_ref, vc_ref, bias_ref, o_ref, *, rows):
    r = pl.program_id(1)
    r0 = r - _na_row_offset(r, rows)
    band = pl.ds(pl.multiple_of(r0 * GRID_W, GRID_W), NA_WIN_H * GRID_W)
    nt = (((1,), (1,)), ((), ()))
    for h in range(NA_HEADS):
        q = (q_ref[0, h] * (HEAD_DIM ** -0.5)).astype(BF16)
        kb = k_ref[0, h, band, :]
        vb = v_ref[0, h, band, :]
        s_loc = lax.dot_general(q, kb, nt, preferred_element_type=F32) + bias_ref[0, h]
        s_ctx = lax.dot_general(q, kc_ref[0, h], nt, preferred_element_type=F32)
        m = jnp.maximum(jnp.max(s_loc, axis=-1, keepdims=True), jnp.max(s_ctx, axis=-1, keepdims=True))
        p_loc = jnp.exp(s_loc - m)
        p_ctx = jnp.exp(s_ctx - m)
        den = jnp.sum(p_loc, axis=-1, keepdims=True) + jnp.sum(p_ctx, axis=-1, keepdims=True)
        o = (jnp.dot(p_loc.astype(BF16), vb, preferred_element_type=F32)
             + jnp.dot(p_ctx.astype(BF16), vc_ref[0, h], preferred_element_type=F32))
        o_ref[0, h] = o / den


def na_bias_table(rel_bias):
    cols = jnp.arange(GRID_W, dtype=jnp.int32)
    c0 = jnp.clip(cols - NA_WIN_W // 2, 0, GRID_W - NA_WIN_W)
    in_win = (cols[None, :] >= c0[:, None]) & (cols[None, :] < c0[:, None] + NA_WIN_W)
    col_idx = jnp.clip(cols[None, :] - cols[:, None] + (NA_WIN_W - 1), 0, 2 * NA_WIN_W - 2)
    off = jnp.arange(NA_WIN_H, dtype=jnp.int32)
    row_idx = off[None, :] - off[:, None] + (NA_WIN_H - 1)
    t = rel_bias[:, row_idx]
    t = t[..., col_idx]
    t = jnp.where(in_win[None, None, None], t, MASK_VALUE)
    return t.transpose(1, 0, 3, 2, 4).reshape(NA_WIN_H, NA_HEADS, GRID_W, NA_WIN_H * GRID_W)


def neighbourhood_attention(q, k, v, kc, vc, bias):
    B, H, T, d = q.shape
    C = kc.shape[2]
    rows = T // GRID_W
    assert rows >= NA_WIN_H
    return pl.pallas_call(
        functools.partial(_na_kernel, rows=rows),
        grid=(B, rows),
        in_specs=[pl.BlockSpec((1, H, GRID_W, d), lambda b, r: (b, 0, r, 0)),
                  pl.BlockSpec((1, H, T, d), lambda b, r: (b, 0, 0, 0)),
                  pl.BlockSpec((1, H, T, d), lambda b, r: (b, 0, 0, 0)),
                  pl.BlockSpec((1, H, C, d), lambda b, r: (b, 0, 0, 0)),
                  pl.BlockSpec((1, H, C, d), lambda b, r: (b, 0, 0, 0)),
                  pl.BlockSpec((1, H, GRID_W, NA_WIN_H * GRID_W),
                               lambda b, r: (_na_row_offset(r, rows), 0, 0, 0))],
        out_specs=pl.BlockSpec((1, H, GRID_W, d), lambda b, r: (b, 0, r, 0)),
        out_shape=jax.ShapeDtypeStruct((B, H, T, d), F32),
        compiler_params=_params("parallel", "arbitrary"),
        name="neighbourhood_attention",
    )(q, k, v, kc, vc, bias)


def _qk_prep_kernel(x_ref, g_ref, cos_ref, sin_ref, o_ref, *, rope):
    x = x_ref[0, 0]
    y = x * lax.rsqrt(jnp.mean(x * x, axis=-1, keepdims=True) + EPS) * g_ref[...]
    if rope:
        half = HEAD_DIM // 2
        x1, x2 = y[:, :half], y[:, half:]
        cos, sin = cos_ref[...], sin_ref[...]
        y = jnp.concatenate([x1 * cos - x2 * sin, x1 * sin + x2 * cos], axis=-1)
    o_ref[0, 0] = y.astype(o_ref.dtype)


def qk_prep(x, g, cos, sin, rope, out_dtype):
    B, H, L, d = x.shape
    tl = min(512, L)
    return pl.pallas_call(
        functools.partial(_qk_prep_kernel, rope=rope),
        grid=(B, H, L // tl),
        in_specs=[pl.BlockSpec((1, 1, tl, d), lambda b, h, i: (b, h, i, 0)),
                  pl.BlockSpec((1, d), lambda b, h, i: (0, 0)),
                  pl.BlockSpec((tl, d // 2), lambda b, h, i: (i, 0)),
                  pl.BlockSpec((tl, d // 2), lambda b, h, i: (i, 0))],
        out_specs=pl.BlockSpec((1, 1, tl, d), lambda b, h, i: (b, h, i, 0)),
        out_shape=jax.ShapeDtypeStruct((B, H, L, d), out_dtype),
        compiler_params=_params("parallel", "parallel", "parallel"),
        name="qk_prep",
    )(x, g.reshape(1, d), cos, sin)


def _attn_kernel(q_ref, k_ref, v_ref, o_ref, *, G, bq):
    q = (q_ref[0, 0].reshape(G * bq, HEAD_DIM) * (HEAD_DIM ** -0.5)).astype(BF16)
    s = lax.dot_general(q, k_ref[0, 0], (((1,), (1,)), ((), ())), preferred_element_type=F32)
    m = jnp.max(s, axis=-1, keepdims=True)
    p = jnp.exp(s - m)
    den = jnp.sum(p, axis=-1, keepdims=True)
    o = jnp.dot(p.astype(BF16), v_ref[0, 0], preferred_element_type=F32) / den
    o_ref[0, 0] = o.reshape(G, bq, HEAD_DIM)


def grouped_attention(q, k, v):
    B, Hk, G, L, d = q.shape
    S = k.shape[2]
    bq = min(128, L)
    return pl.pallas_call(
        functools.partial(_attn_kernel, G=G, bq=bq),
        grid=(B, Hk, L // bq),
        in_specs=[pl.BlockSpec((1, 1, G, bq, d), lambda b, h, i: (b, h, 0, i, 0)),
                  pl.BlockSpec((1, 1, S, d), lambda b, h, i: (b, h, 0, 0)),
                  pl.BlockSpec((1, 1, S, d), lambda b, h, i: (b, h, 0, 0))],
        out_specs=pl.BlockSpec((1, 1, G, bq, d), lambda b, h, i: (b, h, 0, i, 0)),
        out_shape=jax.ShapeDtypeStruct((B, Hk, G, L, d), F32),
        compiler_params=_params("parallel", "parallel", "arbitrary"),
        name="grouped_attention",
    )(q, k, v)


def _proj_residual_kernel(m_ref, w_ref, x_ref, gate_ref, o_ref):
    y = jnp.dot(m_ref[0].astype(BF16), w_ref[...], preferred_element_type=F32)
    o_ref[0] = x_ref[0] + gate_ref[0] * y


def proj_residual(mix, w, x, gate):
    B, L, K = mix.shape
    D = w.shape[1]
    tm = min(512, L)
    gate_map = (lambda b, i: (b, 0, 0)) if gate.shape[0] == B else (lambda b, i: (0, 0, 0))
    return pl.pallas_call(
        _proj_residual_kernel,
        grid=(B, L // tm),
        in_specs=[pl.BlockSpec((1, tm, K), lambda b, i: (b, i, 0)),
                  pl.BlockSpec((K, D), lambda b, i: (0, 0)),
                  pl.BlockSpec((1, tm, D), lambda b, i: (b, i, 0)),
                  pl.BlockSpec((1, 1, D), gate_map)],
        out_specs=pl.BlockSpec((1, tm, D), lambda b, i: (b, i, 0)),
        out_shape=jax.ShapeDtypeStruct((B, L, D), F32),
        compiler_params=_params("parallel", "parallel"),
        name="proj_residual",
    )(mix, w, x, gate)


def _top_rows(s, k, order=None, payload=None):
    if order is None:
        order = lax.broadcasted_iota(jnp.int32, s.shape, 0)
    after_all = jnp.iinfo(jnp.int32).max
    vals, picks = [], []
    for _ in range(k):
        m = jnp.max(s, axis=0, keepdims=True)
        first = jnp.min(jnp.where(s == m, order, after_all), axis=0, keepdims=True)
        hit = order == first
        vals.append(m)
        if payload is None:
            picks.append(first)
        else:
            picks.append(jnp.max(jnp.where(hit, payload, -1), axis=0, keepdims=True))
        s = jnp.where(hit, -jnp.inf, s)
    return jnp.concatenate(vals, axis=0), jnp.concatenate(picks, axis=0)


def _pair_candidates(s0, i0, s1, i1):
    K = PEER_TOPK
    t = s0.shape[1]
    sub = lax.broadcasted_iota(jnp.int32, (SUBLANES, t), 0)
    scores, order, ids = [], [], []

    def add(a_rows, b_rows, a_of_row, b_of_row):
        sa, ia = a_rows
        sb, ib = b_rows
        ok = (a_of_row + 1) * (b_of_row + 1) <= K
        scores.append(jnp.where(ok, sa + sb, -jnp.inf))
        order.append(a_of_row * K + b_of_row)
        ids.append(ia * PEER_N_KEYS + ib)

    row = lambda x, r: (x[0][r:r + 1], x[1][r:r + 1])
    rows = lambda x, r: (x[0][r:r + SUBLANES], x[1][r:r + SUBLANES])
    A, Bv = (s0, i0), (s1, i1)
    add(row(A, 0), rows(Bv, 0), jnp.zeros_like(sub), sub)
    add(row(A, 0), rows(Bv, SUBLANES), jnp.zeros_like(sub), sub + SUBLANES)
    for a in range(1, 4):
        add(row(A, a), rows(Bv, 0), jnp.full_like(sub, a), sub)
    for b in range(3):
        dup = sub < 4
        sa, ia = rows(A, 0)
        add((jnp.where(dup, -jnp.inf, sa), ia), row(Bv, b), sub, jnp.full_like(sub, b))
    add(rows(A, SUBLANES), row(Bv, 0), sub + SUBLANES, jnp.zeros_like(sub))
    cat = lambda xs: jnp.concatenate(xs, axis=0)
    return cat(scores), cat(order), cat(ids)


def _peer_topk_kernel(q_ref, keys_ref, idx_ref, g_ref):
    K = PEER_TOPK
    assert K == 2 * SUBLANES
    nt = (((1,), (1,)), ((), ()))
    for h in range(PEER_HEADS):
        tops = []
        for p in range(2):
            col = (2 * h + p) * PEER_D_KEY
            qhp = q_ref[:, col:col + PEER_D_KEY].astype(BF16)
            s = lax.dot_general(keys_ref[h, p], qhp, nt, preferred_element_type=F32)
            tops.append(_top_rows(s, K))
        (s0, i0), (s1, i1) = tops
        cand_s, cand_order, cand_i = _pair_candidates(s0, i0, s1, i1)
        best_s, best_i = _top_rows(cand_s, K, order=cand_order, payload=cand_i)
        e = jnp.exp(best_s - jnp.max(best_s, axis=0, keepdims=True))
        g_ref[h] = e / jnp.sum(e, axis=0, keepdims=True)
        idx_ref[h] = best_i


def peer_topk(q, keys):
    N = q.shape[0]
    tm = LANES
    return pl.pallas_call(
        _peer_topk_kernel,
        grid=(N // tm,),
        in_specs=[pl.BlockSpec((tm, q.shape[1]), lambda i: (i, 0)),
                  pl.BlockSpec(keys.shape, lambda i: (0, 0, 0, 0))],
        out_specs=[pl.BlockSpec((PEER_HEADS, PEER_TOPK, tm), lambda i: (0, 0, i)),
                   pl.BlockSpec((PEER_HEADS, PEER_TOPK, tm), lambda i: (0, 0, i))],
        out_shape=[jax.ShapeDtypeStruct((PEER_HEADS, PEER_TOPK, N), jnp.int32),
                   jax.ShapeDtypeStruct((PEER_HEADS, PEER_TOPK, N), F32)],
        compiler_params=_params("parallel"),
        name="peer_topk",
    )(q, keys)


_HALF = SUBLANES // 2
_PEER_TOKENS = 64
_GROUPS = PEER_PAIRS // SUBLANES


def pack_expert_table(w):
    E, D = w.shape
    assert D == SUBLANES * LANES
    bits = lax.bitcast_convert_type(w.astype(BF16), jnp.uint16).astype(jnp.uint32).reshape(E, _HALF, 2, LANES)
    return (bits[:, :, 0] | (bits[:, :, 1] << 16)).reshape(E * _HALF, LANES)


def pack_expert_offsets(idx):
    assert PEER_N_KEYS * PEER_N_KEYS * _HALF <= 1 << 16
    off = idx * _HALF
    return off[:, 0::2] | (off[:, 1::2] << 16)


def _expert_rows(tab_ref, word):
    starts = (word & 0xFFFF, lax.shift_right_logical(word, 16))
    return [pltpu.bitcast(tab_ref[pl.ds(pl.multiple_of(s, _HALF), _HALF), :], BF16).astype(F32) for s in starts]


def _merge_pair(a, b, shift, first):
    if shift == _HALF:
        return jnp.where(first, a, b) + pltpu.roll(jnp.where(first, b, a), shift, axis=0)
    bs = pltpu.roll(b, shift, axis=0)
    return jnp.where(first, a, bs) + pltpu.roll(jnp.where(first, bs, a), SUBLANES - shift, axis=0)


_MERGE_ORDER = (0, 4, 2, 6, 1, 5, 3, 7)


def _merge8(ps):
    sub = lax.broadcasted_iota(jnp.int32, (SUBLANES, LANES), 0)
    shift = _HALF
    while len(ps) > 1:
        first = (sub % (2 * shift)) < shift
        ps = [_merge_pair(ps[2 * i], ps[2 * i + 1], shift, first) for i in range(len(ps) // 2)]
        shift //= 2
    return ps[0]


def _peer_act_kernel(idx_ref, h_ref, g_ref, tab_ref, o_ref, part_ref):
    tn = h_ref.shape[0]

    def products(t, slot):
        x = h_ref[t]
        for gi in range(_GROUPS):
            rows = []
            for w in range(_HALF):
                rows += _expert_rows(tab_ref, idx_ref[t, gi * _HALF + w])
            part_ref[slot, pl.ds(gi * SUBLANES, SUBLANES), :] = _merge8([rows[j] * x for j in _MERGE_ORDER])

    def reduce(t, slot):
        o_ref[pl.ds(t, 1), :] = jnp.sum(part_ref[slot].T, axis=0, keepdims=True)

    part_ref[1] = jnp.zeros((PEER_PAIRS, LANES), F32)

    def two_tokens(i, carry):
        t = 2 * i
        reduce(jnp.maximum(t - 1, 0), 1)
        products(t, 0)
        reduce(t, 0)
        products(t + 1, 1)
        return carry

    lax.fori_loop(0, tn // 2, two_tokens, 0)
    reduce(tn - 1, 1)
    o_ref[...] = g_ref[...] * jax.nn.gelu(o_ref[...], approximate=True)


def peer_act(offs, h3, g, table):
    N = offs.shape[0]
    tn = _PEER_TOKENS
    return pl.pallas_call(
        _peer_act_kernel,
        grid=(N // tn,),
        in_specs=[pl.BlockSpec((tn, PEER_PAIRS // 2), lambda i: (i, 0), memory_space=pltpu.SMEM),
                  pl.BlockSpec((tn, SUBLANES, LANES), lambda i: (i, 0, 0)),
                  pl.BlockSpec((tn, PEER_PAIRS), lambda i: (i, 0)),
                  pl.BlockSpec(memory_space=pltpu.VMEM)],
        out_specs=pl.BlockSpec((tn, PEER_PAIRS), lambda i: (i, 0)),
        out_shape=jax.ShapeDtypeStruct((N, PEER_PAIRS), F32),
        scratch_shapes=[pltpu.VMEM((2, PEER_PAIRS, LANES), F32)],
        compiler_params=_params("arbitrary"),
        name="peer_act",
    )(offs, h3, g, table)


_MIX_CHAINS = 4


def _peer_mix_kernel(idx_ref, coef_ref, tab_ref, x_ref, gate_ref, o_ref, cb_ref):
    tn = x_ref.shape[0]

    def spread(t, slot):
        row = coef_ref[pl.ds(t, 1), :]
        cb_ref[slot] = jnp.broadcast_to(row, (PEER_PAIRS, PEER_PAIRS)).T

    def mix(t, slot):
        accs = [None] * _MIX_CHAINS
        for w in range(PEER_PAIRS // 2):
            for k, row in enumerate(_expert_rows(tab_ref, idx_ref[t, w])):
                j = 2 * w + k
                term = jnp.broadcast_to(cb_ref[slot, pl.ds(j, 1), :], (SUBLANES, LANES)) * row
                a = j % _MIX_CHAINS
                accs[a] = term if accs[a] is None else accs[a] + term
        y = (accs[0] + accs[1]) + (accs[2] + accs[3])
        o_ref[t] = x_ref[t] + gate_ref[0] * y

    spread(0, 0)

    def two_tokens(i, carry):
        t = 2 * i
        spread(t + 1, 1)
        mix(t, 0)
        spread(jnp.minimum(t + 2, tn - 1), 0)
        mix(t + 1, 1)
        return carry

    lax.fori_loop(0, tn // 2, two_tokens, 0)


def peer_mix(offs, coef, table, x3, gate3, tokens_per_batch):
    N = offs.shape[0]
    tn = _PEER_TOKENS
    assert tokens_per_batch % tn == 0
    if gate3.shape[0] == 1:
        gate_map = lambda i: (0, 0, 0)
    else:
        gate_map = lambda i: ((i * tn) // tokens_per_batch, 0, 0)
    return pl.pallas_call(
        _peer_mix_kernel,
        grid=(N // tn,),
        in_specs=[pl.BlockSpec((tn, PEER_PAIRS // 2), lambda i: (i, 0), memory_space=pltpu.SMEM),
                  pl.BlockSpec((tn, PEER_PAIRS), lambda i: (i, 0)),
                  pl.BlockSpec(memory_space=pltpu.VMEM),
                  pl.BlockSpec((tn, SUBLANES, LANES), lambda i: (i, 0, 0)),
                  pl.BlockSpec((1, SUBLANES, LANES), gate_map)],
        out_specs=pl.BlockSpec((tn, SUBLANES, LANES), lambda i: (i, 0, 0)),
        out_shape=jax.ShapeDtypeStruct((N, SUBLANES, LANES), F32),
        scratch_shapes=[pltpu.VMEM((2, PEER_PAIRS, LANES), F32)],
        compiler_params=_params("arbitrary"),
        name="peer_mix",
    )(offs, coef, table, x3, gate3)


def _rmsnorm_kernel(x_ref, g_ref, o_ref):
    x = x_ref[0]
    o_ref[0] = x * lax.rsqrt(jnp.mean(x * x, axis=-1, keepdims=True) + EPS) * g_ref[...]


def rmsnorm(x, g):
    B, L, D = x.shape
    tm = min(512, L)
    return pl.pallas_call(
        _rmsnorm_kernel,
        grid=(B, L // tm),
        in_specs=[pl.BlockSpec((1, tm, D), lambda b, i: (b, i, 0)),
                  pl.BlockSpec((1, D), lambda b, i: (0, 0))],
        out_specs=pl.BlockSpec((1, tm, D), lambda b, i: (b, i, 0)),
        out_shape=jax.ShapeDtypeStruct((B, L, D), F32),
        compiler_params=_params("parallel", "parallel"),
        name="final_rmsnorm",
    )(x, g.reshape(1, D))


def _split_heads(z, n, dtype=None):
    B, L, _ = z.shape
    z = z.reshape(B, L, n, HEAD_DIM).transpose(0, 2, 1, 3)
    return z if dtype is None else z.astype(dtype)


def _merge_heads(o):
    B, H, L, d = o.shape
    return o.transpose(0, 2, 1, 3).reshape(B, L, H * d)


def _rope_tables(T):
    t = jnp.arange(T, dtype=jnp.int32)
    row = (t // GRID_W).astype(F32)
    col = (t % GRID_W).astype(F32)
    n_freq = HEAD_DIM // 4
    inv_freq = ROPE_THETA ** (-jnp.arange(n_freq, dtype=F32) / n_freq)
    ang = jnp.concatenate([row[:, None] * inv_freq, col[:, None] * inv_freq], axis=-1)
    return jnp.cos(ang), jnp.sin(ang)


def _peer_ffn_residual(x, g2, sc2, sh2, gate, w_q, keys, u_tab, v_tab):
    B, L, D = x.shape
    N = B * L
    q, h = norm_mod_proj(x, g2, sc2, sh2, w_q, (w_q.shape[1],), emit_h=True)
    idx_t, g_t = peer_topk(q.reshape(N, -1), keys)
    offs = pack_expert_offsets(idx_t.reshape(PEER_PAIRS, N).T)
    gates = g_t.reshape(PEER_PAIRS, N).T
    coef = peer_act(offs, h.reshape(N, SUBLANES, LANES), gates, u_tab)
    gate3 = gate.reshape(gate.shape[0], SUBLANES, LANES)
    out = peer_mix(offs, coef, v_tab, x.reshape(N, SUBLANES, LANES), gate3, L)
    return out.reshape(B, L, D)


def kernel(x, c, ctx, c_ctx, norm1_g, norm2_g, w_ada, b_ada, w_in, conv_w, conv_b, conv_ln_g, conv_ln_b,
           na_rel_bias, gqa_q_norm, gqa_k_norm, w_out, peer_w_q, peer_keys, peer_u, peer_v, final_norm_g):
    B, T, D = x.shape
    depth = w_in.shape[0]
    G = GQA_HEADS // GQA_KV_HEADS
    cos, sin = _rope_tables(T)
    ones_c = jnp.ones((ctx.shape[1], HEAD_DIM // 2), F32)
    ada_rows = -(-(B + 1) // SUBLANES) * SUBLANES
    ada_in = jnp.zeros((ada_rows, D), F32).at[:B].set(c).at[B].set(c_ctx)

    for l in range(depth):
        last = l == depth - 1
        mod = ada_mod(ada_in, w_ada[l], b_ada[l])
        sh1, sc1, g1, sh2, sc2, g2 = [m.reshape(B, 1, D) for m in jnp.split(mod[:B], 6, axis=-1)]
        csh1, csc1, cg1, csh2, csc2, cg2 = [m.reshape(1, 1, D) for m in jnp.split(mod[B:B + 1], 6, axis=-1)]
        w_in_b = w_in[l].astype(BF16)
        w_out_b = w_out[l].astype(BF16)
        w_q_b = peer_w_q[l].astype(BF16)
        keys_b = peer_keys[l].astype(BF16)
        u_tab = pack_expert_table(peer_u[l])
        v_tab = pack_expert_table(peer_v[l])

        za, na_q, g_q, na_k, na_v, g_k, g_v = norm_mod_proj(x, norm1_g[l], sc1, sh1, w_in_b, IN_SPLITS)
        zca, cna_q, cg_q, cna_k, cna_v, cg_k, cg_v = norm_mod_proj(ctx, norm1_g[l], csc1, csh1, w_in_b, IN_SPLITS)

        kc_na = _split_heads(cna_k, NA_HEADS, BF16)
        vc_na = _split_heads(cna_v, NA_HEADS, BF16)
        kc_g = qk_prep(_split_heads(cg_k, GQA_KV_HEADS), gqa_k_norm[l], ones_c, ones_c, False, BF16)
        vc_g = _split_heads(cg_v, GQA_KV_HEADS, BF16)

        a = conformer_conv(za, conv_w[l], conv_b[l], conv_ln_g[l], conv_ln_b[l])
        bm = neighbourhood_attention(_split_heads(na_q, NA_HEADS), _split_heads(na_k, NA_HEADS, BF16),
                                     _split_heads(na_v, NA_HEADS, BF16), kc_na, vc_na,
                                     na_bias_table(na_rel_bias[l]))
        qh = qk_prep(_split_heads(g_q, GQA_HEADS), gqa_q_norm[l], cos, sin, True, F32)
        kh = qk_prep(_split_heads(g_k, GQA_KV_HEADS), gqa_k_norm[l], cos, sin, True, BF16)
        k_all = jnp.concatenate([kh, kc_g], axis=2)
        v_all = jnp.concatenate([_split_heads(g_v, GQA_KV_HEADS, BF16), vc_g], axis=2)
        gm = grouped_attention(qh.reshape(B, GQA_KV_HEADS, G, T, HEAD_DIM), k_all, v_all)
        mix = jnp.concatenate([a, _merge_heads(bm), _merge_heads(gm.reshape(B, GQA_HEADS, T, HEAD_DIM))], axis=-1)
        x = proj_residual(mix, w_out_b, x, g1)

        x = _peer_ffn_residual(x, norm2_g[l], sc2, sh2, g2, w_q_b, keys_b, u_tab, v_tab)

        if not last:
            C = ctx.shape[1]
            ac = conformer_conv(zca, conv_w[l], conv_b[l], conv_ln_g[l], conv_ln_b[l])
            bc = grouped_attention(_split_heads(cna_q, NA_HEADS).reshape(B, NA_HEADS, 1, C, HEAD_DIM), kc_na, vc_na)
            qc_g = qk_prep(_split_heads(cg_q, GQA_HEADS), gqa_q_norm[l], ones_c, ones_c, False, F32)
            gc = grouped_attention(qc_g.reshape(B, GQA_KV_HEADS, G, C, HEAD_DIM), kc_g, vc_g)
            mixc = jnp.concatenate([ac, _merge_heads(bc.reshape(B, NA_HEADS, C, HEAD_DIM)),
                                    _merge_heads(gc.reshape(B, GQA_HEADS, C, HEAD_DIM))], axis=-1)
            ctx = proj_residual(mixc, w_out_b, ctx, cg1)
            ctx = _peer_ffn_residual(ctx, norm2_g[l], csc2, csh2, cg2, w_q_b, keys_b, u_tab, v_tab)
    return rmsnorm(x, final_norm_g)
```

```python
import functools

import jax
import jax.numpy as jnp
from jax import lax
from jax.experimental import pallas as pl
from jax.experimental.pallas import tpu as pltpu

F32 = jnp.float32
BF16 = jnp.bfloat16

GRID_W = 64
HEAD_DIM = 64
CONV_CH = 256
CONV_WIDTH = 31
NA_HEADS = 6
NA_WIN_H = 8
NA_WIN_W = 16
GQA_HEADS = 6
GQA_KV_HEADS = 2
ROPE_THETA = 10000.0
PEER_HEADS = 8
PEER_N_KEYS = 128
PEER_D_KEY = 128
PEER_TOPK = 16
EPS = 1e-6

NA_W = NA_HEADS * HEAD_DIM
GQA_W = GQA_HEADS * HEAD_DIM
GQA_KV_W = GQA_KV_HEADS * HEAD_DIM
IN_SPLITS = (2 * CONV_CH, NA_W, GQA_W, NA_W, NA_W, GQA_KV_W, GQA_KV_W)
PEER_PAIRS = PEER_HEADS * PEER_TOPK

LANES = 128
SUBLANES = 8
_HALF = SUBLANES // 2
assert PEER_N_KEYS * PEER_N_KEYS * _HALF <= 1 << 16
VMEM_LIMIT = 48 * 1024 * 1024
MASK_VALUE = -1e30


def _params(*sem):
    return pltpu.CompilerParams(dimension_semantics=sem, vmem_limit_bytes=VMEM_LIMIT)


def _ada_kernel(a_ref, w_ref, b_ref, o_ref):
    a = a_ref[...]
    s = a * jax.nn.sigmoid(a)
    o_ref[...] = jnp.dot(s.astype(BF16), w_ref[...].astype(BF16), preferred_element_type=F32) + b_ref[...]


def ada_mod(a, w, b):
    R, D = a.shape
    N = w.shape[1]
    tn = 1024
    return pl.pallas_call(
        _ada_kernel,
        grid=(N // tn,),
        in_specs=[pl.BlockSpec((R, D), lambda j: (0, 0)),
                  pl.BlockSpec((D, tn), lambda j: (0, j)),
                  pl.BlockSpec((1, tn), lambda j: (0, j))],
        out_specs=pl.BlockSpec((R, tn), lambda j: (0, j)),
        out_shape=jax.ShapeDtypeStruct((R, N), F32),
        compiler_params=_params("parallel"),
        name="ada_mod",
    )(a, w, b.reshape(1, N))


def _norm_mod_proj_kernel(x_ref, g_ref, sc_ref, sh_ref, w_ref, *out_refs, splits, emit_h):
    x = x_ref[0]
    y = x * lax.rsqrt(jnp.mean(x * x, axis=-1, keepdims=True) + EPS) * g_ref[...]
    h = y * (1.0 + sc_ref[0]) + sh_ref[0]
    z = jnp.dot(h.astype(BF16), w_ref[...], preferred_element_type=F32)
    off = 0
    for o_ref, width in zip(out_refs, splits):
        o_ref[0] = z[:, off:off + width]
        off += width
    if emit_h:
        out_refs[len(splits)][0] = h


def norm_mod_proj(x, g, sc, sh, w, splits, emit_h=False):
    B, L, D = x.shape
    N = w.shape[1]
    assert sum(splits) == N
    tm = min(256, L)
    per_batch = sc.shape[0] == B
    mod_map = (lambda b, i: (b, 0, 0)) if per_batch else (lambda b, i: (0, 0, 0))
    out_shape = [jax.ShapeDtypeStruct((B, L, s), F32) for s in splits]
    out_specs = [pl.BlockSpec((1, tm, s), lambda b, i: (b, i, 0)) for s in splits]
    if emit_h:
        out_shape.append(jax.ShapeDtypeStruct((B, L, D), F32))
        out_specs.append(pl.BlockSpec((1, tm, D), lambda b, i: (b, i, 0)))
    return pl.pallas_call(
        functools.partial(_norm_mod_proj_kernel, splits=splits, emit_h=emit_h),
        grid=(B, L // tm),
        in_specs=[pl.BlockSpec((1, tm, D), lambda b, i: (b, i, 0)),
                  pl.BlockSpec((1, D), lambda b, i: (0, 0)),
                  pl.BlockSpec((1, 1, D), mod_map),
                  pl.BlockSpec((1, 1, D), mod_map),
                  pl.BlockSpec((D, N), lambda b, i: (0, 0))],
        out_specs=out_specs,
        out_shape=out_shape,
        compiler_params=_params("parallel", "parallel"),
        name="norm_mod_proj",
    )(x, g.reshape(1, D), sc, sh, w)


_CONV_PAD = 16


def _conv_kernel(za_ref, w_ref, b_ref, lg_ref, lb_ref, o_ref, upad_ref, *, L, tc):
    c = pl.program_id(1)

    @pl.when(c == 0)
    def _():
        val = za_ref[0, :, :CONV_CH]
        gate = za_ref[0, :, CONV_CH:]
        zeros = jnp.zeros((_CONV_PAD, CONV_CH), F32)
        upad_ref[pl.ds(0, _CONV_PAD), :] = zeros
        upad_ref[pl.ds(_CONV_PAD + L, _CONV_PAD), :] = zeros
        upad_ref[pl.ds(_CONV_PAD, L), :] = val * jax.nn.sigmoid(gate)

    start = pl.multiple_of(c * tc, SUBLANES)
    win = upad_ref[pl.ds(start, tc + 2 * _CONV_PAD), :]
    acc = jnp.zeros((tc, CONV_CH), F32)
    first = _CONV_PAD - CONV_WIDTH // 2
    for j in range(CONV_WIDTH):
        acc = acc + win[first + j:first + j + tc, :] * w_ref[j:j + 1, :]
    y = acc + b_ref[...]
    mu = jnp.mean(y, axis=-1, keepdims=True)
    d = y - mu
    var = jnp.mean(d * d, axis=-1, keepdims=True)
    yn = d * lax.rsqrt(var + EPS) * lg_ref[...] + lb_ref[...]
    o_ref[0] = yn * jax.nn.sigmoid(yn)


def conformer_conv(za, w_dw, b_dw, ln_g, ln_b):
    B, L, _ = za.shape
    tc = min(256, L)
    vec = lambda v: v.reshape(1, CONV_CH)
    return pl.pallas_call(
        functools.partial(_conv_kernel, L=L, tc=tc),
        grid=(B, L // tc),
        in_specs=[pl.BlockSpec((1, L, 2 * CONV_CH), lambda b, c: (b, 0, 0)),
                  pl.BlockSpec((CONV_WIDTH, CONV_CH), lambda b, c: (0, 0)),
                  pl.BlockSpec((1, CONV_CH), lambda b, c: (0, 0)),
                  pl.BlockSpec((1, CONV_CH), lambda b, c: (0, 0)),
                  pl.BlockSpec((1, CONV_CH), lambda b, c: (0, 0))],
        out_specs=pl.BlockSpec((1, tc, CONV_CH), lambda b, c: (b, c, 0)),
        out_shape=jax.ShapeDtypeStruct((B, L, CONV_CH), F32),
        scratch_shapes=[pltpu.VMEM((L + 2 * _CONV_PAD, CONV_CH), F32)],
        compiler_params=_params("parallel", "arbitrary"),
        name="conformer_conv",
    )(za, w_dw, vec(b_dw), vec(ln_g), vec(ln_b))


def _na_row_offset(r, rows):
    return r - jnp.clip(r - NA_WIN_H // 2, 0, rows - NA_WIN_H)


def _na_kernel(q_ref, k_ref, v_ref, kc_ref, vc_ref, bias_ref, o_ref, *, rows):
    r = pl.program_id(1)
    r0 = r - _na_row_offset(r, rows)
    band = pl.ds(pl.multiple_of(r0 * GRID_W, GRID_W), NA_WIN_H * GRID_W)
    nt = (((1,), (1,)), ((), ()))
    for h in range(NA_HEADS):
        q = (q_ref[0, h] * (HEAD_DIM ** -0.5)).astype(BF16)
        kb = k_ref[0, h, band, :]
        vb = v_ref[0, h, band, :]
        s_loc = lax.dot_general(q, kb, nt, preferred_element_type=F32) + bias_ref[0, h]
        s_ctx = lax.dot_general(q, kc_ref[0, h], nt, preferred_element_type=F32)
        m = jnp.maximum(jnp.max(s_loc, axis=-1, keepdims=True), jnp.max(s_ctx, axis=-1, keepdims=True))
        p_loc = jnp.exp(s_loc - m)
        p_ctx = jnp.exp(s_ctx - m)
        den = jnp.sum(p_loc, axis=-1, keepdims=True) + jnp.sum(p_ctx, axis=-1, keepdims=True)
        o = (jnp.dot(p_loc.astype(BF16), vb, preferred_element_type=F32)
             + jnp.dot(p_ctx.astype(BF16), vc_ref[0, h], preferred_element_type=F32))
        o_ref[0, h] = o / den


def na_bias_table(rel_bias):
    cols = jnp.arange(GRID_W, dtype=jnp.int32)
    c0 = jnp.clip(cols - NA_WIN_W // 2, 0, GRID_W - NA_WIN_W)
    in_win = (cols[None, :] >= c0[:, None]) & (cols[None, :] < c0[:, None] + NA_WIN_W)
    col_idx = jnp.clip(cols[None, :] - cols[:, None] + (NA_WIN_W - 1), 0, 2 * NA_WIN_W - 2)
    off = jnp.arange(NA_WIN_H, dtype=jnp.int32)
    row_idx = off[None, :] - off[:, None] + (NA_WIN_H - 1)
    t = rel_bias[:, row_idx]
    t = t[..., col_idx]
    t = jnp.where(in_win[None, None, None], t, MASK_VALUE)
    return t.transpose(1, 0, 3, 2, 4).reshape(NA_WIN_H, NA_HEADS, GRID_W, NA_WIN_H * GRID_W)


def neighbourhood_attention(q, k, v, kc, vc, bias):
    B, H, T, d = q.shape
    C = kc.shape[2]
    rows = T // GRID_W
    assert rows >= NA_WIN_H
    return pl.pallas_call(
        functools.partial(_na_kernel, rows=rows),
        grid=(B, rows),
        in_specs=[pl.BlockSpec((1, H, GRID_W, d), lambda b, r: (b, 0, r, 0)),
                  pl.BlockSpec((1, H, T, d), lambda b, r: (b, 0, 0, 0)),
                  pl.BlockSpec((1, H, T, d), lambda b, r: (b, 0, 0, 0)),
                  pl.BlockSpec((1, H, C, d), lambda b, r: (b, 0, 0, 0)),
                  pl.BlockSpec((1, H, C, d), lambda b, r: (b, 0, 0, 0)),
                  pl.BlockSpec((1, H, GRID_W, NA_WIN_H * GRID_W),
                               lambda b, r: (_na_row_offset(r, rows), 0, 0, 0))],
        out_specs=pl.BlockSpec((1, H, GRID_W, d), lambda b, r: (b, 0, r, 0)),
        out_shape=jax.ShapeDtypeStruct((B, H, T, d), F32),
        compiler_params=_params("parallel", "arbitrary"),
        name="neighbourhood_attention",
    )(q, k, v, kc, vc, bias)


def _qk_prep_kernel(x_ref, g_ref, cos_ref, sin_ref, o_ref, *, rope):
    x = x_ref[0, 0]
    y = x * lax.rsqrt(jnp.mean(x * x, axis=-1, keepdims=True) + EPS) * g_ref[...]
    if rope:
        half = HEAD_DIM // 2
        x1, x2 = y[:, :half], y[:, half:]
        cos, sin = cos_ref[...], sin_ref[...]
        y = jnp.concatenate([x1 * cos - x2 * sin, x1 * sin + x2 * cos], axis=-1)
    o_ref[0, 0] = y.astype(o_ref.dtype)


def qk_prep(x, g, cos, sin, rope, out_dtype):
    B, H, L, d = x.shape
    tl = min(512, L)
    return pl.pallas_call(
        functools.partial(_qk_prep_kernel, rope=rope),
        grid=(B, H, L // tl),
        in_specs=[pl.BlockSpec((1, 1, tl, d), lambda b, h, i: (b, h, i, 0)),
                  pl.BlockSpec((1, d), lambda b, h, i: (0, 0)),
                  pl.BlockSpec((tl, d // 2), lambda b, h, i: (i, 0)),
                  pl.BlockSpec((tl, d // 2), lambda b, h, i: (i, 0))],
        out_specs=pl.BlockSpec((1, 1, tl, d), lambda b, h, i: (b, h, i, 0)),
        out_shape=jax.ShapeDtypeStruct((B, H, L, d), out_dtype),
        compiler_params=_params("parallel", "parallel", "parallel"),
        name="qk_prep",
    )(x, g.reshape(1, d), cos, sin)


def _attn_kernel(q_ref, k_ref, v_ref, o_ref, *, G, bq):
    q = (q_ref[0, 0].reshape(G * bq, HEAD_DIM) * (HEAD_DIM ** -0.5)).astype(BF16)
    s = lax.dot_general(q, k_ref[0, 0], (((1,), (1,)), ((), ())), preferred_element_type=F32)
    m = jnp.max(s, axis=-1, keepdims=True)
    p = jnp.exp(s - m)
    den = jnp.sum(p, axis=-1, keepdims=True)
    o = jnp.dot(p.astype(BF16), v_ref[0, 0], preferred_element_type=F32) / den
    o_ref[0, 0] = o.reshape(G, bq, HEAD_DIM)


def grouped_attention(q, k, v):
    B, Hk, G, L, d = q.shape
    S = k.shape[2]
    bq = min(128, L)
    return pl.pallas_call(
        functools.partial(_attn_kernel, G=G, bq=bq),
        grid=(B, Hk, L // bq),
        in_specs=[pl.BlockSpec((1, 1, G, bq, d), lambda b, h, i: (b, h, 0, i, 0)),
                  pl.BlockSpec((1, 1, S, d), lambda b, h, i: (b, h, 0, 0)),
                  pl.BlockSpec((1, 1, S, d), lambda b, h, i: (b, h, 0, 0))],
        out_specs=pl.BlockSpec((1, 1, G, bq, d), lambda b, h, i: (b, h, 0, i, 0)),
        out_shape=jax.ShapeDtypeStruct((B, Hk, G, L, d), F32),
        compiler_params=_params("parallel", "parallel", "arbitrary"),
        name="grouped_attention",
    )(q, k, v)


def _proj_residual_kernel(m_ref, w_ref, x_ref, gate_ref, o_ref):
    y = jnp.dot(m_ref[0].astype(BF16), w_ref[...], preferred_element_type=F32)
    o_ref[0] = x_ref[0] + gate_ref[0] * y


def proj_residual(mix, w, x, gate):
    B, L, K = mix.shape
    D = w.shape[1]
    tm = min(512, L)
    gate_map = (lambda b, i: (b, 0, 0)) if gate.shape[0] == B else (lambda b, i: (0, 0, 0))
    return pl.pallas_call(
        _proj_residual_kernel,
        grid=(B, L // tm),
        in_specs=[pl.BlockSpec((1, tm, K), lambda b, i: (b, i, 0)),
                  pl.BlockSpec((K, D), lambda b, i: (0, 0)),
                  pl.BlockSpec((1, tm, D), lambda b, i: (b, i, 0)),
                  pl.BlockSpec((1, 1, D), gate_map)],
        out_specs=pl.BlockSpec((1, tm, D), lambda b, i: (b, i, 0)),
        out_shape=jax.ShapeDtypeStruct((B, L, D), F32),
        compiler_params=_params("parallel", "parallel"),
        name="proj_residual",
    )(mix, w, x, gate)


def _top_rows(s, k, order=None, payload=None):
    if order is None:
        order = lax.broadcasted_iota(jnp.int32, s.shape, 0)
    after_all = jnp.iinfo(jnp.int32).max
    vals, picks = [], []
    for _ in range(k):
        m = jnp.max(s, axis=0, keepdims=True)
        first = jnp.min(jnp.where(s == m, order, after_all), axis=0, keepdims=True)
        hit = order == first
        vals.append(m)
        if payload is None:
            picks.append(first)
        else:
            picks.append(jnp.max(jnp.where(hit, payload, -1), axis=0, keepdims=True))
        s = jnp.where(hit, -jnp.inf, s)
    return jnp.concatenate(vals, axis=0), jnp.concatenate(picks, axis=0)


def _pair_candidates(s0, i0, s1, i1):
    K = PEER_TOPK
    t = s0.shape[1]
    sub = lax.broadcasted_iota(jnp.int32, (SUBLANES, t), 0)
    scores, order, ids = [], [], []

    def add(a_rows, b_rows, a_of_row, b_of_row):
        sa, ia = a_rows
        sb, ib = b_rows
        ok = (a_of_row + 1) * (b_of_row + 1) <= K
        scores.append(jnp.where(ok, sa + sb, -jnp.inf))
        order.append(a_of_row * K + b_of_row)
        ids.append(ia * PEER_N_KEYS + ib)

    row = lambda x, r: (x[0][r:r + 1], x[1][r:r + 1])
    rows = lambda x, r: (x[0][r:r + SUBLANES], x[1][r:r + SUBLANES])
    A, Bv = (s0, i0), (s1, i1)
    add(row(A, 0), rows(Bv, 0), jnp.zeros_like(sub), sub)
    add(row(A, 0), rows(Bv, SUBLANES), jnp.zeros_like(sub), sub + SUBLANES)
    for a in range(1, 4):
        add(row(A, a), rows(Bv, 0), jnp.full_like(sub, a), sub)
    for b in range(3):
        dup = sub < 4
        sa, ia = rows(A, 0)
        add((jnp.where(dup, -jnp.inf, sa), ia), row(Bv, b), sub, jnp.full_like(sub, b))
    add(rows(A, SUBLANES), row(Bv, 0), sub + SUBLANES, jnp.zeros_like(sub))
    cat = lambda xs: jnp.concatenate(xs, axis=0)
    return cat(scores), cat(order), cat(ids)


def _peer_topk_kernel(q_ref, keys_ref, offs_ref, g_ref):
    K = PEER_TOPK
    assert K == 2 * SUBLANES
    nt = (((1,), (1,)), ((), ()))
    for h in range(PEER_HEADS):
        tops = []
        for p in range(2):
            col = (2 * h + p) * PEER_D_KEY
            qhp = q_ref[:, col:col + PEER_D_KEY].astype(BF16)
            s = lax.dot_general(keys_ref[h, p], qhp, nt, preferred_element_type=F32)
            tops.append(_top_rows(s, K))
        (s0, i0), (s1, i1) = tops
        cand_s, cand_order, cand_i = _pair_candidates(s0, i0, s1, i1)
        best_s, best_i = _top_rows(cand_s, K, order=cand_order, payload=cand_i)
        e = jnp.exp(best_s - jnp.max(best_s, axis=0, keepdims=True))
        g_ref[h] = e / jnp.sum(e, axis=0, keepdims=True)
        off = best_i * _HALF
        lo = jnp.concatenate([off[0:_HALF], off[SUBLANES:SUBLANES + _HALF]], axis=0)
        hi = jnp.concatenate([off[_HALF:SUBLANES], off[SUBLANES + _HALF:K]], axis=0)
        offs_ref[h] = lo | (hi << 16)


def peer_topk(q, keys):
    N = q.shape[0]
    tm = LANES
    return pl.pallas_call(
        _peer_topk_kernel,
        grid=(N // tm,),
        in_specs=[pl.BlockSpec((tm, q.shape[1]), lambda i: (i, 0)),
                  pl.BlockSpec(keys.shape, lambda i: (0, 0, 0, 0))],
        out_specs=[pl.BlockSpec((PEER_HEADS, PEER_TOPK // 2, tm), lambda i: (0, 0, i)),
                   pl.BlockSpec((PEER_HEADS, PEER_TOPK, tm), lambda i: (0, 0, i))],
        out_shape=[jax.ShapeDtypeStruct((PEER_HEADS, PEER_TOPK // 2, N), jnp.int32),
                   jax.ShapeDtypeStruct((PEER_HEADS, PEER_TOPK, N), F32)],
        compiler_params=_params("parallel"),
        name="peer_topk",
    )(q, keys)


_PEER_TOKENS = 64
_GROUPS = PEER_PAIRS // SUBLANES


def pack_expert_table(w):
    E, D = w.shape
    assert D == SUBLANES * LANES
    bits = lax.bitcast_convert_type(w.astype(BF16), jnp.uint16).astype(jnp.uint32).reshape(E, _HALF, 2, LANES)
    return (bits[:, :, 0] | (bits[:, :, 1] << 16)).reshape(E * _HALF, LANES)


def _pair_position(word, half):
    return SUBLANES * (word // _HALF) + _HALF * half + word % _HALF


def _expert_rows(tab_ref, word):
    starts = (word & 0xFFFF, lax.shift_right_logical(word, 16))
    return [pltpu.bitcast(tab_ref[pl.ds(pl.multiple_of(s, _HALF), _HALF), :], BF16).astype(F32) for s in starts]


def _merge_pair(a, b, shift, first):
    if shift == _HALF:
        return jnp.where(first, a, b) + pltpu.roll(jnp.where(first, b, a), shift, axis=0)
    bs = pltpu.roll(b, shift, axis=0)
    return jnp.where(first, a, bs) + pltpu.roll(jnp.where(first, bs, a), SUBLANES - shift, axis=0)


_MERGE_ORDER = (0, 4, 2, 6, 1, 5, 3, 7)


def _merge8(ps):
    sub = lax.broadcasted_iota(jnp.int32, (SUBLANES, LANES), 0)
    shift = _HALF
    while len(ps) > 1:
        first = (sub % (2 * shift)) < shift
        ps = [_merge_pair(ps[2 * i], ps[2 * i + 1], shift, first) for i in range(len(ps) // 2)]
        shift //= 2
    return ps[0]


def _peer_act_kernel(idx_ref, h_ref, g_ref, tab_ref, o_ref, part_ref):
    tn = h_ref.shape[0]

    def products(t, slot):
        x = h_ref[t]
        for gi in range(_GROUPS):
            rows = []
            for w in range(_HALF):
                rows += _expert_rows(tab_ref, idx_ref[t, gi * _HALF + w])
            prods = [rows[2 * (j % _HALF) + j // _HALF] * x for j in _MERGE_ORDER]
            part_ref[slot, pl.ds(gi * SUBLANES, SUBLANES), :] = _merge8(prods)

    def reduce(t, slot):
        o_ref[pl.ds(t, 1), :] = jnp.sum(part_ref[slot].T, axis=0, keepdims=True)

    part_ref[1] = jnp.zeros((PEER_PAIRS, LANES), F32)

    def two_tokens(i, carry):
        t = 2 * i
        reduce(jnp.maximum(t - 1, 0), 1)
        products(t, 0)
        reduce(t, 0)
        products(t + 1, 1)
        return carry

    lax.fori_loop(0, tn // 2, two_tokens, 0)
    reduce(tn - 1, 1)
    o_ref[...] = g_ref[...] * jax.nn.gelu(o_ref[...], approximate=True)


def peer_act(offs, h3, g, table):
    N = offs.shape[0]
    tn = _PEER_TOKENS
    return pl.pallas_call(
        _peer_act_kernel,
        grid=(N // tn,),
        in_specs=[pl.BlockSpec((tn, PEER_PAIRS // 2), lambda i: (i, 0), memory_space=pltpu.SMEM),
                  pl.BlockSpec((tn, SUBLANES, LANES), lambda i: (i, 0, 0)),
                  pl.BlockSpec((tn, PEER_PAIRS), lambda i: (i, 0)),
                  pl.BlockSpec(memory_space=pltpu.VMEM)],
        out_specs=pl.BlockSpec((tn, PEER_PAIRS), lambda i: (i, 0)),
        out_shape=jax.ShapeDtypeStruct((N, PEER_PAIRS), F32),
        scratch_shapes=[pltpu.VMEM((2, PEER_PAIRS, LANES), F32)],
        compiler_params=_params("arbitrary"),
        name="peer_act",
    )(offs, h3, g, table)


_MIX_CHAINS = 4


def _peer_mix_kernel(idx_ref, coef_ref, tab_ref, x_ref, gate_ref, o_ref, cb_ref):
    tn = x_ref.shape[0]

    def spread(t, slot):
        row = coef_ref[pl.ds(t, 1), :]
        cb_ref[slot] = jnp.broadcast_to(row, (PEER_PAIRS, PEER_PAIRS)).T

    def mix(t, slot):
        accs = [None] * _MIX_CHAINS
        for w in range(PEER_PAIRS // 2):
            for k, row in enumerate(_expert_rows(tab_ref, idx_ref[t, w])):
                j = _pair_position(w, k)
                term = jnp.broadcast_to(cb_ref[slot, pl.ds(j, 1), :], (SUBLANES, LANES)) * row
                a = j % _MIX_CHAINS
                accs[a] = term if accs[a] is None else accs[a] + term
        y = (accs[0] + accs[1]) + (accs[2] + accs[3])
        o_ref[t] = x_ref[t] + gate_ref[0] * y

    spread(0, 0)

    def two_tokens(i, carry):
        t = 2 * i
        spread(t + 1, 1)
        mix(t, 0)
        spread(jnp.minimum(t + 2, tn - 1), 0)
        mix(t + 1, 1)
        return carry

    lax.fori_loop(0, tn // 2, two_tokens, 0)


def peer_mix(offs, coef, table, x3, gate3, tokens_per_batch):
    N = offs.shape[0]
    tn = _PEER_TOKENS
    assert tokens_per_batch % tn == 0
    if gate3.shape[0] == 1:
        gate_map = lambda i: (0, 0, 0)
    else:
        gate_map = lambda i: ((i * tn) // tokens_per_batch, 0, 0)
    return pl.pallas_call(
        _peer_mix_kernel,
        grid=(N // tn,),
        in_specs=[pl.BlockSpec((tn, PEER_PAIRS // 2), lambda i: (i, 0), memory_space=pltpu.SMEM),
                  pl.BlockSpec((tn, PEER_PAIRS), lambda i: (i, 0)),
                  pl.BlockSpec(memory_space=pltpu.VMEM),
                  pl.BlockSpec((tn, SUBLANES, LANES), lambda i: (i, 0, 0)),
                  pl.BlockSpec((1, SUBLANES, LANES), gate_map)],
        out_specs=pl.BlockSpec((tn, SUBLANES, LANES), lambda i: (i, 0, 0)),
        out_shape=jax.ShapeDtypeStruct((N, SUBLANES, LANES), F32),
        scratch_shapes=[pltpu.VMEM((2, PEER_PAIRS, LANES), F32)],
        compiler_params=_params("arbitrary"),
        name="peer_mix",
    )(offs, coef, table, x3, gate3)


def _rmsnorm_kernel(x_ref, g_ref, o_ref):
    x = x_ref[0]
    o_ref[0] = x * lax.rsqrt(jnp.mean(x * x, axis=-1, keepdims=True) + EPS) * g_ref[...]


def rmsnorm(x, g):
    B, L, D = x.shape
    tm = min(512, L)
    return pl.pallas_call(
        _rmsnorm_kernel,
        grid=(B, L // tm),
        in_specs=[pl.BlockSpec((1, tm, D), lambda b, i: (b, i, 0)),
                  pl.BlockSpec((1, D), lambda b, i: (0, 0))],
        out_specs=pl.BlockSpec((1, tm, D), lambda b, i: (b, i, 0)),
        out_shape=jax.ShapeDtypeStruct((B, L, D), F32),
        compiler_params=_params("parallel", "parallel"),
        name="final_rmsnorm",
    )(x, g.reshape(1, D))


def _split_heads(z, n, dtype=None):
    B, L, _ = z.shape
    z = z.reshape(B, L, n, HEAD_DIM).transpose(0, 2, 1, 3)
    return z if dtype is None else z.astype(dtype)


def _merge_heads(o):
    B, H, L, d = o.shape
    return o.transpose(0, 2, 1, 3).reshape(B, L, H * d)


def _rope_tables(T):
    t = jnp.arange(T, dtype=jnp.int32)
    row = (t // GRID_W).astype(F32)
    col = (t % GRID_W).astype(F32)
    n_freq = HEAD_DIM // 4
    inv_freq = ROPE_THETA ** (-jnp.arange(n_freq, dtype=F32) / n_freq)
    ang = jnp.concatenate([row[:, None] * inv_freq, col[:, None] * inv_freq], axis=-1)
    return jnp.cos(ang), jnp.sin(ang)


def _peer_ffn_residual(x, g2, sc2, sh2, gate, w_q, keys, u_tab, v_tab):
    B, L, D = x.shape
    N = B * L
    q, h = norm_mod_proj(x, g2, sc2, sh2, w_q, (w_q.shape[1],), emit_h=True)
    offs_t, g_t = peer_topk(q.reshape(N, -1), keys)
    offs = offs_t.reshape(PEER_PAIRS // 2, N).T
    gates = g_t.reshape(PEER_PAIRS, N).T
    coef = peer_act(offs, h.reshape(N, SUBLANES, LANES), gates, u_tab)
    gate3 = gate.reshape(gate.shape[0], SUBLANES, LANES)
    out = peer_mix(offs, coef, v_tab, x.reshape(N, SUBLANES, LANES), gate3, L)
    return out.reshape(B, L, D)


def kernel(x, c, ctx, c_ctx, norm1_g, norm2_g, w_ada, b_ada, w_in, conv_w, conv_b, conv_ln_g, conv_ln_b,
           na_rel_bias, gqa_q_norm, gqa_k_norm, w_out, peer_w_q, peer_keys, peer_u, peer_v, final_norm_g):
    B, T, D = x.shape
    depth = w_in.shape[0]
    G = GQA_HEADS // GQA_KV_HEADS
    cos, sin = _rope_tables(T)
    ones_c = jnp.ones((ctx.shape[1], HEAD_DIM // 2), F32)
    ada_rows = -(-(B + 1) // SUBLANES) * SUBLANES
    ada_in = jnp.zeros((ada_rows, D), F32).at[:B].set(c).at[B].set(c_ctx)

    for l in range(depth):
        last = l == depth - 1
        mod = ada_mod(ada_in, w_ada[l], b_ada[l])
        sh1, sc1, g1, sh2, sc2, g2 = [m.reshape(B, 1, D) for m in jnp.split(mod[:B], 6, axis=-1)]
        csh1, csc1, cg1, csh2, csc2, cg2 = [m.reshape(1, 1, D) for m in jnp.split(mod[B:B + 1], 6, axis=-1)]
        w_in_b = w_in[l].astype(BF16)
        w_out_b = w_out[l].astype(BF16)
        w_q_b = peer_w_q[l].astype(BF16)
        keys_b = peer_keys[l].astype(BF16)
        u_tab = pack_expert_table(peer_u[l])
        v_tab = pack_expert_table(peer_v[l])

        za, na_q, g_q, na_k, na_v, g_k, g_v = norm_mod_proj(x, norm1_g[l], sc1, sh1, w_in_b, IN_SPLITS)
        zca, cna_q, cg_q, cna_k, cna_v, cg_k, cg_v = norm_mod_proj(ctx, norm1_g[l], csc1, csh1, w_in_b, IN_SPLITS)

        kc_na = _split_heads(cna_k, NA_HEADS, BF16)
        vc_na = _split_heads(cna_v, NA_HEADS, BF16)
        kc_g = qk_prep(_split_heads(cg_k, GQA_KV_HEADS), gqa_k_norm[l], ones_c, ones_c, False, BF16)
        vc_g = _split_heads(cg_v, GQA_KV_HEADS, BF16)

        a = conformer_conv(za, conv_w[l], conv_b[l], conv_ln_g[l], conv_ln_b[l])
        bm = neighbourhood_attention(_split_heads(na_q, NA_HEADS), _split_heads(na_k, NA_HEADS, BF16),
                                     _split_heads(na_v, NA_HEADS, BF16), kc_na, vc_na,
                                     na_bias_table(na_rel_bias[l]))
        qh = qk_prep(_split_heads(g_q, GQA_HEADS), gqa_q_norm[l], cos, sin, True, F32)
        kh = qk_prep(_split_heads(g_k, GQA_KV_HEADS), gqa_k_norm[l], cos, sin, True, BF16)
        k_all = jnp.concatenate([kh, kc_g], axis=2)
        v_all = jnp.concatenate([_split_heads(g_v, GQA_KV_HEADS, BF16), vc_g], axis=2)
        gm = grouped_attention(qh.reshape(B, GQA_KV_HEADS, G, T, HEAD_DIM), k_all, v_all)
        mix = jnp.concatenate([a, _merge_heads(bm), _merge_heads(gm.reshape(B, GQA_HEADS, T, HEAD_DIM))], axis=-1)
        x = proj_residual(mix, w_out_b, x, g1)

        x = _peer_ffn_residual(x, norm2_g[l], sc2, sh2, g2, w_q_b, keys_b, u_tab, v_tab)

        if not last:
            C = ctx.shape[1]
            ac = conformer_conv(zca, conv_w[l], conv_b[l], conv_ln_g[l], conv_ln_b[l])
            bc = grouped_attention(_split_heads(cna_q, NA_HEADS).reshape(B, NA_HEADS, 1, C, HEAD_DIM), kc_na, vc_na)
            qc_g = qk_prep(_split_heads(cg_q, GQA_HEADS), gqa_q_norm[l], ones_c, ones_c, False, F32)
            gc = grouped_attention(qc_g.reshape(B, GQA_KV_HEADS, G, C, HEAD_DIM), kc_g, vc_g)
            mixc = jnp.concatenate([ac, _merge_heads(bc.reshape(B, NA_HEADS, C, HEAD_DIM)),
                                    _merge_heads(gc.reshape(B, GQA_HEADS, C, HEAD_DIM))], axis=-1)
            ctx = proj_residual(mixc, w_out_b, ctx, cg1)
            ctx = _peer_ffn_residual(ctx, norm2_g[l], csc2, csh2, cg2, w_q_b, keys_b, u_tab, v_tab)
    return rmsnorm(x, final_norm_g)
```

```python
import functools

import jax
import jax.numpy as jnp
from jax import lax
from jax.experimental import pallas as pl
from jax.experimental.pallas import tpu as pltpu

F32 = jnp.float32
BF16 = jnp.bfloat16

GRID_W = 64
HEAD_DIM = 64
CONV_CH = 256
CONV_WIDTH = 31
NA_HEADS = 6
NA_WIN_H = 8
NA_WIN_W = 16
GQA_HEADS = 6
GQA_KV_HEADS = 2
ROPE_THETA = 10000.0
PEER_HEADS = 8
PEER_N_KEYS = 128
PEER_D_KEY = 128
PEER_TOPK = 16
EPS = 1e-6

PEER_PAIRS = PEER_HEADS * PEER_TOPK

LANES = 128
SUBLANES = 8
_HALF = SUBLANES // 2
assert PEER_N_KEYS * PEER_N_KEYS * _HALF <= 1 << 16
VMEM_LIMIT = 48 * 1024 * 1024
MASK_VALUE = -1e30


def _params(*sem):
    return pltpu.CompilerParams(dimension_semantics=sem, vmem_limit_bytes=VMEM_LIMIT)


def _ada_kernel(a_ref, w_ref, b_ref, o_ref):
    a = a_ref[...]
    s = a * jax.nn.sigmoid(a)
    o_ref[...] = jnp.dot(s.astype(BF16), w_ref[...].astype(BF16), preferred_element_type=F32) + b_ref[...]


def ada_mod(a, w, b):
    R, D = a.shape
    N = w.shape[1]
    tn = 1024
    return pl.pallas_call(
        _ada_kernel,
        grid=(N // tn,),
        in_specs=[pl.BlockSpec((R, D), lambda j: (0, 0)),
                  pl.BlockSpec((D, tn), lambda j: (0, j)),
                  pl.BlockSpec((1, tn), lambda j: (0, j))],
        out_specs=pl.BlockSpec((R, tn), lambda j: (0, j)),
        out_shape=jax.ShapeDtypeStruct((R, N), F32),
        compiler_params=_params("parallel"),
        name="ada_mod",
    )(a, w, b.reshape(1, N))


def _head_norm_rope(zh, gain, cos, sin):
    y = zh * lax.rsqrt(jnp.mean(zh * zh, axis=-1, keepdims=True) + EPS) * gain
    if cos is None:
        return y
    half = HEAD_DIM // 2
    x1, x2 = y[:, :half], y[:, half:]
    return jnp.concatenate([x1 * cos - x2 * sin, x1 * sin + x2 * cos], axis=-1)


def _norm_mod_proj_kernel(*refs, outs, emit_h, has_gains, rope):
    x_ref, g_ref, sc_ref, sh_ref, w_ref = refs[:5]
    n_in = 5
    gains_ref = cos = sin = None
    if has_gains:
        gains_ref = refs[n_in]
        n_in += 1
    if rope:
        cos, sin = refs[n_in][...], refs[n_in + 1][...]
        n_in += 2
    out_refs = refs[n_in:]
    x = x_ref[0]
    y = x * lax.rsqrt(jnp.mean(x * x, axis=-1, keepdims=True) + EPS) * g_ref[...]
    h = y * (1.0 + sc_ref[0]) + sh_ref[0]
    z = jnp.dot(h.astype(BF16), w_ref[...], preferred_element_type=F32)
    off = 0
    for o_ref, spec in zip(out_refs, outs):
        if spec[0] == "plain":
            o_ref[0] = z[:, off:off + spec[1]]
            off += spec[1]
            continue
        for hd in range(spec[1]):
            zh = z[:, off:off + HEAD_DIM]
            off += HEAD_DIM
            if spec[0] == "heads_norm":
                zh = _head_norm_rope(zh, gains_ref[spec[3]:spec[3] + 1, :], cos, sin)
            o_ref[0, hd] = (zh * spec[2]).astype(o_ref.dtype)
    if emit_h:
        out_refs[len(outs)][0] = h


def norm_mod_proj(x, g, sc, sh, w, outs, emit_h=False, gains=None, rope=None):
    B, L, D = x.shape
    N = w.shape[1]
    assert sum(o[1] if o[0] == "plain" else o[1] * HEAD_DIM for o in outs) == N
    tm = min(256, L)
    per_batch = sc.shape[0] == B
    mod_map = (lambda b, i: (b, 0, 0)) if per_batch else (lambda b, i: (0, 0, 0))
    in_specs = [pl.BlockSpec((1, tm, D), lambda b, i: (b, i, 0)),
                pl.BlockSpec((1, D), lambda b, i: (0, 0)),
                pl.BlockSpec((1, 1, D), mod_map),
                pl.BlockSpec((1, 1, D), mod_map),
                pl.BlockSpec((D, N), lambda b, i: (0, 0))]
    args = [x, g.reshape(1, D), sc, sh, w]
    if gains is not None:
        in_specs.append(pl.BlockSpec(gains.shape, lambda b, i: (0, 0)))
        args.append(gains)
    if rope is not None:
        for tab in rope:
            in_specs.append(pl.BlockSpec((tm, HEAD_DIM // 2), lambda b, i: (i, 0)))
            args.append(tab)
    out_shape, out_specs = [], []
    for o in outs:
        if o[0] == "plain":
            out_shape.append(jax.ShapeDtypeStruct((B, L, o[1]), F32))
            out_specs.append(pl.BlockSpec((1, tm, o[1]), lambda b, i: (b, i, 0)))
        else:
            out_shape.append(jax.ShapeDtypeStruct((B, o[1], L, HEAD_DIM), BF16))
            out_specs.append(pl.BlockSpec((1, o[1], tm, HEAD_DIM), lambda b, i: (b, 0, i, 0)))
    if emit_h:
        out_shape.append(jax.ShapeDtypeStruct((B, L, D), F32))
        out_specs.append(pl.BlockSpec((1, tm, D), lambda b, i: (b, i, 0)))
    return pl.pallas_call(
        functools.partial(_norm_mod_proj_kernel, outs=outs, emit_h=emit_h, has_gains=gains is not None,
                          rope=rope is not None),
        grid=(B, L // tm),
        in_specs=in_specs,
        out_specs=out_specs,
        out_shape=out_shape,
        compiler_params=_params("parallel", "parallel"),
        name="norm_mod_proj",
    )(*args)


_CONV_PAD = 16


def _conv_kernel(za_ref, w_ref, b_ref, lg_ref, lb_ref, o_ref, upad_ref, *, L, tc):
    c = pl.program_id(1)

    @pl.when(c == 0)
    def _():
        val = za_ref[0, :, :CONV_CH]
        gate = za_ref[0, :, CONV_CH:]
        zeros = jnp.zeros((_CONV_PAD, CONV_CH), F32)
        upad_ref[pl.ds(0, _CONV_PAD), :] = zeros
        upad_ref[pl.ds(_CONV_PAD + L, _CONV_PAD), :] = zeros
        upad_ref[pl.ds(_CONV_PAD, L), :] = val * jax.nn.sigmoid(gate)

    start = pl.multiple_of(c * tc, SUBLANES)
    win = upad_ref[pl.ds(start, tc + 2 * _CONV_PAD), :]
    acc = jnp.zeros((tc, CONV_CH), F32)
    first = _CONV_PAD - CONV_WIDTH // 2
    for j in range(CONV_WIDTH):
        acc = acc + win[first + j:first + j + tc, :] * w_ref[j:j + 1, :]
    y = acc + b_ref[...]
    mu = jnp.mean(y, axis=-1, keepdims=True)
    d = y - mu
    var = jnp.mean(d * d, axis=-1, keepdims=True)
    yn = d * lax.rsqrt(var + EPS) * lg_ref[...] + lb_ref[...]
    o_ref[0] = (yn * jax.nn.sigmoid(yn)).astype(o_ref.dtype)


def conformer_conv(za, w_dw, b_dw, ln_g, ln_b):
    B, L, _ = za.shape
    tc = min(256, L)
    vec = lambda v: v.reshape(1, CONV_CH)
    return pl.pallas_call(
        functools.partial(_conv_kernel, L=L, tc=tc),
        grid=(B, L // tc),
        in_specs=[pl.BlockSpec((1, L, 2 * CONV_CH), lambda b, c: (b, 0, 0)),
                  pl.BlockSpec((CONV_WIDTH, CONV_CH), lambda b, c: (0, 0)),
                  pl.BlockSpec((1, CONV_CH), lambda b, c: (0, 0)),
                  pl.BlockSpec((1, CONV_CH), lambda b, c: (0, 0)),
                  pl.BlockSpec((1, CONV_CH), lambda b, c: (0, 0))],
        out_specs=pl.BlockSpec((1, tc, CONV_CH), lambda b, c: (b, c, 0)),
        out_shape=jax.ShapeDtypeStruct((B, L, CONV_CH), BF16),
        scratch_shapes=[pltpu.VMEM((L + 2 * _CONV_PAD, CONV_CH), F32)],
        compiler_params=_params("parallel", "arbitrary"),
        name="conformer_conv",
    )(za, w_dw, vec(b_dw), vec(ln_g), vec(ln_b))


def _na_row_offset(r, rows):
    return r - jnp.clip(r - NA_WIN_H // 2, 0, rows - NA_WIN_H)


def _na_kernel(q_ref, k_ref, v_ref, kc_ref, vc_ref, bias_ref, o_ref, *, rows):
    r = pl.program_id(1)
    r0 = r - _na_row_offset(r, rows)
    band = pl.ds(pl.multiple_of(r0 * GRID_W, GRID_W), NA_WIN_H * GRID_W)
    nt = (((1,), (1,)), ((), ()))
    for h in range(NA_HEADS):
        q = q_ref[0, h]
        kb = k_ref[0, h, band, :]
        vb = v_ref[0, h, band, :]
        s_loc = lax.dot_general(q, kb, nt, preferred_element_type=F32) + bias_ref[0, h]
        s_ctx = lax.dot_general(q, kc_ref[0, h], nt, preferred_element_type=F32)
        m = jnp.maximum(jnp.max(s_loc, axis=-1, keepdims=True), jnp.max(s_ctx, axis=-1, keepdims=True))
        p_loc = jnp.exp(s_loc - m)
        p_ctx = jnp.exp(s_ctx - m)
        den = jnp.sum(p_loc, axis=-1, keepdims=True) + jnp.sum(p_ctx, axis=-1, keepdims=True)
        o = (jnp.dot(p_loc.astype(BF16), vb, preferred_element_type=F32)
             + jnp.dot(p_ctx.astype(BF16), vc_ref[0, h], preferred_element_type=F32))
        o_ref[0, :, h * HEAD_DIM:(h + 1) * HEAD_DIM] = (o / den).astype(o_ref.dtype)


def na_bias_table(rel_bias):
    cols = jnp.arange(GRID_W, dtype=jnp.int32)
    c0 = jnp.clip(cols - NA_WIN_W // 2, 0, GRID_W - NA_WIN_W)
    in_win = (cols[None, :] >= c0[:, None]) & (cols[None, :] < c0[:, None] + NA_WIN_W)
    col_idx = jnp.clip(cols[None, :] - cols[:, None] + (NA_WIN_W - 1), 0, 2 * NA_WIN_W - 2)
    off = jnp.arange(NA_WIN_H, dtype=jnp.int32)
    row_idx = off[None, :] - off[:, None] + (NA_WIN_H - 1)
    t = rel_bias[:, row_idx]
    t = t[..., col_idx]
    t = jnp.where(in_win[None, None, None], t, MASK_VALUE)
    return t.transpose(1, 0, 3, 2, 4).reshape(NA_WIN_H, NA_HEADS, GRID_W, NA_WIN_H * GRID_W)


def neighbourhood_attention(q, k, v, kc, vc, bias):
    B, H, T, d = q.shape
    C = kc.shape[2]
    rows = T // GRID_W
    assert rows >= NA_WIN_H
    return pl.pallas_call(
        functools.partial(_na_kernel, rows=rows),
        grid=(B, rows),
        in_specs=[pl.BlockSpec((1, H, GRID_W, d), lambda b, r: (b, 0, r, 0)),
                  pl.BlockSpec((1, H, T, d), lambda b, r: (b, 0, 0, 0)),
                  pl.BlockSpec((1, H, T, d), lambda b, r: (b, 0, 0, 0)),
                  pl.BlockSpec((1, H, C, d), lambda b, r: (b, 0, 0, 0)),
                  pl.BlockSpec((1, H, C, d), lambda b, r: (b, 0, 0, 0)),
                  pl.BlockSpec((1, H, GRID_W, NA_WIN_H * GRID_W),
                               lambda b, r: (_na_row_offset(r, rows), 0, 0, 0))],
        out_specs=pl.BlockSpec((1, GRID_W, H * d), lambda b, r: (b, r, 0)),
        out_shape=jax.ShapeDtypeStruct((B, T, H * d), BF16),
        compiler_params=_params("parallel", "arbitrary"),
        name="neighbourhood_attention",
    )(q, k, v, kc, vc, bias)


def _attn_kernel(*refs, Hk, G, bq, n_sets):
    q_ref = refs[0]
    kv_refs = refs[1:1 + 2 * n_sets]
    o_ref = refs[1 + 2 * n_sets]
    nt = (((1,), (1,)), ((), ()))
    for hk in range(Hk):
        q = q_ref[0, hk * G:(hk + 1) * G].reshape(G * bq, HEAD_DIM)
        ss = [lax.dot_general(q, kv_refs[2 * i][0, hk], nt, preferred_element_type=F32) for i in range(n_sets)]
        m = functools.reduce(jnp.maximum, [jnp.max(s, axis=-1, keepdims=True) for s in ss])
        ps = [jnp.exp(s - m) for s in ss]
        den = functools.reduce(jnp.add, [jnp.sum(p, axis=-1, keepdims=True) for p in ps])
        o = functools.reduce(jnp.add, [jnp.dot(p.astype(BF16), kv_refs[2 * i + 1][0, hk], preferred_element_type=F32)
                                       for i, p in enumerate(ps)])
        o = (o / den).astype(o_ref.dtype)
        for g in range(G):
            hd = hk * G + g
            o_ref[0, :, hd * HEAD_DIM:(hd + 1) * HEAD_DIM] = o[g * bq:(g + 1) * bq]


def grouped_attention(q, kv_sets):
    B, H, L, d = q.shape
    Hk = kv_sets[0][0].shape[1]
    bq = min(128, L)
    in_specs = [pl.BlockSpec((1, H, bq, d), lambda b, i: (b, 0, i, 0))]
    args = [q]
    for k, v in kv_sets:
        for t in (k, v):
            in_specs.append(pl.BlockSpec((1, Hk, t.shape[2], d), lambda b, i: (b, 0, 0, 0)))
            args.append(t)
    return pl.pallas_call(
        functools.partial(_attn_kernel, Hk=Hk, G=H // Hk, bq=bq, n_sets=len(kv_sets)),
        grid=(B, L // bq),
        in_specs=in_specs,
        out_specs=pl.BlockSpec((1, bq, H * d), lambda b, i: (b, i, 0)),
        out_shape=jax.ShapeDtypeStruct((B, L, H * d), BF16),
        compiler_params=_params("parallel", "arbitrary"),
        name="grouped_attention",
    )(*args)


def _proj_residual_kernel(a_ref, b_ref, g_ref, wa_ref, wb_ref, wg_ref, x_ref, gate_ref, o_ref):
    y = (jnp.dot(a_ref[0], wa_ref[...], preferred_element_type=F32)
         + jnp.dot(b_ref[0], wb_ref[...], preferred_element_type=F32)
         + jnp.dot(g_ref[0], wg_ref[...], preferred_element_type=F32))
    o_ref[0] = x_ref[0] + gate_ref[0] * y


def proj_residual(parts, w, x, gate):
    B, L, D = x.shape
    tm = min(512, L)
    gate_map = (lambda b, i: (b, 0, 0)) if gate.shape[0] == B else (lambda b, i: (0, 0, 0))
    widths = [p.shape[2] for p in parts]
    assert sum(widths) == w.shape[0]
    starts = [sum(widths[:i]) for i in range(len(parts))]
    ws = [w[s:s + k] for s, k in zip(starts, widths)]
    return pl.pallas_call(
        _proj_residual_kernel,
        grid=(B, L // tm),
        in_specs=[pl.BlockSpec((1, tm, k), lambda b, i: (b, i, 0)) for k in widths]
        + [pl.BlockSpec((k, D), lambda b, i: (0, 0)) for k in widths]
        + [pl.BlockSpec((1, tm, D), lambda b, i: (b, i, 0)), pl.BlockSpec((1, 1, D), gate_map)],
        out_specs=pl.BlockSpec((1, tm, D), lambda b, i: (b, i, 0)),
        out_shape=jax.ShapeDtypeStruct((B, L, D), F32),
        compiler_params=_params("parallel", "parallel"),
        name="proj_residual",
    )(*parts, *ws, x, gate)


def _top_rows(s, k, order=None, payload=None):
    if order is None:
        order = lax.broadcasted_iota(jnp.int32, s.shape, 0)
    after_all = jnp.iinfo(jnp.int32).max
    vals, picks = [], []
    for _ in range(k):
        m = jnp.max(s, axis=0, keepdims=True)
        first = jnp.min(jnp.where(s == m, order, after_all), axis=0, keepdims=True)
        hit = order == first
        vals.append(m)
        if payload is None:
            picks.append(first)
        else:
            picks.append(jnp.max(jnp.where(hit, payload, -1), axis=0, keepdims=True))
        s = jnp.where(hit, -jnp.inf, s)
    return jnp.concatenate(vals, axis=0), jnp.concatenate(picks, axis=0)


def _pair_candidates(s0, i0, s1, i1):
    K = PEER_TOPK
    t = s0.shape[1]
    sub = lax.broadcasted_iota(jnp.int32, (SUBLANES, t), 0)
    scores, order, ids = [], [], []

    def add(a_rows, b_rows, a_of_row, b_of_row):
        sa, ia = a_rows
        sb, ib = b_rows
        ok = (a_of_row + 1) * (b_of_row + 1) <= K
        scores.append(jnp.where(ok, sa + sb, -jnp.inf))
        order.append(a_of_row * K + b_of_row)
        ids.append(ia * PEER_N_KEYS + ib)

    row = lambda x, r: (x[0][r:r + 1], x[1][r:r + 1])
    rows = lambda x, r: (x[0][r:r + SUBLANES], x[1][r:r + SUBLANES])
    A, Bv = (s0, i0), (s1, i1)
    add(row(A, 0), rows(Bv, 0), jnp.zeros_like(sub), sub)
    add(row(A, 0), rows(Bv, SUBLANES), jnp.zeros_like(sub), sub + SUBLANES)
    for a in range(1, 4):
        add(row(A, a), rows(Bv, 0), jnp.full_like(sub, a), sub)
    for b in range(3):
        dup = sub < 4
        sa, ia = rows(A, 0)
        add((jnp.where(dup, -jnp.inf, sa), ia), row(Bv, b), sub, jnp.full_like(sub, b))
    add(rows(A, SUBLANES), row(Bv, 0), sub + SUBLANES, jnp.zeros_like(sub))
    cat = lambda xs: jnp.concatenate(xs, axis=0)
    return cat(scores), cat(order), cat(ids)


def _peer_topk_kernel(q_ref, keys_ref, offs_ref, g_ref):
    K = PEER_TOPK
    assert K == 2 * SUBLANES
    nt = (((1,), (1,)), ((), ()))
    for h in range(PEER_HEADS):
        tops = []
        for p in range(2):
            col = (2 * h + p) * PEER_D_KEY
            qhp = q_ref[:, col:col + PEER_D_KEY].astype(BF16)
            s = lax.dot_general(keys_ref[h, p], qhp, nt, preferred_element_type=F32)
            tops.append(_top_rows(s, K))
        (s0, i0), (s1, i1) = tops
        cand_s, cand_order, cand_i = _pair_candidates(s0, i0, s1, i1)
        best_s, best_i = _top_rows(cand_s, K, order=cand_order, payload=cand_i)
        e = jnp.exp(best_s - jnp.max(best_s, axis=0, keepdims=True))
        g_ref[h] = e / jnp.sum(e, axis=0, keepdims=True)
        off = best_i * _HALF
        lo = jnp.concatenate([off[0:_HALF], off[SUBLANES:SUBLANES + _HALF]], axis=0)
        hi = jnp.concatenate([off[_HALF:SUBLANES], off[SUBLANES + _HALF:K]], axis=0)
        offs_ref[h] = lo | (hi << 16)


def peer_topk(q, keys):
    N = q.shape[0]
    tm = LANES
    return pl.pallas_call(
        _peer_topk_kernel,
        grid=(N // tm,),
        in_specs=[pl.BlockSpec((tm, q.shape[1]), lambda i: (i, 0)),
                  pl.BlockSpec(keys.shape, lambda i: (0, 0, 0, 0))],
        out_specs=[pl.BlockSpec((PEER_HEADS, PEER_TOPK // 2, tm), lambda i: (0, 0, i)),
                   pl.BlockSpec((PEER_HEADS, PEER_TOPK, tm), lambda i: (0, 0, i))],
        out_shape=[jax.ShapeDtypeStruct((PEER_HEADS, PEER_TOPK // 2, N), jnp.int32),
                   jax.ShapeDtypeStruct((PEER_HEADS, PEER_TOPK, N), F32)],
        compiler_params=_params("parallel"),
        name="peer_topk",
    )(q, keys)


_PEER_TOKENS = 64
_GROUPS = PEER_PAIRS // SUBLANES


def pack_expert_table(w):
    E, D = w.shape
    assert D == SUBLANES * LANES
    bits = lax.bitcast_convert_type(w.astype(BF16), jnp.uint16).astype(jnp.uint32).reshape(E, _HALF, 2, LANES)
    return (bits[:, :, 0] | (bits[:, :, 1] << 16)).reshape(E * _HALF, LANES)


def _pair_position(word, half):
    return SUBLANES * (word // _HALF) + _HALF * half + word % _HALF


def _expert_rows(tab_ref, word):
    starts = (word & 0xFFFF, lax.shift_right_logical(word, 16))
    return [pltpu.bitcast(tab_ref[pl.ds(pl.multiple_of(s, _HALF), _HALF), :], BF16).astype(F32) for s in starts]


def _merge_pair(a, b, shift, first):
    if shift == _HALF:
        return jnp.where(first, a, b) + pltpu.roll(jnp.where(first, b, a), shift, axis=0)
    bs = pltpu.roll(b, shift, axis=0)
    return jnp.where(first, a, bs) + pltpu.roll(jnp.where(first, bs, a), SUBLANES - shift, axis=0)


_MERGE_ORDER = (0, 4, 2, 6, 1, 5, 3, 7)


def _merge8(ps):
    sub = lax.broadcasted_iota(jnp.int32, (SUBLANES, LANES), 0)
    shift = _HALF
    while len(ps) > 1:
        first = (sub % (2 * shift)) < shift
        ps = [_merge_pair(ps[2 * i], ps[2 * i + 1], shift, first) for i in range(len(ps) // 2)]
        shift //= 2
    return ps[0]


def _peer_act_kernel(idx_ref, h_ref, g_ref, tab_ref, o_ref, part_ref):
    tn = h_ref.shape[0]

    def products(t, slot):
        x = h_ref[t]
        for gi in range(_GROUPS):
            rows = []
            for w in range(_HALF):
                rows += _expert_rows(tab_ref, idx_ref[t, gi * _HALF + w])
            prods = [rows[2 * (j % _HALF) + j // _HALF] * x for j in _MERGE_ORDER]
            part_ref[slot, pl.ds(gi * SUBLANES, SUBLANES), :] = _merge8(prods)

    def reduce(t, slot):
        o_ref[pl.ds(t, 1), :] = jnp.sum(part_ref[slot].T, axis=0, keepdims=True)

    part_ref[1] = jnp.zeros((PEER_PAIRS, LANES), F32)

    def two_tokens(i, carry):
        t = 2 * i
        reduce(jnp.maximum(t - 1, 0), 1)
        products(t, 0)
        reduce(t, 0)
        products(t + 1, 1)
        return carry

    lax.fori_loop(0, tn // 2, two_tokens, 0)
    reduce(tn - 1, 1)
    o_ref[...] = g_ref[...] * jax.nn.gelu(o_ref[...], approximate=True)


def peer_act(offs, h3, g, table):
    N = offs.shape[0]
    tn = _PEER_TOKENS
    return pl.pallas_call(
        _peer_act_kernel,
        grid=(N // tn,),
        in_specs=[pl.BlockSpec((tn, PEER_PAIRS // 2), lambda i: (i, 0), memory_space=pltpu.SMEM),
                  pl.BlockSpec((tn, SUBLANES, LANES), lambda i: (i, 0, 0)),
                  pl.BlockSpec((tn, PEER_PAIRS), lambda i: (i, 0)),
                  pl.BlockSpec(memory_space=pltpu.VMEM)],
        out_specs=pl.BlockSpec((tn, PEER_PAIRS), lambda i: (i, 0)),
        out_shape=jax.ShapeDtypeStruct((N, PEER_PAIRS), F32),
        scratch_shapes=[pltpu.VMEM((2, PEER_PAIRS, LANES), F32)],
        compiler_params=_params("arbitrary"),
        name="peer_act",
    )(offs, h3, g, table)


_MIX_CHAINS = 4


def _peer_mix_kernel(idx_ref, coef_ref, tab_ref, x_ref, gate_ref, o_ref, cb_ref):
    tn = x_ref.shape[0]

    def spread(t, slot):
        row = coef_ref[pl.ds(t, 1), :]
        cb_ref[slot] = jnp.broadcast_to(row, (PEER_PAIRS, PEER_PAIRS)).T

    def mix(t, slot):
        accs = [None] * _MIX_CHAINS
        for w in range(PEER_PAIRS // 2):
            for k, row in enumerate(_expert_rows(tab_ref, idx_ref[t, w])):
                j = _pair_position(w, k)
                term = jnp.broadcast_to(cb_ref[slot, pl.ds(j, 1), :], (SUBLANES, LANES)) * row
                a = j % _MIX_CHAINS
                accs[a] = term if accs[a] is None else accs[a] + term
        y = (accs[0] + accs[1]) + (accs[2] + accs[3])
        o_ref[t] = x_ref[t] + gate_ref[0] * y

    spread(0, 0)

    def two_tokens(i, carry):
        t = 2 * i
        spread(t + 1, 1)
        mix(t, 0)
        spread(jnp.minimum(t + 2, tn - 1), 0)
        mix(t + 1, 1)
        return carry

    lax.fori_loop(0, tn // 2, two_tokens, 0)


def peer_mix(offs, coef, table, x3, gate3, tokens_per_batch):
    N = offs.shape[0]
    tn = _PEER_TOKENS
    assert tokens_per_batch % tn == 0
    if gate3.shape[0] == 1:
        gate_map = lambda i: (0, 0, 0)
    else:
        gate_map = lambda i: ((i * tn) // tokens_per_batch, 0, 0)
    return pl.pallas_call(
        _peer_mix_kernel,
        grid=(N // tn,),
        in_specs=[pl.BlockSpec((tn, PEER_PAIRS // 2), lambda i: (i, 0), memory_space=pltpu.SMEM),
                  pl.BlockSpec((tn, PEER_PAIRS), lambda i: (i, 0)),
                  pl.BlockSpec(memory_space=pltpu.VMEM),
                  pl.BlockSpec((tn, SUBLANES, LANES), lambda i: (i, 0, 0)),
                  pl.BlockSpec((1, SUBLANES, LANES), gate_map)],
        out_specs=pl.BlockSpec((tn, SUBLANES, LANES), lambda i: (i, 0, 0)),
        out_shape=jax.ShapeDtypeStruct((N, SUBLANES, LANES), F32),
        scratch_shapes=[pltpu.VMEM((2, PEER_PAIRS, LANES), F32)],
        compiler_params=_params("arbitrary"),
        name="peer_mix",
    )(offs, coef, table, x3, gate3)


def _rmsnorm_kernel(x_ref, g_ref, o_ref):
    x = x_ref[0]
    o_ref[0] = x * lax.rsqrt(jnp.mean(x * x, axis=-1, keepdims=True) + EPS) * g_ref[...]


def rmsnorm(x, g):
    B, L, D = x.shape
    tm = min(512, L)
    return pl.pallas_call(
        _rmsnorm_kernel,
        grid=(B, L // tm),
        in_specs=[pl.BlockSpec((1, tm, D), lambda b, i: (b, i, 0)),
                  pl.BlockSpec((1, D), lambda b, i: (0, 0))],
        out_specs=pl.BlockSpec((1, tm, D), lambda b, i: (b, i, 0)),
        out_shape=jax.ShapeDtypeStruct((B, L, D), F32),
        compiler_params=_params("parallel", "parallel"),
        name="final_rmsnorm",
    )(x, g.reshape(1, D))


def _rope_tables(T):
    t = jnp.arange(T, dtype=jnp.int32)
    row = (t // GRID_W).astype(F32)
    col = (t % GRID_W).astype(F32)
    n_freq = HEAD_DIM // 4
    inv_freq = ROPE_THETA ** (-jnp.arange(n_freq, dtype=F32) / n_freq)
    ang = jnp.concatenate([row[:, None] * inv_freq, col[:, None] * inv_freq], axis=-1)
    return jnp.cos(ang), jnp.sin(ang)


def _peer_ffn_residual(x, g2, sc2, sh2, gate, w_q, keys, u_tab, v_tab):
    B, L, D = x.shape
    N = B * L
    q, h = norm_mod_proj(x, g2, sc2, sh2, w_q, (("plain", w_q.shape[1]),), emit_h=True)
    offs_t, g_t = peer_topk(q.reshape(N, -1), keys)
    offs = offs_t.reshape(PEER_PAIRS // 2, N).T
    gates = g_t.reshape(PEER_PAIRS, N).T
    coef = peer_act(offs, h.reshape(N, SUBLANES, LANES), gates, u_tab)
    gate3 = gate.reshape(gate.shape[0], SUBLANES, LANES)
    out = peer_mix(offs, coef, v_tab, x.reshape(N, SUBLANES, LANES), gate3, L)
    return out.reshape(B, L, D)


_Q_SCALE = HEAD_DIM ** -0.5
_IN_OUTS = (("plain", 2 * CONV_CH), ("heads", NA_HEADS, _Q_SCALE), ("heads_norm", GQA_HEADS, _Q_SCALE, 0),
            ("heads", NA_HEADS, 1.0), ("heads", NA_HEADS, 1.0), ("heads_norm", GQA_KV_HEADS, 1.0, 1),
            ("heads", GQA_KV_HEADS, 1.0))


def kernel(x, c, ctx, c_ctx, norm1_g, norm2_g, w_ada, b_ada, w_in, conv_w, conv_b, conv_ln_g, conv_ln_b,
           na_rel_bias, gqa_q_norm, gqa_k_norm, w_out, peer_w_q, peer_keys, peer_u, peer_v, final_norm_g):
    B, T, D = x.shape
    depth = w_in.shape[0]
    rope = _rope_tables(T)
    ada_rows = -(-(B + 1) // SUBLANES) * SUBLANES
    ada_in = jnp.zeros((ada_rows, D), F32).at[:B].set(c).at[B].set(c_ctx)

    for l in range(depth):
        last = l == depth - 1
        mod = ada_mod(ada_in, w_ada[l], b_ada[l])
        sh1, sc1, g1, sh2, sc2, g2 = [m.reshape(B, 1, D) for m in jnp.split(mod[:B], 6, axis=-1)]
        csh1, csc1, cg1, csh2, csc2, cg2 = [m.reshape(1, 1, D) for m in jnp.split(mod[B:B + 1], 6, axis=-1)]
        w_in_b = w_in[l].astype(BF16)
        w_out_b = w_out[l].astype(BF16)
        w_q_b = peer_w_q[l].astype(BF16)
        keys_b = peer_keys[l].astype(BF16)
        u_tab = pack_expert_table(peer_u[l])
        v_tab = pack_expert_table(peer_v[l])
        qk_gains = jnp.stack([gqa_q_norm[l], gqa_k_norm[l]])

        za, na_q, g_q, na_k, na_v, g_k, g_v = norm_mod_proj(x, norm1_g[l], sc1, sh1, w_in_b, _IN_OUTS,
                                                            gains=qk_gains, rope=rope)
        zca, cna_q, cg_q, cna_k, cna_v, cg_k, cg_v = norm_mod_proj(ctx, norm1_g[l], csc1, csh1, w_in_b, _IN_OUTS,
                                                                   gains=qk_gains)
        a = conformer_conv(za, conv_w[l], conv_b[l], conv_ln_g[l], conv_ln_b[l])
        bm = neighbourhood_attention(na_q, na_k, na_v, cna_k, cna_v, na_bias_table(na_rel_bias[l]))
        gm = grouped_attention(g_q, [(g_k, g_v), (cg_k, cg_v)])
        x = proj_residual((a, bm, gm), w_out_b, x, g1)

        x = _peer_ffn_residual(x, norm2_g[l], sc2, sh2, g2, w_q_b, keys_b, u_tab, v_tab)

        if not last:
            ac = conformer_conv(zca, conv_w[l], conv_b[l], conv_ln_g[l], conv_ln_b[l])
            bc = grouped_attention(cna_q, [(cna_k, cna_v)])
            gc = grouped_attention(cg_q, [(cg_k, cg_v)])
            ctx = proj_residual((ac, bc, gc), w_out_b, ctx, cg1)
            ctx = _peer_ffn_residual(ctx, norm2_g[l], csc2, csh2, cg2, w_q_b, keys_b, u_tab, v_tab)
    return rmsnorm(x, final_norm_g)
```

```python
import functools

import jax
import jax.numpy as jnp
from jax import lax
from jax.experimental import pallas as pl
from jax.experimental.pallas import tpu as pltpu

F32 = jnp.float32
BF16 = jnp.bfloat16

GRID_W = 64
HEAD_DIM = 64
CONV_CH = 256
CONV_WIDTH = 31
NA_HEADS = 6
NA_WIN_H = 8
NA_WIN_W = 16
GQA_HEADS = 6
GQA_KV_HEADS = 2
ROPE_THETA = 10000.0
PEER_HEADS = 8
PEER_N_KEYS = 128
PEER_D_KEY = 128
PEER_TOPK = 16
EPS = 1e-6

PEER_PAIRS = PEER_HEADS * PEER_TOPK

LANES = 128
SUBLANES = 8
_HALF = SUBLANES // 2
assert PEER_N_KEYS * PEER_N_KEYS * _HALF <= 1 << 16
VMEM_LIMIT = 48 * 1024 * 1024
MASK_VALUE = -1e30


def _params(*sem):
    return pltpu.CompilerParams(dimension_semantics=sem, vmem_limit_bytes=VMEM_LIMIT)


def _ada_kernel(a_ref, w_ref, b_ref, o_ref):
    a = a_ref[...]
    s = a * jax.nn.sigmoid(a)
    o_ref[...] = jnp.dot(s.astype(BF16), w_ref[...].astype(BF16), preferred_element_type=F32) + b_ref[...]


def ada_mod(a, w, b):
    R, D = a.shape
    N = w.shape[1]
    tn = 1024
    return pl.pallas_call(
        _ada_kernel,
        grid=(N // tn,),
        in_specs=[pl.BlockSpec((R, D), lambda j: (0, 0)),
                  pl.BlockSpec((D, tn), lambda j: (0, j)),
                  pl.BlockSpec((1, tn), lambda j: (0, j))],
        out_specs=pl.BlockSpec((R, tn), lambda j: (0, j)),
        out_shape=jax.ShapeDtypeStruct((R, N), F32),
        compiler_params=_params("parallel"),
        name="ada_mod",
    )(a, w, b.reshape(1, N))


def _head_norm_rope(zh, gain, cos, sin):
    y = zh * lax.rsqrt(jnp.mean(zh * zh, axis=-1, keepdims=True) + EPS) * gain
    if cos is None:
        return y
    half = HEAD_DIM // 2
    x1, x2 = y[:, :half], y[:, half:]
    return jnp.concatenate([x1 * cos - x2 * sin, x1 * sin + x2 * cos], axis=-1)


def _norm_mod_proj_kernel(*refs, outs, emit_h, has_gains, rope):
    x_ref, g_ref, sc_ref, sh_ref, w_ref = refs[:5]
    n_in = 5
    gains_ref = cos = sin = None
    if has_gains:
        gains_ref = refs[n_in]
        n_in += 1
    if rope:
        cos, sin = refs[n_in][...], refs[n_in + 1][...]
        n_in += 2
    out_refs = refs[n_in:]
    x = x_ref[0]
    y = x * lax.rsqrt(jnp.mean(x * x, axis=-1, keepdims=True) + EPS) * g_ref[...]
    h = y * (1.0 + sc_ref[0]) + sh_ref[0]
    z = jnp.dot(h.astype(BF16), w_ref[...], preferred_element_type=F32)
    off = 0
    for o_ref, spec in zip(out_refs, outs):
        if spec[0] == "plain":
            o_ref[0] = z[:, off:off + spec[1]]
            off += spec[1]
            continue
        for hd in range(spec[1]):
            zh = z[:, off:off + HEAD_DIM]
            off += HEAD_DIM
            if spec[0] == "heads_norm":
                zh = _head_norm_rope(zh, gains_ref[spec[3]:spec[3] + 1, :], cos, sin)
            o_ref[0, hd] = (zh * spec[2]).astype(o_ref.dtype)
    if emit_h:
        out_refs[len(outs)][0] = h


def norm_mod_proj(x, g, sc, sh, w, outs, emit_h=False, gains=None, rope=None):
    B, L, D = x.shape
    N = w.shape[1]
    assert sum(o[1] if o[0] == "plain" else o[1] * HEAD_DIM for o in outs) == N
    tm = min(256, L)
    per_batch = sc.shape[0] == B
    mod_map = (lambda b, i: (b, 0, 0)) if per_batch else (lambda b, i: (0, 0, 0))
    in_specs = [pl.BlockSpec((1, tm, D), lambda b, i: (b, i, 0)),
                pl.BlockSpec((1, D), lambda b, i: (0, 0)),
                pl.BlockSpec((1, 1, D), mod_map),
                pl.BlockSpec((1, 1, D), mod_map),
                pl.BlockSpec((D, N), lambda b, i: (0, 0))]
    args = [x, g.reshape(1, D), sc, sh, w]
    if gains is not None:
        in_specs.append(pl.BlockSpec(gains.shape, lambda b, i: (0, 0)))
        args.append(gains)
    if rope is not None:
        for tab in rope:
            in_specs.append(pl.BlockSpec((tm, HEAD_DIM // 2), lambda b, i: (i, 0)))
            args.append(tab)
    out_shape, out_specs = [], []
    for o in outs:
        if o[0] == "plain":
            out_shape.append(jax.ShapeDtypeStruct((B, L, o[1]), F32))
            out_specs.append(pl.BlockSpec((1, tm, o[1]), lambda b, i: (b, i, 0)))
        else:
            out_shape.append(jax.ShapeDtypeStruct((B, o[1], L, HEAD_DIM), BF16))
            out_specs.append(pl.BlockSpec((1, o[1], tm, HEAD_DIM), lambda b, i: (b, 0, i, 0)))
    if emit_h:
        out_shape.append(jax.ShapeDtypeStruct((B, L, D), F32))
        out_specs.append(pl.BlockSpec((1, tm, D), lambda b, i: (b, i, 0)))
    return pl.pallas_call(
        functools.partial(_norm_mod_proj_kernel, outs=outs, emit_h=emit_h, has_gains=gains is not None,
                          rope=rope is not None),
        grid=(B, L // tm),
        in_specs=in_specs,
        out_specs=out_specs,
        out_shape=out_shape,
        compiler_params=_params("parallel", "parallel"),
        name="norm_mod_proj",
    )(*args)


_CONV_PAD = 16


def _conv_kernel(za_ref, w_ref, b_ref, lg_ref, lb_ref, o_ref, upad_ref, *, L, tc):
    c = pl.program_id(1)

    @pl.when(c == 0)
    def _():
        val = za_ref[0, :, :CONV_CH]
        gate = za_ref[0, :, CONV_CH:]
        zeros = jnp.zeros((_CONV_PAD, CONV_CH), F32)
        upad_ref[pl.ds(0, _CONV_PAD), :] = zeros
        upad_ref[pl.ds(_CONV_PAD + L, _CONV_PAD), :] = zeros
        upad_ref[pl.ds(_CONV_PAD, L), :] = val * jax.nn.sigmoid(gate)

    start = pl.multiple_of(c * tc, SUBLANES)
    win = upad_ref[pl.ds(start, tc + 2 * _CONV_PAD), :]
    acc = jnp.zeros((tc, CONV_CH), F32)
    first = _CONV_PAD - CONV_WIDTH // 2
    for j in range(CONV_WIDTH):
        acc = acc + win[first + j:first + j + tc, :] * w_ref[j:j + 1, :]
    y = acc + b_ref[...]
    mu = jnp.mean(y, axis=-1, keepdims=True)
    d = y - mu
    var = jnp.mean(d * d, axis=-1, keepdims=True)
    yn = d * lax.rsqrt(var + EPS) * lg_ref[...] + lb_ref[...]
    o_ref[0] = (yn * jax.nn.sigmoid(yn)).astype(o_ref.dtype)


def conformer_conv(za, w_dw, b_dw, ln_g, ln_b):
    B, L, _ = za.shape
    tc = min(256, L)
    vec = lambda v: v.reshape(1, CONV_CH)
    return pl.pallas_call(
        functools.partial(_conv_kernel, L=L, tc=tc),
        grid=(B, L // tc),
        in_specs=[pl.BlockSpec((1, L, 2 * CONV_CH), lambda b, c: (b, 0, 0)),
                  pl.BlockSpec((CONV_WIDTH, CONV_CH), lambda b, c: (0, 0)),
                  pl.BlockSpec((1, CONV_CH), lambda b, c: (0, 0)),
                  pl.BlockSpec((1, CONV_CH), lambda b, c: (0, 0)),
                  pl.BlockSpec((1, CONV_CH), lambda b, c: (0, 0))],
        out_specs=pl.BlockSpec((1, tc, CONV_CH), lambda b, c: (b, c, 0)),
        out_shape=jax.ShapeDtypeStruct((B, L, CONV_CH), BF16),
        scratch_shapes=[pltpu.VMEM((L + 2 * _CONV_PAD, CONV_CH), F32)],
        compiler_params=_params("parallel", "arbitrary"),
        name="conformer_conv",
    )(za, w_dw, vec(b_dw), vec(ln_g), vec(ln_b))


def _na_row_offset(r, rows):
    return r - jnp.clip(r - NA_WIN_H // 2, 0, rows - NA_WIN_H)


def _na_kernel(q_ref, k_ref, v_ref, kc_ref, vc_ref, bias_ref, o_ref, *, rows):
    r = pl.program_id(1)
    r0 = r - _na_row_offset(r, rows)
    band = pl.ds(pl.multiple_of(r0 * GRID_W, GRID_W), NA_WIN_H * GRID_W)
    nt = (((1,), (1,)), ((), ()))
    for h in range(NA_HEADS):
        q = q_ref[0, h]
        kb = k_ref[0, h, band, :]
        vb = v_ref[0, h, band, :]
        s_loc = lax.dot_general(q, kb, nt, preferred_element_type=F32) + bias_ref[0, h]
        s_ctx = lax.dot_general(q, kc_ref[0, h], nt, preferred_element_type=F32)
        m = jnp.maximum(jnp.max(s_loc, axis=-1, keepdims=True), jnp.max(s_ctx, axis=-1, keepdims=True))
        p_loc = jnp.exp(s_loc - m)
        p_ctx = jnp.exp(s_ctx - m)
        den = jnp.sum(p_loc, axis=-1, keepdims=True) + jnp.sum(p_ctx, axis=-1, keepdims=True)
        o = (jnp.dot(p_loc.astype(BF16), vb, preferred_element_type=F32)
             + jnp.dot(p_ctx.astype(BF16), vc_ref[0, h], preferred_element_type=F32))
        o_ref[0, :, h * HEAD_DIM:(h + 1) * HEAD_DIM] = (o / den).astype(o_ref.dtype)


def na_bias_table(rel_bias):
    cols = jnp.arange(GRID_W, dtype=jnp.int32)
    c0 = jnp.clip(cols - NA_WIN_W // 2, 0, GRID_W - NA_WIN_W)
    in_win = (cols[None, :] >= c0[:, None]) & (cols[None, :] < c0[:, None] + NA_WIN_W)
    col_idx = jnp.clip(cols[None, :] - cols[:, None] + (NA_WIN_W - 1), 0, 2 * NA_WIN_W - 2)
    off = jnp.arange(NA_WIN_H, dtype=jnp.int32)
    row_idx = off[None, :] - off[:, None] + (NA_WIN_H - 1)
    t = rel_bias[:, row_idx]
    t = t[..., col_idx]
    t = jnp.where(in_win[None, None, None], t, MASK_VALUE)
    return t.transpose(1, 0, 3, 2, 4).reshape(NA_WIN_H, NA_HEADS, GRID_W, NA_WIN_H * GRID_W)


def neighbourhood_attention(q, k, v, kc, vc, bias):
    B, H, T, d = q.shape
    C = kc.shape[2]
    rows = T // GRID_W
    assert rows >= NA_WIN_H
    return pl.pallas_call(
        functools.partial(_na_kernel, rows=rows),
        grid=(B, rows),
        in_specs=[pl.BlockSpec((1, H, GRID_W, d), lambda b, r: (b, 0, r, 0)),
                  pl.BlockSpec((1, H, T, d), lambda b, r: (b, 0, 0, 0)),
                  pl.BlockSpec((1, H, T, d), lambda b, r: (b, 0, 0, 0)),
                  pl.BlockSpec((1, H, C, d), lambda b, r: (b, 0, 0, 0)),
                  pl.BlockSpec((1, H, C, d), lambda b, r: (b, 0, 0, 0)),
                  pl.BlockSpec((1, H, GRID_W, NA_WIN_H * GRID_W),
                               lambda b, r: (_na_row_offset(r, rows), 0, 0, 0))],
        out_specs=pl.BlockSpec((1, GRID_W, H * d), lambda b, r: (b, r, 0)),
        out_shape=jax.ShapeDtypeStruct((B, T, H * d), BF16),
        compiler_params=_params("parallel", "arbitrary"),
        name="neighbourhood_attention",
    )(q, k, v, kc, vc, bias)


def _attn_kernel(*refs, Hk, G, bq, n_sets):
    q_ref = refs[0]
    kv_refs = refs[1:1 + 2 * n_sets]
    o_ref = refs[1 + 2 * n_sets]
    nt = (((1,), (1,)), ((), ()))
    for hk in range(Hk):
        q = q_ref[0, hk * G:(hk + 1) * G].reshape(G * bq, HEAD_DIM)
        ss = [lax.dot_general(q, kv_refs[2 * i][0, hk], nt, preferred_element_type=F32) for i in range(n_sets)]
        m = functools.reduce(jnp.maximum, [jnp.max(s, axis=-1, keepdims=True) for s in ss])
        ps = [jnp.exp(s - m) for s in ss]
        den = functools.reduce(jnp.add, [jnp.sum(p, axis=-1, keepdims=True) for p in ps])
        o = functools.reduce(jnp.add, [jnp.dot(p.astype(BF16), kv_refs[2 * i + 1][0, hk], preferred_element_type=F32)
                                       for i, p in enumerate(ps)])
        o = (o / den).astype(o_ref.dtype)
        for g in range(G):
            hd = hk * G + g
            o_ref[0, :, hd * HEAD_DIM:(hd + 1) * HEAD_DIM] = o[g * bq:(g + 1) * bq]


def grouped_attention(q, kv_sets):
    B, H, L, d = q.shape
    Hk = kv_sets[0][0].shape[1]
    bq = min(128, L)
    in_specs = [pl.BlockSpec((1, H, bq, d), lambda b, i: (b, 0, i, 0))]
    args = [q]
    for k, v in kv_sets:
        for t in (k, v):
            in_specs.append(pl.BlockSpec((1, Hk, t.shape[2], d), lambda b, i: (b, 0, 0, 0)))
            args.append(t)
    return pl.pallas_call(
        functools.partial(_attn_kernel, Hk=Hk, G=H // Hk, bq=bq, n_sets=len(kv_sets)),
        grid=(B, L // bq),
        in_specs=in_specs,
        out_specs=pl.BlockSpec((1, bq, H * d), lambda b, i: (b, i, 0)),
        out_shape=jax.ShapeDtypeStruct((B, L, H * d), BF16),
        compiler_params=_params("parallel", "arbitrary"),
        name="grouped_attention",
    )(*args)


def _proj_residual_kernel(a_ref, b_ref, g_ref, wa_ref, wb_ref, wg_ref, x_ref, gate_ref, o_ref):
    y = (jnp.dot(a_ref[0], wa_ref[...], preferred_element_type=F32)
         + jnp.dot(b_ref[0], wb_ref[...], preferred_element_type=F32)
         + jnp.dot(g_ref[0], wg_ref[...], preferred_element_type=F32))
    o_ref[0] = x_ref[0] + gate_ref[0] * y


def proj_residual(parts, w, x, gate):
    B, L, D = x.shape
    tm = min(512, L)
    gate_map = (lambda b, i: (b, 0, 0)) if gate.shape[0] == B else (lambda b, i: (0, 0, 0))
    widths = [p.shape[2] for p in parts]
    assert sum(widths) == w.shape[0]
    starts = [sum(widths[:i]) for i in range(len(parts))]
    ws = [w[s:s + k] for s, k in zip(starts, widths)]
    return pl.pallas_call(
        _proj_residual_kernel,
        grid=(B, L // tm),
        in_specs=[pl.BlockSpec((1, tm, k), lambda b, i: (b, i, 0)) for k in widths]
        + [pl.BlockSpec((k, D), lambda b, i: (0, 0)) for k in widths]
        + [pl.BlockSpec((1, tm, D), lambda b, i: (b, i, 0)), pl.BlockSpec((1, 1, D), gate_map)],
        out_specs=pl.BlockSpec((1, tm, D), lambda b, i: (b, i, 0)),
        out_shape=jax.ShapeDtypeStruct((B, L, D), F32),
        compiler_params=_params("parallel", "parallel"),
        name="proj_residual",
    )(*parts, *ws, x, gate)


def _top_rows(s, k, order=None, payload=None):
    if order is None:
        order = lax.broadcasted_iota(jnp.int32, s.shape, 0)
    after_all = jnp.iinfo(jnp.int32).max
    vals, picks = [], []
    for _ in range(k):
        m = jnp.max(s, axis=0, keepdims=True)
        first = jnp.min(jnp.where(s == m, order, after_all), axis=0, keepdims=True)
        hit = order == first
        vals.append(m)
        if payload is None:
            picks.append(first)
        else:
            picks.append(jnp.max(jnp.where(hit, payload, -1), axis=0, keepdims=True))
        s = jnp.where(hit, -jnp.inf, s)
    return jnp.concatenate(vals, axis=0), jnp.concatenate(picks, axis=0)


def _pair_candidates(s0, i0, s1, i1):
    K = PEER_TOPK
    t = s0.shape[1]
    sub = lax.broadcasted_iota(jnp.int32, (SUBLANES, t), 0)
    scores, order, ids = [], [], []

    def add(a_rows, b_rows, a_of_row, b_of_row):
        sa, ia = a_rows
        sb, ib = b_rows
        ok = (a_of_row + 1) * (b_of_row + 1) <= K
        scores.append(jnp.where(ok, sa + sb, -jnp.inf))
        order.append(a_of_row * K + b_of_row)
        ids.append(ia * PEER_N_KEYS + ib)

    row = lambda x, r: (x[0][r:r + 1], x[1][r:r + 1])
    rows = lambda x, r: (x[0][r:r + SUBLANES], x[1][r:r + SUBLANES])
    A, Bv = (s0, i0), (s1, i1)
    add(row(A, 0), rows(Bv, 0), jnp.zeros_like(sub), sub)
    add(row(A, 0), rows(Bv, SUBLANES), jnp.zeros_like(sub), sub + SUBLANES)
    for a in range(1, 4):
        add(row(A, a), rows(Bv, 0), jnp.full_like(sub, a), sub)
    for b in range(3):
        dup = sub < 4
        sa, ia = rows(A, 0)
        add((jnp.where(dup, -jnp.inf, sa), ia), row(Bv, b), sub, jnp.full_like(sub, b))
    add(rows(A, SUBLANES), row(Bv, 0), sub + SUBLANES, jnp.zeros_like(sub))
    cat = lambda xs: jnp.concatenate(xs, axis=0)
    return cat(scores), cat(order), cat(ids)


def _peer_topk_kernel(q_ref, keys_ref, offs_ref, g_ref):
    K = PEER_TOPK
    assert K == 2 * SUBLANES
    nt = (((1,), (1,)), ((), ()))
    for h in range(PEER_HEADS):
        tops = []
        for p in range(2):
            col = (2 * h + p) * PEER_D_KEY
            qhp = q_ref[:, col:col + PEER_D_KEY].astype(BF16)
            s = lax.dot_general(keys_ref[h, p], qhp, nt, preferred_element_type=F32)
            tops.append(_top_rows(s, K))
        (s0, i0), (s1, i1) = tops
        cand_s, cand_order, cand_i = _pair_candidates(s0, i0, s1, i1)
        best_s, best_i = _top_rows(cand_s, K, order=cand_order, payload=cand_i)
        e = jnp.exp(best_s - jnp.max(best_s, axis=0, keepdims=True))
        g_ref[h] = e / jnp.sum(e, axis=0, keepdims=True)
        off = best_i * _HALF
        lo = jnp.concatenate([off[0:_HALF], off[SUBLANES:SUBLANES + _HALF]], axis=0)
        hi = jnp.concatenate([off[_HALF:SUBLANES], off[SUBLANES + _HALF:K]], axis=0)
        offs_ref[h] = lo | (hi << 16)


def peer_topk(q, keys):
    N = q.shape[0]
    tm = LANES
    return pl.pallas_call(
        _peer_topk_kernel,
        grid=(N // tm,),
        in_specs=[pl.BlockSpec((tm, q.shape[1]), lambda i: (i, 0)),
                  pl.BlockSpec(keys.shape, lambda i: (0, 0, 0, 0))],
        out_specs=[pl.BlockSpec((PEER_HEADS, PEER_TOPK // 2, tm), lambda i: (0, 0, i)),
                   pl.BlockSpec((PEER_HEADS, PEER_TOPK, tm), lambda i: (0, 0, i))],
        out_shape=[jax.ShapeDtypeStruct((PEER_HEADS, PEER_TOPK // 2, N), jnp.int32),
                   jax.ShapeDtypeStruct((PEER_HEADS, PEER_TOPK, N), F32)],
        compiler_params=_params("parallel"),
        name="peer_topk",
    )(q, keys)


_PEER_TOKENS = 64
_GROUPS = PEER_PAIRS // SUBLANES


def pack_expert_table(w):
    E, D = w.shape
    assert D == SUBLANES * LANES
    bits = lax.bitcast_convert_type(w.astype(BF16), jnp.uint16).astype(jnp.uint32).reshape(E, _HALF, 2, LANES)
    return (bits[:, :, 0] | (bits[:, :, 1] << 16)).reshape(E * _HALF, LANES)


def _pair_position(word, half):
    return SUBLANES * (word // _HALF) + _HALF * half + word % _HALF


def _expert_rows(tab_ref, word):
    starts = (word & 0xFFFF, lax.shift_right_logical(word, 16))
    return [pltpu.bitcast(tab_ref[pl.ds(pl.multiple_of(s, _HALF), _HALF), :], BF16).astype(F32) for s in starts]


def _merge_pair(a, b, shift, first):
    if shift == _HALF:
        return jnp.where(first, a, b) + pltpu.roll(jnp.where(first, b, a), shift, axis=0)
    bs = pltpu.roll(b, shift, axis=0)
    return jnp.where(first, a, bs) + pltpu.roll(jnp.where(first, bs, a), SUBLANES - shift, axis=0)


_MERGE_ORDER = (0, 4, 2, 6, 1, 5, 3, 7)


def _merge8(ps):
    sub = lax.broadcasted_iota(jnp.int32, (SUBLANES, LANES), 0)
    shift = _HALF
    while len(ps) > 1:
        first = (sub % (2 * shift)) < shift
        ps = [_merge_pair(ps[2 * i], ps[2 * i + 1], shift, first) for i in range(len(ps) // 2)]
        shift //= 2
    return ps[0]


def _peer_act_kernel(idx_ref, h_ref, g_ref, tab_ref, o_ref, part_ref):
    tn = h_ref.shape[0]

    def products(t, slot):
        x = h_ref[t]
        for gi in range(_GROUPS):
            rows = []
            for w in range(_HALF):
                rows += _expert_rows(tab_ref, idx_ref[t, gi * _HALF + w])
            prods = [rows[2 * (j % _HALF) + j // _HALF] * x for j in _MERGE_ORDER]
            part_ref[slot, pl.ds(gi * SUBLANES, SUBLANES), :] = _merge8(prods)

    def reduce(t, slot):
        o_ref[pl.ds(t, 1), :] = jnp.sum(part_ref[slot].T, axis=0, keepdims=True)

    part_ref[...] = jnp.zeros(part_ref.shape, F32)

    def two_tokens(i, carry):
        t = 2 * i
        reduce(jnp.maximum(t - 2, 0), 0)
        reduce(jnp.maximum(t - 1, 0), 1)
        products(t, 0)
        products(t + 1, 1)
        return carry

    lax.fori_loop(0, tn // 2, two_tokens, 0)
    reduce(tn - 2, 0)
    reduce(tn - 1, 1)
    o_ref[...] = g_ref[...] * jax.nn.gelu(o_ref[...], approximate=True)


def peer_act(offs, h3, g, table):
    N = offs.shape[0]
    tn = _PEER_TOKENS
    return pl.pallas_call(
        _peer_act_kernel,
        grid=(N // tn,),
        in_specs=[pl.BlockSpec((tn, PEER_PAIRS // 2), lambda i: (i, 0), memory_space=pltpu.SMEM),
                  pl.BlockSpec((tn, SUBLANES, LANES), lambda i: (i, 0, 0)),
                  pl.BlockSpec((tn, PEER_PAIRS), lambda i: (i, 0)),
                  pl.BlockSpec(memory_space=pltpu.VMEM)],
        out_specs=pl.BlockSpec((tn, PEER_PAIRS), lambda i: (i, 0)),
        out_shape=jax.ShapeDtypeStruct((N, PEER_PAIRS), F32),
        scratch_shapes=[pltpu.VMEM((2, PEER_PAIRS, LANES), F32)],
        compiler_params=_params("arbitrary"),
        name="peer_act",
    )(offs, h3, g, table)


_MIX_CHAINS = 4


def _peer_mix_kernel(idx_ref, coef_ref, tab_ref, x_ref, gate_ref, o_ref, cb_ref):
    tn = x_ref.shape[0]

    def spread(t):
        row = coef_ref[pl.ds(t, 1), :]
        return jnp.broadcast_to(row, (PEER_PAIRS, PEER_PAIRS)).T

    cb_ref[...] = spread(0)

    def token(t, carry):
        accs = [None] * _MIX_CHAINS
        for w in range(PEER_PAIRS // 2):
            for k, row in enumerate(_expert_rows(tab_ref, idx_ref[t, w])):
                j = _pair_position(w, k)
                term = jnp.broadcast_to(cb_ref[pl.ds(j, 1), :], (SUBLANES, LANES)) * row
                a = j % _MIX_CHAINS
                accs[a] = term if accs[a] is None else accs[a] + term
        y = (accs[0] + accs[1]) + (accs[2] + accs[3])
        o_ref[t] = x_ref[t] + gate_ref[0] * y
        cb_ref[...] = spread(jnp.minimum(t + 1, tn - 1))
        return carry

    lax.fori_loop(0, tn, token, 0)


def peer_mix(offs, coef, table, x3, gate3, tokens_per_batch):
    N = offs.shape[0]
    tn = _PEER_TOKENS
    assert tokens_per_batch % tn == 0
    if gate3.shape[0] == 1:
        gate_map = lambda i: (0, 0, 0)
    else:
        gate_map = lambda i: ((i * tn) // tokens_per_batch, 0, 0)
    return pl.pallas_call(
        _peer_mix_kernel,
        grid=(N // tn,),
        in_specs=[pl.BlockSpec((tn, PEER_PAIRS // 2), lambda i: (i, 0), memory_space=pltpu.SMEM),
                  pl.BlockSpec((tn, PEER_PAIRS), lambda i: (i, 0)),
                  pl.BlockSpec(memory_space=pltpu.VMEM),
                  pl.BlockSpec((tn, SUBLANES, LANES), lambda i: (i, 0, 0)),
                  pl.BlockSpec((1, SUBLANES, LANES), gate_map)],
        out_specs=pl.BlockSpec((tn, SUBLANES, LANES), lambda i: (i, 0, 0)),
        out_shape=jax.ShapeDtypeStruct((N, SUBLANES, LANES), F32),
        scratch_shapes=[pltpu.VMEM((PEER_PAIRS, LANES), F32)],
        compiler_params=_params("arbitrary"),
        name="peer_mix",
    )(offs, coef, table, x3, gate3)


def _rmsnorm_kernel(x_ref, g_ref, o_ref):
    x = x_ref[0]
    o_ref[0] = x * lax.rsqrt(jnp.mean(x * x, axis=-1, keepdims=True) + EPS) * g_ref[...]


def rmsnorm(x, g):
    B, L, D = x.shape
    tm = min(512, L)
    return pl.pallas_call(
        _rmsnorm_kernel,
        grid=(B, L // tm),
        in_specs=[pl.BlockSpec((1, tm, D), lambda b, i: (b, i, 0)),
                  pl.BlockSpec((1, D), lambda b, i: (0, 0))],
        out_specs=pl.BlockSpec((1, tm, D), lambda b, i: (b, i, 0)),
        out_shape=jax.ShapeDtypeStruct((B, L, D), F32),
        compiler_params=_params("parallel", "parallel"),
        name="final_rmsnorm",
    )(x, g.reshape(1, D))


def _rope_tables(T):
    t = jnp.arange(T, dtype=jnp.int32)
    row = (t // GRID_W).astype(F32)
    col = (t % GRID_W).astype(F32)
    n_freq = HEAD_DIM // 4
    inv_freq = ROPE_THETA ** (-jnp.arange(n_freq, dtype=F32) / n_freq)
    ang = jnp.concatenate([row[:, None] * inv_freq, col[:, None] * inv_freq], axis=-1)
    return jnp.cos(ang), jnp.sin(ang)


def _peer_ffn_residual(x, g2, sc2, sh2, gate, w_q, keys, u_tab, v_tab):
    B, L, D = x.shape
    N = B * L
    q, h = norm_mod_proj(x, g2, sc2, sh2, w_q, (("plain", w_q.shape[1]),), emit_h=True)
    offs_t, g_t = peer_topk(q.reshape(N, -1), keys)
    offs = offs_t.reshape(PEER_PAIRS // 2, N).T
    gates = g_t.reshape(PEER_PAIRS, N).T
    coef = peer_act(offs, h.reshape(N, SUBLANES, LANES), gates, u_tab)
    gate3 = gate.reshape(gate.shape[0], SUBLANES, LANES)
    out = peer_mix(offs, coef, v_tab, x.reshape(N, SUBLANES, LANES), gate3, L)
    return out.reshape(B, L, D)


_Q_SCALE = HEAD_DIM ** -0.5
_IN_OUTS = (("plain", 2 * CONV_CH), ("heads", NA_HEADS, _Q_SCALE), ("heads_norm", GQA_HEADS, _Q_SCALE, 0),
            ("heads", NA_HEADS, 1.0), ("heads", NA_HEADS, 1.0), ("heads_norm", GQA_KV_HEADS, 1.0, 1),
            ("heads", GQA_KV_HEADS, 1.0))


def kernel(x, c, ctx, c_ctx, norm1_g, norm2_g, w_ada, b_ada, w_in, conv_w, conv_b, conv_ln_g, conv_ln_b,
           na_rel_bias, gqa_q_norm, gqa_k_norm, w_out, peer_w_q, peer_keys, peer_u, peer_v, final_norm_g):
    B, T, D = x.shape
    depth = w_in.shape[0]
    rope = _rope_tables(T)
    ada_rows = -(-(B + 1) // SUBLANES) * SUBLANES
    ada_in = jnp.zeros((ada_rows, D), F32).at[:B].set(c).at[B].set(c_ctx)

    for l in range(depth):
        last = l == depth - 1
        mod = ada_mod(ada_in, w_ada[l], b_ada[l])
        sh1, sc1, g1, sh2, sc2, g2 = [m.reshape(B, 1, D) for m in jnp.split(mod[:B], 6, axis=-1)]
        csh1, csc1, cg1, csh2, csc2, cg2 = [m.reshape(1, 1, D) for m in jnp.split(mod[B:B + 1], 6, axis=-1)]
        w_in_b = w_in[l].astype(BF16)
        w_out_b = w_out[l].astype(BF16)
        w_q_b = peer_w_q[l].astype(BF16)
        keys_b = peer_keys[l].astype(BF16)
        u_tab = pack_expert_table(peer_u[l])
        v_tab = pack_expert_table(peer_v[l])
        qk_gains = jnp.stack([gqa_q_norm[l], gqa_k_norm[l]])

        za, na_q, g_q, na_k, na_v, g_k, g_v = norm_mod_proj(x, norm1_g[l], sc1, sh1, w_in_b, _IN_OUTS,
                                                            gains=qk_gains, rope=rope)
        zca, cna_q, cg_q, cna_k, cna_v, cg_k, cg_v = norm_mod_proj(ctx, norm1_g[l], csc1, csh1, w_in_b, _IN_OUTS,
                                                                   gains=qk_gains)
        a = conformer_conv(za, conv_w[l], conv_b[l], conv_ln_g[l], conv_ln_b[l])
        bm = neighbourhood_attention(na_q, na_k, na_v, cna_k, cna_v, na_bias_table(na_rel_bias[l]))
        gm = grouped_attention(g_q, [(g_k, g_v), (cg_k, cg_v)])
        x = proj_residual((a, bm, gm), w_out_b, x, g1)

        x = _peer_ffn_residual(x, norm2_g[l], sc2, sh2, g2, w_q_b, keys_b, u_tab, v_tab)

        if not last:
            ac = conformer_conv(zca, conv_w[l], conv_b[l], conv_ln_g[l], conv_ln_b[l])
            bc = grouped_attention(cna_q, [(cna_k, cna_v)])
            gc = grouped_attention(cg_q, [(cg_k, cg_v)])
            ctx = proj_residual((ac, bc, gc), w_out_b, ctx, cg1)
            ctx = _peer_ffn_residual(ctx, norm2_g[l], csc2, csh2, cg2, w_q_b, keys_b, u_tab, v_tab)
    return rmsnorm(x, final_norm_g)
```

```python
import functools

import jax
import jax.numpy as jnp
from jax import lax
from jax.experimental import pallas as pl
from jax.experimental.pallas import tpu as pltpu

F32 = jnp.float32
BF16 = jnp.bfloat16

GRID_W = 64
HEAD_DIM = 64
CONV_CH = 256
CONV_WIDTH = 31
NA_HEADS = 6
NA_WIN_H = 8
NA_WIN_W = 16
GQA_HEADS = 6
GQA_KV_HEADS = 2
ROPE_THETA = 10000.0
PEER_HEADS = 8
PEER_N_KEYS = 128
PEER_D_KEY = 128
PEER_TOPK = 16
EPS = 1e-6

PEER_PAIRS = PEER_HEADS * PEER_TOPK

LANES = 128
SUBLANES = 8
_HALF = SUBLANES // 2
assert PEER_N_KEYS * PEER_N_KEYS * _HALF <= 1 << 16
VMEM_LIMIT = 48 * 1024 * 1024
MASK_VALUE = -1e30


def _params(*sem):
    return pltpu.CompilerParams(dimension_semantics=sem, vmem_limit_bytes=VMEM_LIMIT)


def _ada_kernel(a_ref, w_ref, b_ref, o_ref):
    a = a_ref[...]
    s = a * jax.nn.sigmoid(a)
    o_ref[...] = jnp.dot(s.astype(BF16), w_ref[...].astype(BF16), preferred_element_type=F32) + b_ref[...]


def ada_mod(a, w, b):
    R, D = a.shape
    N = w.shape[1]
    tn = 1024
    return pl.pallas_call(
        _ada_kernel,
        grid=(N // tn,),
        in_specs=[pl.BlockSpec((R, D), lambda j: (0, 0)),
                  pl.BlockSpec((D, tn), lambda j: (0, j)),
                  pl.BlockSpec((1, tn), lambda j: (0, j))],
        out_specs=pl.BlockSpec((R, tn), lambda j: (0, j)),
        out_shape=jax.ShapeDtypeStruct((R, N), F32),
        compiler_params=_params("parallel"),
        name="ada_mod",
    )(a, w, b.reshape(1, N))


def _head_norm_rope(zh, gain, cos, sin):
    y = zh * lax.rsqrt(jnp.mean(zh * zh, axis=-1, keepdims=True) + EPS) * gain
    if cos is None:
        return y
    half = HEAD_DIM // 2
    x1, x2 = y[:, :half], y[:, half:]
    return jnp.concatenate([x1 * cos - x2 * sin, x1 * sin + x2 * cos], axis=-1)


def _norm_mod_proj_kernel(*refs, outs, emit_h, has_gains, rope):
    x_ref, g_ref, sc_ref, sh_ref, w_ref = refs[:5]
    n_in = 5
    gains_ref = cos = sin = None
    if has_gains:
        gains_ref = refs[n_in]
        n_in += 1
    if rope:
        cos, sin = refs[n_in][...], refs[n_in + 1][...]
        n_in += 2
    out_refs = refs[n_in:]
    x = x_ref[0]
    y = x * lax.rsqrt(jnp.mean(x * x, axis=-1, keepdims=True) + EPS) * g_ref[...]
    h = y * (1.0 + sc_ref[0]) + sh_ref[0]
    z = jnp.dot(h.astype(BF16), w_ref[...], preferred_element_type=F32)
    off = 0
    for o_ref, spec in zip(out_refs, outs):
        if spec[0] == "plain":
            o_ref[0] = z[:, off:off + spec[1]]
            off += spec[1]
            continue
        for hd in range(spec[1]):
            zh = z[:, off:off + HEAD_DIM]
            off += HEAD_DIM
            if spec[0] == "heads_norm":
                zh = _head_norm_rope(zh, gains_ref[spec[3]:spec[3] + 1, :], cos, sin)
            o_ref[0, hd] = (zh * spec[2]).astype(o_ref.dtype)
    if emit_h:
        out_refs[len(outs)][0] = h


def norm_mod_proj(x, g, sc, sh, w, outs, emit_h=False, gains=None, rope=None):
    B, L, D = x.shape
    N = w.shape[1]
    assert sum(o[1] if o[0] == "plain" else o[1] * HEAD_DIM for o in outs) == N
    tm = min(256, L)
    per_batch = sc.shape[0] == B
    mod_map = (lambda b, i: (b, 0, 0)) if per_batch else (lambda b, i: (0, 0, 0))
    in_specs = [pl.BlockSpec((1, tm, D), lambda b, i: (b, i, 0)),
                pl.BlockSpec((1, D), lambda b, i: (0, 0)),
                pl.BlockSpec((1, 1, D), mod_map),
                pl.BlockSpec((1, 1, D), mod_map),
                pl.BlockSpec((D, N), lambda b, i: (0, 0))]
    args = [x, g.reshape(1, D), sc, sh, w]
    if gains is not None:
        in_specs.append(pl.BlockSpec(gains.shape, lambda b, i: (0, 0)))
        args.append(gains)
    if rope is not None:
        for tab in rope:
            in_specs.append(pl.BlockSpec((tm, HEAD_DIM // 2), lambda b, i: (i, 0)))
            args.append(tab)
    out_shape, out_specs = [], []
    for o in outs:
        if o[0] == "plain":
            out_shape.append(jax.ShapeDtypeStruct((B, L, o[1]), F32))
            out_specs.append(pl.BlockSpec((1, tm, o[1]), lambda b, i: (b, i, 0)))
        else:
            out_shape.append(jax.ShapeDtypeStruct((B, o[1], L, HEAD_DIM), BF16))
            out_specs.append(pl.BlockSpec((1, o[1], tm, HEAD_DIM), lambda b, i: (b, 0, i, 0)))
    if emit_h:
        out_shape.append(jax.ShapeDtypeStruct((B, L, D), F32))
        out_specs.append(pl.BlockSpec((1, tm, D), lambda b, i: (b, i, 0)))
    return pl.pallas_call(
        functools.partial(_norm_mod_proj_kernel, outs=outs, emit_h=emit_h, has_gains=gains is not None,
                          rope=rope is not None),
        grid=(B, L // tm),
        in_specs=in_specs,
        out_specs=out_specs,
        out_shape=out_shape,
        compiler_params=_params("parallel", "parallel"),
        name="norm_mod_proj",
    )(*args)


_CONV_PAD = 16


def _conv_kernel(za_ref, w_ref, b_ref, lg_ref, lb_ref, o_ref, upad_ref, *, L, tc):
    c = pl.program_id(1)

    @pl.when(c == 0)
    def _():
        val = za_ref[0, :, :CONV_CH]
        gate = za_ref[0, :, CONV_CH:]
        zeros = jnp.zeros((_CONV_PAD, CONV_CH), F32)
        upad_ref[pl.ds(0, _CONV_PAD), :] = zeros
        upad_ref[pl.ds(_CONV_PAD + L, _CONV_PAD), :] = zeros
        upad_ref[pl.ds(_CONV_PAD, L), :] = val * jax.nn.sigmoid(gate)

    start = pl.multiple_of(c * tc, SUBLANES)
    win = upad_ref[pl.ds(start, tc + 2 * _CONV_PAD), :]
    acc = jnp.zeros((tc, CONV_CH), F32)
    first = _CONV_PAD - CONV_WIDTH // 2
    for j in range(CONV_WIDTH):
        acc = acc + win[first + j:first + j + tc, :] * w_ref[j:j + 1, :]
    y = acc + b_ref[...]
    mu = jnp.mean(y, axis=-1, keepdims=True)
    d = y - mu
    var = jnp.mean(d * d, axis=-1, keepdims=True)
    yn = d * lax.rsqrt(var + EPS) * lg_ref[...] + lb_ref[...]
    o_ref[0] = (yn * jax.nn.sigmoid(yn)).astype(o_ref.dtype)


def conformer_conv(za, w_dw, b_dw, ln_g, ln_b):
    B, L, _ = za.shape
    tc = min(256, L)
    vec = lambda v: v.reshape(1, CONV_CH)
    return pl.pallas_call(
        functools.partial(_conv_kernel, L=L, tc=tc),
        grid=(B, L // tc),
        in_specs=[pl.BlockSpec((1, L, 2 * CONV_CH), lambda b, c: (b, 0, 0)),
                  pl.BlockSpec((CONV_WIDTH, CONV_CH), lambda b, c: (0, 0)),
                  pl.BlockSpec((1, CONV_CH), lambda b, c: (0, 0)),
                  pl.BlockSpec((1, CONV_CH), lambda b, c: (0, 0)),
                  pl.BlockSpec((1, CONV_CH), lambda b, c: (0, 0))],
        out_specs=pl.BlockSpec((1, tc, CONV_CH), lambda b, c: (b, c, 0)),
        out_shape=jax.ShapeDtypeStruct((B, L, CONV_CH), BF16),
        scratch_shapes=[pltpu.VMEM((L + 2 * _CONV_PAD, CONV_CH), F32)],
        compiler_params=_params("parallel", "arbitrary"),
        name="conformer_conv",
    )(za, w_dw, vec(b_dw), vec(ln_g), vec(ln_b))


def _na_row_offset(r, rows):
    return r - jnp.clip(r - NA_WIN_H // 2, 0, rows - NA_WIN_H)


def _na_kernel(q_ref, k_ref, v_ref, kc_ref, vc_ref, bias_ref, o_ref, *, rows):
    r = pl.program_id(1)
    r0 = r - _na_row_offset(r, rows)
    band = pl.ds(pl.multiple_of(r0 * GRID_W, GRID_W), NA_WIN_H * GRID_W)
    nt = (((1,), (1,)), ((), ()))
    for h in range(NA_HEADS):
        q = q_ref[0, h]
        kb = k_ref[0, h, band, :]
        vb = v_ref[0, h, band, :]
        s_loc = lax.dot_general(q, kb, nt, preferred_element_type=F32) + bias_ref[0, h]
        s_ctx = lax.dot_general(q, kc_ref[0, h], nt, preferred_element_type=F32)
        m = jnp.maximum(jnp.max(s_loc, axis=-1, keepdims=True), jnp.max(s_ctx, axis=-1, keepdims=True))
        p_loc = jnp.exp(s_loc - m)
        p_ctx = jnp.exp(s_ctx - m)
        den = jnp.sum(p_loc, axis=-1, keepdims=True) + jnp.sum(p_ctx, axis=-1, keepdims=True)
        o = (jnp.dot(p_loc.astype(BF16), vb, preferred_element_type=F32)
             + jnp.dot(p_ctx.astype(BF16), vc_ref[0, h], preferred_element_type=F32))
        o_ref[0, :, h * HEAD_DIM:(h + 1) * HEAD_DIM] = (o / den).astype(o_ref.dtype)


def na_bias_table(rel_bias):
    cols = jnp.arange(GRID_W, dtype=jnp.int32)
    c0 = jnp.clip(cols - NA_WIN_W // 2, 0, GRID_W - NA_WIN_W)
    in_win = (cols[None, :] >= c0[:, None]) & (cols[None, :] < c0[:, None] + NA_WIN_W)
    col_idx = jnp.clip(cols[None, :] - cols[:, None] + (NA_WIN_W - 1), 0, 2 * NA_WIN_W - 2)
    off = jnp.arange(NA_WIN_H, dtype=jnp.int32)
    row_idx = off[None, :] - off[:, None] + (NA_WIN_H - 1)
    t = rel_bias[:, row_idx]
    t = t[..., col_idx]
    t = jnp.where(in_win[None, None, None], t, MASK_VALUE)
    return t.transpose(1, 0, 3, 2, 4).reshape(NA_WIN_H, NA_HEADS, GRID_W, NA_WIN_H * GRID_W)


def neighbourhood_attention(q, k, v, kc, vc, bias):
    B, H, T, d = q.shape
    C = kc.shape[2]
    rows = T // GRID_W
    assert rows >= NA_WIN_H
    return pl.pallas_call(
        functools.partial(_na_kernel, rows=rows),
        grid=(B, rows),
        in_specs=[pl.BlockSpec((1, H, GRID_W, d), lambda b, r: (b, 0, r, 0)),
                  pl.BlockSpec((1, H, T, d), lambda b, r: (b, 0, 0, 0)),
                  pl.BlockSpec((1, H, T, d), lambda b, r: (b, 0, 0, 0)),
                  pl.BlockSpec((1, H, C, d), lambda b, r: (b, 0, 0, 0)),
                  pl.BlockSpec((1, H, C, d), lambda b, r: (b, 0, 0, 0)),
                  pl.BlockSpec((1, H, GRID_W, NA_WIN_H * GRID_W),
                               lambda b, r: (_na_row_offset(r, rows), 0, 0, 0))],
        out_specs=pl.BlockSpec((1, GRID_W, H * d), lambda b, r: (b, r, 0)),
        out_shape=jax.ShapeDtypeStruct((B, T, H * d), BF16),
        compiler_params=_params("parallel", "arbitrary"),
        name="neighbourhood_attention",
    )(q, k, v, kc, vc, bias)


def _attn_kernel(*refs, Hk, G, bq, n_sets):
    q_ref = refs[0]
    kv_refs = refs[1:1 + 2 * n_sets]
    o_ref = refs[1 + 2 * n_sets]
    nt = (((1,), (1,)), ((), ()))
    for hk in range(Hk):
        q = q_ref[0, hk * G:(hk + 1) * G].reshape(G * bq, HEAD_DIM)
        ss = [lax.dot_general(q, kv_refs[2 * i][0, hk], nt, preferred_element_type=F32) for i in range(n_sets)]
        m = functools.reduce(jnp.maximum, [jnp.max(s, axis=-1, keepdims=True) for s in ss])
        ps = [jnp.exp(s - m) for s in ss]
        den = functools.reduce(jnp.add, [jnp.sum(p, axis=-1, keepdims=True) for p in ps])
        o = functools.reduce(jnp.add, [jnp.dot(p.astype(BF16), kv_refs[2 * i + 1][0, hk], preferred_element_type=F32)
                                       for i, p in enumerate(ps)])
        o = (o / den).astype(o_ref.dtype)
        for g in range(G):
            hd = hk * G + g
            o_ref[0, :, hd * HEAD_DIM:(hd + 1) * HEAD_DIM] = o[g * bq:(g + 1) * bq]


def grouped_attention(q, kv_sets):
    B, H, L, d = q.shape
    Hk = kv_sets[0][0].shape[1]
    bq = min(128, L)
    in_specs = [pl.BlockSpec((1, H, bq, d), lambda b, i: (b, 0, i, 0))]
    args = [q]
    for k, v in kv_sets:
        for t in (k, v):
            in_specs.append(pl.BlockSpec((1, Hk, t.shape[2], d), lambda b, i: (b, 0, 0, 0)))
            args.append(t)
    return pl.pallas_call(
        functools.partial(_attn_kernel, Hk=Hk, G=H // Hk, bq=bq, n_sets=len(kv_sets)),
        grid=(B, L // bq),
        in_specs=in_specs,
        out_specs=pl.BlockSpec((1, bq, H * d), lambda b, i: (b, i, 0)),
        out_shape=jax.ShapeDtypeStruct((B, L, H * d), BF16),
        compiler_params=_params("parallel", "arbitrary"),
        name="grouped_attention",
    )(*args)


def _proj_residual_kernel(a_ref, b_ref, g_ref, wa_ref, wb_ref, wg_ref, x_ref, gate_ref, o_ref):
    y = (jnp.dot(a_ref[0], wa_ref[...], preferred_element_type=F32)
         + jnp.dot(b_ref[0], wb_ref[...], preferred_element_type=F32)
         + jnp.dot(g_ref[0], wg_ref[...], preferred_element_type=F32))
    o_ref[0] = x_ref[0] + gate_ref[0] * y


def proj_residual(parts, w, x, gate):
    B, L, D = x.shape
    tm = min(512, L)
    gate_map = (lambda b, i: (b, 0, 0)) if gate.shape[0] == B else (lambda b, i: (0, 0, 0))
    widths = [p.shape[2] for p in parts]
    assert sum(widths) == w.shape[0]
    starts = [sum(widths[:i]) for i in range(len(parts))]
    ws = [w[s:s + k] for s, k in zip(starts, widths)]
    return pl.pallas_call(
        _proj_residual_kernel,
        grid=(B, L // tm),
        in_specs=[pl.BlockSpec((1, tm, k), lambda b, i: (b, i, 0)) for k in widths]
        + [pl.BlockSpec((k, D), lambda b, i: (0, 0)) for k in widths]
        + [pl.BlockSpec((1, tm, D), lambda b, i: (b, i, 0)), pl.BlockSpec((1, 1, D), gate_map)],
        out_specs=pl.BlockSpec((1, tm, D), lambda b, i: (b, i, 0)),
        out_shape=jax.ShapeDtypeStruct((B, L, D), F32),
        compiler_params=_params("parallel", "parallel"),
        name="proj_residual",
    )(*parts, *ws, x, gate)


def _top_rows(s, k, order=None, payload=None):
    if order is None:
        order = lax.broadcasted_iota(jnp.int32, s.shape, 0)
    after_all = jnp.iinfo(jnp.int32).max
    vals, picks = [], []
    for _ in range(k):
        m = jnp.max(s, axis=0, keepdims=True)
        first = jnp.min(jnp.where(s == m, order, after_all), axis=0, keepdims=True)
        hit = order == first
        vals.append(m)
        if payload is None:
            picks.append(first)
        else:
            picks.append(jnp.max(jnp.where(hit, payload, -1), axis=0, keepdims=True))
        s = jnp.where(hit, -jnp.inf, s)
    return jnp.concatenate(vals, axis=0), jnp.concatenate(picks, axis=0)


def _pair_candidates(s0, i0, s1, i1):
    K = PEER_TOPK
    t = s0.shape[1]
    sub = lax.broadcasted_iota(jnp.int32, (SUBLANES, t), 0)
    scores, order, ids = [], [], []

    def add(a_rows, b_rows, a_of_row, b_of_row):
        sa, ia = a_rows
        sb, ib = b_rows
        ok = (a_of_row + 1) * (b_of_row + 1) <= K
        scores.append(jnp.where(ok, sa + sb, -jnp.inf))
        order.append(a_of_row * K + b_of_row)
        ids.append(ia * PEER_N_KEYS + ib)

    row = lambda x, r: (x[0][r:r + 1], x[1][r:r + 1])
    rows = lambda x, r: (x[0][r:r + SUBLANES], x[1][r:r + SUBLANES])
    A, Bv = (s0, i0), (s1, i1)
    add(row(A, 0), rows(Bv, 0), jnp.zeros_like(sub), sub)
    add(row(A, 0), rows(Bv, SUBLANES), jnp.zeros_like(sub), sub + SUBLANES)
    for a in range(1, 4):
        add(row(A, a), rows(Bv, 0), jnp.full_like(sub, a), sub)
    for b in range(3):
        dup = sub < 4
        sa, ia = rows(A, 0)
        add((jnp.where(dup, -jnp.inf, sa), ia), row(Bv, b), sub, jnp.full_like(sub, b))
    add(rows(A, SUBLANES), row(Bv, 0), sub + SUBLANES, jnp.zeros_like(sub))
    cat = lambda xs: jnp.concatenate(xs, axis=0)
    return cat(scores), cat(order), cat(ids)


def _peer_topk_kernel(q_ref, keys_ref, offs_ref, g_ref):
    K = PEER_TOPK
    assert K == 2 * SUBLANES
    nt = (((1,), (1,)), ((), ()))
    words, gates = [], []
    for h in range(PEER_HEADS):
        tops = []
        for p in range(2):
            col = (2 * h + p) * PEER_D_KEY
            qhp = q_ref[:, col:col + PEER_D_KEY].astype(BF16)
            s = lax.dot_general(keys_ref[h, p], qhp, nt, preferred_element_type=F32)
            tops.append(_top_rows(s, K))
        (s0, i0), (s1, i1) = tops
        cand_s, cand_order, cand_i = _pair_candidates(s0, i0, s1, i1)
        best_s, best_i = _top_rows(cand_s, K, order=cand_order, payload=cand_i)
        e = jnp.exp(best_s - jnp.max(best_s, axis=0, keepdims=True))
        gates.append(e / jnp.sum(e, axis=0, keepdims=True))
        off = best_i * _HALF
        lo = jnp.concatenate([off[0:_HALF], off[SUBLANES:SUBLANES + _HALF]], axis=0)
        hi = jnp.concatenate([off[_HALF:SUBLANES], off[SUBLANES + _HALF:K]], axis=0)
        words.append(lo | (hi << 16))
    g_ref[...] = jnp.concatenate(gates, axis=0).T
    words.append(jnp.zeros((PEER_PAIRS // 2, q_ref.shape[0]), jnp.int32))
    offs_ref[...] = jnp.concatenate(words, axis=0).T[:, :PEER_PAIRS // 2]


def peer_topk(q, keys):
    N = q.shape[0]
    tm = LANES
    return pl.pallas_call(
        _peer_topk_kernel,
        grid=(N // tm,),
        in_specs=[pl.BlockSpec((tm, q.shape[1]), lambda i: (i, 0)),
                  pl.BlockSpec(keys.shape, lambda i: (0, 0, 0, 0))],
        out_specs=[pl.BlockSpec((tm, PEER_PAIRS // 2), lambda i: (i, 0)),
                   pl.BlockSpec((tm, PEER_PAIRS), lambda i: (i, 0))],
        out_shape=[jax.ShapeDtypeStruct((N, PEER_PAIRS // 2), jnp.int32),
                   jax.ShapeDtypeStruct((N, PEER_PAIRS), F32)],
        compiler_params=_params("parallel"),
        name="peer_topk",
    )(q, keys)


_PEER_TOKENS = 64
_GROUPS = PEER_PAIRS // SUBLANES


def pack_expert_table(w):
    E, D = w.shape
    assert D == SUBLANES * LANES
    bits = lax.bitcast_convert_type(w.astype(BF16), jnp.uint16).astype(jnp.uint32).reshape(E, _HALF, 2, LANES)
    return (bits[:, :, 0] | (bits[:, :, 1] << 16)).reshape(E * _HALF, LANES)


def _pair_position(word, half):
    return SUBLANES * (word // _HALF) + _HALF * half + word % _HALF


def _expert_rows(tab_ref, word):
    starts = (word & 0xFFFF, lax.shift_right_logical(word, 16))
    return [pltpu.bitcast(tab_ref[pl.ds(pl.multiple_of(s, _HALF), _HALF), :], BF16).astype(F32) for s in starts]


def _token_row(ref, t):
    row = ref[pl.ds(t, 1), :]
    return jnp.concatenate([row[:, s * LANES:(s + 1) * LANES] for s in range(SUBLANES)], axis=0)


def _merge_pair(a, b, shift, first):
    if shift == _HALF:
        return jnp.where(first, a, b) + pltpu.roll(jnp.where(first, b, a), shift, axis=0)
    bs = pltpu.roll(b, shift, axis=0)
    return jnp.where(first, a, bs) + pltpu.roll(jnp.where(first, bs, a), SUBLANES - shift, axis=0)


_MERGE_ORDER = (0, 4, 2, 6, 1, 5, 3, 7)


def _merge8(ps):
    sub = lax.broadcasted_iota(jnp.int32, (SUBLANES, LANES), 0)
    shift = _HALF
    while len(ps) > 1:
        first = (sub % (2 * shift)) < shift
        ps = [_merge_pair(ps[2 * i], ps[2 * i + 1], shift, first) for i in range(len(ps) // 2)]
        shift //= 2
    return ps[0]


def _peer_act_kernel(idx_ref, h_ref, g_ref, tab_ref, o_ref, part_ref):
    tn = h_ref.shape[0]

    def products(t, slot):
        x = _token_row(h_ref, t)
        for gi in range(_GROUPS):
            rows = []
            for w in range(_HALF):
                rows += _expert_rows(tab_ref, idx_ref[t, gi * _HALF + w])
            prods = [rows[2 * (j % _HALF) + j // _HALF] * x for j in _MERGE_ORDER]
            part_ref[slot, pl.ds(gi * SUBLANES, SUBLANES), :] = _merge8(prods)

    def reduce(t, slot):
        o_ref[pl.ds(t, 1), :] = jnp.sum(part_ref[slot].T, axis=0, keepdims=True)

    part_ref[...] = jnp.zeros(part_ref.shape, F32)

    def two_tokens(i, carry):
        t = 2 * i
        reduce(jnp.maximum(t - 2, 0), 0)
        reduce(jnp.maximum(t - 1, 0), 1)
        products(t, 0)
        products(t + 1, 1)
        return carry

    lax.fori_loop(0, tn // 2, two_tokens, 0)
    reduce(tn - 2, 0)
    reduce(tn - 1, 1)
    o_ref[...] = g_ref[...] * jax.nn.gelu(o_ref[...], approximate=True)


def peer_act(offs, h, g, table):
    N = offs.shape[0]
    tn = _PEER_TOKENS
    return pl.pallas_call(
        _peer_act_kernel,
        grid=(N // tn,),
        in_specs=[pl.BlockSpec((tn, PEER_PAIRS // 2), lambda i: (i, 0), memory_space=pltpu.SMEM),
                  pl.BlockSpec((tn, SUBLANES * LANES), lambda i: (i, 0)),
                  pl.BlockSpec((tn, PEER_PAIRS), lambda i: (i, 0)),
                  pl.BlockSpec(memory_space=pltpu.VMEM)],
        out_specs=pl.BlockSpec((tn, PEER_PAIRS), lambda i: (i, 0)),
        out_shape=jax.ShapeDtypeStruct((N, PEER_PAIRS), F32),
        scratch_shapes=[pltpu.VMEM((2, PEER_PAIRS, LANES), F32)],
        compiler_params=_params("arbitrary"),
        name="peer_act",
    )(offs, h, g, table)


_MIX_CHAINS = 4


def _peer_mix_kernel(idx_ref, coef_ref, tab_ref, x_ref, gate_ref, o_ref, cb_ref):
    tn = x_ref.shape[0]

    def spread(t):
        row = coef_ref[pl.ds(t, 1), :]
        return jnp.broadcast_to(row, (PEER_PAIRS, PEER_PAIRS)).T

    cb_ref[...] = spread(0)

    def token(t, carry):
        accs = [None] * _MIX_CHAINS
        for w in range(PEER_PAIRS // 2):
            for k, row in enumerate(_expert_rows(tab_ref, idx_ref[t, w])):
                j = _pair_position(w, k)
                term = jnp.broadcast_to(cb_ref[pl.ds(j, 1), :], (SUBLANES, LANES)) * row
                a = j % _MIX_CHAINS
                accs[a] = term if accs[a] is None else accs[a] + term
        y = (accs[0] + accs[1]) + (accs[2] + accs[3])
        y_row = jnp.concatenate([y[s:s + 1] for s in range(SUBLANES)], axis=1)
        o_ref[pl.ds(t, 1), :] = x_ref[pl.ds(t, 1), :] + gate_ref[0] * y_row
        cb_ref[...] = spread(jnp.minimum(t + 1, tn - 1))
        return carry

    lax.fori_loop(0, tn, token, 0)


def peer_mix(offs, coef, table, x, gate, tokens_per_batch):
    N, D = x.shape
    tn = _PEER_TOKENS
    assert tokens_per_batch % tn == 0
    if gate.shape[0] == 1:
        gate_map = lambda i: (0, 0, 0)
    else:
        gate_map = lambda i: ((i * tn) // tokens_per_batch, 0, 0)
    return pl.pallas_call(
        _peer_mix_kernel,
        grid=(N // tn,),
        in_specs=[pl.BlockSpec((tn, PEER_PAIRS // 2), lambda i: (i, 0), memory_space=pltpu.SMEM),
                  pl.BlockSpec((tn, PEER_PAIRS), lambda i: (i, 0)),
                  pl.BlockSpec(memory_space=pltpu.VMEM),
                  pl.BlockSpec((tn, D), lambda i: (i, 0)),
                  pl.BlockSpec((1, 1, D), gate_map)],
        out_specs=pl.BlockSpec((tn, D), lambda i: (i, 0)),
        out_shape=jax.ShapeDtypeStruct((N, D), F32),
        scratch_shapes=[pltpu.VMEM((PEER_PAIRS, LANES), F32)],
        compiler_params=_params("arbitrary"),
        name="peer_mix",
    )(offs, coef, table, x, gate)


def _rmsnorm_kernel(x_ref, g_ref, o_ref):
    x = x_ref[0]
    o_ref[0] = x * lax.rsqrt(jnp.mean(x * x, axis=-1, keepdims=True) + EPS) * g_ref[...]


def rmsnorm(x, g):
    B, L, D = x.shape
    tm = min(512, L)
    return pl.pallas_call(
        _rmsnorm_kernel,
        grid=(B, L // tm),
        in_specs=[pl.BlockSpec((1, tm, D), lambda b, i: (b, i, 0)),
                  pl.BlockSpec((1, D), lambda b, i: (0, 0))],
        out_specs=pl.BlockSpec((1, tm, D), lambda b, i: (b, i, 0)),
        out_shape=jax.ShapeDtypeStruct((B, L, D), F32),
        compiler_params=_params("parallel", "parallel"),
        name="final_rmsnorm",
    )(x, g.reshape(1, D))


def _rope_tables(T):
    t = jnp.arange(T, dtype=jnp.int32)
    row = (t // GRID_W).astype(F32)
    col = (t % GRID_W).astype(F32)
    n_freq = HEAD_DIM // 4
    inv_freq = ROPE_THETA ** (-jnp.arange(n_freq, dtype=F32) / n_freq)
    ang = jnp.concatenate([row[:, None] * inv_freq, col[:, None] * inv_freq], axis=-1)
    return jnp.cos(ang), jnp.sin(ang)


def _peer_ffn_residual(x, g2, sc2, sh2, gate, w_q, keys, u_tab, v_tab):
    B, L, D = x.shape
    N = B * L
    q, h = norm_mod_proj(x, g2, sc2, sh2, w_q, (("plain", w_q.shape[1]),), emit_h=True)
    offs, gates = peer_topk(q.reshape(N, -1), keys)
    coef = peer_act(offs, h.reshape(N, D), gates, u_tab)
    out = peer_mix(offs, coef, v_tab, x.reshape(N, D), gate, L)
    return out.reshape(B, L, D)


_Q_SCALE = HEAD_DIM ** -0.5
_IN_OUTS = (("plain", 2 * CONV_CH), ("heads", NA_HEADS, _Q_SCALE), ("heads_norm", GQA_HEADS, _Q_SCALE, 0),
            ("heads", NA_HEADS, 1.0), ("heads", NA_HEADS, 1.0), ("heads_norm", GQA_KV_HEADS, 1.0, 1),
            ("heads", GQA_KV_HEADS, 1.0))


def kernel(x, c, ctx, c_ctx, norm1_g, norm2_g, w_ada, b_ada, w_in, conv_w, conv_b, conv_ln_g, conv_ln_b,
           na_rel_bias, gqa_q_norm, gqa_k_norm, w_out, peer_w_q, peer_keys, peer_u, peer_v, final_norm_g):
    B, T, D = x.shape
    depth = w_in.shape[0]
    rope = _rope_tables(T)
    ada_rows = -(-(B + 1) // SUBLANES) * SUBLANES
    ada_in = jnp.zeros((ada_rows, D), F32).at[:B].set(c).at[B].set(c_ctx)

    for l in range(depth):
        last = l == depth - 1
        mod = ada_mod(ada_in, w_ada[l], b_ada[l])
        sh1, sc1, g1, sh2, sc2, g2 = [m.reshape(B, 1, D) for m in jnp.split(mod[:B], 6, axis=-1)]
        csh1, csc1, cg1, csh2, csc2, cg2 = [m.reshape(1, 1, D) for m in jnp.split(mod[B:B + 1], 6, axis=-1)]
        w_in_b = w_in[l].astype(BF16)
        w_out_b = w_out[l].astype(BF16)
        w_q_b = peer_w_q[l].astype(BF16)
        keys_b = peer_keys[l].astype(BF16)
        u_tab = pack_expert_table(peer_u[l])
        v_tab = pack_expert_table(peer_v[l])
        qk_gains = jnp.stack([gqa_q_norm[l], gqa_k_norm[l]])

        za, na_q, g_q, na_k, na_v, g_k, g_v = norm_mod_proj(x, norm1_g[l], sc1, sh1, w_in_b, _IN_OUTS,
                                                            gains=qk_gains, rope=rope)
        zca, cna_q, cg_q, cna_k, cna_v, cg_k, cg_v = norm_mod_proj(ctx, norm1_g[l], csc1, csh1, w_in_b, _IN_OUTS,
                                                                   gains=qk_gains)
        a = conformer_conv(za, conv_w[l], conv_b[l], conv_ln_g[l], conv_ln_b[l])
        bm = neighbourhood_attention(na_q, na_k, na_v, cna_k, cna_v, na_bias_table(na_rel_bias[l]))
        gm = grouped_attention(g_q, [(g_k, g_v), (cg_k, cg_v)])
        x = proj_residual((a, bm, gm), w_out_b, x, g1)

        x = _peer_ffn_residual(x, norm2_g[l], sc2, sh2, g2, w_q_b, keys_b, u_tab, v_tab)

        if not last:
            ac = conformer_conv(zca, conv_w[l], conv_b[l], conv_ln_g[l], conv_ln_b[l])
            bc = grouped_attention(cna_q, [(cna_k, cna_v)])
            gc = grouped_attention(cg_q, [(cg_k, cg_v)])
            ctx = proj_residual((ac, bc, gc), w_out_b, ctx, cg1)
            ctx = _peer_ffn_residual(ctx, norm2_g[l], csc2, csh2, cg2, w_q_b, keys_b, u_tab, v_tab)
    return rmsnorm(x, final_norm_g)
```

```python
import functools

import jax
import jax.numpy as jnp
from jax import lax
from jax.experimental import pallas as pl
from jax.experimental.pallas import tpu as pltpu

F32 = jnp.float32
BF16 = jnp.bfloat16

GRID_W = 64
HEAD_DIM = 64
CONV_CH = 256
CONV_WIDTH = 31
NA_HEADS = 6
NA_WIN_H = 8
NA_WIN_W = 16
GQA_HEADS = 6
GQA_KV_HEADS = 2
ROPE_THETA = 10000.0
PEER_HEADS = 8
PEER_N_KEYS = 128
PEER_D_KEY = 128
PEER_TOPK = 16
EPS = 1e-6

PEER_PAIRS = PEER_HEADS * PEER_TOPK

LANES = 128
SUBLANES = 8
_HALF = SUBLANES // 2
assert PEER_N_KEYS * PEER_N_KEYS * _HALF <= 1 << 16
VMEM_LIMIT = 48 * 1024 * 1024
MASK_VALUE = -1e30


def _params(*sem):
    return pltpu.CompilerParams(dimension_semantics=sem, vmem_limit_bytes=VMEM_LIMIT)


def _ada_kernel(a_ref, w_ref, b_ref, o_ref):
    a = a_ref[...]
    s = a * jax.nn.sigmoid(a)
    o_ref[...] = jnp.dot(s.astype(BF16), w_ref[...].astype(BF16), preferred_element_type=F32) + b_ref[...]


def ada_mod(a, w, b):
    R, D = a.shape
    N = w.shape[1]
    tn = 1024
    return pl.pallas_call(
        _ada_kernel,
        grid=(N // tn,),
        in_specs=[pl.BlockSpec((R, D), lambda j: (0, 0)),
                  pl.BlockSpec((D, tn), lambda j: (0, j)),
                  pl.BlockSpec((1, tn), lambda j: (0, j))],
        out_specs=pl.BlockSpec((R, tn), lambda j: (0, j)),
        out_shape=jax.ShapeDtypeStruct((R, N), F32),
        compiler_params=_params("parallel"),
        name="ada_mod",
    )(a, w, b.reshape(1, N))


def _head_norm_rope(zh, gain, cos, sin):
    y = zh * lax.rsqrt(jnp.mean(zh * zh, axis=-1, keepdims=True) + EPS) * gain
    if cos is None:
        return y
    half = HEAD_DIM // 2
    x1, x2 = y[:, :half], y[:, half:]
    return jnp.concatenate([x1 * cos - x2 * sin, x1 * sin + x2 * cos], axis=-1)


def _norm_mod_proj_kernel(*refs, outs, emit_h, has_gains, rope):
    x_ref, g_ref, sc_ref, sh_ref, w_ref = refs[:5]
    n_in = 5
    gains_ref = cos = sin = None
    if has_gains:
        gains_ref = refs[n_in]
        n_in += 1
    if rope:
        cos, sin = refs[n_in][...], refs[n_in + 1][...]
        n_in += 2
    out_refs = refs[n_in:]
    x = x_ref[0]
    y = x * lax.rsqrt(jnp.mean(x * x, axis=-1, keepdims=True) + EPS) * g_ref[...]
    h = y * (1.0 + sc_ref[0]) + sh_ref[0]
    z = jnp.dot(h.astype(BF16), w_ref[...], preferred_element_type=F32)
    off = 0
    for o_ref, spec in zip(out_refs, outs):
        if spec[0] == "plain":
            o_ref[0] = z[:, off:off + spec[1]]
            off += spec[1]
            continue
        for hd in range(spec[1]):
            zh = z[:, off:off + HEAD_DIM]
            off += HEAD_DIM
            if spec[0] == "heads_norm":
                zh = _head_norm_rope(zh, gains_ref[spec[3]:spec[3] + 1, :], cos, sin)
            o_ref[0, hd] = (zh * spec[2]).astype(o_ref.dtype)
    if emit_h:
        out_refs[len(outs)][0] = h


def norm_mod_proj(x, g, sc, sh, w, outs, emit_h=False, gains=None, rope=None):
    B, L, D = x.shape
    N = w.shape[1]
    assert sum(o[1] if o[0] == "plain" else o[1] * HEAD_DIM for o in outs) == N
    tm = min(256, L)
    per_batch = sc.shape[0] == B
    mod_map = (lambda b, i: (b, 0, 0)) if per_batch else (lambda b, i: (0, 0, 0))
    in_specs = [pl.BlockSpec((1, tm, D), lambda b, i: (b, i, 0)),
                pl.BlockSpec((1, D), lambda b, i: (0, 0)),
                pl.BlockSpec((1, 1, D), mod_map),
                pl.BlockSpec((1, 1, D), mod_map),
                pl.BlockSpec((D, N), lambda b, i: (0, 0))]
    args = [x, g.reshape(1, D), sc, sh, w]
    if gains is not None:
        in_specs.append(pl.BlockSpec(gains.shape, lambda b, i: (0, 0)))
        args.append(gains)
    if rope is not None:
        for tab in rope:
            in_specs.append(pl.BlockSpec((tm, HEAD_DIM // 2), lambda b, i: (i, 0)))
            args.append(tab)
    out_shape, out_specs = [], []
    for o in outs:
        if o[0] == "plain":
            out_shape.append(jax.ShapeDtypeStruct((B, L, o[1]), F32))
            out_specs.append(pl.BlockSpec((1, tm, o[1]), lambda b, i: (b, i, 0)))
        else:
            out_shape.append(jax.ShapeDtypeStruct((B, o[1], L, HEAD_DIM), BF16))
            out_specs.append(pl.BlockSpec((1, o[1], tm, HEAD_DIM), lambda b, i: (b, 0, i, 0)))
    if emit_h:
        out_shape.append(jax.ShapeDtypeStruct((B, L, D), F32))
        out_specs.append(pl.BlockSpec((1, tm, D), lambda b, i: (b, i, 0)))
    return pl.pallas_call(
        functools.partial(_norm_mod_proj_kernel, outs=outs, emit_h=emit_h, has_gains=gains is not None,
                          rope=rope is not None),
        grid=(B, L // tm),
        in_specs=in_specs,
        out_specs=out_specs,
        out_shape=out_shape,
        compiler_params=_params("parallel", "parallel"),
        name="norm_mod_proj",
    )(*args)


_CONV_PAD = 16


def _conv_kernel(za_ref, w_ref, b_ref, lg_ref, lb_ref, o_ref, upad_ref, *, L, tc):
    c = pl.program_id(1)

    @pl.when(c == 0)
    def _():
        val = za_ref[0, :, :CONV_CH]
        gate = za_ref[0, :, CONV_CH:]
        zeros = jnp.zeros((_CONV_PAD, CONV_CH), F32)
        upad_ref[pl.ds(0, _CONV_PAD), :] = zeros
        upad_ref[pl.ds(_CONV_PAD + L, _CONV_PAD), :] = zeros
        upad_ref[pl.ds(_CONV_PAD, L), :] = val * jax.nn.sigmoid(gate)

    start = pl.multiple_of(c * tc, SUBLANES)
    win = upad_ref[pl.ds(start, tc + 2 * _CONV_PAD), :]
    acc = jnp.zeros((tc, CONV_CH), F32)
    first = _CONV_PAD - CONV_WIDTH // 2
    for j in range(CONV_WIDTH):
        acc = acc + win[first + j:first + j + tc, :] * w_ref[j:j + 1, :]
    y = acc + b_ref[...]
    mu = jnp.mean(y, axis=-1, keepdims=True)
    d = y - mu
    var = jnp.mean(d * d, axis=-1, keepdims=True)
    yn = d * lax.rsqrt(var + EPS) * lg_ref[...] + lb_ref[...]
    o_ref[0] = (yn * jax.nn.sigmoid(yn)).astype(o_ref.dtype)


def conformer_conv(za, w_dw, b_dw, ln_g, ln_b):
    B, L, _ = za.shape
    tc = min(256, L)
    vec = lambda v: v.reshape(1, CONV_CH)
    return pl.pallas_call(
        functools.partial(_conv_kernel, L=L, tc=tc),
        grid=(B, L // tc),
        in_specs=[pl.BlockSpec((1, L, 2 * CONV_CH), lambda b, c: (b, 0, 0)),
                  pl.BlockSpec((CONV_WIDTH, CONV_CH), lambda b, c: (0, 0)),
                  pl.BlockSpec((1, CONV_CH), lambda b, c: (0, 0)),
                  pl.BlockSpec((1, CONV_CH), lambda b, c: (0, 0)),
                  pl.BlockSpec((1, CONV_CH), lambda b, c: (0, 0))],
        out_specs=pl.BlockSpec((1, tc, CONV_CH), lambda b, c: (b, c, 0)),
        out_shape=jax.ShapeDtypeStruct((B, L, CONV_CH), BF16),
        scratch_shapes=[pltpu.VMEM((L + 2 * _CONV_PAD, CONV_CH), F32)],
        compiler_params=_params("parallel", "arbitrary"),
        name="conformer_conv",
    )(za, w_dw, vec(b_dw), vec(ln_g), vec(ln_b))


def _na_row_offset(r, rows):
    return r - jnp.clip(r - NA_WIN_H // 2, 0, rows - NA_WIN_H)


def _na_kernel(q_ref, k_ref, v_ref, kc_ref, vc_ref, bias_ref, o_ref, *, rows):
    r = pl.program_id(1)
    r0 = r - _na_row_offset(r, rows)
    band = pl.ds(pl.multiple_of(r0 * GRID_W, GRID_W), NA_WIN_H * GRID_W)
    nt = (((1,), (1,)), ((), ()))
    heads = range(NA_HEADS)
    s_loc = [lax.dot_general(q_ref[0, h], k_ref[0, h, band, :], nt, preferred_element_type=F32) + bias_ref[0, h]
             for h in heads]
    s_ctx = [lax.dot_general(q_ref[0, h], kc_ref[0, h], nt, preferred_element_type=F32) for h in heads]
    p_loc, p_ctx, den = [], [], []
    for h in heads:
        m = jnp.maximum(jnp.max(s_loc[h], axis=-1, keepdims=True), jnp.max(s_ctx[h], axis=-1, keepdims=True))
        p_loc.append(jnp.exp(s_loc[h] - m))
        p_ctx.append(jnp.exp(s_ctx[h] - m))
        den.append(jnp.sum(p_loc[h], axis=-1, keepdims=True) + jnp.sum(p_ctx[h], axis=-1, keepdims=True))
    outs = []
    for h in heads:
        o = (jnp.dot(p_loc[h].astype(BF16), v_ref[0, h, band, :], preferred_element_type=F32)
             + jnp.dot(p_ctx[h].astype(BF16), vc_ref[0, h], preferred_element_type=F32))
        outs.append((o / den[h]).astype(o_ref.dtype))
    o_ref[0] = jnp.concatenate(outs, axis=-1)


def na_bias_table(rel_bias):
    cols = jnp.arange(GRID_W, dtype=jnp.int32)
    c0 = jnp.clip(cols - NA_WIN_W // 2, 0, GRID_W - NA_WIN_W)
    in_win = (cols[None, :] >= c0[:, None]) & (cols[None, :] < c0[:, None] + NA_WIN_W)
    col_idx = jnp.clip(cols[None, :] - cols[:, None] + (NA_WIN_W - 1), 0, 2 * NA_WIN_W - 2)
    off = jnp.arange(NA_WIN_H, dtype=jnp.int32)
    row_idx = off[None, :] - off[:, None] + (NA_WIN_H - 1)
    t = rel_bias[:, row_idx]
    t = t[..., col_idx]
    t = jnp.where(in_win[None, None, None], t, MASK_VALUE)
    return t.transpose(1, 0, 3, 2, 4).reshape(NA_WIN_H, NA_HEADS, GRID_W, NA_WIN_H * GRID_W)


def neighbourhood_attention(q, k, v, kc, vc, bias):
    B, H, T, d = q.shape
    C = kc.shape[2]
    rows = T // GRID_W
    assert rows >= NA_WIN_H
    return pl.pallas_call(
        functools.partial(_na_kernel, rows=rows),
        grid=(B, rows),
        in_specs=[pl.BlockSpec((1, H, GRID_W, d), lambda b, r: (b, 0, r, 0)),
                  pl.BlockSpec((1, H, T, d), lambda b, r: (b, 0, 0, 0)),
                  pl.BlockSpec((1, H, T, d), lambda b, r: (b, 0, 0, 0)),
                  pl.BlockSpec((1, H, C, d), lambda b, r: (b, 0, 0, 0)),
                  pl.BlockSpec((1, H, C, d), lambda b, r: (b, 0, 0, 0)),
                  pl.BlockSpec((1, H, GRID_W, NA_WIN_H * GRID_W),
                               lambda b, r: (_na_row_offset(r, rows), 0, 0, 0))],
        out_specs=pl.BlockSpec((1, GRID_W, H * d), lambda b, r: (b, r, 0)),
        out_shape=jax.ShapeDtypeStruct((B, T, H * d), BF16),
        compiler_params=_params("parallel", "arbitrary"),
        name="neighbourhood_attention",
    )(q, k, v, kc, vc, bias)


def _attn_kernel(*refs, Hk, G, bq, n_sets):
    q_ref = refs[0]
    kv_refs = refs[1:1 + 2 * n_sets]
    o_ref = refs[1 + 2 * n_sets]
    nt = (((1,), (1,)), ((), ()))
    sets = range(n_sets)
    qs = [q_ref[0, hk * G:(hk + 1) * G].reshape(G * bq, HEAD_DIM) for hk in range(Hk)]
    ss = [[lax.dot_general(qs[hk], kv_refs[2 * i][0, hk], nt, preferred_element_type=F32) for i in sets]
          for hk in range(Hk)]
    ps, dens = [], []
    for hk in range(Hk):
        m = functools.reduce(jnp.maximum, [jnp.max(s, axis=-1, keepdims=True) for s in ss[hk]])
        ps.append([jnp.exp(s - m) for s in ss[hk]])
        dens.append(functools.reduce(jnp.add, [jnp.sum(p, axis=-1, keepdims=True) for p in ps[hk]]))
    outs = []
    for hk in range(Hk):
        o = functools.reduce(jnp.add, [jnp.dot(ps[hk][i].astype(BF16), kv_refs[2 * i + 1][0, hk],
                                               preferred_element_type=F32) for i in sets])
        o = (o / dens[hk]).astype(o_ref.dtype)
        outs += [o[g * bq:(g + 1) * bq] for g in range(G)]
    o_ref[0] = jnp.concatenate(outs, axis=-1)


def grouped_attention(q, kv_sets):
    B, H, L, d = q.shape
    Hk = kv_sets[0][0].shape[1]
    bq = min(128, L)
    in_specs = [pl.BlockSpec((1, H, bq, d), lambda b, i: (b, 0, i, 0))]
    args = [q]
    for k, v in kv_sets:
        for t in (k, v):
            in_specs.append(pl.BlockSpec((1, Hk, t.shape[2], d), lambda b, i: (b, 0, 0, 0)))
            args.append(t)
    return pl.pallas_call(
        functools.partial(_attn_kernel, Hk=Hk, G=H // Hk, bq=bq, n_sets=len(kv_sets)),
        grid=(B, L // bq),
        in_specs=in_specs,
        out_specs=pl.BlockSpec((1, bq, H * d), lambda b, i: (b, i, 0)),
        out_shape=jax.ShapeDtypeStruct((B, L, H * d), BF16),
        compiler_params=_params("parallel", "arbitrary"),
        name="grouped_attention",
    )(*args)


def _proj_residual_kernel(a_ref, b_ref, g_ref, wa_ref, wb_ref, wg_ref, x_ref, gate_ref, o_ref):
    y = (jnp.dot(a_ref[0], wa_ref[...], preferred_element_type=F32)
         + jnp.dot(b_ref[0], wb_ref[...], preferred_element_type=F32)
         + jnp.dot(g_ref[0], wg_ref[...], preferred_element_type=F32))
    o_ref[0] = x_ref[0] + gate_ref[0] * y


def proj_residual(parts, w, x, gate):
    B, L, D = x.shape
    tm = min(512, L)
    gate_map = (lambda b, i: (b, 0, 0)) if gate.shape[0] == B else (lambda b, i: (0, 0, 0))
    widths = [p.shape[2] for p in parts]
    assert sum(widths) == w.shape[0]
    starts = [sum(widths[:i]) for i in range(len(parts))]
    ws = [w[s:s + k] for s, k in zip(starts, widths)]
    return pl.pallas_call(
        _proj_residual_kernel,
        grid=(B, L // tm),
        in_specs=[pl.BlockSpec((1, tm, k), lambda b, i: (b, i, 0)) for k in widths]
        + [pl.BlockSpec((k, D), lambda b, i: (0, 0)) for k in widths]
        + [pl.BlockSpec((1, tm, D), lambda b, i: (b, i, 0)), pl.BlockSpec((1, 1, D), gate_map)],
        out_specs=pl.BlockSpec((1, tm, D), lambda b, i: (b, i, 0)),
        out_shape=jax.ShapeDtypeStruct((B, L, D), F32),
        compiler_params=_params("parallel", "parallel"),
        name="proj_residual",
    )(*parts, *ws, x, gate)


def _top_rows(s, k, order=None, payload=None):
    if order is None:
        order = lax.broadcasted_iota(jnp.int32, s.shape, 0)
    after_all = jnp.iinfo(jnp.int32).max
    vals, picks = [], []
    for _ in range(k):
        m = jnp.max(s, axis=0, keepdims=True)
        first = jnp.min(jnp.where(s == m, order, after_all), axis=0, keepdims=True)
        hit = order == first
        vals.append(m)
        if payload is None:
            picks.append(first)
        else:
            picks.append(jnp.max(jnp.where(hit, payload, -1), axis=0, keepdims=True))
        s = jnp.where(hit, -jnp.inf, s)
    return jnp.concatenate(vals, axis=0), jnp.concatenate(picks, axis=0)


def _pair_candidates(s0, i0, s1, i1):
    K = PEER_TOPK
    t = s0.shape[1]
    sub = lax.broadcasted_iota(jnp.int32, (SUBLANES, t), 0)
    scores, order, ids = [], [], []

    def add(a_rows, b_rows, a_of_row, b_of_row):
        sa, ia = a_rows
        sb, ib = b_rows
        ok = (a_of_row + 1) * (b_of_row + 1) <= K
        scores.append(jnp.where(ok, sa + sb, -jnp.inf))
        order.append(a_of_row * K + b_of_row)
        ids.append(ia * PEER_N_KEYS + ib)

    row = lambda x, r: (x[0][r:r + 1], x[1][r:r + 1])
    rows = lambda x, r: (x[0][r:r + SUBLANES], x[1][r:r + SUBLANES])
    A, Bv = (s0, i0), (s1, i1)
    add(row(A, 0), rows(Bv, 0), jnp.zeros_like(sub), sub)
    add(row(A, 0), rows(Bv, SUBLANES), jnp.zeros_like(sub), sub + SUBLANES)
    for a in range(1, 4):
        add(row(A, a), rows(Bv, 0), jnp.full_like(sub, a), sub)
    for b in range(3):
        dup = sub < 4
        sa, ia = rows(A, 0)
        add((jnp.where(dup, -jnp.inf, sa), ia), row(Bv, b), sub, jnp.full_like(sub, b))
    add(rows(A, SUBLANES), row(Bv, 0), sub + SUBLANES, jnp.zeros_like(sub))
    cat = lambda xs: jnp.concatenate(xs, axis=0)
    return cat(scores), cat(order), cat(ids)


def _peer_topk_kernel(q_ref, keys_ref, offs_ref, g_ref):
    K = PEER_TOPK
    assert K == 2 * SUBLANES
    nt = (((1,), (1,)), ((), ()))
    words, gates = [], []
    for h in range(PEER_HEADS):
        tops = []
        for p in range(2):
            col = (2 * h + p) * PEER_D_KEY
            qhp = q_ref[:, col:col + PEER_D_KEY].astype(BF16)
            s = lax.dot_general(keys_ref[h, p], qhp, nt, preferred_element_type=F32)
            tops.append(_top_rows(s, K))
        (s0, i0), (s1, i1) = tops
        cand_s, cand_order, cand_i = _pair_candidates(s0, i0, s1, i1)
        best_s, best_i = _top_rows(cand_s, K, order=cand_order, payload=cand_i)
        e = jnp.exp(best_s - jnp.max(best_s, axis=0, keepdims=True))
        gates.append(e / jnp.sum(e, axis=0, keepdims=True))
        off = best_i * _HALF
        lo = jnp.concatenate([off[0:_HALF], off[SUBLANES:SUBLANES + _HALF]], axis=0)
        hi = jnp.concatenate([off[_HALF:SUBLANES], off[SUBLANES + _HALF:K]], axis=0)
        words.append(lo | (hi << 16))
    g_ref[...] = jnp.concatenate(gates, axis=0).T
    words.append(jnp.zeros((PEER_PAIRS // 2, q_ref.shape[0]), jnp.int32))
    offs_ref[...] = jnp.concatenate(words, axis=0).T[:, :PEER_PAIRS // 2]


def peer_topk(q, keys):
    N = q.shape[0]
    tm = LANES
    return pl.pallas_call(
        _peer_topk_kernel,
        grid=(N // tm,),
        in_specs=[pl.BlockSpec((tm, q.shape[1]), lambda i: (i, 0)),
                  pl.BlockSpec(keys.shape, lambda i: (0, 0, 0, 0))],
        out_specs=[pl.BlockSpec((tm, PEER_PAIRS // 2), lambda i: (i, 0)),
                   pl.BlockSpec((tm, PEER_PAIRS), lambda i: (i, 0))],
        out_shape=[jax.ShapeDtypeStruct((N, PEER_PAIRS // 2), jnp.int32),
                   jax.ShapeDtypeStruct((N, PEER_PAIRS), F32)],
        compiler_params=_params("parallel"),
        name="peer_topk",
    )(q, keys)


_PEER_TOKENS = 64
_GROUPS = PEER_PAIRS // SUBLANES


def pack_expert_table(w):
    E, D = w.shape
    assert D == SUBLANES * LANES
    bits = lax.bitcast_convert_type(w.astype(BF16), jnp.uint16).astype(jnp.uint32).reshape(E * _HALF, 2, LANES)
    return bits[:, 0] | (bits[:, 1] << 16)


def _pair_position(word, half):
    return SUBLANES * (word // _HALF) + _HALF * half + word % _HALF


def _expert_rows(tab_ref, word):
    starts = (word & 0xFFFF, lax.shift_right_logical(word, 16))
    return [pltpu.bitcast(tab_ref[pl.ds(pl.multiple_of(s, _HALF), _HALF), :], BF16).astype(F32) for s in starts]


def _token_row(ref, t):
    row = ref[pl.ds(t, 1), :]
    return jnp.concatenate([row[:, s * LANES:(s + 1) * LANES] for s in range(SUBLANES)], axis=0)


def _merge_pair(a, b, shift, first):
    if shift == _HALF:
        return jnp.where(first, a, b) + pltpu.roll(jnp.where(first, b, a), shift, axis=0)
    bs = pltpu.roll(b, shift, axis=0)
    return jnp.where(first, a, bs) + pltpu.roll(jnp.where(first, bs, a), SUBLANES - shift, axis=0)


_MERGE_ORDER = (0, 4, 2, 6, 1, 5, 3, 7)


def _merge8(ps):
    sub = lax.broadcasted_iota(jnp.int32, (SUBLANES, LANES), 0)
    shift = _HALF
    while len(ps) > 1:
        first = (sub % (2 * shift)) < shift
        ps = [_merge_pair(ps[2 * i], ps[2 * i + 1], shift, first) for i in range(len(ps) // 2)]
        shift //= 2
    return ps[0]


def _peer_act_kernel(idx_ref, h_ref, g_ref, tab_ref, o_ref, part_ref):
    tn = h_ref.shape[0]

    def products(t, slot):
        x = _token_row(h_ref, t)
        for gi in range(_GROUPS):
            rows = []
            for w in range(_HALF):
                rows += _expert_rows(tab_ref, idx_ref[t, gi * _HALF + w])
            prods = [rows[2 * (j % _HALF) + j // _HALF] * x for j in _MERGE_ORDER]
            part_ref[slot, pl.ds(gi * SUBLANES, SUBLANES), :] = _merge8(prods)

    def reduce(t, slot):
        o_ref[pl.ds(t, 1), :] = jnp.sum(part_ref[slot].T, axis=0, keepdims=True)

    part_ref[...] = jnp.zeros(part_ref.shape, F32)

    def two_tokens(i, carry):
        t = 2 * i
        reduce(jnp.maximum(t - 2, 0), 0)
        reduce(jnp.maximum(t - 1, 0), 1)
        products(t, 0)
        products(t + 1, 1)
        return carry

    lax.fori_loop(0, tn // 2, two_tokens, 0)
    reduce(tn - 2, 0)
    reduce(tn - 1, 1)
    o_ref[...] = g_ref[...] * jax.nn.gelu(o_ref[...], approximate=True)


def peer_act(offs, h, g, table):
    N = offs.shape[0]
    tn = _PEER_TOKENS
    return pl.pallas_call(
        _peer_act_kernel,
        grid=(N // tn,),
        in_specs=[pl.BlockSpec((tn, PEER_PAIRS // 2), lambda i: (i, 0), memory_space=pltpu.SMEM),
                  pl.BlockSpec((tn, SUBLANES * LANES), lambda i: (i, 0)),
                  pl.BlockSpec((tn, PEER_PAIRS), lambda i: (i, 0)),
                  pl.BlockSpec(memory_space=pltpu.VMEM)],
        out_specs=pl.BlockSpec((tn, PEER_PAIRS), lambda i: (i, 0)),
        out_shape=jax.ShapeDtypeStruct((N, PEER_PAIRS), F32),
        scratch_shapes=[pltpu.VMEM((2, PEER_PAIRS, LANES), F32)],
        compiler_params=_params("arbitrary"),
        name="peer_act",
    )(offs, h, g, table)


_MIX_CHAINS = 4


def _peer_mix_kernel(idx_ref, coef_ref, tab_ref, x_ref, gate_ref, o_ref, cb_ref):
    tn = x_ref.shape[0]

    def spread(t):
        row = coef_ref[pl.ds(t, 1), :]
        return jnp.broadcast_to(row, (PEER_PAIRS, PEER_PAIRS)).T

    cb_ref[...] = spread(0)

    def token(t, carry):
        accs = [None] * _MIX_CHAINS
        for w in range(PEER_PAIRS // 2):
            for k, row in enumerate(_expert_rows(tab_ref, idx_ref[t, w])):
                j = _pair_position(w, k)
                term = jnp.broadcast_to(cb_ref[pl.ds(j, 1), :], (SUBLANES, LANES)) * row
                a = j % _MIX_CHAINS
                accs[a] = term if accs[a] is None else accs[a] + term
        y = (accs[0] + accs[1]) + (accs[2] + accs[3])
        y_row = jnp.concatenate([y[s:s + 1] for s in range(SUBLANES)], axis=1)
        o_ref[pl.ds(t, 1), :] = x_ref[pl.ds(t, 1), :] + gate_ref[0] * y_row
        cb_ref[...] = spread(jnp.minimum(t + 1, tn - 1))
        return carry

    lax.fori_loop(0, tn, token, 0)


def peer_mix(offs, coef, table, x, gate, tokens_per_batch):
    N, D = x.shape
    tn = _PEER_TOKENS
    assert tokens_per_batch % tn == 0
    if gate.shape[0] == 1:
        gate_map = lambda i: (0, 0, 0)
    else:
        gate_map = lambda i: ((i * tn) // tokens_per_batch, 0, 0)
    return pl.pallas_call(
        _peer_mix_kernel,
        grid=(N // tn,),
        in_specs=[pl.BlockSpec((tn, PEER_PAIRS // 2), lambda i: (i, 0), memory_space=pltpu.SMEM),
                  pl.BlockSpec((tn, PEER_PAIRS), lambda i: (i, 0)),
                  pl.BlockSpec(memory_space=pltpu.VMEM),
                  pl.BlockSpec((tn, D), lambda i: (i, 0)),
                  pl.BlockSpec((1, 1, D), gate_map)],
        out_specs=pl.BlockSpec((tn, D), lambda i: (i, 0)),
        out_shape=jax.ShapeDtypeStruct((N, D), F32),
        scratch_shapes=[pltpu.VMEM((PEER_PAIRS, LANES), F32)],
        compiler_params=_params("arbitrary"),
        name="peer_mix",
    )(offs, coef, table, x, gate)


def _rmsnorm_kernel(x_ref, g_ref, o_ref):
    x = x_ref[0]
    o_ref[0] = x * lax.rsqrt(jnp.mean(x * x, axis=-1, keepdims=True) + EPS) * g_ref[...]


def rmsnorm(x, g):
    B, L, D = x.shape
    tm = min(512, L)
    return pl.pallas_call(
        _rmsnorm_kernel,
        grid=(B, L // tm),
        in_specs=[pl.BlockSpec((1, tm, D), lambda b, i: (b, i, 0)),
                  pl.BlockSpec((1, D), lambda b, i: (0, 0))],
        out_specs=pl.BlockSpec((1, tm, D), lambda b, i: (b, i, 0)),
        out_shape=jax.ShapeDtypeStruct((B, L, D), F32),
        compiler_params=_params("parallel", "parallel"),
        name="final_rmsnorm",
    )(x, g.reshape(1, D))


def _rope_tables(T):
    t = jnp.arange(T, dtype=jnp.int32)
    row = (t // GRID_W).astype(F32)
    col = (t % GRID_W).astype(F32)
    n_freq = HEAD_DIM // 4
    inv_freq = ROPE_THETA ** (-jnp.arange(n_freq, dtype=F32) / n_freq)
    ang = jnp.concatenate([row[:, None] * inv_freq, col[:, None] * inv_freq], axis=-1)
    return jnp.cos(ang), jnp.sin(ang)


def _peer_ffn_residual(x, g2, sc2, sh2, gate, w_q, keys, u_tab, v_tab):
    B, L, D = x.shape
    N = B * L
    q, h = norm_mod_proj(x, g2, sc2, sh2, w_q, (("plain", w_q.shape[1]),), emit_h=True)
    offs, gates = peer_topk(q.reshape(N, -1), keys)
    coef = peer_act(offs, h.reshape(N, D), gates, u_tab)
    out = peer_mix(offs, coef, v_tab, x.reshape(N, D), gate, L)
    return out.reshape(B, L, D)


_Q_SCALE = HEAD_DIM ** -0.5
_IN_OUTS = (("plain", 2 * CONV_CH), ("heads", NA_HEADS, _Q_SCALE), ("heads_norm", GQA_HEADS, _Q_SCALE, 0),
            ("heads", NA_HEADS, 1.0), ("heads", NA_HEADS, 1.0), ("heads_norm", GQA_KV_HEADS, 1.0, 1),
            ("heads", GQA_KV_HEADS, 1.0))


def kernel(x, c, ctx, c_ctx, norm1_g, norm2_g, w_ada, b_ada, w_in, conv_w, conv_b, conv_ln_g, conv_ln_b,
           na_rel_bias, gqa_q_norm, gqa_k_norm, w_out, peer_w_q, peer_keys, peer_u, peer_v, final_norm_g):
    B, T, D = x.shape
    depth = w_in.shape[0]
    rope = _rope_tables(T)
    ada_rows = -(-(B + 1) // SUBLANES) * SUBLANES
    ada_in = jnp.zeros((ada_rows, D), F32).at[:B].set(c).at[B].set(c_ctx)

    for l in range(depth):
        last = l == depth - 1
        mod = ada_mod(ada_in, w_ada[l], b_ada[l])
        sh1, sc1, g1, sh2, sc2, g2 = [m.reshape(B, 1, D) for m in jnp.split(mod[:B], 6, axis=-1)]
        csh1, csc1, cg1, csh2, csc2, cg2 = [m.reshape(1, 1, D) for m in jnp.split(mod[B:B + 1], 6, axis=-1)]
        w_in_b = w_in[l].astype(BF16)
        w_out_b = w_out[l].astype(BF16)
        w_q_b = peer_w_q[l].astype(BF16)
        keys_b = peer_keys[l].astype(BF16)
        u_tab = pack_expert_table(peer_u[l])
        v_tab = pack_expert_table(peer_v[l])
        qk_gains = jnp.stack([gqa_q_norm[l], gqa_k_norm[l]])

        za, na_q, g_q, na_k, na_v, g_k, g_v = norm_mod_proj(x, norm1_g[l], sc1, sh1, w_in_b, _IN_OUTS,
                                                            gains=qk_gains, rope=rope)
        zca, cna_q, cg_q, cna_k, cna_v, cg_k, cg_v = norm_mod_proj(ctx, norm1_g[l], csc1, csh1, w_in_b, _IN_OUTS,
                                                                   gains=qk_gains)
        a = conformer_conv(za, conv_w[l], conv_b[l], conv_ln_g[l], conv_ln_b[l])
        bm = neighbourhood_attention(na_q, na_k, na_v, cna_k, cna_v, na_bias_table(na_rel_bias[l]))
        gm = grouped_attention(g_q, [(g_k, g_v), (cg_k, cg_v)])
        x = proj_residual((a, bm, gm), w_out_b, x, g1)

        x = _peer_ffn_residual(x, norm2_g[l], sc2, sh2, g2, w_q_b, keys_b, u_tab, v_tab)

        if not last:
            ac = conformer_conv(zca, conv_w[l], conv_b[l], conv_ln_g[l], conv_ln_b[l])
            bc = grouped_attention(cna_q, [(cna_k, cna_v)])
            gc = grouped_attention(cg_q, [(cg_k, cg_v)])
            ctx = proj_residual((ac, bc, gc), w_out_b, ctx, cg1)
            ctx = _peer_ffn_residual(ctx, norm2_g[l], csc2, csh2, cg2, w_q_b, keys_b, u_tab, v_tab)
    return rmsnorm(x, final_norm_g)
```

```python
import functools

import jax
import jax.numpy as jnp
from jax import lax
from jax.experimental import pallas as pl
from jax.experimental.pallas import tpu as pltpu

F32 = jnp.float32
BF16 = jnp.bfloat16

GRID_W = 64
HEAD_DIM = 64
CONV_CH = 256
CONV_WIDTH = 31
NA_HEADS = 6
NA_WIN_H = 8
NA_WIN_W = 16
GQA_HEADS = 6
GQA_KV_HEADS = 2
ROPE_THETA = 10000.0
PEER_HEADS = 8
PEER_N_KEYS = 128
PEER_D_KEY = 128
PEER_TOPK = 16
EPS = 1e-6

PEER_PAIRS = PEER_HEADS * PEER_TOPK

LANES = 128
SUBLANES = 8
_HALF = SUBLANES // 2
assert PEER_N_KEYS * PEER_N_KEYS * _HALF <= 1 << 16
VMEM_LIMIT = 48 * 1024 * 1024
MASK_VALUE = -1e30


def _params(*sem):
    return pltpu.CompilerParams(dimension_semantics=sem, vmem_limit_bytes=VMEM_LIMIT)


def _ada_kernel(a_ref, w_ref, b_ref, o_ref):
    a = a_ref[...]
    s = a * jax.nn.sigmoid(a)
    o_ref[...] = jnp.dot(s.astype(BF16), w_ref[...].astype(BF16), preferred_element_type=F32) + b_ref[...]


def ada_mod(a, w, b):
    R, D = a.shape
    N = w.shape[1]
    tn = 1024
    return pl.pallas_call(
        _ada_kernel,
        grid=(N // tn,),
        in_specs=[pl.BlockSpec((R, D), lambda j: (0, 0)),
                  pl.BlockSpec((D, tn), lambda j: (0, j)),
                  pl.BlockSpec((1, tn), lambda j: (0, j))],
        out_specs=pl.BlockSpec((R, tn), lambda j: (0, j)),
        out_shape=jax.ShapeDtypeStruct((R, N), F32),
        compiler_params=_params("parallel"),
        name="ada_mod",
    )(a, w, b.reshape(1, N))


def _pair_norm_rope(x, gain, rope_cos, rope_sin):
    lane = lax.broadcasted_iota(jnp.int32, x.shape, 1)
    lower = lane < HEAD_DIM
    sq = x * x
    ms_lo = jnp.sum(jnp.where(lower, sq, 0.0), axis=-1, keepdims=True) * (1.0 / HEAD_DIM)
    ms_hi = jnp.sum(jnp.where(lower, 0.0, sq), axis=-1, keepdims=True) * (1.0 / HEAD_DIM)
    y = x * jnp.where(lower, lax.rsqrt(ms_lo + EPS), lax.rsqrt(ms_hi + EPS)) * gain
    if rope_cos is None:
        return y
    half = HEAD_DIM // 2
    first = (lane % HEAD_DIM) < half
    partner = jnp.where(first, pltpu.roll(y, LANES - half, axis=1), pltpu.roll(y, half, axis=1))
    return y * rope_cos + partner * rope_sin


def _norm_mod_proj_kernel(*refs, outs, emit_h, has_gains, rope):
    x_ref, g_ref, sc_ref, sh_ref, w_ref = refs[:5]
    n_in = 5
    gains_ref = cos = sin = None
    if has_gains:
        gains_ref = refs[n_in]
        n_in += 1
    if rope:
        cos, sin = refs[n_in][...], refs[n_in + 1][...]
        n_in += 2
    out_refs = refs[n_in:]
    x = x_ref[0]
    y = x * lax.rsqrt(jnp.mean(x * x, axis=-1, keepdims=True) + EPS) * g_ref[...]
    h = y * (1.0 + sc_ref[0]) + sh_ref[0]
    z = jnp.dot(h.astype(BF16), w_ref[...], preferred_element_type=F32)
    off = 0
    for o_ref, spec in zip(out_refs, outs):
        if spec[0] == "f32":
            o_ref[0] = z[:, off:off + spec[1]]
            off += spec[1]
        elif spec[0] == "bf16":
            o_ref[0] = (z[:, off:off + spec[1]] * spec[2]).astype(BF16)
            off += spec[1]
        else:
            groups = []
            for _ in range(spec[1]):
                v = _pair_norm_rope(z[:, off:off + LANES], gains_ref[spec[3]:spec[3] + 1, :], cos, sin)
                groups.append((v * spec[2]).astype(BF16))
                off += LANES
            o_ref[0] = groups[0] if len(groups) == 1 else jnp.concatenate(groups, axis=-1)
    if emit_h:
        out_refs[len(outs)][0] = h


def _out_width(spec):
    return spec[1] * LANES if spec[0] == "norm" else spec[1]


def norm_mod_proj(x, g, sc, sh, w, outs, emit_h=False, gains=None, rope=None):
    B, L, D = x.shape
    N = w.shape[1]
    assert sum(_out_width(o) for o in outs) == N
    tm = min(256, L)
    per_batch = sc.shape[0] == B
    mod_map = (lambda b, i: (b, 0, 0)) if per_batch else (lambda b, i: (0, 0, 0))
    in_specs = [pl.BlockSpec((1, tm, D), lambda b, i: (b, i, 0)),
                pl.BlockSpec((1, D), lambda b, i: (0, 0)),
                pl.BlockSpec((1, 1, D), mod_map),
                pl.BlockSpec((1, 1, D), mod_map),
                pl.BlockSpec((D, N), lambda b, i: (0, 0))]
    args = [x, g.reshape(1, D), sc, sh, w]
    if gains is not None:
        in_specs.append(pl.BlockSpec(gains.shape, lambda b, i: (0, 0)))
        args.append(gains)
    if rope is not None:
        for tab in rope:
            in_specs.append(pl.BlockSpec((tm, LANES), lambda b, i: (i, 0)))
            args.append(tab)
    out_shape = [jax.ShapeDtypeStruct((B, L, _out_width(o)), F32 if o[0] == "f32" else BF16) for o in outs]
    out_specs = [pl.BlockSpec((1, tm, _out_width(o)), lambda b, i: (b, i, 0)) for o in outs]
    if emit_h:
        out_shape.append(jax.ShapeDtypeStruct((B, L, D), F32))
        out_specs.append(pl.BlockSpec((1, tm, D), lambda b, i: (b, i, 0)))
    return pl.pallas_call(
        functools.partial(_norm_mod_proj_kernel, outs=outs, emit_h=emit_h, has_gains=gains is not None,
                          rope=rope is not None),
        grid=(B, L // tm),
        in_specs=in_specs,
        out_specs=out_specs,
        out_shape=out_shape,
        compiler_params=_params("parallel", "parallel"),
        name="norm_mod_proj",
    )(*args)


_CONV_PAD = 16


def _conv_kernel(za_ref, w_ref, b_ref, lg_ref, lb_ref, o_ref, upad_ref, *, L, tc):
    c = pl.program_id(1)

    @pl.when(c == 0)
    def _():
        val = za_ref[0, :, :CONV_CH]
        gate = za_ref[0, :, CONV_CH:]
        zeros = jnp.zeros((_CONV_PAD, CONV_CH), F32)
        upad_ref[pl.ds(0, _CONV_PAD), :] = zeros
        upad_ref[pl.ds(_CONV_PAD + L, _CONV_PAD), :] = zeros
        upad_ref[pl.ds(_CONV_PAD, L), :] = val * jax.nn.sigmoid(gate)

    start = pl.multiple_of(c * tc, SUBLANES)
    win = upad_ref[pl.ds(start, tc + 2 * _CONV_PAD), :]
    acc = jnp.zeros((tc, CONV_CH), F32)
    first = _CONV_PAD - CONV_WIDTH // 2
    for j in range(CONV_WIDTH):
        acc = acc + win[first + j:first + j + tc, :] * w_ref[j:j + 1, :]
    y = acc + b_ref[...]
    mu = jnp.mean(y, axis=-1, keepdims=True)
    d = y - mu
    var = jnp.mean(d * d, axis=-1, keepdims=True)
    yn = d * lax.rsqrt(var + EPS) * lg_ref[...] + lb_ref[...]
    o_ref[0] = (yn * jax.nn.sigmoid(yn)).astype(o_ref.dtype)


def conformer_conv(za, w_dw, b_dw, ln_g, ln_b):
    B, L, _ = za.shape
    tc = min(256, L)
    vec = lambda v: v.reshape(1, CONV_CH)
    return pl.pallas_call(
        functools.partial(_conv_kernel, L=L, tc=tc),
        grid=(B, L // tc),
        in_specs=[pl.BlockSpec((1, L, 2 * CONV_CH), lambda b, c: (b, 0, 0)),
                  pl.BlockSpec((CONV_WIDTH, CONV_CH), lambda b, c: (0, 0)),
                  pl.BlockSpec((1, CONV_CH), lambda b, c: (0, 0)),
                  pl.BlockSpec((1, CONV_CH), lambda b, c: (0, 0)),
                  pl.BlockSpec((1, CONV_CH), lambda b, c: (0, 0))],
        out_specs=pl.BlockSpec((1, tc, CONV_CH), lambda b, c: (b, c, 0)),
        out_shape=jax.ShapeDtypeStruct((B, L, CONV_CH), BF16),
        scratch_shapes=[pltpu.VMEM((L + 2 * _CONV_PAD, CONV_CH), F32)],
        compiler_params=_params("parallel", "arbitrary"),
        name="conformer_conv",
    )(za, w_dw, vec(b_dw), vec(ln_g), vec(ln_b))


def _na_row_offset(r, rows):
    return r - jnp.clip(r - NA_WIN_H // 2, 0, rows - NA_WIN_H)


def _head_pair_rows(qg, lower):
    zero = jnp.zeros_like(qg)
    return jnp.concatenate([jnp.where(lower, qg, zero), jnp.where(lower, zero, qg)], axis=0)


def _na_kernel(q_ref, k_ref, v_ref, kc_ref, vc_ref, bias_ref, o_ref, *, rows):
    r = pl.program_id(1)
    r0 = r - _na_row_offset(r, rows)
    band = pl.ds(pl.multiple_of(r0 * GRID_W, GRID_W), NA_WIN_H * GRID_W)
    nt = (((1,), (1,)), ((), ()))
    lower = lax.broadcasted_iota(jnp.int32, (GRID_W, LANES), 1) < HEAD_DIM
    groups = range(NA_HEADS // 2)
    cols = [slice(g * LANES, (g + 1) * LANES) for g in groups]
    qs = [_head_pair_rows(q_ref[0, :, cols[g]], lower) for g in groups]
    s_loc = [lax.dot_general(qs[g], k_ref[0, band, cols[g]], nt, preferred_element_type=F32) + bias_ref[0, g]
             for g in groups]
    s_ctx = [lax.dot_general(qs[g], kc_ref[0, :, cols[g]], nt, preferred_element_type=F32) for g in groups]
    p_loc, p_ctx, den = [], [], []
    for g in groups:
        m = jnp.maximum(jnp.max(s_loc[g], axis=-1, keepdims=True), jnp.max(s_ctx[g], axis=-1, keepdims=True))
        p_loc.append(jnp.exp(s_loc[g] - m))
        p_ctx.append(jnp.exp(s_ctx[g] - m))
        den.append(jnp.sum(p_loc[g], axis=-1, keepdims=True) + jnp.sum(p_ctx[g], axis=-1, keepdims=True))
    outs = []
    for g in groups:
        o = (jnp.dot(p_loc[g].astype(BF16), v_ref[0, band, cols[g]], preferred_element_type=F32)
             + jnp.dot(p_ctx[g].astype(BF16), vc_ref[0, :, cols[g]], preferred_element_type=F32)) / den[g]
        outs.append(jnp.where(lower, o[:GRID_W], o[GRID_W:]).astype(o_ref.dtype))
    o_ref[0] = jnp.concatenate(outs, axis=-1)


def na_bias_table(rel_bias):
    cols = jnp.arange(GRID_W, dtype=jnp.int32)
    c0 = jnp.clip(cols - NA_WIN_W // 2, 0, GRID_W - NA_WIN_W)
    in_win = (cols[None, :] >= c0[:, None]) & (cols[None, :] < c0[:, None] + NA_WIN_W)
    col_idx = jnp.clip(cols[None, :] - cols[:, None] + (NA_WIN_W - 1), 0, 2 * NA_WIN_W - 2)
    off = jnp.arange(NA_WIN_H, dtype=jnp.int32)
    row_idx = off[None, :] - off[:, None] + (NA_WIN_H - 1)
    t = rel_bias[:, row_idx]
    t = t[..., col_idx]
    t = jnp.where(in_win[None, None, None], t, MASK_VALUE)
    return t.transpose(1, 0, 3, 2, 4).reshape(NA_WIN_H, NA_HEADS // 2, 2 * GRID_W, NA_WIN_H * GRID_W)


def neighbourhood_attention(q, k, v, kc, vc, bias):
    B, T, W = q.shape
    C = kc.shape[1]
    rows = T // GRID_W
    assert rows >= NA_WIN_H and W == NA_HEADS * HEAD_DIM
    return pl.pallas_call(
        functools.partial(_na_kernel, rows=rows),
        grid=(B, rows),
        in_specs=[pl.BlockSpec((1, GRID_W, W), lambda b, r: (b, r, 0)),
                  pl.BlockSpec((1, T, W), lambda b, r: (b, 0, 0)),
                  pl.BlockSpec((1, T, W), lambda b, r: (b, 0, 0)),
                  pl.BlockSpec((1, C, W), lambda b, r: (b, 0, 0)),
                  pl.BlockSpec((1, C, W), lambda b, r: (b, 0, 0)),
                  pl.BlockSpec((1,) + bias.shape[1:], lambda b, r: (_na_row_offset(r, rows), 0, 0, 0))],
        out_specs=pl.BlockSpec((1, GRID_W, W), lambda b, r: (b, r, 0)),
        out_shape=jax.ShapeDtypeStruct((B, T, W), BF16),
        compiler_params=_params("parallel", "arbitrary"),
        name="neighbourhood_attention",
    )(q, k, v, kc, vc, bias)


def _pair_attn_kernel(*refs, Gq, Gk, bq, n_sets):
    q_ref = refs[0]
    kv_refs = refs[1:1 + 2 * n_sets]
    o_ref = refs[1 + 2 * n_sets]
    nt = (((1,), (1,)), ((), ()))
    sets = range(n_sets)
    lower = lax.broadcasted_iota(jnp.int32, (bq, LANES), 1) < HEAD_DIM
    col = lambda g: slice(g * LANES, (g + 1) * LANES)
    pairs = [_head_pair_rows(q_ref[0, :, col(g)], lower) for g in range(Gq)]
    if Gk == 1:
        jobs = [(jnp.concatenate([p[:bq] for p in pairs], axis=0), col(0)),
                (jnp.concatenate([p[bq:] for p in pairs], axis=0), col(0))]
    else:
        jobs = [(pairs[g], col(g)) for g in range(Gq)]
    ss = [[lax.dot_general(rows, kv_refs[2 * i][0, :, kc], nt, preferred_element_type=F32) for i in sets]
          for rows, kc in jobs]
    ps, dens = [], []
    for s in ss:
        m = functools.reduce(jnp.maximum, [jnp.max(x, axis=-1, keepdims=True) for x in s])
        ps.append([jnp.exp(x - m) for x in s])
        dens.append(functools.reduce(jnp.add, [jnp.sum(p, axis=-1, keepdims=True) for p in ps[-1]]))
    os_ = []
    for (rows, kc), p, den in zip(jobs, ps, dens):
        o = functools.reduce(jnp.add, [jnp.dot(p[i].astype(BF16), kv_refs[2 * i + 1][0, :, kc],
                                               preferred_element_type=F32) for i in sets])
        os_.append(o / den)
    outs = []
    for g in range(Gq):
        if Gk == 1:
            first, second = os_[0][g * bq:(g + 1) * bq], os_[1][g * bq:(g + 1) * bq]
        else:
            first, second = os_[g][:bq], os_[g][bq:]
        outs.append(jnp.where(lower, first, second).astype(o_ref.dtype))
    o_ref[0] = jnp.concatenate(outs, axis=-1)


def pair_attention(q, kv_sets):
    B, L, Wq = q.shape
    Wk = kv_sets[0][0].shape[2]
    Gq, Gk = Wq // LANES, Wk // LANES
    assert Gk in (1, Gq)
    bq = min(128, L)
    in_specs = [pl.BlockSpec((1, bq, Wq), lambda b, i: (b, i, 0))]
    args = [q]
    for k, v in kv_sets:
        for t in (k, v):
            in_specs.append(pl.BlockSpec((1, t.shape[1], Wk), lambda b, i: (b, 0, 0)))
            args.append(t)
    return pl.pallas_call(
        functools.partial(_pair_attn_kernel, Gq=Gq, Gk=Gk, bq=bq, n_sets=len(kv_sets)),
        grid=(B, L // bq),
        in_specs=in_specs,
        out_specs=pl.BlockSpec((1, bq, Wq), lambda b, i: (b, i, 0)),
        out_shape=jax.ShapeDtypeStruct((B, L, Wq), BF16),
        compiler_params=_params("parallel", "arbitrary"),
        name="pair_attention",
    )(*args)


def _proj_residual_kernel(a_ref, b_ref, g_ref, wa_ref, wb_ref, wg_ref, x_ref, gate_ref, o_ref):
    y = (jnp.dot(a_ref[0], wa_ref[...], preferred_element_type=F32)
         + jnp.dot(b_ref[0], wb_ref[...], preferred_element_type=F32)
         + jnp.dot(g_ref[0], wg_ref[...], preferred_element_type=F32))
    o_ref[0] = x_ref[0] + gate_ref[0] * y


def proj_residual(parts, w, x, gate):
    B, L, D = x.shape
    tm = min(512, L)
    gate_map = (lambda b, i: (b, 0, 0)) if gate.shape[0] == B else (lambda b, i: (0, 0, 0))
    widths = [p.shape[2] for p in parts]
    assert sum(widths) == w.shape[0]
    starts = [sum(widths[:i]) for i in range(len(parts))]
    ws = [w[s:s + k] for s, k in zip(starts, widths)]
    return pl.pallas_call(
        _proj_residual_kernel,
        grid=(B, L // tm),
        in_specs=[pl.BlockSpec((1, tm, k), lambda b, i: (b, i, 0)) for k in widths]
        + [pl.BlockSpec((k, D), lambda b, i: (0, 0)) for k in widths]
        + [pl.BlockSpec((1, tm, D), lambda b, i: (b, i, 0)), pl.BlockSpec((1, 1, D), gate_map)],
        out_specs=pl.BlockSpec((1, tm, D), lambda b, i: (b, i, 0)),
        out_shape=jax.ShapeDtypeStruct((B, L, D), F32),
        compiler_params=_params("parallel", "parallel"),
        name="proj_residual",
    )(*parts, *ws, x, gate)


def _top_rows(s, k, order=None, payload=None):
    if order is None:
        order = lax.broadcasted_iota(jnp.int32, s.shape, 0)
    after_all = jnp.iinfo(jnp.int32).max
    vals, picks = [], []
    for _ in range(k):
        m = jnp.max(s, axis=0, keepdims=True)
        first = jnp.min(jnp.where(s == m, order, after_all), axis=0, keepdims=True)
        hit = order == first
        vals.append(m)
        if payload is None:
            picks.append(first)
        else:
            picks.append(jnp.max(jnp.where(hit, payload, -1), axis=0, keepdims=True))
        s = jnp.where(hit, -jnp.inf, s)
    return jnp.concatenate(vals, axis=0), jnp.concatenate(picks, axis=0)


def _pair_candidates(s0, i0, s1, i1):
    K = PEER_TOPK
    t = s0.shape[1]
    sub = lax.broadcasted_iota(jnp.int32, (SUBLANES, t), 0)
    scores, order, ids = [], [], []

    def add(a_rows, b_rows, a_of_row, b_of_row):
        sa, ia = a_rows
        sb, ib = b_rows
        ok = (a_of_row + 1) * (b_of_row + 1) <= K
        scores.append(jnp.where(ok, sa + sb, -jnp.inf))
        order.append(a_of_row * K + b_of_row)
        ids.append(ia * PEER_N_KEYS + ib)

    row = lambda x, r: (x[0][r:r + 1], x[1][r:r + 1])
    rows = lambda x, r: (x[0][r:r + SUBLANES], x[1][r:r + SUBLANES])
    A, Bv = (s0, i0), (s1, i1)
    add(row(A, 0), rows(Bv, 0), jnp.zeros_like(sub), sub)
    add(row(A, 0), rows(Bv, SUBLANES), jnp.zeros_like(sub), sub + SUBLANES)
    for a in range(1, 4):
        add(row(A, a), rows(Bv, 0), jnp.full_like(sub, a), sub)
    for b in range(3):
        dup = sub < 4
        sa, ia = rows(A, 0)
        add((jnp.where(dup, -jnp.inf, sa), ia), row(Bv, b), sub, jnp.full_like(sub, b))
    add(rows(A, SUBLANES), row(Bv, 0), sub + SUBLANES, jnp.zeros_like(sub))
    cat = lambda xs: jnp.concatenate(xs, axis=0)
    return cat(scores), cat(order), cat(ids)


def _peer_topk_kernel(q_ref, keys_ref, offs_ref, g_ref):
    K = PEER_TOPK
    assert K == 2 * SUBLANES
    nt = (((1,), (1,)), ((), ()))
    words, gates = [], []
    for h in range(PEER_HEADS):
        tops = []
        for p in range(2):
            col = (2 * h + p) * PEER_D_KEY
            qhp = q_ref[:, col:col + PEER_D_KEY].astype(BF16)
            s = lax.dot_general(keys_ref[h, p], qhp, nt, preferred_element_type=F32)
            tops.append(_top_rows(s, K))
        (s0, i0), (s1, i1) = tops
        cand_s, cand_order, cand_i = _pair_candidates(s0, i0, s1, i1)
        best_s, best_i = _top_rows(cand_s, K, order=cand_order, payload=cand_i)
        e = jnp.exp(best_s - jnp.max(best_s, axis=0, keepdims=True))
        gates.append(e / jnp.sum(e, axis=0, keepdims=True))
        off = best_i * _HALF
        lo = jnp.concatenate([off[0:_HALF], off[SUBLANES:SUBLANES + _HALF]], axis=0)
        hi = jnp.concatenate([off[_HALF:SUBLANES], off[SUBLANES + _HALF:K]], axis=0)
        words.append(lo | (hi << 16))
    g_ref[...] = jnp.concatenate(gates, axis=0).T
    words.append(jnp.zeros((PEER_PAIRS // 2, q_ref.shape[0]), jnp.int32))
    offs_ref[...] = jnp.concatenate(words, axis=0).T[:, :PEER_PAIRS // 2]


def peer_topk(q, keys):
    N = q.shape[0]
    tm = LANES
    return pl.pallas_call(
        _peer_topk_kernel,
        grid=(N // tm,),
        in_specs=[pl.BlockSpec((tm, q.shape[1]), lambda i: (i, 0)),
                  pl.BlockSpec(keys.shape, lambda i: (0, 0, 0, 0))],
        out_specs=[pl.BlockSpec((tm, PEER_PAIRS // 2), lambda i: (i, 0)),
                   pl.BlockSpec((tm, PEER_PAIRS), lambda i: (i, 0))],
        out_shape=[jax.ShapeDtypeStruct((N, PEER_PAIRS // 2), jnp.int32),
                   jax.ShapeDtypeStruct((N, PEER_PAIRS), F32)],
        compiler_params=_params("parallel"),
        name="peer_topk",
    )(q, keys)


_PEER_TOKENS = 64
_GROUPS = PEER_PAIRS // SUBLANES


def pack_expert_table(w):
    E, D = w.shape
    assert D == SUBLANES * LANES
    bits = lax.bitcast_convert_type(w.astype(BF16), jnp.uint16).astype(jnp.uint32).reshape(E * _HALF, 2, LANES)
    return bits[:, 0] | (bits[:, 1] << 16)


def _pair_position(word, half):
    return SUBLANES * (word // _HALF) + _HALF * half + word % _HALF


def _expert_rows(tab_ref, word):
    starts = (word & 0xFFFF, lax.shift_right_logical(word, 16))
    return [pltpu.bitcast(tab_ref[pl.ds(pl.multiple_of(s, _HALF), _HALF), :], BF16).astype(F32) for s in starts]


def _token_row(ref, t):
    row = ref[pl.ds(t, 1), :]
    return jnp.concatenate([row[:, s * LANES:(s + 1) * LANES] for s in range(SUBLANES)], axis=0)


def _merge_pair(a, b, shift, first):
    if shift == _HALF:
        return jnp.where(first, a, b) + pltpu.roll(jnp.where(first, b, a), shift, axis=0)
    bs = pltpu.roll(b, shift, axis=0)
    return jnp.where(first, a, bs) + pltpu.roll(jnp.where(first, bs, a), SUBLANES - shift, axis=0)


_MERGE_ORDER = (0, 4, 2, 6, 1, 5, 3, 7)


def _merge8(ps):
    sub = lax.broadcasted_iota(jnp.int32, (SUBLANES, LANES), 0)
    shift = _HALF
    while len(ps) > 1:
        first = (sub % (2 * shift)) < shift
        ps = [_merge_pair(ps[2 * i], ps[2 * i + 1], shift, first) for i in range(len(ps) // 2)]
        shift //= 2
    return ps[0]


def _peer_act_kernel(idx_ref, h_ref, g_ref, tab_ref, o_ref, part_ref):
    tn = h_ref.shape[0]

    def products(t, slot):
        x = _token_row(h_ref, t)
        for gi in range(_GROUPS):
            rows = []
            for w in range(_HALF):
                rows += _expert_rows(tab_ref, idx_ref[t, gi * _HALF + w])
            prods = [rows[2 * (j % _HALF) + j // _HALF] * x for j in _MERGE_ORDER]
            part_ref[slot, pl.ds(gi * SUBLANES, SUBLANES), :] = _merge8(prods)

    def reduce(t, slot):
        o_ref[pl.ds(t, 1), :] = jnp.sum(part_ref[slot].T, axis=0, keepdims=True)

    part_ref[...] = jnp.zeros(part_ref.shape, F32)

    def two_tokens(i, carry):
        t = 2 * i
        reduce(jnp.maximum(t - 2, 0), 0)
        reduce(jnp.maximum(t - 1, 0), 1)
        products(t, 0)
        products(t + 1, 1)
        return carry

    lax.fori_loop(0, tn // 2, two_tokens, 0)
    reduce(tn - 2, 0)
    reduce(tn - 1, 1)
    o_ref[...] = g_ref[...] * jax.nn.gelu(o_ref[...], approximate=True)


def peer_act(offs, h, g, table):
    N = offs.shape[0]
    tn = _PEER_TOKENS
    return pl.pallas_call(
        _peer_act_kernel,
        grid=(N // tn,),
        in_specs=[pl.BlockSpec((tn, PEER_PAIRS // 2), lambda i: (i, 0), memory_space=pltpu.SMEM),
                  pl.BlockSpec((tn, SUBLANES * LANES), lambda i: (i, 0)),
                  pl.BlockSpec((tn, PEER_PAIRS), lambda i: (i, 0)),
                  pl.BlockSpec(memory_space=pltpu.VMEM)],
        out_specs=pl.BlockSpec((tn, PEER_PAIRS), lambda i: (i, 0)),
        out_shape=jax.ShapeDtypeStruct((N, PEER_PAIRS), F32),
        scratch_shapes=[pltpu.VMEM((2, PEER_PAIRS, LANES), F32)],
        compiler_params=_params("arbitrary"),
        name="peer_act",
    )(offs, h, g, table)


_MIX_CHAINS = 4


def _peer_mix_kernel(idx_ref, coef_ref, tab_ref, x_ref, gate_ref, o_ref, cb_ref):
    tn = x_ref.shape[0]

    def spread(t):
        row = coef_ref[pl.ds(t, 1), :]
        return jnp.broadcast_to(row, (PEER_PAIRS, PEER_PAIRS)).T

    cb_ref[...] = spread(0)

    def token(t, carry):
        accs = [None] * _MIX_CHAINS
        for w in range(PEER_PAIRS // 2):
            for k, row in enumerate(_expert_rows(tab_ref, idx_ref[t, w])):
                j = _pair_position(w, k)
                term = jnp.broadcast_to(cb_ref[pl.ds(j, 1), :], (SUBLANES, LANES)) * row
                a = j % _MIX_CHAINS
                accs[a] = term if accs[a] is None else accs[a] + term
        y = (accs[0] + accs[1]) + (accs[2] + accs[3])
        y_row = jnp.concatenate([y[s:s + 1] for s in range(SUBLANES)], axis=1)
        o_ref[pl.ds(t, 1), :] = x_ref[pl.ds(t, 1), :] + gate_ref[0] * y_row
        cb_ref[...] = spread(jnp.minimum(t + 1, tn - 1))
        return carry

    lax.fori_loop(0, tn, token, 0)


def peer_mix(offs, coef, table, x, gate, tokens_per_batch):
    N, D = x.shape
    tn = _PEER_TOKENS
    assert tokens_per_batch % tn == 0
    if gate.shape[0] == 1:
        gate_map = lambda i: (0, 0, 0)
    else:
        gate_map = lambda i: ((i * tn) // tokens_per_batch, 0, 0)
    return pl.pallas_call(
        _peer_mix_kernel,
        grid=(N // tn,),
        in_specs=[pl.BlockSpec((tn, PEER_PAIRS // 2), lambda i: (i, 0), memory_space=pltpu.SMEM),
                  pl.BlockSpec((tn, PEER_PAIRS), lambda i: (i, 0)),
                  pl.BlockSpec(memory_space=pltpu.VMEM),
                  pl.BlockSpec((tn, D), lambda i: (i, 0)),
                  pl.BlockSpec((1, 1, D), gate_map)],
        out_specs=pl.BlockSpec((tn, D), lambda i: (i, 0)),
        out_shape=jax.ShapeDtypeStruct((N, D), F32),
        scratch_shapes=[pltpu.VMEM((PEER_PAIRS, LANES), F32)],
        compiler_params=_params("arbitrary"),
        name="peer_mix",
    )(offs, coef, table, x, gate)


def _rmsnorm_kernel(x_ref, g_ref, o_ref):
    x = x_ref[0]
    o_ref[0] = x * lax.rsqrt(jnp.mean(x * x, axis=-1, keepdims=True) + EPS) * g_ref[...]


def rmsnorm(x, g):
    B, L, D = x.shape
    tm = min(512, L)
    return pl.pallas_call(
        _rmsnorm_kernel,
        grid=(B, L // tm),
        in_specs=[pl.BlockSpec((1, tm, D), lambda b, i: (b, i, 0)),
                  pl.BlockSpec((1, D), lambda b, i: (0, 0))],
        out_specs=pl.BlockSpec((1, tm, D), lambda b, i: (b, i, 0)),
        out_shape=jax.ShapeDtypeStruct((B, L, D), F32),
        compiler_params=_params("parallel", "parallel"),
        name="final_rmsnorm",
    )(x, g.reshape(1, D))


def _rope_tables(T):
    t = jnp.arange(T, dtype=jnp.int32)
    row = (t // GRID_W).astype(F32)
    col = (t % GRID_W).astype(F32)
    n_freq = HEAD_DIM // 4
    inv_freq = ROPE_THETA ** (-jnp.arange(n_freq, dtype=F32) / n_freq)
    ang = jnp.concatenate([row[:, None] * inv_freq, col[:, None] * inv_freq], axis=-1)
    cos, sin = jnp.cos(ang), jnp.sin(ang)
    return jnp.tile(cos, (1, 4)), jnp.tile(jnp.concatenate([-sin, sin], axis=-1), (1, 2))


def _peer_ffn_residual(x, g2, sc2, sh2, gate, w_q, keys, u_tab, v_tab):
    B, L, D = x.shape
    N = B * L
    q, h = norm_mod_proj(x, g2, sc2, sh2, w_q, (("f32", w_q.shape[1]),), emit_h=True)
    offs, gates = peer_topk(q.reshape(N, -1), keys)
    coef = peer_act(offs, h.reshape(N, D), gates, u_tab)
    out = peer_mix(offs, coef, v_tab, x.reshape(N, D), gate, L)
    return out.reshape(B, L, D)


_Q_SCALE = HEAD_DIM ** -0.5
_NA_W = NA_HEADS * HEAD_DIM
_GQA_W = GQA_HEADS * HEAD_DIM
_GQA_KV_W = GQA_KV_HEADS * HEAD_DIM
_IN_OUTS = (("f32", 2 * CONV_CH), ("bf16", _NA_W, _Q_SCALE), ("norm", _GQA_W // LANES, _Q_SCALE, 0),
            ("bf16", _NA_W, 1.0), ("bf16", _NA_W, 1.0), ("norm", _GQA_KV_W // LANES, 1.0, 1),
            ("bf16", _GQA_KV_W, 1.0))
_OFF_G_Q = 2 * CONV_CH + _NA_W
_OFF_G_OUT = CONV_CH + _NA_W
_GQA_PAIR_ORDER = tuple(h for g in range(GQA_HEADS // GQA_KV_HEADS) for h in (g, GQA_HEADS // GQA_KV_HEADS + g))
assert GQA_KV_HEADS == 2 and 2 * HEAD_DIM == LANES


def _reorder_heads(w, start, axis):
    idx = jnp.concatenate([jnp.arange(HEAD_DIM) + start + h * HEAD_DIM for h in _GQA_PAIR_ORDER])
    full = jnp.arange(w.shape[axis]).at[start:start + _GQA_W].set(idx)
    return jnp.take(w, full, axis=axis)


def kernel(x, c, ctx, c_ctx, norm1_g, norm2_g, w_ada, b_ada, w_in, conv_w, conv_b, conv_ln_g, conv_ln_b,
           na_rel_bias, gqa_q_norm, gqa_k_norm, w_out, peer_w_q, peer_keys, peer_u, peer_v, final_norm_g):
    B, T, D = x.shape
    depth = w_in.shape[0]
    rope = _rope_tables(T)
    ada_rows = -(-(B + 1) // SUBLANES) * SUBLANES
    ada_in = jnp.zeros((ada_rows, D), F32).at[:B].set(c).at[B].set(c_ctx)

    for l in range(depth):
        last = l == depth - 1
        mod = ada_mod(ada_in, w_ada[l], b_ada[l])
        sh1, sc1, g1, sh2, sc2, g2 = [m.reshape(B, 1, D) for m in jnp.split(mod[:B], 6, axis=-1)]
        csh1, csc1, cg1, csh2, csc2, cg2 = [m.reshape(1, 1, D) for m in jnp.split(mod[B:B + 1], 6, axis=-1)]
        w_in_b = _reorder_heads(w_in[l], _OFF_G_Q, 1).astype(BF16)
        w_out_b = _reorder_heads(w_out[l], _OFF_G_OUT, 0).astype(BF16)
        w_q_b = peer_w_q[l].astype(BF16)
        keys_b = peer_keys[l].astype(BF16)
        u_tab = pack_expert_table(peer_u[l])
        v_tab = pack_expert_table(peer_v[l])
        qk_gains = jnp.stack([jnp.tile(gqa_q_norm[l], 2), jnp.tile(gqa_k_norm[l], 2)])

        za, na_q, g_q, na_k, na_v, g_k, g_v = norm_mod_proj(x, norm1_g[l], sc1, sh1, w_in_b, _IN_OUTS,
                                                            gains=qk_gains, rope=rope)
        zca, cna_q, cg_q, cna_k, cna_v, cg_k, cg_v = norm_mod_proj(ctx, norm1_g[l], csc1, csh1, w_in_b, _IN_OUTS,
                                                                   gains=qk_gains)
        a = conformer_conv(za, conv_w[l], conv_b[l], conv_ln_g[l], conv_ln_b[l])
        bm = neighbourhood_attention(na_q, na_k, na_v, cna_k, cna_v, na_bias_table(na_rel_bias[l]))
        gm = pair_attention(g_q, [(g_k, g_v), (cg_k, cg_v)])
        x = proj_residual((a, bm, gm), w_out_b, x, g1)

        x = _peer_ffn_residual(x, norm2_g[l], sc2, sh2, g2, w_q_b, keys_b, u_tab, v_tab)

        if not last:
            ac = conformer_conv(zca, conv_w[l], conv_b[l], conv_ln_g[l], conv_ln_b[l])
            bc = pair_attention(cna_q, [(cna_k, cna_v)])
            gc = pair_attention(cg_q, [(cg_k, cg_v)])
            ctx = proj_residual((ac, bc, gc), w_out_b, ctx, cg1)
            ctx = _peer_ffn_residual(ctx, norm2_g[l], csc2, csh2, cg2, w_q_b, keys_b, u_tab, v_tab)
    return rmsnorm(x, final_norm_g)
```

```python
import functools

import jax
import jax.numpy as jnp
from jax import lax
from jax.experimental import pallas as pl
from jax.experimental.pallas import tpu as pltpu

F32 = jnp.float32
BF16 = jnp.bfloat16

GRID_W = 64
HEAD_DIM = 64
CONV_CH = 256
CONV_WIDTH = 31
NA_HEADS = 6
NA_WIN_H = 8
NA_WIN_W = 16
GQA_HEADS = 6
GQA_KV_HEADS = 2
ROPE_THETA = 10000.0
PEER_HEADS = 8
PEER_N_KEYS = 128
PEER_D_KEY = 128
PEER_TOPK = 16
EPS = 1e-6

PEER_PAIRS = PEER_HEADS * PEER_TOPK

LANES = 128
SUBLANES = 8
_HALF = SUBLANES // 2
assert PEER_N_KEYS * PEER_N_KEYS * _HALF <= 1 << 16
VMEM_LIMIT = 48 * 1024 * 1024
MASK_VALUE = -1e30


def _params(*sem):
    return pltpu.CompilerParams(dimension_semantics=sem, vmem_limit_bytes=VMEM_LIMIT)


def _ada_kernel(a_ref, w_ref, b_ref, o_ref):
    a = a_ref[...]
    s = a * jax.nn.sigmoid(a)
    o_ref[...] = jnp.dot(s.astype(BF16), w_ref[...].astype(BF16), preferred_element_type=F32) + b_ref[...]


def ada_mod(a, w, b):
    R, D = a.shape
    N = w.shape[1]
    tn = 1024
    return pl.pallas_call(
        _ada_kernel,
        grid=(N // tn,),
        in_specs=[pl.BlockSpec((R, D), lambda j: (0, 0)),
                  pl.BlockSpec((D, tn), lambda j: (0, j)),
                  pl.BlockSpec((1, tn), lambda j: (0, j))],
        out_specs=pl.BlockSpec((R, tn), lambda j: (0, j)),
        out_shape=jax.ShapeDtypeStruct((R, N), F32),
        compiler_params=_params("parallel"),
        name="ada_mod",
    )(a, w, b.reshape(1, N))


def _pair_norm_rope(x, gain, rope_cos, rope_sin):
    lane = lax.broadcasted_iota(jnp.int32, x.shape, 1)
    lower = lane < HEAD_DIM
    sq = x * x
    ms_lo = jnp.sum(jnp.where(lower, sq, 0.0), axis=-1, keepdims=True) * (1.0 / HEAD_DIM)
    ms_hi = jnp.sum(jnp.where(lower, 0.0, sq), axis=-1, keepdims=True) * (1.0 / HEAD_DIM)
    y = x * jnp.where(lower, lax.rsqrt(ms_lo + EPS), lax.rsqrt(ms_hi + EPS)) * gain
    if rope_cos is None:
        return y
    half = HEAD_DIM // 2
    first = (lane % HEAD_DIM) < half
    partner = jnp.where(first, pltpu.roll(y, LANES - half, axis=1), pltpu.roll(y, half, axis=1))
    return y * rope_cos + partner * rope_sin


def _norm_mod_proj_kernel(*refs, outs, emit_h, has_gains, rope):
    x_ref, g_ref, sc_ref, sh_ref, w_ref = refs[:5]
    n_in = 5
    gains_ref = cos = sin = None
    if has_gains:
        gains_ref = refs[n_in]
        n_in += 1
    if rope:
        cos, sin = refs[n_in][...], refs[n_in + 1][...]
        n_in += 2
    out_refs = refs[n_in:]
    x = x_ref[0]
    y = x * lax.rsqrt(jnp.mean(x * x, axis=-1, keepdims=True) + EPS) * g_ref[...]
    h = y * (1.0 + sc_ref[0]) + sh_ref[0]
    z = jnp.dot(h.astype(BF16), w_ref[...], preferred_element_type=F32)
    off = 0
    for o_ref, spec in zip(out_refs, outs):
        if spec[0] == "f32":
            o_ref[0] = z[:, off:off + spec[1]]
            off += spec[1]
        elif spec[0] == "bf16":
            o_ref[0] = (z[:, off:off + spec[1]] * spec[2]).astype(BF16)
            off += spec[1]
        else:
            groups = []
            for _ in range(spec[1]):
                v = _pair_norm_rope(z[:, off:off + LANES], gains_ref[spec[3]:spec[3] + 1, :], cos, sin)
                groups.append((v * spec[2]).astype(BF16))
                off += LANES
            o_ref[0] = groups[0] if len(groups) == 1 else jnp.concatenate(groups, axis=-1)
    if emit_h:
        out_refs[len(outs)][0] = h


def _out_width(spec):
    return spec[1] * LANES if spec[0] == "norm" else spec[1]


def norm_mod_proj(x, g, sc, sh, w, outs, emit_h=False, gains=None, rope=None):
    B, L, D = x.shape
    N = w.shape[1]
    assert sum(_out_width(o) for o in outs) == N
    tm = min(256, L)
    per_batch = sc.shape[0] == B
    mod_map = (lambda b, i: (b, 0, 0)) if per_batch else (lambda b, i: (0, 0, 0))
    in_specs = [pl.BlockSpec((1, tm, D), lambda b, i: (b, i, 0)),
                pl.BlockSpec((1, D), lambda b, i: (0, 0)),
                pl.BlockSpec((1, 1, D), mod_map),
                pl.BlockSpec((1, 1, D), mod_map),
                pl.BlockSpec((D, N), lambda b, i: (0, 0))]
    args = [x, g.reshape(1, D), sc, sh, w]
    if gains is not None:
        in_specs.append(pl.BlockSpec(gains.shape, lambda b, i: (0, 0)))
        args.append(gains)
    if rope is not None:
        for tab in rope:
            in_specs.append(pl.BlockSpec((tm, LANES), lambda b, i: (i, 0)))
            args.append(tab)
    out_shape = [jax.ShapeDtypeStruct((B, L, _out_width(o)), F32 if o[0] == "f32" else BF16) for o in outs]
    out_specs = [pl.BlockSpec((1, tm, _out_width(o)), lambda b, i: (b, i, 0)) for o in outs]
    if emit_h:
        out_shape.append(jax.ShapeDtypeStruct((B, L, D), F32))
        out_specs.append(pl.BlockSpec((1, tm, D), lambda b, i: (b, i, 0)))
    return pl.pallas_call(
        functools.partial(_norm_mod_proj_kernel, outs=outs, emit_h=emit_h, has_gains=gains is not None,
                          rope=rope is not None),
        grid=(B, L // tm),
        in_specs=in_specs,
        out_specs=out_specs,
        out_shape=out_shape,
        compiler_params=_params("parallel", "parallel"),
        name="norm_mod_proj",
    )(*args)


_CONV_PAD = 16


def _conv_kernel(za_ref, w_ref, b_ref, lg_ref, lb_ref, o_ref, upad_ref, *, L, tc):
    c = pl.program_id(1)

    @pl.when(c == 0)
    def _():
        val = za_ref[0, :, :CONV_CH]
        gate = za_ref[0, :, CONV_CH:]
        zeros = jnp.zeros((_CONV_PAD, CONV_CH), F32)
        upad_ref[pl.ds(0, _CONV_PAD), :] = zeros
        upad_ref[pl.ds(_CONV_PAD + L, _CONV_PAD), :] = zeros
        upad_ref[pl.ds(_CONV_PAD, L), :] = val * jax.nn.sigmoid(gate)

    start = pl.multiple_of(c * tc, SUBLANES)
    win = upad_ref[pl.ds(start, tc + 2 * _CONV_PAD), :]
    acc = jnp.zeros((tc, CONV_CH), F32)
    first = _CONV_PAD - CONV_WIDTH // 2
    for j in range(CONV_WIDTH):
        acc = acc + win[first + j:first + j + tc, :] * w_ref[j:j + 1, :]
    y = acc + b_ref[...]
    mu = jnp.mean(y, axis=-1, keepdims=True)
    d = y - mu
    var = jnp.mean(d * d, axis=-1, keepdims=True)
    yn = d * lax.rsqrt(var + EPS) * lg_ref[...] + lb_ref[...]
    o_ref[0] = (yn * jax.nn.sigmoid(yn)).astype(o_ref.dtype)


def conformer_conv(za, w_dw, b_dw, ln_g, ln_b):
    B, L, _ = za.shape
    tc = min(256, L)
    vec = lambda v: v.reshape(1, CONV_CH)
    return pl.pallas_call(
        functools.partial(_conv_kernel, L=L, tc=tc),
        grid=(B, L // tc),
        in_specs=[pl.BlockSpec((1, L, 2 * CONV_CH), lambda b, c: (b, 0, 0)),
                  pl.BlockSpec((CONV_WIDTH, CONV_CH), lambda b, c: (0, 0)),
                  pl.BlockSpec((1, CONV_CH), lambda b, c: (0, 0)),
                  pl.BlockSpec((1, CONV_CH), lambda b, c: (0, 0)),
                  pl.BlockSpec((1, CONV_CH), lambda b, c: (0, 0))],
        out_specs=pl.BlockSpec((1, tc, CONV_CH), lambda b, c: (b, c, 0)),
        out_shape=jax.ShapeDtypeStruct((B, L, CONV_CH), BF16),
        scratch_shapes=[pltpu.VMEM((L + 2 * _CONV_PAD, CONV_CH), F32)],
        compiler_params=_params("parallel", "arbitrary"),
        name="conformer_conv",
    )(za, w_dw, vec(b_dw), vec(ln_g), vec(ln_b))


def _na_row_offset(r, rows):
    return r - jnp.clip(r - NA_WIN_H // 2, 0, rows - NA_WIN_H)


def _head_pair_rows(qg, lower):
    zero = jnp.zeros_like(qg)
    return jnp.concatenate([jnp.where(lower, qg, zero), jnp.where(lower, zero, qg)], axis=0)


def _na_kernel(q_ref, k_ref, v_ref, kc_ref, vc_ref, bias_ref, o_ref, *, rows):
    r = pl.program_id(1)
    r0 = r - _na_row_offset(r, rows)
    band = pl.ds(pl.multiple_of(r0 * GRID_W, GRID_W), NA_WIN_H * GRID_W)
    nt = (((1,), (1,)), ((), ()))
    lower = lax.broadcasted_iota(jnp.int32, (GRID_W, LANES), 1) < HEAD_DIM
    groups = range(NA_HEADS // 2)
    cols = [slice(g * LANES, (g + 1) * LANES) for g in groups]
    qs = [_head_pair_rows(q_ref[0, :, cols[g]], lower) for g in groups]
    s_loc = [lax.dot_general(qs[g], k_ref[0, band, cols[g]], nt, preferred_element_type=F32) + bias_ref[0, g]
             for g in groups]
    s_ctx = [lax.dot_general(qs[g], kc_ref[0, :, cols[g]], nt, preferred_element_type=F32) for g in groups]
    p_loc, p_ctx, den = [], [], []
    for g in groups:
        m = jnp.maximum(jnp.max(s_loc[g], axis=-1, keepdims=True), jnp.max(s_ctx[g], axis=-1, keepdims=True))
        p_loc.append(jnp.exp(s_loc[g] - m))
        p_ctx.append(jnp.exp(s_ctx[g] - m))
        den.append(jnp.sum(p_loc[g], axis=-1, keepdims=True) + jnp.sum(p_ctx[g], axis=-1, keepdims=True))
    outs = []
    for g in groups:
        o = (jnp.dot(p_loc[g].astype(BF16), v_ref[0, band, cols[g]], preferred_element_type=F32)
             + jnp.dot(p_ctx[g].astype(BF16), vc_ref[0, :, cols[g]], preferred_element_type=F32)) / den[g]
        outs.append(jnp.where(lower, o[:GRID_W], o[GRID_W:]).astype(o_ref.dtype))
    o_ref[0] = jnp.concatenate(outs, axis=-1)


def na_bias_table(rel_bias):
    cols = jnp.arange(GRID_W, dtype=jnp.int32)
    c0 = jnp.clip(cols - NA_WIN_W // 2, 0, GRID_W - NA_WIN_W)
    in_win = (cols[None, :] >= c0[:, None]) & (cols[None, :] < c0[:, None] + NA_WIN_W)
    col_idx = jnp.clip(cols[None, :] - cols[:, None] + (NA_WIN_W - 1), 0, 2 * NA_WIN_W - 2)
    off = jnp.arange(NA_WIN_H, dtype=jnp.int32)
    row_idx = off[None, :] - off[:, None] + (NA_WIN_H - 1)
    t = rel_bias[:, row_idx]
    t = t[..., col_idx]
    t = jnp.where(in_win[None, None, None], t, MASK_VALUE)
    return t.transpose(1, 0, 3, 2, 4).reshape(NA_WIN_H, NA_HEADS // 2, 2 * GRID_W, NA_WIN_H * GRID_W)


def neighbourhood_attention(q, k, v, kc, vc, bias):
    B, T, W = q.shape
    C = kc.shape[1]
    rows = T // GRID_W
    assert rows >= NA_WIN_H and W == NA_HEADS * HEAD_DIM
    return pl.pallas_call(
        functools.partial(_na_kernel, rows=rows),
        grid=(B, rows),
        in_specs=[pl.BlockSpec((1, GRID_W, W), lambda b, r: (b, r, 0)),
                  pl.BlockSpec((1, T, W), lambda b, r: (b, 0, 0)),
                  pl.BlockSpec((1, T, W), lambda b, r: (b, 0, 0)),
                  pl.BlockSpec((1, C, W), lambda b, r: (b, 0, 0)),
                  pl.BlockSpec((1, C, W), lambda b, r: (b, 0, 0)),
                  pl.BlockSpec((1,) + bias.shape[1:], lambda b, r: (_na_row_offset(r, rows), 0, 0, 0))],
        out_specs=pl.BlockSpec((1, GRID_W, W), lambda b, r: (b, r, 0)),
        out_shape=jax.ShapeDtypeStruct((B, T, W), BF16),
        compiler_params=_params("parallel", "arbitrary"),
        name="neighbourhood_attention",
    )(q, k, v, kc, vc, bias)


def _pair_attn_kernel(*refs, Gq, Gk, bq, n_sets):
    q_ref = refs[0]
    kv_refs = refs[1:1 + 2 * n_sets]
    o_ref = refs[1 + 2 * n_sets]
    nt = (((1,), (1,)), ((), ()))
    sets = range(n_sets)
    lower = lax.broadcasted_iota(jnp.int32, (bq, LANES), 1) < HEAD_DIM
    col = lambda g: slice(g * LANES, (g + 1) * LANES)
    pairs = [_head_pair_rows(q_ref[0, :, col(g)], lower) for g in range(Gq)]
    if Gk == 1:
        jobs = [(jnp.concatenate([p[:bq] for p in pairs], axis=0), col(0)),
                (jnp.concatenate([p[bq:] for p in pairs], axis=0), col(0))]
    else:
        jobs = [(pairs[g], col(g)) for g in range(Gq)]
    ss = [[lax.dot_general(rows, kv_refs[2 * i][0, :, kc], nt, preferred_element_type=F32) for i in sets]
          for rows, kc in jobs]
    ps, dens = [], []
    for s in ss:
        m = functools.reduce(jnp.maximum, [jnp.max(x, axis=-1, keepdims=True) for x in s])
        ps.append([jnp.exp(x - m) for x in s])
        dens.append(functools.reduce(jnp.add, [jnp.sum(p, axis=-1, keepdims=True) for p in ps[-1]]))
    os_ = []
    for (rows, kc), p, den in zip(jobs, ps, dens):
        o = functools.reduce(jnp.add, [jnp.dot(p[i].astype(BF16), kv_refs[2 * i + 1][0, :, kc],
                                               preferred_element_type=F32) for i in sets])
        os_.append(o / den)
    outs = []
    for g in range(Gq):
        if Gk == 1:
            first, second = os_[0][g * bq:(g + 1) * bq], os_[1][g * bq:(g + 1) * bq]
        else:
            first, second = os_[g][:bq], os_[g][bq:]
        outs.append(jnp.where(lower, first, second).astype(o_ref.dtype))
    o_ref[0] = jnp.concatenate(outs, axis=-1)


def pair_attention(q, kv_sets):
    B, L, Wq = q.shape
    Wk = kv_sets[0][0].shape[2]
    Gq, Gk = Wq // LANES, Wk // LANES
    assert Gk in (1, Gq)
    bq = min(128, L)
    in_specs = [pl.BlockSpec((1, bq, Wq), lambda b, i: (b, i, 0))]
    args = [q]
    for k, v in kv_sets:
        for t in (k, v):
            in_specs.append(pl.BlockSpec((1, t.shape[1], Wk), lambda b, i: (b, 0, 0)))
            args.append(t)
    return pl.pallas_call(
        functools.partial(_pair_attn_kernel, Gq=Gq, Gk=Gk, bq=bq, n_sets=len(kv_sets)),
        grid=(B, L // bq),
        in_specs=in_specs,
        out_specs=pl.BlockSpec((1, bq, Wq), lambda b, i: (b, i, 0)),
        out_shape=jax.ShapeDtypeStruct((B, L, Wq), BF16),
        compiler_params=_params("parallel", "arbitrary"),
        name="pair_attention",
    )(*args)


def _proj_residual_kernel(a_ref, b_ref, g_ref, wa_ref, wb_ref, wg_ref, x_ref, gate_ref, o_ref):
    y = (jnp.dot(a_ref[0], wa_ref[...], preferred_element_type=F32)
         + jnp.dot(b_ref[0], wb_ref[...], preferred_element_type=F32)
         + jnp.dot(g_ref[0], wg_ref[...], preferred_element_type=F32))
    o_ref[0] = x_ref[0] + gate_ref[0] * y


def proj_residual(parts, w, x, gate):
    B, L, D = x.shape
    tm = min(512, L)
    gate_map = (lambda b, i: (b, 0, 0)) if gate.shape[0] == B else (lambda b, i: (0, 0, 0))
    widths = [p.shape[2] for p in parts]
    assert sum(widths) == w.shape[0]
    starts = [sum(widths[:i]) for i in range(len(parts))]
    ws = [w[s:s + k] for s, k in zip(starts, widths)]
    return pl.pallas_call(
        _proj_residual_kernel,
        grid=(B, L // tm),
        in_specs=[pl.BlockSpec((1, tm, k), lambda b, i: (b, i, 0)) for k in widths]
        + [pl.BlockSpec((k, D), lambda b, i: (0, 0)) for k in widths]
        + [pl.BlockSpec((1, tm, D), lambda b, i: (b, i, 0)), pl.BlockSpec((1, 1, D), gate_map)],
        out_specs=pl.BlockSpec((1, tm, D), lambda b, i: (b, i, 0)),
        out_shape=jax.ShapeDtypeStruct((B, L, D), F32),
        compiler_params=_params("parallel", "parallel"),
        name="proj_residual",
    )(*parts, *ws, x, gate)


def _top_rows(s, k, order=None, payload=None):
    if order is None:
        order = lax.broadcasted_iota(jnp.int32, s.shape, 0)
    after_all = jnp.iinfo(jnp.int32).max
    vals, picks = [], []
    for _ in range(k):
        m = jnp.max(s, axis=0, keepdims=True)
        first = jnp.min(jnp.where(s == m, order, after_all), axis=0, keepdims=True)
        hit = order == first
        vals.append(m)
        if payload is None:
            picks.append(first)
        else:
            picks.append(jnp.max(jnp.where(hit, payload, -1), axis=0, keepdims=True))
        s = jnp.where(hit, -jnp.inf, s)
    return jnp.concatenate(vals, axis=0), jnp.concatenate(picks, axis=0)


def _pair_candidates(s0, i0, s1, i1):
    K = PEER_TOPK
    t = s0.shape[1]
    sub = lax.broadcasted_iota(jnp.int32, (SUBLANES, t), 0)
    scores, order, ids = [], [], []

    def add(a_rows, b_rows, a_of_row, b_of_row):
        sa, ia = a_rows
        sb, ib = b_rows
        ok = (a_of_row + 1) * (b_of_row + 1) <= K
        scores.append(jnp.where(ok, sa + sb, -jnp.inf))
        order.append(a_of_row * K + b_of_row)
        ids.append(ia * PEER_N_KEYS + ib)

    row = lambda x, r: (x[0][r:r + 1], x[1][r:r + 1])
    rows = lambda x, r: (x[0][r:r + SUBLANES], x[1][r:r + SUBLANES])
    A, Bv = (s0, i0), (s1, i1)
    add(row(A, 0), rows(Bv, 0), jnp.zeros_like(sub), sub)
    add(row(A, 0), rows(Bv, SUBLANES), jnp.zeros_like(sub), sub + SUBLANES)
    for a in range(1, 4):
        add(row(A, a), rows(Bv, 0), jnp.full_like(sub, a), sub)
    for b in range(3):
        dup = sub < 4
        sa, ia = rows(A, 0)
        add((jnp.where(dup, -jnp.inf, sa), ia), row(Bv, b), sub, jnp.full_like(sub, b))
    add(rows(A, SUBLANES), row(Bv, 0), sub + SUBLANES, jnp.zeros_like(sub))
    cat = lambda xs: jnp.concatenate(xs, axis=0)
    return cat(scores), cat(order), cat(ids)


def _peer_topk_kernel(q_ref, keys_ref, offs_ref, g_ref):
    K = PEER_TOPK
    assert K == 2 * SUBLANES
    nt = (((1,), (1,)), ((), ()))
    words, gates = [], []
    for h in range(PEER_HEADS):
        tops = []
        for p in range(2):
            col = (2 * h + p) * PEER_D_KEY
            qhp = q_ref[:, col:col + PEER_D_KEY]
            s = lax.dot_general(keys_ref[h, p], qhp, nt, preferred_element_type=F32)
            tops.append(_top_rows(s, K))
        (s0, i0), (s1, i1) = tops
        cand_s, cand_order, cand_i = _pair_candidates(s0, i0, s1, i1)
        best_s, best_i = _top_rows(cand_s, K, order=cand_order, payload=cand_i)
        e = jnp.exp(best_s - jnp.max(best_s, axis=0, keepdims=True))
        gates.append(e / jnp.sum(e, axis=0, keepdims=True))
        off = best_i * _HALF
        lo = jnp.concatenate([off[0:_HALF], off[SUBLANES:SUBLANES + _HALF]], axis=0)
        hi = jnp.concatenate([off[_HALF:SUBLANES], off[SUBLANES + _HALF:K]], axis=0)
        words.append(lo | (hi << 16))
    g_ref[...] = jnp.concatenate(gates, axis=0).T
    words.append(jnp.zeros((PEER_PAIRS // 2, q_ref.shape[0]), jnp.int32))
    offs_ref[...] = jnp.concatenate(words, axis=0).T[:, :PEER_PAIRS // 2]


def peer_topk(q, keys):
    N = q.shape[0]
    tm = LANES
    return pl.pallas_call(
        _peer_topk_kernel,
        grid=(N // tm,),
        in_specs=[pl.BlockSpec((tm, q.shape[1]), lambda i: (i, 0)),
                  pl.BlockSpec(keys.shape, lambda i: (0, 0, 0, 0))],
        out_specs=[pl.BlockSpec((tm, PEER_PAIRS // 2), lambda i: (i, 0)),
                   pl.BlockSpec((tm, PEER_PAIRS), lambda i: (i, 0))],
        out_shape=[jax.ShapeDtypeStruct((N, PEER_PAIRS // 2), jnp.int32),
                   jax.ShapeDtypeStruct((N, PEER_PAIRS), F32)],
        compiler_params=_params("parallel"),
        name="peer_topk",
    )(q, keys)


_PEER_TOKENS = 64
_GROUPS = PEER_PAIRS // SUBLANES


def pack_expert_table(w):
    E, D = w.shape
    assert D == SUBLANES * LANES
    rows = w.astype(BF16).reshape(E * _HALF, 2, LANES)
    return lax.bitcast_convert_type(jnp.stack([rows[:, 0], rows[:, 1]], axis=-1), jnp.uint32)


def _pair_position(word, half):
    return SUBLANES * (word // _HALF) + _HALF * half + word % _HALF


def _expert_rows(tab_ref, word):
    starts = (word & 0xFFFF, lax.shift_right_logical(word, 16))
    return [pltpu.bitcast(tab_ref[pl.ds(pl.multiple_of(s, _HALF), _HALF), :], BF16).astype(F32) for s in starts]


def _token_row(ref, t):
    row = ref[pl.ds(t, 1), :]
    return jnp.concatenate([row[:, s * LANES:(s + 1) * LANES] for s in range(SUBLANES)], axis=0)


def _merge_pair(a, b, shift, first):
    if shift == _HALF:
        return jnp.where(first, a, b) + pltpu.roll(jnp.where(first, b, a), shift, axis=0)
    bs = pltpu.roll(b, shift, axis=0)
    return jnp.where(first, a, bs) + pltpu.roll(jnp.where(first, bs, a), SUBLANES - shift, axis=0)


_MERGE_ORDER = (0, 4, 2, 6, 1, 5, 3, 7)


def _merge8(ps):
    sub = lax.broadcasted_iota(jnp.int32, (SUBLANES, LANES), 0)
    shift = _HALF
    while len(ps) > 1:
        first = (sub % (2 * shift)) < shift
        ps = [_merge_pair(ps[2 * i], ps[2 * i + 1], shift, first) for i in range(len(ps) // 2)]
        shift //= 2
    return ps[0]


def _peer_act_kernel(idx_ref, h_ref, g_ref, tab_ref, o_ref, part_ref):
    tn = h_ref.shape[0]

    def products(t, slot):
        x = _token_row(h_ref, t)
        for gi in range(_GROUPS):
            rows = []
            for w in range(_HALF):
                rows += _expert_rows(tab_ref, idx_ref[t, gi * _HALF + w])
            prods = [rows[2 * (j % _HALF) + j // _HALF] * x for j in _MERGE_ORDER]
            part_ref[slot, pl.ds(gi * SUBLANES, SUBLANES), :] = _merge8(prods)

    def reduce(t, slot):
        o_ref[pl.ds(t, 1), :] = jnp.sum(part_ref[slot].T, axis=0, keepdims=True)

    part_ref[...] = jnp.zeros(part_ref.shape, F32)

    def two_tokens(i, carry):
        t = 2 * i
        reduce(jnp.maximum(t - 2, 0), 0)
        reduce(jnp.maximum(t - 1, 0), 1)
        products(t, 0)
        products(t + 1, 1)
        return carry

    lax.fori_loop(0, tn // 2, two_tokens, 0)
    reduce(tn - 2, 0)
    reduce(tn - 1, 1)
    o_ref[...] = g_ref[...] * jax.nn.gelu(o_ref[...], approximate=True)


def peer_act(offs, h, g, table):
    N = offs.shape[0]
    tn = _PEER_TOKENS
    return pl.pallas_call(
        _peer_act_kernel,
        grid=(N // tn,),
        in_specs=[pl.BlockSpec((tn, PEER_PAIRS // 2), lambda i: (i, 0), memory_space=pltpu.SMEM),
                  pl.BlockSpec((tn, SUBLANES * LANES), lambda i: (i, 0)),
                  pl.BlockSpec((tn, PEER_PAIRS), lambda i: (i, 0)),
                  pl.BlockSpec(memory_space=pltpu.VMEM)],
        out_specs=pl.BlockSpec((tn, PEER_PAIRS), lambda i: (i, 0)),
        out_shape=jax.ShapeDtypeStruct((N, PEER_PAIRS), F32),
        scratch_shapes=[pltpu.VMEM((2, PEER_PAIRS, LANES), F32)],
        compiler_params=_params("arbitrary"),
        name="peer_act",
    )(offs, h, g, table)


_MIX_CHAINS = 4


def _peer_mix_kernel(idx_ref, coef_ref, tab_ref, x_ref, gate_ref, o_ref, cb_ref):
    tn = x_ref.shape[0]

    def spread(t):
        row = coef_ref[pl.ds(t, 1), :]
        return jnp.broadcast_to(row, (PEER_PAIRS, PEER_PAIRS)).T

    cb_ref[...] = spread(0)

    def token(t, carry):
        accs = [None] * _MIX_CHAINS
        for w in range(PEER_PAIRS // 2):
            for k, row in enumerate(_expert_rows(tab_ref, idx_ref[t, w])):
                j = _pair_position(w, k)
                term = jnp.broadcast_to(cb_ref[pl.ds(j, 1), :], (SUBLANES, LANES)) * row
                a = j % _MIX_CHAINS
                accs[a] = term if accs[a] is None else accs[a] + term
        y = (accs[0] + accs[1]) + (accs[2] + accs[3])
        y_row = jnp.concatenate([y[s:s + 1] for s in range(SUBLANES)], axis=1)
        o_ref[pl.ds(t, 1), :] = x_ref[pl.ds(t, 1), :] + gate_ref[0] * y_row
        cb_ref[...] = spread(jnp.minimum(t + 1, tn - 1))
        return carry

    lax.fori_loop(0, tn, token, 0)


def peer_mix(offs, coef, table, x, gate, tokens_per_batch):
    N, D = x.shape
    tn = _PEER_TOKENS
    assert tokens_per_batch % tn == 0
    if gate.shape[0] == 1:
        gate_map = lambda i: (0, 0, 0)
    else:
        gate_map = lambda i: ((i * tn) // tokens_per_batch, 0, 0)
    return pl.pallas_call(
        _peer_mix_kernel,
        grid=(N // tn,),
        in_specs=[pl.BlockSpec((tn, PEER_PAIRS // 2), lambda i: (i, 0), memory_space=pltpu.SMEM),
                  pl.BlockSpec((tn, PEER_PAIRS), lambda i: (i, 0)),
                  pl.BlockSpec(memory_space=pltpu.VMEM),
                  pl.BlockSpec((tn, D), lambda i: (i, 0)),
                  pl.BlockSpec((1, 1, D), gate_map)],
        out_specs=pl.BlockSpec((tn, D), lambda i: (i, 0)),
        out_shape=jax.ShapeDtypeStruct((N, D), F32),
        scratch_shapes=[pltpu.VMEM((PEER_PAIRS, LANES), F32)],
        compiler_params=_params("arbitrary"),
        name="peer_mix",
    )(offs, coef, table, x, gate)


def _rmsnorm_kernel(x_ref, g_ref, o_ref):
    x = x_ref[0]
    o_ref[0] = x * lax.rsqrt(jnp.mean(x * x, axis=-1, keepdims=True) + EPS) * g_ref[...]


def rmsnorm(x, g):
    B, L, D = x.shape
    tm = min(512, L)
    return pl.pallas_call(
        _rmsnorm_kernel,
        grid=(B, L // tm),
        in_specs=[pl.BlockSpec((1, tm, D), lambda b, i: (b, i, 0)),
                  pl.BlockSpec((1, D), lambda b, i: (0, 0))],
        out_specs=pl.BlockSpec((1, tm, D), lambda b, i: (b, i, 0)),
        out_shape=jax.ShapeDtypeStruct((B, L, D), F32),
        compiler_params=_params("parallel", "parallel"),
        name="final_rmsnorm",
    )(x, g.reshape(1, D))


def _rope_tables(T):
    t = jnp.arange(T, dtype=jnp.int32)
    row = (t // GRID_W).astype(F32)
    col = (t % GRID_W).astype(F32)
    n_freq = HEAD_DIM // 4
    inv_freq = ROPE_THETA ** (-jnp.arange(n_freq, dtype=F32) / n_freq)
    ang = jnp.concatenate([row[:, None] * inv_freq, col[:, None] * inv_freq], axis=-1)
    cos, sin = jnp.cos(ang), jnp.sin(ang)
    return jnp.tile(cos, (1, 4)), jnp.tile(jnp.concatenate([-sin, sin], axis=-1), (1, 2))


def _peer_ffn_residual(x, g2, sc2, sh2, gate, w_q, keys, u_tab, v_tab):
    B, L, D = x.shape
    N = B * L
    q, h = norm_mod_proj(x, g2, sc2, sh2, w_q, (("bf16", w_q.shape[1], 1.0),), emit_h=True)
    offs, gates = peer_topk(q.reshape(N, -1), keys)
    coef = peer_act(offs, h.reshape(N, D), gates, u_tab)
    out = peer_mix(offs, coef, v_tab, x.reshape(N, D), gate, L)
    return out.reshape(B, L, D)


_Q_SCALE = HEAD_DIM ** -0.5
_NA_W = NA_HEADS * HEAD_DIM
_GQA_W = GQA_HEADS * HEAD_DIM
_GQA_KV_W = GQA_KV_HEADS * HEAD_DIM
_IN_OUTS = (("f32", 2 * CONV_CH), ("bf16", _NA_W, _Q_SCALE), ("norm", _GQA_W // LANES, _Q_SCALE, 0),
            ("bf16", _NA_W, 1.0), ("bf16", _NA_W, 1.0), ("norm", _GQA_KV_W // LANES, 1.0, 1),
            ("bf16", _GQA_KV_W, 1.0))
_OFF_G_Q = 2 * CONV_CH + _NA_W
_OFF_G_OUT = CONV_CH + _NA_W
_GQA_PAIR_ORDER = tuple(h for g in range(GQA_HEADS // GQA_KV_HEADS) for h in (g, GQA_HEADS // GQA_KV_HEADS + g))
assert GQA_KV_HEADS == 2 and 2 * HEAD_DIM == LANES


def _reorder_heads(w, start, axis):
    idx = jnp.concatenate([jnp.arange(HEAD_DIM) + start + h * HEAD_DIM for h in _GQA_PAIR_ORDER])
    full = jnp.arange(w.shape[axis]).at[start:start + _GQA_W].set(idx)
    return jnp.take(w, full, axis=axis)


def kernel(x, c, ctx, c_ctx, norm1_g, norm2_g, w_ada, b_ada, w_in, conv_w, conv_b, conv_ln_g, conv_ln_b,
           na_rel_bias, gqa_q_norm, gqa_k_norm, w_out, peer_w_q, peer_keys, peer_u, peer_v, final_norm_g):
    B, T, D = x.shape
    depth = w_in.shape[0]
    rope = _rope_tables(T)
    ada_rows = -(-(B + 1) // SUBLANES) * SUBLANES
    ada_in = jnp.zeros((ada_rows, D), F32).at[:B].set(c).at[B].set(c_ctx)

    for l in range(depth):
        last = l == depth - 1
        mod = ada_mod(ada_in, w_ada[l], b_ada[l])
        sh1, sc1, g1, sh2, sc2, g2 = [m.reshape(B, 1, D) for m in jnp.split(mod[:B], 6, axis=-1)]
        csh1, csc1, cg1, csh2, csc2, cg2 = [m.reshape(1, 1, D) for m in jnp.split(mod[B:B + 1], 6, axis=-1)]
        w_in_b = _reorder_heads(w_in[l], _OFF_G_Q, 1).astype(BF16)
        w_out_b = _reorder_heads(w_out[l], _OFF_G_OUT, 0).astype(BF16)
        w_q_b = peer_w_q[l].astype(BF16)
        keys_b = peer_keys[l].astype(BF16)
        u_tab = pack_expert_table(peer_u[l])
        v_tab = pack_expert_table(peer_v[l])
        qk_gains = jnp.stack([jnp.tile(gqa_q_norm[l], 2), jnp.tile(gqa_k_norm[l], 2)])

        za, na_q, g_q, na_k, na_v, g_k, g_v = norm_mod_proj(x, norm1_g[l], sc1, sh1, w_in_b, _IN_OUTS,
                                                            gains=qk_gains, rope=rope)
        zca, cna_q, cg_q, cna_k, cna_v, cg_k, cg_v = norm_mod_proj(ctx, norm1_g[l], csc1, csh1, w_in_b, _IN_OUTS,
                                                                   gains=qk_gains)
        a = conformer_conv(za, conv_w[l], conv_b[l], conv_ln_g[l], conv_ln_b[l])
        bm = neighbourhood_attention(na_q, na_k, na_v, cna_k, cna_v, na_bias_table(na_rel_bias[l]))
        gm = pair_attention(g_q, [(g_k, g_v), (cg_k, cg_v)])
        x = proj_residual((a, bm, gm), w_out_b, x, g1)

        x = _peer_ffn_residual(x, norm2_g[l], sc2, sh2, g2, w_q_b, keys_b, u_tab, v_tab)

        if not last:
            ac = conformer_conv(zca, conv_w[l], conv_b[l], conv_ln_g[l], conv_ln_b[l])
            bc = pair_attention(cna_q, [(cna_k, cna_v)])
            gc = pair_attention(cg_q, [(cg_k, cg_v)])
            ctx = proj_residual((ac, bc, gc), w_out_b, ctx, cg1)
            ctx = _peer_ffn_residual(ctx, norm2_g[l], csc2, csh2, cg2, w_q_b, keys_b, u_tab, v_tab)
    return rmsnorm(x, final_norm_g)
```

```python
import functools

import jax
import jax.numpy as jnp
from jax import lax
from jax.experimental import pallas as pl
from jax.experimental.pallas import tpu as pltpu

F32 = jnp.float32
BF16 = jnp.bfloat16

GRID_W = 64
HEAD_DIM = 64
CONV_CH = 256
CONV_WIDTH = 31
NA_HEADS = 6
NA_WIN_H = 8
NA_WIN_W = 16
GQA_HEADS = 6
GQA_KV_HEADS = 2
ROPE_THETA = 10000.0
PEER_HEADS = 8
PEER_N_KEYS = 128
PEER_D_KEY = 128
PEER_TOPK = 16
EPS = 1e-6

PEER_PAIRS = PEER_HEADS * PEER_TOPK

LANES = 128
SUBLANES = 8
_HALF = SUBLANES // 2
assert PEER_N_KEYS * PEER_N_KEYS * _HALF <= 1 << 16
VMEM_LIMIT = 48 * 1024 * 1024
MASK_VALUE = -1e30


def _params(*sem):
    return pltpu.CompilerParams(dimension_semantics=sem, vmem_limit_bytes=VMEM_LIMIT)


def _ada_kernel(a_ref, w_ref, b_ref, o_ref):
    a = a_ref[...]
    s = a * jax.nn.sigmoid(a)
    o_ref[...] = jnp.dot(s.astype(BF16), w_ref[...].astype(BF16), preferred_element_type=F32) + b_ref[...]


def ada_mod(a, w, b):
    R, D = a.shape
    N = w.shape[1]
    tn = 1024
    return pl.pallas_call(
        _ada_kernel,
        grid=(N // tn,),
        in_specs=[pl.BlockSpec((R, D), lambda j: (0, 0)),
                  pl.BlockSpec((D, tn), lambda j: (0, j)),
                  pl.BlockSpec((1, tn), lambda j: (0, j))],
        out_specs=pl.BlockSpec((R, tn), lambda j: (0, j)),
        out_shape=jax.ShapeDtypeStruct((R, N), F32),
        compiler_params=_params("parallel"),
        name="ada_mod",
    )(a, w, b.reshape(1, N))


def _pair_norm_rope(x, gain, rope_cos, rope_sin):
    lane = lax.broadcasted_iota(jnp.int32, x.shape, 1)
    lower = lane < HEAD_DIM
    sq = x * x
    ms_lo = jnp.sum(jnp.where(lower, sq, 0.0), axis=-1, keepdims=True) * (1.0 / HEAD_DIM)
    ms_hi = jnp.sum(jnp.where(lower, 0.0, sq), axis=-1, keepdims=True) * (1.0 / HEAD_DIM)
    y = x * jnp.where(lower, lax.rsqrt(ms_lo + EPS), lax.rsqrt(ms_hi + EPS)) * gain
    if rope_cos is None:
        return y
    half = HEAD_DIM // 2
    first = (lane % HEAD_DIM) < half
    partner = jnp.where(first, pltpu.roll(y, LANES - half, axis=1), pltpu.roll(y, half, axis=1))
    return y * rope_cos + partner * rope_sin


def _norm_mod_proj_kernel(*refs, outs, emit_h, has_gains, rope):
    x_ref, g_ref, sc_ref, sh_ref, w_ref = refs[:5]
    n_in = 5
    gains_ref = cos = sin = None
    if has_gains:
        gains_ref = refs[n_in]
        n_in += 1
    if rope:
        cos, sin = refs[n_in][...], refs[n_in + 1][...]
        n_in += 2
    out_refs = refs[n_in:]
    x = x_ref[0]
    y = x * lax.rsqrt(jnp.mean(x * x, axis=-1, keepdims=True) + EPS) * g_ref[...]
    h = y * (1.0 + sc_ref[0]) + sh_ref[0]
    z = jnp.dot(h.astype(BF16), w_ref[...], preferred_element_type=F32)
    off = 0
    for o_ref, spec in zip(out_refs, outs):
        if spec[0] == "f32":
            o_ref[0] = z[:, off:off + spec[1]]
            off += spec[1]
        elif spec[0] == "bf16":
            o_ref[0] = (z[:, off:off + spec[1]] * spec[2]).astype(BF16)
            off += spec[1]
        else:
            groups = []
            for _ in range(spec[1]):
                v = _pair_norm_rope(z[:, off:off + LANES], gains_ref[spec[3]:spec[3] + 1, :], cos, sin)
                groups.append((v * spec[2]).astype(BF16))
                off += LANES
            o_ref[0] = groups[0] if len(groups) == 1 else jnp.concatenate(groups, axis=-1)
    if emit_h:
        out_refs[len(outs)][0] = h


def _out_width(spec):
    return spec[1] * LANES if spec[0] == "norm" else spec[1]


def norm_mod_proj(x, g, sc, sh, w, outs, emit_h=False, gains=None, rope=None):
    B, L, D = x.shape
    N = w.shape[1]
    assert sum(_out_width(o) for o in outs) == N
    tm = min(256, L)
    per_batch = sc.shape[0] == B
    mod_map = (lambda b, i: (b, 0, 0)) if per_batch else (lambda b, i: (0, 0, 0))
    in_specs = [pl.BlockSpec((1, tm, D), lambda b, i: (b, i, 0)),
                pl.BlockSpec((1, D), lambda b, i: (0, 0)),
                pl.BlockSpec((1, 1, D), mod_map),
                pl.BlockSpec((1, 1, D), mod_map),
                pl.BlockSpec((D, N), lambda b, i: (0, 0))]
    args = [x, g.reshape(1, D), sc, sh, w]
    if gains is not None:
        in_specs.append(pl.BlockSpec(gains.shape, lambda b, i: (0, 0)))
        args.append(gains)
    if rope is not None:
        for tab in rope:
            in_specs.append(pl.BlockSpec((tm, LANES), lambda b, i: (i, 0)))
            args.append(tab)
    out_shape = [jax.ShapeDtypeStruct((B, L, _out_width(o)), F32 if o[0] == "f32" else BF16) for o in outs]
    out_specs = [pl.BlockSpec((1, tm, _out_width(o)), lambda b, i: (b, i, 0)) for o in outs]
    if emit_h:
        out_shape.append(jax.ShapeDtypeStruct((B, L, D), F32))
        out_specs.append(pl.BlockSpec((1, tm, D), lambda b, i: (b, i, 0)))
    return pl.pallas_call(
        functools.partial(_norm_mod_proj_kernel, outs=outs, emit_h=emit_h, has_gains=gains is not None,
                          rope=rope is not None),
        grid=(B, L // tm),
        in_specs=in_specs,
        out_specs=out_specs,
        out_shape=out_shape,
        compiler_params=_params("parallel", "parallel"),
        name="norm_mod_proj",
    )(*args)


_CONV_PAD = 16


def _conv_kernel(za_ref, w_ref, b_ref, lg_ref, lb_ref, o_ref, upad_ref, *, L, tc):
    c = pl.program_id(1)

    @pl.when(c == 0)
    def _():
        val = za_ref[0, :, :CONV_CH]
        gate = za_ref[0, :, CONV_CH:]
        zeros = jnp.zeros((_CONV_PAD, CONV_CH), F32)
        upad_ref[pl.ds(0, _CONV_PAD), :] = zeros
        upad_ref[pl.ds(_CONV_PAD + L, _CONV_PAD), :] = zeros
        upad_ref[pl.ds(_CONV_PAD, L), :] = val * jax.nn.sigmoid(gate)

    start = pl.multiple_of(c * tc, SUBLANES)
    win = upad_ref[pl.ds(start, tc + 2 * _CONV_PAD), :]
    acc = jnp.zeros((tc, CONV_CH), F32)
    first = _CONV_PAD - CONV_WIDTH // 2
    for j in range(CONV_WIDTH):
        acc = acc + win[first + j:first + j + tc, :] * w_ref[j:j + 1, :]
    y = acc + b_ref[...]
    mu = jnp.mean(y, axis=-1, keepdims=True)
    d = y - mu
    var = jnp.mean(d * d, axis=-1, keepdims=True)
    yn = d * lax.rsqrt(var + EPS) * lg_ref[...] + lb_ref[...]
    o_ref[0] = (yn * jax.nn.sigmoid(yn)).astype(o_ref.dtype)


def conformer_conv(za, w_dw, b_dw, ln_g, ln_b):
    B, L, _ = za.shape
    tc = min(256, L)
    vec = lambda v: v.reshape(1, CONV_CH)
    return pl.pallas_call(
        functools.partial(_conv_kernel, L=L, tc=tc),
        grid=(B, L // tc),
        in_specs=[pl.BlockSpec((1, L, 2 * CONV_CH), lambda b, c: (b, 0, 0)),
                  pl.BlockSpec((CONV_WIDTH, CONV_CH), lambda b, c: (0, 0)),
                  pl.BlockSpec((1, CONV_CH), lambda b, c: (0, 0)),
                  pl.BlockSpec((1, CONV_CH), lambda b, c: (0, 0)),
                  pl.BlockSpec((1, CONV_CH), lambda b, c: (0, 0))],
        out_specs=pl.BlockSpec((1, tc, CONV_CH), lambda b, c: (b, c, 0)),
        out_shape=jax.ShapeDtypeStruct((B, L, CONV_CH), BF16),
        scratch_shapes=[pltpu.VMEM((L + 2 * _CONV_PAD, CONV_CH), F32)],
        compiler_params=_params("parallel", "arbitrary"),
        name="conformer_conv",
    )(za, w_dw, vec(b_dw), vec(ln_g), vec(ln_b))


def _na_row_offset(r, rows):
    return r - jnp.clip(r - NA_WIN_H // 2, 0, rows - NA_WIN_H)


def _head_pair_rows(qg, lower):
    zero = jnp.zeros_like(qg)
    return jnp.concatenate([jnp.where(lower, qg, zero), jnp.where(lower, zero, qg)], axis=0)


def _na_kernel(q_ref, k_ref, v_ref, kc_ref, vc_ref, bias_ref, o_ref, *, rows):
    r = pl.program_id(1)
    r0 = r - _na_row_offset(r, rows)
    band = pl.ds(pl.multiple_of(r0 * GRID_W, GRID_W), NA_WIN_H * GRID_W)
    nt = (((1,), (1,)), ((), ()))
    lower = lax.broadcasted_iota(jnp.int32, (GRID_W, LANES), 1) < HEAD_DIM
    groups = range(NA_HEADS // 2)
    cols = [slice(g * LANES, (g + 1) * LANES) for g in groups]
    qs = [_head_pair_rows(q_ref[0, :, cols[g]], lower) for g in groups]
    s_loc = [lax.dot_general(qs[g], k_ref[0, band, cols[g]], nt, preferred_element_type=F32) + bias_ref[0, g]
             for g in groups]
    s_ctx = [lax.dot_general(qs[g], kc_ref[0, :, cols[g]], nt, preferred_element_type=F32) for g in groups]
    p_loc, p_ctx, den = [], [], []
    for g in groups:
        m = jnp.maximum(jnp.max(s_loc[g], axis=-1, keepdims=True), jnp.max(s_ctx[g], axis=-1, keepdims=True))
        p_loc.append(jnp.exp(s_loc[g] - m))
        p_ctx.append(jnp.exp(s_ctx[g] - m))
        den.append(jnp.sum(p_loc[g], axis=-1, keepdims=True) + jnp.sum(p_ctx[g], axis=-1, keepdims=True))
    outs = []
    for g in groups:
        o = (jnp.dot(p_loc[g].astype(BF16), v_ref[0, band, cols[g]], preferred_element_type=F32)
             + jnp.dot(p_ctx[g].astype(BF16), vc_ref[0, :, cols[g]], preferred_element_type=F32)) / den[g]
        outs.append(jnp.where(lower, o[:GRID_W], o[GRID_W:]).astype(o_ref.dtype))
    o_ref[0] = jnp.concatenate(outs, axis=-1)


def na_bias_table(rel_bias):
    cols = jnp.arange(GRID_W, dtype=jnp.int32)
    c0 = jnp.clip(cols - NA_WIN_W // 2, 0, GRID_W - NA_WIN_W)
    in_win = (cols[None, :] >= c0[:, None]) & (cols[None, :] < c0[:, None] + NA_WIN_W)
    col_idx = jnp.clip(cols[None, :] - cols[:, None] + (NA_WIN_W - 1), 0, 2 * NA_WIN_W - 2)
    off = jnp.arange(NA_WIN_H, dtype=jnp.int32)
    row_idx = off[None, :] - off[:, None] + (NA_WIN_H - 1)
    t = rel_bias[:, row_idx]
    t = t[..., col_idx]
    t = jnp.where(in_win[None, None, None], t, MASK_VALUE)
    return t.transpose(1, 0, 3, 2, 4).reshape(NA_WIN_H, NA_HEADS // 2, 2 * GRID_W, NA_WIN_H * GRID_W)


def neighbourhood_attention(q, k, v, kc, vc, bias):
    B, T, W = q.shape
    C = kc.shape[1]
    rows = T // GRID_W
    assert rows >= NA_WIN_H and W == NA_HEADS * HEAD_DIM
    return pl.pallas_call(
        functools.partial(_na_kernel, rows=rows),
        grid=(B, rows),
        in_specs=[pl.BlockSpec((1, GRID_W, W), lambda b, r: (b, r, 0)),
                  pl.BlockSpec((1, T, W), lambda b, r: (b, 0, 0)),
                  pl.BlockSpec((1, T, W), lambda b, r: (b, 0, 0)),
                  pl.BlockSpec((1, C, W), lambda b, r: (b, 0, 0)),
                  pl.BlockSpec((1, C, W), lambda b, r: (b, 0, 0)),
                  pl.BlockSpec((1,) + bias.shape[1:], lambda b, r: (_na_row_offset(r, rows), 0, 0, 0))],
        out_specs=pl.BlockSpec((1, GRID_W, W), lambda b, r: (b, r, 0)),
        out_shape=jax.ShapeDtypeStruct((B, T, W), BF16),
        compiler_params=_params("parallel", "arbitrary"),
        name="neighbourhood_attention",
    )(q, k, v, kc, vc, bias)


def _pair_attn_kernel(*refs, Gq, Gk, bq, n_sets):
    q_ref = refs[0]
    kv_refs = refs[1:1 + 2 * n_sets]
    o_ref = refs[1 + 2 * n_sets]
    nt = (((1,), (1,)), ((), ()))
    sets = range(n_sets)
    lower = lax.broadcasted_iota(jnp.int32, (bq, LANES), 1) < HEAD_DIM
    col = lambda g: slice(g * LANES, (g + 1) * LANES)
    pairs = [_head_pair_rows(q_ref[0, :, col(g)], lower) for g in range(Gq)]
    if Gk == 1:
        jobs = [(jnp.concatenate([p[:bq] for p in pairs], axis=0), col(0)),
                (jnp.concatenate([p[bq:] for p in pairs], axis=0), col(0))]
    else:
        jobs = [(pairs[g], col(g)) for g in range(Gq)]
    ss = [[lax.dot_general(rows, kv_refs[2 * i][0, :, kc], nt, preferred_element_type=F32) for i in sets]
          for rows, kc in jobs]
    ps, dens = [], []
    for s in ss:
        m = functools.reduce(jnp.maximum, [jnp.max(x, axis=-1, keepdims=True) for x in s])
        ps.append([jnp.exp(x - m) for x in s])
        dens.append(functools.reduce(jnp.add, [jnp.sum(p, axis=-1, keepdims=True) for p in ps[-1]]))
    os_ = []
    for (rows, kc), p, den in zip(jobs, ps, dens):
        o = functools.reduce(jnp.add, [jnp.dot(p[i].astype(BF16), kv_refs[2 * i + 1][0, :, kc],
                                               preferred_element_type=F32) for i in sets])
        os_.append(o / den)
    outs = []
    for g in range(Gq):
        if Gk == 1:
            first, second = os_[0][g * bq:(g + 1) * bq], os_[1][g * bq:(g + 1) * bq]
        else:
            first, second = os_[g][:bq], os_[g][bq:]
        outs.append(jnp.where(lower, first, second).astype(o_ref.dtype))
    o_ref[0] = jnp.concatenate(outs, axis=-1)


def pair_attention(q, kv_sets):
    B, L, Wq = q.shape
    Wk = kv_sets[0][0].shape[2]
    Gq, Gk = Wq // LANES, Wk // LANES
    assert Gk in (1, Gq)
    bq = min(256, L)
    in_specs = [pl.BlockSpec((1, bq, Wq), lambda b, i: (b, i, 0))]
    args = [q]
    for k, v in kv_sets:
        for t in (k, v):
            in_specs.append(pl.BlockSpec((1, t.shape[1], Wk), lambda b, i: (b, 0, 0)))
            args.append(t)
    return pl.pallas_call(
        functools.partial(_pair_attn_kernel, Gq=Gq, Gk=Gk, bq=bq, n_sets=len(kv_sets)),
        grid=(B, L // bq),
        in_specs=in_specs,
        out_specs=pl.BlockSpec((1, bq, Wq), lambda b, i: (b, i, 0)),
        out_shape=jax.ShapeDtypeStruct((B, L, Wq), BF16),
        compiler_params=_params("parallel", "arbitrary"),
        name="pair_attention",
    )(*args)


def _proj_residual_kernel(a_ref, b_ref, g_ref, wa_ref, wb_ref, wg_ref, x_ref, gate_ref, o_ref):
    y = (jnp.dot(a_ref[0], wa_ref[...], preferred_element_type=F32)
         + jnp.dot(b_ref[0], wb_ref[...], preferred_element_type=F32)
         + jnp.dot(g_ref[0], wg_ref[...], preferred_element_type=F32))
    o_ref[0] = x_ref[0] + gate_ref[0] * y


def proj_residual(parts, w, x, gate):
    B, L, D = x.shape
    tm = min(512, L)
    gate_map = (lambda b, i: (b, 0, 0)) if gate.shape[0] == B else (lambda b, i: (0, 0, 0))
    widths = [p.shape[2] for p in parts]
    assert sum(widths) == w.shape[0]
    starts = [sum(widths[:i]) for i in range(len(parts))]
    ws = [w[s:s + k] for s, k in zip(starts, widths)]
    return pl.pallas_call(
        _proj_residual_kernel,
        grid=(B, L // tm),
        in_specs=[pl.BlockSpec((1, tm, k), lambda b, i: (b, i, 0)) for k in widths]
        + [pl.BlockSpec((k, D), lambda b, i: (0, 0)) for k in widths]
        + [pl.BlockSpec((1, tm, D), lambda b, i: (b, i, 0)), pl.BlockSpec((1, 1, D), gate_map)],
        out_specs=pl.BlockSpec((1, tm, D), lambda b, i: (b, i, 0)),
        out_shape=jax.ShapeDtypeStruct((B, L, D), F32),
        compiler_params=_params("parallel", "parallel"),
        name="proj_residual",
    )(*parts, *ws, x, gate)


def _top_rows(s, k, order=None, payload=None):
    if order is None:
        order = lax.broadcasted_iota(jnp.int32, s.shape, 0).astype(F32)
    vals, picks = [], []
    for _ in range(k):
        m = jnp.max(s, axis=0, keepdims=True)
        first = jnp.min(jnp.where(s == m, order, jnp.inf), axis=0, keepdims=True)
        hit = order == first
        vals.append(m)
        if payload is None:
            picks.append(first)
        else:
            picks.append(jnp.max(jnp.where(hit, payload, -1.0), axis=0, keepdims=True))
        s = jnp.where(hit, -jnp.inf, s)
    return jnp.concatenate(vals, axis=0), jnp.concatenate(picks, axis=0)


def _pair_candidates(s0, i0, s1, i1):
    K = PEER_TOPK
    t = s0.shape[1]
    sub = lax.broadcasted_iota(jnp.int32, (SUBLANES, t), 0)
    scores, order, ids = [], [], []

    def add(a_rows, b_rows, a_of_row, b_of_row):
        sa, ia = a_rows
        sb, ib = b_rows
        ok = (a_of_row + 1) * (b_of_row + 1) <= K
        scores.append(jnp.where(ok, sa + sb, -jnp.inf))
        order.append((a_of_row * K + b_of_row).astype(F32))
        ids.append(ia * float(PEER_N_KEYS) + ib)

    row = lambda x, r: (x[0][r:r + 1], x[1][r:r + 1])
    rows = lambda x, r: (x[0][r:r + SUBLANES], x[1][r:r + SUBLANES])
    A, Bv = (s0, i0), (s1, i1)
    add(row(A, 0), rows(Bv, 0), jnp.zeros_like(sub), sub)
    add(row(A, 0), rows(Bv, SUBLANES), jnp.zeros_like(sub), sub + SUBLANES)
    for a in range(1, 4):
        add(row(A, a), rows(Bv, 0), jnp.full_like(sub, a), sub)
    for b in range(3):
        dup = sub < 4
        sa, ia = rows(A, 0)
        add((jnp.where(dup, -jnp.inf, sa), ia), row(Bv, b), sub, jnp.full_like(sub, b))
    add(rows(A, SUBLANES), row(Bv, 0), sub + SUBLANES, jnp.zeros_like(sub))
    cat = lambda xs: jnp.concatenate(xs, axis=0)
    return cat(scores), cat(order), cat(ids)


def _peer_topk_kernel(q_ref, keys_ref, offs_ref, g_ref):
    K = PEER_TOPK
    assert K == 2 * SUBLANES
    nt = (((1,), (1,)), ((), ()))
    words, gates = [], []
    for h in range(PEER_HEADS):
        tops = []
        for p in range(2):
            col = (2 * h + p) * PEER_D_KEY
            qhp = q_ref[:, col:col + PEER_D_KEY]
            s = lax.dot_general(keys_ref[h, p], qhp, nt, preferred_element_type=F32)
            tops.append(_top_rows(s, K))
        (s0, i0), (s1, i1) = tops
        cand_s, cand_order, cand_i = _pair_candidates(s0, i0, s1, i1)
        best_s, best_i = _top_rows(cand_s, K, order=cand_order, payload=cand_i)
        e = jnp.exp(best_s - jnp.max(best_s, axis=0, keepdims=True))
        gates.append(e / jnp.sum(e, axis=0, keepdims=True))
        off = best_i.astype(jnp.int32) * _HALF
        lo = jnp.concatenate([off[0:_HALF], off[SUBLANES:SUBLANES + _HALF]], axis=0)
        hi = jnp.concatenate([off[_HALF:SUBLANES], off[SUBLANES + _HALF:K]], axis=0)
        words.append(lo | (hi << 16))
    g_ref[...] = jnp.concatenate(gates, axis=0).T
    words.append(jnp.zeros((PEER_PAIRS // 2, q_ref.shape[0]), jnp.int32))
    offs_ref[...] = jnp.concatenate(words, axis=0).T[:, :PEER_PAIRS // 2]


def peer_topk(q, keys):
    N = q.shape[0]
    tm = LANES
    return pl.pallas_call(
        _peer_topk_kernel,
        grid=(N // tm,),
        in_specs=[pl.BlockSpec((tm, q.shape[1]), lambda i: (i, 0)),
                  pl.BlockSpec(keys.shape, lambda i: (0, 0, 0, 0))],
        out_specs=[pl.BlockSpec((tm, PEER_PAIRS // 2), lambda i: (i, 0)),
                   pl.BlockSpec((tm, PEER_PAIRS), lambda i: (i, 0))],
        out_shape=[jax.ShapeDtypeStruct((N, PEER_PAIRS // 2), jnp.int32),
                   jax.ShapeDtypeStruct((N, PEER_PAIRS), F32)],
        compiler_params=_params("parallel"),
        name="peer_topk",
    )(q, keys)


_PEER_TOKENS = 64
_GROUPS = PEER_PAIRS // SUBLANES


def pack_expert_table(w):
    E, D = w.shape
    assert D == SUBLANES * LANES
    rows = w.astype(BF16).reshape(E * _HALF, 2, LANES)
    return lax.bitcast_convert_type(jnp.stack([rows[:, 0], rows[:, 1]], axis=-1), jnp.uint32)


def _pair_position(word, half):
    return SUBLANES * (word // _HALF) + _HALF * half + word % _HALF


def _expert_rows(tab_ref, word):
    starts = (word & 0xFFFF, lax.shift_right_logical(word, 16))
    return [pltpu.bitcast(tab_ref[pl.ds(pl.multiple_of(s, _HALF), _HALF), :], BF16).astype(F32) for s in starts]


def _token_row(ref, t):
    row = ref[pl.ds(t, 1), :]
    return jnp.concatenate([row[:, s * LANES:(s + 1) * LANES] for s in range(SUBLANES)], axis=0)


def _merge_pair(a, b, shift, first):
    if shift == _HALF:
        return jnp.where(first, a, b) + pltpu.roll(jnp.where(first, b, a), shift, axis=0)
    bs = pltpu.roll(b, shift, axis=0)
    return jnp.where(first, a, bs) + pltpu.roll(jnp.where(first, bs, a), SUBLANES - shift, axis=0)


_MERGE_ORDER = (0, 4, 2, 6, 1, 5, 3, 7)


def _merge8(ps):
    sub = lax.broadcasted_iota(jnp.int32, (SUBLANES, LANES), 0)
    shift = _HALF
    while len(ps) > 1:
        first = (sub % (2 * shift)) < shift
        ps = [_merge_pair(ps[2 * i], ps[2 * i + 1], shift, first) for i in range(len(ps) // 2)]
        shift //= 2
    return ps[0]


def _peer_act_kernel(idx_ref, h_ref, g_ref, tab_ref, o_ref, part_ref):
    tn = h_ref.shape[0]

    def products(t, slot):
        x = _token_row(h_ref, t)
        for gi in range(_GROUPS):
            rows = []
            for w in range(_HALF):
                rows += _expert_rows(tab_ref, idx_ref[t, gi * _HALF + w])
            prods = [rows[2 * (j % _HALF) + j // _HALF] * x for j in _MERGE_ORDER]
            part_ref[slot, pl.ds(gi * SUBLANES, SUBLANES), :] = _merge8(prods)

    def reduce(t, slot):
        o_ref[pl.ds(t, 1), :] = jnp.sum(part_ref[slot].T, axis=0, keepdims=True)

    part_ref[...] = jnp.zeros(part_ref.shape, F32)

    def two_tokens(i, carry):
        t = 2 * i
        reduce(jnp.maximum(t - 2, 0), 0)
        reduce(jnp.maximum(t - 1, 0), 1)
        products(t, 0)
        products(t + 1, 1)
        return carry

    lax.fori_loop(0, tn // 2, two_tokens, 0)
    reduce(tn - 2, 0)
    reduce(tn - 1, 1)
    o_ref[...] = g_ref[...] * jax.nn.gelu(o_ref[...], approximate=True)


def peer_act(offs, h, g, table):
    N = offs.shape[0]
    tn = _PEER_TOKENS
    return pl.pallas_call(
        _peer_act_kernel,
        grid=(N // tn,),
        in_specs=[pl.BlockSpec((tn, PEER_PAIRS // 2), lambda i: (i, 0), memory_space=pltpu.SMEM),
                  pl.BlockSpec((tn, SUBLANES * LANES), lambda i: (i, 0)),
                  pl.BlockSpec((tn, PEER_PAIRS), lambda i: (i, 0)),
                  pl.BlockSpec(memory_space=pltpu.VMEM)],
        out_specs=pl.BlockSpec((tn, PEER_PAIRS), lambda i: (i, 0)),
        out_shape=jax.ShapeDtypeStruct((N, PEER_PAIRS), F32),
        scratch_shapes=[pltpu.VMEM((2, PEER_PAIRS, LANES), F32)],
        compiler_params=_params("arbitrary"),
        name="peer_act",
    )(offs, h, g, table)


_MIX_CHAINS = 4


def _peer_mix_kernel(idx_ref, coef_ref, tab_ref, x_ref, gate_ref, o_ref, cb_ref):
    tn = x_ref.shape[0]

    def spread(t):
        row = coef_ref[pl.ds(t, 1), :]
        return jnp.broadcast_to(row, (PEER_PAIRS, PEER_PAIRS)).T

    cb_ref[...] = spread(0)

    def token(t, carry):
        accs = [None] * _MIX_CHAINS
        for w in range(PEER_PAIRS // 2):
            for k, row in enumerate(_expert_rows(tab_ref, idx_ref[t, w])):
                j = _pair_position(w, k)
                term = jnp.broadcast_to(cb_ref[pl.ds(j, 1), :], (SUBLANES, LANES)) * row
                a = j % _MIX_CHAINS
                accs[a] = term if accs[a] is None else accs[a] + term
        y = (accs[0] + accs[1]) + (accs[2] + accs[3])
        y_row = jnp.concatenate([y[s:s + 1] for s in range(SUBLANES)], axis=1)
        o_ref[pl.ds(t, 1), :] = x_ref[pl.ds(t, 1), :] + gate_ref[0] * y_row
        cb_ref[...] = spread(jnp.minimum(t + 1, tn - 1))
        return carry

    lax.fori_loop(0, tn, token, 0)


def peer_mix(offs, coef, table, x, gate, tokens_per_batch):
    N, D = x.shape
    tn = _PEER_TOKENS
    assert tokens_per_batch % tn == 0
    if gate.shape[0] == 1:
        gate_map = lambda i: (0, 0, 0)
    else:
        gate_map = lambda i: ((i * tn) // tokens_per_batch, 0, 0)
    return pl.pallas_call(
        _peer_mix_kernel,
        grid=(N // tn,),
        in_specs=[pl.BlockSpec((tn, PEER_PAIRS // 2), lambda i: (i, 0), memory_space=pltpu.SMEM),
                  pl.BlockSpec((tn, PEER_PAIRS), lambda i: (i, 0)),
                  pl.BlockSpec(memory_space=pltpu.VMEM),
                  pl.BlockSpec((tn, D), lambda i: (i, 0)),
                  pl.BlockSpec((1, 1, D), gate_map)],
        out_specs=pl.BlockSpec((tn, D), lambda i: (i, 0)),
        out_shape=jax.ShapeDtypeStruct((N, D), F32),
        scratch_shapes=[pltpu.VMEM((PEER_PAIRS, LANES), F32)],
        compiler_params=_params("arbitrary"),
        name="peer_mix",
    )(offs, coef, table, x, gate)


def _rmsnorm_kernel(x_ref, g_ref, o_ref):
    x = x_ref[0]
    o_ref[0] = x * lax.rsqrt(jnp.mean(x * x, axis=-1, keepdims=True) + EPS) * g_ref[...]


def rmsnorm(x, g):
    B, L, D = x.shape
    tm = min(512, L)
    return pl.pallas_call(
        _rmsnorm_kernel,
        grid=(B, L // tm),
        in_specs=[pl.BlockSpec((1, tm, D), lambda b, i: (b, i, 0)),
                  pl.BlockSpec((1, D), lambda b, i: (0, 0))],
        out_specs=pl.BlockSpec((1, tm, D), lambda b, i: (b, i, 0)),
        out_shape=jax.ShapeDtypeStruct((B, L, D), F32),
        compiler_params=_params("parallel", "parallel"),
        name="final_rmsnorm",
    )(x, g.reshape(1, D))


def _rope_tables(T):
    t = jnp.arange(T, dtype=jnp.int32)
    row = (t // GRID_W).astype(F32)
    col = (t % GRID_W).astype(F32)
    n_freq = HEAD_DIM // 4
    inv_freq = ROPE_THETA ** (-jnp.arange(n_freq, dtype=F32) / n_freq)
    ang = jnp.concatenate([row[:, None] * inv_freq, col[:, None] * inv_freq], axis=-1)
    cos, sin = jnp.cos(ang), jnp.sin(ang)
    return jnp.tile(cos, (1, 4)), jnp.tile(jnp.concatenate([-sin, sin], axis=-1), (1, 2))


def _peer_ffn_residual(x, g2, sc2, sh2, gate, w_q, keys, u_tab, v_tab):
    B, L, D = x.shape
    N = B * L
    q, h = norm_mod_proj(x, g2, sc2, sh2, w_q, (("bf16", w_q.shape[1], 1.0),), emit_h=True)
    offs, gates = peer_topk(q.reshape(N, -1), keys)
    coef = peer_act(offs, h.reshape(N, D), gates, u_tab)
    out = peer_mix(offs, coef, v_tab, x.reshape(N, D), gate, L)
    return out.reshape(B, L, D)


_Q_SCALE = HEAD_DIM ** -0.5
_NA_W = NA_HEADS * HEAD_DIM
_GQA_W = GQA_HEADS * HEAD_DIM
_GQA_KV_W = GQA_KV_HEADS * HEAD_DIM
_IN_OUTS = (("f32", 2 * CONV_CH), ("bf16", _NA_W, _Q_SCALE), ("norm", _GQA_W // LANES, _Q_SCALE, 0),
            ("bf16", _NA_W, 1.0), ("bf16", _NA_W, 1.0), ("norm", _GQA_KV_W // LANES, 1.0, 1),
            ("bf16", _GQA_KV_W, 1.0))
_OFF_G_Q = 2 * CONV_CH + _NA_W
_OFF_G_OUT = CONV_CH + _NA_W
_GQA_PAIR_ORDER = tuple(h for g in range(GQA_HEADS // GQA_KV_HEADS) for h in (g, GQA_HEADS // GQA_KV_HEADS + g))
assert GQA_KV_HEADS == 2 and 2 * HEAD_DIM == LANES


def _reorder_heads(w, start, axis):
    idx = jnp.concatenate([jnp.arange(HEAD_DIM) + start + h * HEAD_DIM for h in _GQA_PAIR_ORDER])
    full = jnp.arange(w.shape[axis]).at[start:start + _GQA_W].set(idx)
    return jnp.take(w, full, axis=axis)


def kernel(x, c, ctx, c_ctx, norm1_g, norm2_g, w_ada, b_ada, w_in, conv_w, conv_b, conv_ln_g, conv_ln_b,
           na_rel_bias, gqa_q_norm, gqa_k_norm, w_out, peer_w_q, peer_keys, peer_u, peer_v, final_norm_g):
    B, T, D = x.shape
    depth = w_in.shape[0]
    rope = _rope_tables(T)
    ada_rows = -(-(B + 1) // SUBLANES) * SUBLANES
    ada_in = jnp.zeros((ada_rows, D), F32).at[:B].set(c).at[B].set(c_ctx)

    for l in range(depth):
        last = l == depth - 1
        mod = ada_mod(ada_in, w_ada[l], b_ada[l])
        sh1, sc1, g1, sh2, sc2, g2 = [m.reshape(B, 1, D) for m in jnp.split(mod[:B], 6, axis=-1)]
        csh1, csc1, cg1, csh2, csc2, cg2 = [m.reshape(1, 1, D) for m in jnp.split(mod[B:B + 1], 6, axis=-1)]
        w_in_b = _reorder_heads(w_in[l], _OFF_G_Q, 1).astype(BF16)
        w_out_b = _reorder_heads(w_out[l], _OFF_G_OUT, 0).astype(BF16)
        w_q_b = peer_w_q[l].astype(BF16)
        keys_b = peer_keys[l].astype(BF16)
        u_tab = pack_expert_table(peer_u[l])
        v_tab = pack_expert_table(peer_v[l])
        qk_gains = jnp.stack([jnp.tile(gqa_q_norm[l], 2), jnp.tile(gqa_k_norm[l], 2)])

        za, na_q, g_q, na_k, na_v, g_k, g_v = norm_mod_proj(x, norm1_g[l], sc1, sh1, w_in_b, _IN_OUTS,
                                                            gains=qk_gains, rope=rope)
        zca, cna_q, cg_q, cna_k, cna_v, cg_k, cg_v = norm_mod_proj(ctx, norm1_g[l], csc1, csh1, w_in_b, _IN_OUTS,
                                                                   gains=qk_gains)
        a = conformer_conv(za, conv_w[l], conv_b[l], conv_ln_g[l], conv_ln_b[l])
        bm = neighbourhood_attention(na_q, na_k, na_v, cna_k, cna_v, na_bias_table(na_rel_bias[l]))
        gm = pair_attention(g_q, [(g_k, g_v), (cg_k, cg_v)])
        x = proj_residual((a, bm, gm), w_out_b, x, g1)

        x = _peer_ffn_residual(x, norm2_g[l], sc2, sh2, g2, w_q_b, keys_b, u_tab, v_tab)

        if not last:
            ac = conformer_conv(zca, conv_w[l], conv_b[l], conv_ln_g[l], conv_ln_b[l])
            bc = pair_attention(cna_q, [(cna_k, cna_v)])
            gc = pair_attention(cg_q, [(cg_k, cg_v)])
            ctx = proj_residual((ac, bc, gc), w_out_b, ctx, cg1)
            ctx = _peer_ffn_residual(ctx, norm2_g[l], csc2, csh2, cg2, w_q_b, keys_b, u_tab, v_tab)
    return rmsnorm(x, final_norm_g)
```

```python
import functools

import jax
import jax.numpy as jnp
from jax import lax
from jax.experimental import pallas as pl
from jax.experimental.pallas import tpu as pltpu

F32 = jnp.float32
BF16 = jnp.bfloat16

GRID_W = 64
HEAD_DIM = 64
CONV_CH = 256
CONV_WIDTH = 31
NA_HEADS = 6
NA_WIN_H = 8
NA_WIN_W = 16
GQA_HEADS = 6
GQA_KV_HEADS = 2
ROPE_THETA = 10000.0
PEER_HEADS = 8
PEER_N_KEYS = 128
PEER_D_KEY = 128
PEER_TOPK = 16
EPS = 1e-6

PEER_PAIRS = PEER_HEADS * PEER_TOPK

LANES = 128
SUBLANES = 8
_HALF = SUBLANES // 2
assert PEER_N_KEYS * PEER_N_KEYS * _HALF <= 1 << 16
VMEM_LIMIT = 48 * 1024 * 1024
MASK_VALUE = -1e30


def _params(*sem):
    return pltpu.CompilerParams(dimension_semantics=sem, vmem_limit_bytes=VMEM_LIMIT)


def _ada_kernel(a_ref, w_ref, b_ref, o_ref):
    a = a_ref[...]
    s = a * jax.nn.sigmoid(a)
    o_ref[...] = jnp.dot(s.astype(BF16), w_ref[...].astype(BF16), preferred_element_type=F32) + b_ref[...]


def ada_mod(a, w, b):
    R, D = a.shape
    N = w.shape[1]
    tn = 1024
    return pl.pallas_call(
        _ada_kernel,
        grid=(N // tn,),
        in_specs=[pl.BlockSpec((R, D), lambda j: (0, 0)),
                  pl.BlockSpec((D, tn), lambda j: (0, j)),
                  pl.BlockSpec((1, tn), lambda j: (0, j))],
        out_specs=pl.BlockSpec((R, tn), lambda j: (0, j)),
        out_shape=jax.ShapeDtypeStruct((R, N), F32),
        compiler_params=_params("parallel"),
        name="ada_mod",
    )(a, w, b.reshape(1, N))


def _pair_norm_rope(x, gain, rope_cos, rope_sin):
    lane = lax.broadcasted_iota(jnp.int32, x.shape, 1)
    lower = lane < HEAD_DIM
    sq = x * x
    ms_lo = jnp.sum(jnp.where(lower, sq, 0.0), axis=-1, keepdims=True) * (1.0 / HEAD_DIM)
    ms_hi = jnp.sum(jnp.where(lower, 0.0, sq), axis=-1, keepdims=True) * (1.0 / HEAD_DIM)
    y = x * jnp.where(lower, lax.rsqrt(ms_lo + EPS), lax.rsqrt(ms_hi + EPS)) * gain
    if rope_cos is None:
        return y
    half = HEAD_DIM // 2
    first = (lane % HEAD_DIM) < half
    partner = jnp.where(first, pltpu.roll(y, LANES - half, axis=1), pltpu.roll(y, half, axis=1))
    return y * rope_cos + partner * rope_sin


def _norm_mod_proj_kernel(*refs, outs, emit_h, has_gains, rope):
    x_ref, g_ref, sc_ref, sh_ref, w_ref = refs[:5]
    n_in = 5
    gains_ref = cos = sin = None
    if has_gains:
        gains_ref = refs[n_in]
        n_in += 1
    if rope:
        cos, sin = refs[n_in][...], refs[n_in + 1][...]
        n_in += 2
    out_refs = refs[n_in:]
    x = x_ref[0]
    y = x * lax.rsqrt(jnp.mean(x * x, axis=-1, keepdims=True) + EPS) * g_ref[...]
    h = y * (1.0 + sc_ref[0]) + sh_ref[0]
    z = jnp.dot(h.astype(BF16), w_ref[...], preferred_element_type=F32)
    off = 0
    for o_ref, spec in zip(out_refs, outs):
        if spec[0] == "f32":
            o_ref[0] = z[:, off:off + spec[1]]
            off += spec[1]
        elif spec[0] == "bf16":
            o_ref[0] = (z[:, off:off + spec[1]] * spec[2]).astype(BF16)
            off += spec[1]
        else:
            groups = []
            for _ in range(spec[1]):
                v = _pair_norm_rope(z[:, off:off + LANES], gains_ref[spec[3]:spec[3] + 1, :], cos, sin)
                groups.append((v * spec[2]).astype(BF16))
                off += LANES
            o_ref[0] = groups[0] if len(groups) == 1 else jnp.concatenate(groups, axis=-1)
    if emit_h:
        out_refs[len(outs)][0] = h


def _out_width(spec):
    return spec[1] * LANES if spec[0] == "norm" else spec[1]


def norm_mod_proj(x, g, sc, sh, w, outs, emit_h=False, gains=None, rope=None):
    B, L, D = x.shape
    N = w.shape[1]
    assert sum(_out_width(o) for o in outs) == N
    tm = min(256, L)
    per_batch = sc.shape[0] == B
    mod_map = (lambda b, i: (b, 0, 0)) if per_batch else (lambda b, i: (0, 0, 0))
    in_specs = [pl.BlockSpec((1, tm, D), lambda b, i: (b, i, 0)),
                pl.BlockSpec((1, D), lambda b, i: (0, 0)),
                pl.BlockSpec((1, 1, D), mod_map),
                pl.BlockSpec((1, 1, D), mod_map),
                pl.BlockSpec((D, N), lambda b, i: (0, 0))]
    args = [x, g.reshape(1, D), sc, sh, w]
    if gains is not None:
        in_specs.append(pl.BlockSpec(gains.shape, lambda b, i: (0, 0)))
        args.append(gains)
    if rope is not None:
        for tab in rope:
            in_specs.append(pl.BlockSpec((tm, LANES), lambda b, i: (i, 0)))
            args.append(tab)
    out_shape = [jax.ShapeDtypeStruct((B, L, _out_width(o)), F32 if o[0] == "f32" else BF16) for o in outs]
    out_specs = [pl.BlockSpec((1, tm, _out_width(o)), lambda b, i: (b, i, 0)) for o in outs]
    if emit_h:
        out_shape.append(jax.ShapeDtypeStruct((B, L, D), F32))
        out_specs.append(pl.BlockSpec((1, tm, D), lambda b, i: (b, i, 0)))
    return pl.pallas_call(
        functools.partial(_norm_mod_proj_kernel, outs=outs, emit_h=emit_h, has_gains=gains is not None,
                          rope=rope is not None),
        grid=(B, L // tm),
        in_specs=in_specs,
        out_specs=out_specs,
        out_shape=out_shape,
        compiler_params=_params("parallel", "parallel"),
        name="norm_mod_proj",
    )(*args)


_CONV_PAD = 16


def _conv_kernel(za_ref, w_ref, b_ref, lg_ref, lb_ref, o_ref, upad_ref, *, L, tc):
    c = pl.program_id(1)

    @pl.when(c == 0)
    def _():
        val = za_ref[0, :, :CONV_CH]
        gate = za_ref[0, :, CONV_CH:]
        zeros = jnp.zeros((_CONV_PAD, CONV_CH), F32)
        upad_ref[pl.ds(0, _CONV_PAD), :] = zeros
        upad_ref[pl.ds(_CONV_PAD + L, _CONV_PAD), :] = zeros
        upad_ref[pl.ds(_CONV_PAD, L), :] = val * jax.nn.sigmoid(gate)

    start = pl.multiple_of(c * tc, SUBLANES)
    win = upad_ref[pl.ds(start, tc + 2 * _CONV_PAD), :]
    acc = jnp.zeros((tc, CONV_CH), F32)
    first = _CONV_PAD - CONV_WIDTH // 2
    rows = tc + 2 * _CONV_PAD
    shifted = [win if c == 0 else pltpu.roll(win, rows - c, axis=0) for c in range(SUBLANES)]
    for j in range(CONV_WIDTH):
        a, c = divmod(first + j, SUBLANES)
        acc = acc + shifted[c][a * SUBLANES:a * SUBLANES + tc, :] * w_ref[j:j + 1, :]
    y = acc + b_ref[...]
    mu = jnp.mean(y, axis=-1, keepdims=True)
    d = y - mu
    var = jnp.mean(d * d, axis=-1, keepdims=True)
    yn = d * lax.rsqrt(var + EPS) * lg_ref[...] + lb_ref[...]
    o_ref[0] = (yn * jax.nn.sigmoid(yn)).astype(o_ref.dtype)


def conformer_conv(za, w_dw, b_dw, ln_g, ln_b):
    B, L, _ = za.shape
    tc = min(256, L)
    vec = lambda v: v.reshape(1, CONV_CH)
    return pl.pallas_call(
        functools.partial(_conv_kernel, L=L, tc=tc),
        grid=(B, L // tc),
        in_specs=[pl.BlockSpec((1, L, 2 * CONV_CH), lambda b, c: (b, 0, 0)),
                  pl.BlockSpec((CONV_WIDTH, CONV_CH), lambda b, c: (0, 0)),
                  pl.BlockSpec((1, CONV_CH), lambda b, c: (0, 0)),
                  pl.BlockSpec((1, CONV_CH), lambda b, c: (0, 0)),
                  pl.BlockSpec((1, CONV_CH), lambda b, c: (0, 0))],
        out_specs=pl.BlockSpec((1, tc, CONV_CH), lambda b, c: (b, c, 0)),
        out_shape=jax.ShapeDtypeStruct((B, L, CONV_CH), BF16),
        scratch_shapes=[pltpu.VMEM((L + 2 * _CONV_PAD, CONV_CH), F32)],
        compiler_params=_params("parallel", "arbitrary"),
        name="conformer_conv",
    )(za, w_dw, vec(b_dw), vec(ln_g), vec(ln_b))


def _na_row_offset(r, rows):
    return r - jnp.clip(r - NA_WIN_H // 2, 0, rows - NA_WIN_H)


def _head_pair_rows(qg, lower):
    zero = jnp.zeros_like(qg)
    return jnp.concatenate([jnp.where(lower, qg, zero), jnp.where(lower, zero, qg)], axis=0)


def _na_kernel(q_ref, k_ref, v_ref, kc_ref, vc_ref, bias_ref, o_ref, *, rows):
    r = pl.program_id(1)
    r0 = r - _na_row_offset(r, rows)
    band = pl.ds(pl.multiple_of(r0 * GRID_W, GRID_W), NA_WIN_H * GRID_W)
    nt = (((1,), (1,)), ((), ()))
    lower = lax.broadcasted_iota(jnp.int32, (GRID_W, LANES), 1) < HEAD_DIM
    groups = range(NA_HEADS // 2)
    cols = [slice(g * LANES, (g + 1) * LANES) for g in groups]
    qs = [_head_pair_rows(q_ref[0, :, cols[g]], lower) for g in groups]
    s_loc = [lax.dot_general(qs[g], k_ref[0, band, cols[g]], nt, preferred_element_type=F32) + bias_ref[0, g]
             for g in groups]
    s_ctx = [lax.dot_general(qs[g], kc_ref[0, :, cols[g]], nt, preferred_element_type=F32) for g in groups]
    p_loc, p_ctx, den = [], [], []
    for g in groups:
        m = jnp.maximum(jnp.max(s_loc[g], axis=-1, keepdims=True), jnp.max(s_ctx[g], axis=-1, keepdims=True))
        p_loc.append(jnp.exp(s_loc[g] - m))
        p_ctx.append(jnp.exp(s_ctx[g] - m))
        den.append(jnp.sum(p_loc[g], axis=-1, keepdims=True) + jnp.sum(p_ctx[g], axis=-1, keepdims=True))
    outs = []
    for g in groups:
        o = (jnp.dot(p_loc[g].astype(BF16), v_ref[0, band, cols[g]], preferred_element_type=F32)
             + jnp.dot(p_ctx[g].astype(BF16), vc_ref[0, :, cols[g]], preferred_element_type=F32)) / den[g]
        outs.append(jnp.where(lower, o[:GRID_W], o[GRID_W:]).astype(o_ref.dtype))
    o_ref[0] = jnp.concatenate(outs, axis=-1)


def na_bias_table(rel_bias):
    cols = jnp.arange(GRID_W, dtype=jnp.int32)
    c0 = jnp.clip(cols - NA_WIN_W // 2, 0, GRID_W - NA_WIN_W)
    in_win = (cols[None, :] >= c0[:, None]) & (cols[None, :] < c0[:, None] + NA_WIN_W)
    col_idx = jnp.clip(cols[None, :] - cols[:, None] + (NA_WIN_W - 1), 0, 2 * NA_WIN_W - 2)
    off = jnp.arange(NA_WIN_H, dtype=jnp.int32)
    row_idx = off[None, :] - off[:, None] + (NA_WIN_H - 1)
    t = rel_bias[:, row_idx]
    t = t[..., col_idx]
    t = jnp.where(in_win[None, None, None], t, MASK_VALUE)
    return t.transpose(1, 0, 3, 2, 4).reshape(NA_WIN_H, NA_HEADS // 2, 2 * GRID_W, NA_WIN_H * GRID_W)


def neighbourhood_attention(q, k, v, kc, vc, bias):
    B, T, W = q.shape
    C = kc.shape[1]
    rows = T // GRID_W
    assert rows >= NA_WIN_H and W == NA_HEADS * HEAD_DIM
    return pl.pallas_call(
        functools.partial(_na_kernel, rows=rows),
        grid=(B, rows),
        in_specs=[pl.BlockSpec((1, GRID_W, W), lambda b, r: (b, r, 0)),
                  pl.BlockSpec((1, T, W), lambda b, r: (b, 0, 0)),
                  pl.BlockSpec((1, T, W), lambda b, r: (b, 0, 0)),
                  pl.BlockSpec((1, C, W), lambda b, r: (b, 0, 0)),
                  pl.BlockSpec((1, C, W), lambda b, r: (b, 0, 0)),
                  pl.BlockSpec((1,) + bias.shape[1:], lambda b, r: (_na_row_offset(r, rows), 0, 0, 0))],
        out_specs=pl.BlockSpec((1, GRID_W, W), lambda b, r: (b, r, 0)),
        out_shape=jax.ShapeDtypeStruct((B, T, W), BF16),
        compiler_params=_params("parallel", "arbitrary"),
        name="neighbourhood_attention",
    )(q, k, v, kc, vc, bias)


def _pair_attn_kernel(*refs, Gq, Gk, bq, n_sets):
    q_ref = refs[0]
    kv_refs = refs[1:1 + 2 * n_sets]
    o_ref = refs[1 + 2 * n_sets]
    nt = (((1,), (1,)), ((), ()))
    sets = range(n_sets)
    lower = lax.broadcasted_iota(jnp.int32, (bq, LANES), 1) < HEAD_DIM
    col = lambda g: slice(g * LANES, (g + 1) * LANES)
    pairs = [_head_pair_rows(q_ref[0, :, col(g)], lower) for g in range(Gq)]
    if Gk == 1:
        jobs = [(jnp.concatenate([p[:bq] for p in pairs], axis=0), col(0)),
                (jnp.concatenate([p[bq:] for p in pairs], axis=0), col(0))]
    else:
        jobs = [(pairs[g], col(g)) for g in range(Gq)]
    ss = [[lax.dot_general(rows, kv_refs[2 * i][0, :, kc], nt, preferred_element_type=F32) for i in sets]
          for rows, kc in jobs]
    ps, dens = [], []
    for s in ss:
        m = functools.reduce(jnp.maximum, [jnp.max(x, axis=-1, keepdims=True) for x in s])
        ps.append([jnp.exp(x - m) for x in s])
        dens.append(functools.reduce(jnp.add, [jnp.sum(p, axis=-1, keepdims=True) for p in ps[-1]]))
    os_ = []
    for (rows, kc), p, den in zip(jobs, ps, dens):
        o = functools.reduce(jnp.add, [jnp.dot(p[i].astype(BF16), kv_refs[2 * i + 1][0, :, kc],
                                               preferred_element_type=F32) for i in sets])
        os_.append(o / den)
    outs = []
    for g in range(Gq):
        if Gk == 1:
            first, second = os_[0][g * bq:(g + 1) * bq], os_[1][g * bq:(g + 1) * bq]
        else:
            first, second = os_[g][:bq], os_[g][bq:]
        outs.append(jnp.where(lower, first, second).astype(o_ref.dtype))
    o_ref[0] = jnp.concatenate(outs, axis=-1)


def pair_attention(q, kv_sets):
    B, L, Wq = q.shape
    Wk = kv_sets[0][0].shape[2]
    Gq, Gk = Wq // LANES, Wk // LANES
    assert Gk in (1, Gq)
    bq = min(256, L)
    in_specs = [pl.BlockSpec((1, bq, Wq), lambda b, i: (b, i, 0))]
    args = [q]
    for k, v in kv_sets:
        for t in (k, v):
            in_specs.append(pl.BlockSpec((1, t.shape[1], Wk), lambda b, i: (b, 0, 0)))
            args.append(t)
    return pl.pallas_call(
        functools.partial(_pair_attn_kernel, Gq=Gq, Gk=Gk, bq=bq, n_sets=len(kv_sets)),
        grid=(B, L // bq),
        in_specs=in_specs,
        out_specs=pl.BlockSpec((1, bq, Wq), lambda b, i: (b, i, 0)),
        out_shape=jax.ShapeDtypeStruct((B, L, Wq), BF16),
        compiler_params=_params("parallel", "arbitrary"),
        name="pair_attention",
    )(*args)


def _proj_residual_kernel(a_ref, b_ref, g_ref, wa_ref, wb_ref, wg_ref, x_ref, gate_ref, o_ref):
    y = (jnp.dot(a_ref[0], wa_ref[...], preferred_element_type=F32)
         + jnp.dot(b_ref[0], wb_ref[...], preferred_element_type=F32)
         + jnp.dot(g_ref[0], wg_ref[...], preferred_element_type=F32))
    o_ref[0] = x_ref[0] + gate_ref[0] * y


def proj_residual(parts, w, x, gate):
    B, L, D = x.shape
    tm = min(512, L)
    gate_map = (lambda b, i: (b, 0, 0)) if gate.shape[0] == B else (lambda b, i: (0, 0, 0))
    widths = [p.shape[2] for p in parts]
    assert sum(widths) == w.shape[0]
    starts = [sum(widths[:i]) for i in range(len(parts))]
    ws = [w[s:s + k] for s, k in zip(starts, widths)]
    return pl.pallas_call(
        _proj_residual_kernel,
        grid=(B, L // tm),
        in_specs=[pl.BlockSpec((1, tm, k), lambda b, i: (b, i, 0)) for k in widths]
        + [pl.BlockSpec((k, D), lambda b, i: (0, 0)) for k in widths]
        + [pl.BlockSpec((1, tm, D), lambda b, i: (b, i, 0)), pl.BlockSpec((1, 1, D), gate_map)],
        out_specs=pl.BlockSpec((1, tm, D), lambda b, i: (b, i, 0)),
        out_shape=jax.ShapeDtypeStruct((B, L, D), F32),
        compiler_params=_params("parallel", "parallel"),
        name="proj_residual",
    )(*parts, *ws, x, gate)


def _top_rows(s, k, order=None, payload=None):
    if order is None:
        order = lax.broadcasted_iota(jnp.int32, s.shape, 0).astype(F32)
    vals, picks = [], []
    for _ in range(k):
        m = jnp.max(s, axis=0, keepdims=True)
        first = jnp.min(jnp.where(s == m, order, jnp.inf), axis=0, keepdims=True)
        hit = order == first
        vals.append(m)
        if payload is None:
            picks.append(first)
        else:
            picks.append(jnp.max(jnp.where(hit, payload, -1.0), axis=0, keepdims=True))
        s = jnp.where(hit, -jnp.inf, s)
    return jnp.concatenate(vals, axis=0), jnp.concatenate(picks, axis=0)


def _pair_candidates(s0, i0, s1, i1):
    K = PEER_TOPK
    t = s0.shape[1]
    sub = lax.broadcasted_iota(jnp.int32, (SUBLANES, t), 0)
    scores, order, ids = [], [], []

    def add(a_rows, b_rows, a_of_row, b_of_row):
        sa, ia = a_rows
        sb, ib = b_rows
        ok = (a_of_row + 1) * (b_of_row + 1) <= K
        scores.append(jnp.where(ok, sa + sb, -jnp.inf))
        order.append((a_of_row * K + b_of_row).astype(F32))
        ids.append(ia * float(PEER_N_KEYS) + ib)

    row = lambda x, r: (x[0][r:r + 1], x[1][r:r + 1])
    rows = lambda x, r: (x[0][r:r + SUBLANES], x[1][r:r + SUBLANES])
    A, Bv = (s0, i0), (s1, i1)
    add(row(A, 0), rows(Bv, 0), jnp.zeros_like(sub), sub)
    add(row(A, 0), rows(Bv, SUBLANES), jnp.zeros_like(sub), sub + SUBLANES)
    for a in range(1, 4):
        add(row(A, a), rows(Bv, 0), jnp.full_like(sub, a), sub)
    for b in range(3):
        dup = sub < 4
        sa, ia = rows(A, 0)
        add((jnp.where(dup, -jnp.inf, sa), ia), row(Bv, b), sub, jnp.full_like(sub, b))
    add(rows(A, SUBLANES), row(Bv, 0), sub + SUBLANES, jnp.zeros_like(sub))
    cat = lambda xs: jnp.concatenate(xs, axis=0)
    return cat(scores), cat(order), cat(ids)


def _peer_topk_kernel(q_ref, keys_ref, offs_ref, g_ref):
    K = PEER_TOPK
    assert K == 2 * SUBLANES
    nt = (((1,), (1,)), ((), ()))
    words, gates = [], []
    for h in range(PEER_HEADS):
        tops = []
        for p in range(2):
            col = (2 * h + p) * PEER_D_KEY
            qhp = q_ref[:, col:col + PEER_D_KEY]
            s = lax.dot_general(keys_ref[h, p], qhp, nt, preferred_element_type=F32)
            tops.append(_top_rows(s, K))
        (s0, i0), (s1, i1) = tops
        cand_s, cand_order, cand_i = _pair_candidates(s0, i0, s1, i1)
        best_s, best_i = _top_rows(cand_s, K, order=cand_order, payload=cand_i)
        e = jnp.exp(best_s - jnp.max(best_s, axis=0, keepdims=True))
        gates.append(e / jnp.sum(e, axis=0, keepdims=True))
        off = best_i.astype(jnp.int32) * _HALF
        lo = jnp.concatenate([off[0:_HALF], off[SUBLANES:SUBLANES + _HALF]], axis=0)
        hi = jnp.concatenate([off[_HALF:SUBLANES], off[SUBLANES + _HALF:K]], axis=0)
        words.append(lo | (hi << 16))
    g_ref[...] = jnp.concatenate(gates, axis=0).T
    words.append(jnp.zeros((PEER_PAIRS // 2, q_ref.shape[0]), jnp.int32))
    offs_ref[...] = jnp.concatenate(words, axis=0).T[:, :PEER_PAIRS // 2]


def peer_topk(q, keys):
    N = q.shape[0]
    tm = LANES
    return pl.pallas_call(
        _peer_topk_kernel,
        grid=(N // tm,),
        in_specs=[pl.BlockSpec((tm, q.shape[1]), lambda i: (i, 0)),
                  pl.BlockSpec(keys.shape, lambda i: (0, 0, 0, 0))],
        out_specs=[pl.BlockSpec((tm, PEER_PAIRS // 2), lambda i: (i, 0)),
                   pl.BlockSpec((tm, PEER_PAIRS), lambda i: (i, 0))],
        out_shape=[jax.ShapeDtypeStruct((N, PEER_PAIRS // 2), jnp.int32),
                   jax.ShapeDtypeStruct((N, PEER_PAIRS), F32)],
        compiler_params=_params("parallel"),
        name="peer_topk",
    )(q, keys)


_PEER_TOKENS = 64
_GROUPS = PEER_PAIRS // SUBLANES


def _pack_table_kernel(w_ref, o_ref):
    te = w_ref.shape[0]
    bits = lambda x: pltpu.bitcast(x.astype(BF16).astype(F32), jnp.uint32)
    for i in range(_HALF):
        lo = bits(w_ref[:, (2 * i) * LANES:(2 * i + 1) * LANES]) >> 16
        hi = bits(w_ref[:, (2 * i + 1) * LANES:(2 * i + 2) * LANES])
        o_ref[pl.ds(i, te, stride=_HALF), :] = lo | hi


def pack_expert_table(w):
    E, D = w.shape
    assert D == SUBLANES * LANES
    te = 256
    return pl.pallas_call(
        _pack_table_kernel,
        grid=(E // te,),
        in_specs=[pl.BlockSpec((te, D), lambda i: (i, 0))],
        out_specs=pl.BlockSpec((te * _HALF, LANES), lambda i: (i, 0)),
        out_shape=jax.ShapeDtypeStruct((E * _HALF, LANES), jnp.uint32),
        compiler_params=_params("parallel"),
        name="pack_expert_table",
    )(w)


def _pair_position(word, half):
    return SUBLANES * (word // _HALF) + _HALF * half + word % _HALF


def _expert_rows(tab_ref, word):
    starts = (word & 0xFFFF, lax.shift_right_logical(word, 16))
    return [pltpu.bitcast(tab_ref[pl.ds(pl.multiple_of(s, _HALF), _HALF), :], BF16).astype(F32) for s in starts]


def _token_row(ref, t):
    row = ref[pl.ds(t, 1), :]
    return jnp.concatenate([row[:, s * LANES:(s + 1) * LANES] for s in range(SUBLANES)], axis=0)


def _merge_pair(a, b, shift, first):
    if shift == _HALF:
        return jnp.where(first, a, b) + pltpu.roll(jnp.where(first, b, a), shift, axis=0)
    bs = pltpu.roll(b, shift, axis=0)
    return jnp.where(first, a, bs) + pltpu.roll(jnp.where(first, bs, a), SUBLANES - shift, axis=0)


_MERGE_ORDER = (0, 4, 2, 6, 1, 5, 3, 7)


def _merge8(ps):
    sub = lax.broadcasted_iota(jnp.int32, (SUBLANES, LANES), 0)
    shift = _HALF
    while len(ps) > 1:
        first = (sub % (2 * shift)) < shift
        ps = [_merge_pair(ps[2 * i], ps[2 * i + 1], shift, first) for i in range(len(ps) // 2)]
        shift //= 2
    return ps[0]


def _peer_act_kernel(idx_ref, h_ref, g_ref, tab_ref, o_ref, part_ref):
    tn = h_ref.shape[0]

    def products(t, slot):
        x = _token_row(h_ref, t)
        for gi in range(_GROUPS):
            rows = []
            for w in range(_HALF):
                rows += _expert_rows(tab_ref, idx_ref[t, gi * _HALF + w])
            prods = [rows[2 * (j % _HALF) + j // _HALF] * x for j in _MERGE_ORDER]
            part_ref[slot, pl.ds(gi * SUBLANES, SUBLANES), :] = _merge8(prods)

    def reduce(t, slot):
        o_ref[pl.ds(t, 1), :] = jnp.sum(part_ref[slot].T, axis=0, keepdims=True)

    part_ref[...] = jnp.zeros(part_ref.shape, F32)

    def two_tokens(i, carry):
        t = 2 * i
        reduce(jnp.maximum(t - 2, 0), 0)
        reduce(jnp.maximum(t - 1, 0), 1)
        products(t, 0)
        products(t + 1, 1)
        return carry

    lax.fori_loop(0, tn // 2, two_tokens, 0)
    reduce(tn - 2, 0)
    reduce(tn - 1, 1)
    o_ref[...] = g_ref[...] * jax.nn.gelu(o_ref[...], approximate=True)


def peer_act(offs, h, g, table):
    N = offs.shape[0]
    tn = _PEER_TOKENS
    return pl.pallas_call(
        _peer_act_kernel,
        grid=(N // tn,),
        in_specs=[pl.BlockSpec((tn, PEER_PAIRS // 2), lambda i: (i, 0), memory_space=pltpu.SMEM),
                  pl.BlockSpec((tn, SUBLANES * LANES), lambda i: (i, 0)),
                  pl.BlockSpec((tn, PEER_PAIRS), lambda i: (i, 0)),
                  pl.BlockSpec(memory_space=pltpu.VMEM)],
        out_specs=pl.BlockSpec((tn, PEER_PAIRS), lambda i: (i, 0)),
        out_shape=jax.ShapeDtypeStruct((N, PEER_PAIRS), F32),
        scratch_shapes=[pltpu.VMEM((2, PEER_PAIRS, LANES), F32)],
        compiler_params=_params("arbitrary"),
        name="peer_act",
    )(offs, h, g, table)


_MIX_CHAINS = 4


def _peer_mix_kernel(idx_ref, coef_ref, tab_ref, x_ref, gate_ref, o_ref, cb_ref):
    tn = x_ref.shape[0]

    def spread(t):
        row = coef_ref[pl.ds(t, 1), :]
        return jnp.broadcast_to(row, (PEER_PAIRS, PEER_PAIRS)).T

    cb_ref[...] = spread(0)

    def token(t, carry):
        accs = [None] * _MIX_CHAINS
        for w in range(PEER_PAIRS // 2):
            for k, row in enumerate(_expert_rows(tab_ref, idx_ref[t, w])):
                j = _pair_position(w, k)
                term = jnp.broadcast_to(cb_ref[pl.ds(j, 1), :], (SUBLANES, LANES)) * row
                a = j % _MIX_CHAINS
                accs[a] = term if accs[a] is None else accs[a] + term
        y = (accs[0] + accs[1]) + (accs[2] + accs[3])
        y_row = jnp.concatenate([y[s:s + 1] for s in range(SUBLANES)], axis=1)
        o_ref[pl.ds(t, 1), :] = x_ref[pl.ds(t, 1), :] + gate_ref[0] * y_row
        cb_ref[...] = spread(jnp.minimum(t + 1, tn - 1))
        return carry

    lax.fori_loop(0, tn, token, 0)


def peer_mix(offs, coef, table, x, gate, tokens_per_batch):
    N, D = x.shape
    tn = _PEER_TOKENS
    assert tokens_per_batch % tn == 0
    if gate.shape[0] == 1:
        gate_map = lambda i: (0, 0, 0)
    else:
        gate_map = lambda i: ((i * tn) // tokens_per_batch, 0, 0)
    return pl.pallas_call(
        _peer_mix_kernel,
        grid=(N // tn,),
        in_specs=[pl.BlockSpec((tn, PEER_PAIRS // 2), lambda i: (i, 0), memory_space=pltpu.SMEM),
                  pl.BlockSpec((tn, PEER_PAIRS), lambda i: (i, 0)),
                  pl.BlockSpec(memory_space=pltpu.VMEM),
                  pl.BlockSpec((tn, D), lambda i: (i, 0)),
                  pl.BlockSpec((1, 1, D), gate_map)],
        out_specs=pl.BlockSpec((tn, D), lambda i: (i, 0)),
        out_shape=jax.ShapeDtypeStruct((N, D), F32),
        scratch_shapes=[pltpu.VMEM((PEER_PAIRS, LANES), F32)],
        compiler_params=_params("arbitrary"),
        name="peer_mix",
    )(offs, coef, table, x, gate)


def _rmsnorm_kernel(x_ref, g_ref, o_ref):
    x = x_ref[0]
    o_ref[0] = x * lax.rsqrt(jnp.mean(x * x, axis=-1, keepdims=True) + EPS) * g_ref[...]


def rmsnorm(x, g):
    B, L, D = x.shape
    tm = min(512, L)
    return pl.pallas_call(
        _rmsnorm_kernel,
        grid=(B, L // tm),
        in_specs=[pl.BlockSpec((1, tm, D), lambda b, i: (b, i, 0)),
                  pl.BlockSpec((1, D), lambda b, i: (0, 0))],
        out_specs=pl.BlockSpec((1, tm, D), lambda b, i: (b, i, 0)),
        out_shape=jax.ShapeDtypeStruct((B, L, D), F32),
        compiler_params=_params("parallel", "parallel"),
        name="final_rmsnorm",
    )(x, g.reshape(1, D))


def _rope_tables(T):
    t = jnp.arange(T, dtype=jnp.int32)
    row = (t // GRID_W).astype(F32)
    col = (t % GRID_W).astype(F32)
    n_freq = HEAD_DIM // 4
    inv_freq = ROPE_THETA ** (-jnp.arange(n_freq, dtype=F32) / n_freq)
    ang = jnp.concatenate([row[:, None] * inv_freq, col[:, None] * inv_freq], axis=-1)
    cos, sin = jnp.cos(ang), jnp.sin(ang)
    return jnp.tile(cos, (1, 4)), jnp.tile(jnp.concatenate([-sin, sin], axis=-1), (1, 2))


def _peer_ffn_residual(x, g2, sc2, sh2, gate, w_q, keys, u_tab, v_tab):
    B, L, D = x.shape
    N = B * L
    q, h = norm_mod_proj(x, g2, sc2, sh2, w_q, (("bf16", w_q.shape[1], 1.0),), emit_h=True)
    offs, gates = peer_topk(q.reshape(N, -1), keys)
    coef = peer_act(offs, h.reshape(N, D), gates, u_tab)
    out = peer_mix(offs, coef, v_tab, x.reshape(N, D), gate, L)
    return out.reshape(B, L, D)


_Q_SCALE = HEAD_DIM ** -0.5
_NA_W = NA_HEADS * HEAD_DIM
_GQA_W = GQA_HEADS * HEAD_DIM
_GQA_KV_W = GQA_KV_HEADS * HEAD_DIM
_IN_OUTS = (("f32", 2 * CONV_CH), ("bf16", _NA_W, _Q_SCALE), ("norm", _GQA_W // LANES, _Q_SCALE, 0),
            ("bf16", _NA_W, 1.0), ("bf16", _NA_W, 1.0), ("norm", _GQA_KV_W // LANES, 1.0, 1),
            ("bf16", _GQA_KV_W, 1.0))
_OFF_G_Q = 2 * CONV_CH + _NA_W
_OFF_G_OUT = CONV_CH + _NA_W
_GQA_PAIR_ORDER = tuple(h for g in range(GQA_HEADS // GQA_KV_HEADS) for h in (g, GQA_HEADS // GQA_KV_HEADS + g))
assert GQA_KV_HEADS == 2 and 2 * HEAD_DIM == LANES


def _reorder_heads(w, start, axis):
    idx = jnp.concatenate([jnp.arange(HEAD_DIM) + start + h * HEAD_DIM for h in _GQA_PAIR_ORDER])
    full = jnp.arange(w.shape[axis]).at[start:start + _GQA_W].set(idx)
    return jnp.take(w, full, axis=axis)


def kernel(x, c, ctx, c_ctx, norm1_g, norm2_g, w_ada, b_ada, w_in, conv_w, conv_b, conv_ln_g, conv_ln_b,
           na_rel_bias, gqa_q_norm, gqa_k_norm, w_out, peer_w_q, peer_keys, peer_u, peer_v, final_norm_g):
    B, T, D = x.shape
    depth = w_in.shape[0]
    rope = _rope_tables(T)
    ada_rows = -(-(B + 1) // SUBLANES) * SUBLANES
    ada_in = jnp.zeros((ada_rows, D), F32).at[:B].set(c).at[B].set(c_ctx)

    for l in range(depth):
        last = l == depth - 1
        mod = ada_mod(ada_in, w_ada[l], b_ada[l])
        sh1, sc1, g1, sh2, sc2, g2 = [m.reshape(B, 1, D) for m in jnp.split(mod[:B], 6, axis=-1)]
        csh1, csc1, cg1, csh2, csc2, cg2 = [m.reshape(1, 1, D) for m in jnp.split(mod[B:B + 1], 6, axis=-1)]
        w_in_b = _reorder_heads(w_in[l], _OFF_G_Q, 1).astype(BF16)
        w_out_b = _reorder_heads(w_out[l], _OFF_G_OUT, 0).astype(BF16)
        w_q_b = peer_w_q[l].astype(BF16)
        keys_b = peer_keys[l].astype(BF16)
        u_tab = pack_expert_table(peer_u[l])
        v_tab = pack_expert_table(peer_v[l])
        qk_gains = jnp.stack([jnp.tile(gqa_q_norm[l], 2), jnp.tile(gqa_k_norm[l], 2)])

        za, na_q, g_q, na_k, na_v, g_k, g_v = norm_mod_proj(x, norm1_g[l], sc1, sh1, w_in_b, _IN_OUTS,
                                                            gains=qk_gains, rope=rope)
        zca, cna_q, cg_q, cna_k, cna_v, cg_k, cg_v = norm_mod_proj(ctx, norm1_g[l], csc1, csh1, w_in_b, _IN_OUTS,
                                                                   gains=qk_gains)
        a = conformer_conv(za, conv_w[l], conv_b[l], conv_ln_g[l], conv_ln_b[l])
        bm = neighbourhood_attention(na_q, na_k, na_v, cna_k, cna_v, na_bias_table(na_rel_bias[l]))
        gm = pair_attention(g_q, [(g_k, g_v), (cg_k, cg_v)])
        x = proj_residual((a, bm, gm), w_out_b, x, g1)

        x = _peer_ffn_residual(x, norm2_g[l], sc2, sh2, g2, w_q_b, keys_b, u_tab, v_tab)

        if not last:
            ac = conformer_conv(zca, conv_w[l], conv_b[l], conv_ln_g[l], conv_ln_b[l])
            bc = pair_attention(cna_q, [(cna_k, cna_v)])
            gc = pair_attention(cg_q, [(cg_k, cg_v)])
            ctx = proj_residual((ac, bc, gc), w_out_b, ctx, cg1)
            ctx = _peer_ffn_residual(ctx, norm2_g[l], csc2, csh2, cg2, w_q_b, keys_b, u_tab, v_tab)
    return rmsnorm(x, final_norm_g)
```

```python
import functools

import jax
import jax.numpy as jnp
from jax import lax
from jax.experimental import pallas as pl
from jax.experimental.pallas import tpu as pltpu

F32 = jnp.float32
BF16 = jnp.bfloat16

GRID_W = 64
HEAD_DIM = 64
CONV_CH = 256
CONV_WIDTH = 31
NA_HEADS = 6
NA_WIN_H = 8
NA_WIN_W = 16
GQA_HEADS = 6
GQA_KV_HEADS = 2
ROPE_THETA = 10000.0
PEER_HEADS = 8
PEER_N_KEYS = 128
PEER_D_KEY = 128
PEER_TOPK = 16
EPS = 1e-6

PEER_PAIRS = PEER_HEADS * PEER_TOPK

LANES = 128
SUBLANES = 8
_HALF = SUBLANES // 2
assert PEER_N_KEYS * PEER_N_KEYS * _HALF <= 1 << 16
VMEM_LIMIT = 48 * 1024 * 1024
MASK_VALUE = -1e30


def _params(*sem):
    return pltpu.CompilerParams(dimension_semantics=sem, vmem_limit_bytes=VMEM_LIMIT)


def _ada_kernel(a_ref, w_ref, b_ref, o_ref):
    a = a_ref[...]
    s = a * jax.nn.sigmoid(a)
    o_ref[...] = jnp.dot(s.astype(BF16), w_ref[...].astype(BF16), preferred_element_type=F32) + b_ref[...]


def ada_mod(a, w, b):
    R, D = a.shape
    N = w.shape[1]
    tn = 1024
    return pl.pallas_call(
        _ada_kernel,
        grid=(N // tn,),
        in_specs=[pl.BlockSpec((R, D), lambda j: (0, 0)),
                  pl.BlockSpec((D, tn), lambda j: (0, j)),
                  pl.BlockSpec((1, tn), lambda j: (0, j))],
        out_specs=pl.BlockSpec((R, tn), lambda j: (0, j)),
        out_shape=jax.ShapeDtypeStruct((R, N), F32),
        compiler_params=_params("parallel"),
        name="ada_mod",
    )(a, w, b.reshape(1, N))


def _pair_norm_rope(x, gain, rope_cos, rope_sin):
    lane = lax.broadcasted_iota(jnp.int32, x.shape, 1)
    lower = lane < HEAD_DIM
    sq = x * x
    ms_lo = jnp.sum(jnp.where(lower, sq, 0.0), axis=-1, keepdims=True) * (1.0 / HEAD_DIM)
    ms_hi = jnp.sum(jnp.where(lower, 0.0, sq), axis=-1, keepdims=True) * (1.0 / HEAD_DIM)
    y = x * jnp.where(lower, lax.rsqrt(ms_lo + EPS), lax.rsqrt(ms_hi + EPS)) * gain
    if rope_cos is None:
        return y
    half = HEAD_DIM // 2
    first = (lane % HEAD_DIM) < half
    partner = jnp.where(first, pltpu.roll(y, LANES - half, axis=1), pltpu.roll(y, half, axis=1))
    return y * rope_cos + partner * rope_sin


def _norm_mod_proj_kernel(*refs, outs, emit_h, has_gains, rope):
    x_ref, g_ref, sc_ref, sh_ref, w_ref = refs[:5]
    n_in = 5
    gains_ref = cos = sin = None
    if has_gains:
        gains_ref = refs[n_in]
        n_in += 1
    if rope:
        cos, sin = refs[n_in][...], refs[n_in + 1][...]
        n_in += 2
    out_refs = refs[n_in:]
    x = x_ref[0]
    y = x * lax.rsqrt(jnp.mean(x * x, axis=-1, keepdims=True) + EPS) * g_ref[...]
    h = y * (1.0 + sc_ref[0]) + sh_ref[0]
    z = jnp.dot(h.astype(BF16), w_ref[...], preferred_element_type=F32)
    off = 0
    for o_ref, spec in zip(out_refs, outs):
        if spec[0] == "f32":
            o_ref[0] = z[:, off:off + spec[1]]
            off += spec[1]
        elif spec[0] == "bf16":
            o_ref[0] = (z[:, off:off + spec[1]] * spec[2]).astype(BF16)
            off += spec[1]
        else:
            groups = []
            for _ in range(spec[1]):
                v = _pair_norm_rope(z[:, off:off + LANES], gains_ref[spec[3]:spec[3] + 1, :], cos, sin)
                groups.append((v * spec[2]).astype(BF16))
                off += LANES
            o_ref[0] = groups[0] if len(groups) == 1 else jnp.concatenate(groups, axis=-1)
    if emit_h:
        out_refs[len(outs)][0] = h


def _out_width(spec):
    return spec[1] * LANES if spec[0] == "norm" else spec[1]


def norm_mod_proj(x, g, sc, sh, w, outs, emit_h=False, gains=None, rope=None):
    B, L, D = x.shape
    N = w.shape[1]
    assert sum(_out_width(o) for o in outs) == N
    tm = min(256, L)
    per_batch = sc.shape[0] == B
    mod_map = (lambda b, i: (b, 0, 0)) if per_batch else (lambda b, i: (0, 0, 0))
    in_specs = [pl.BlockSpec((1, tm, D), lambda b, i: (b, i, 0)),
                pl.BlockSpec((1, D), lambda b, i: (0, 0)),
                pl.BlockSpec((1, 1, D), mod_map),
                pl.BlockSpec((1, 1, D), mod_map),
                pl.BlockSpec((D, N), lambda b, i: (0, 0))]
    args = [x, g.reshape(1, D), sc, sh, w]
    if gains is not None:
        in_specs.append(pl.BlockSpec(gains.shape, lambda b, i: (0, 0)))
        args.append(gains)
    if rope is not None:
        for tab in rope:
            in_specs.append(pl.BlockSpec((tm, LANES), lambda b, i: (i, 0)))
            args.append(tab)
    out_shape = [jax.ShapeDtypeStruct((B, L, _out_width(o)), F32 if o[0] == "f32" else BF16) for o in outs]
    out_specs = [pl.BlockSpec((1, tm, _out_width(o)), lambda b, i: (b, i, 0)) for o in outs]
    if emit_h:
        out_shape.append(jax.ShapeDtypeStruct((B, L, D), F32))
        out_specs.append(pl.BlockSpec((1, tm, D), lambda b, i: (b, i, 0)))
    return pl.pallas_call(
        functools.partial(_norm_mod_proj_kernel, outs=outs, emit_h=emit_h, has_gains=gains is not None,
                          rope=rope is not None),
        grid=(B, L // tm),
        in_specs=in_specs,
        out_specs=out_specs,
        out_shape=out_shape,
        compiler_params=_params("parallel", "parallel"),
        name="norm_mod_proj",
    )(*args)


_CONV_PAD = 16


def _conv_kernel(za_ref, w_ref, b_ref, lg_ref, lb_ref, o_ref, upad_ref, *, L, tc):
    c = pl.program_id(1)

    @pl.when(c == 0)
    def _():
        val = za_ref[0, :, :CONV_CH]
        gate = za_ref[0, :, CONV_CH:]
        zeros = jnp.zeros((_CONV_PAD, CONV_CH), F32)
        upad_ref[pl.ds(0, _CONV_PAD), :] = zeros
        upad_ref[pl.ds(_CONV_PAD + L, _CONV_PAD), :] = zeros
        upad_ref[pl.ds(_CONV_PAD, L), :] = val * jax.nn.sigmoid(gate)

    start = pl.multiple_of(c * tc, SUBLANES)
    win = upad_ref[pl.ds(start, tc + 2 * _CONV_PAD), :]
    acc = jnp.zeros((tc, CONV_CH), F32)
    first = _CONV_PAD - CONV_WIDTH // 2
    rows = tc + 2 * _CONV_PAD
    shifted = [win if c == 0 else pltpu.roll(win, rows - c, axis=0) for c in range(SUBLANES)]
    for j in range(CONV_WIDTH):
        a, c = divmod(first + j, SUBLANES)
        acc = acc + shifted[c][a * SUBLANES:a * SUBLANES + tc, :] * w_ref[j:j + 1, :]
    y = acc + b_ref[...]
    mu = jnp.mean(y, axis=-1, keepdims=True)
    d = y - mu
    var = jnp.mean(d * d, axis=-1, keepdims=True)
    yn = d * lax.rsqrt(var + EPS) * lg_ref[...] + lb_ref[...]
    o_ref[0] = (yn * jax.nn.sigmoid(yn)).astype(o_ref.dtype)


def conformer_conv(za, w_dw, b_dw, ln_g, ln_b):
    B, L, _ = za.shape
    tc = min(256, L)
    vec = lambda v: v.reshape(1, CONV_CH)
    return pl.pallas_call(
        functools.partial(_conv_kernel, L=L, tc=tc),
        grid=(B, L // tc),
        in_specs=[pl.BlockSpec((1, L, 2 * CONV_CH), lambda b, c: (b, 0, 0)),
                  pl.BlockSpec((CONV_WIDTH, CONV_CH), lambda b, c: (0, 0)),
                  pl.BlockSpec((1, CONV_CH), lambda b, c: (0, 0)),
                  pl.BlockSpec((1, CONV_CH), lambda b, c: (0, 0)),
                  pl.BlockSpec((1, CONV_CH), lambda b, c: (0, 0))],
        out_specs=pl.BlockSpec((1, tc, CONV_CH), lambda b, c: (b, c, 0)),
        out_shape=jax.ShapeDtypeStruct((B, L, CONV_CH), BF16),
        scratch_shapes=[pltpu.VMEM((L + 2 * _CONV_PAD, CONV_CH), F32)],
        compiler_params=_params("parallel", "arbitrary"),
        name="conformer_conv",
    )(za, w_dw, vec(b_dw), vec(ln_g), vec(ln_b))


_NA_ROWS_PER_STEP = 2


def _na_row_offset(r, rows):
    return r - jnp.clip(r - NA_WIN_H // 2, 0, rows - NA_WIN_H)


def _head_pair_rows(qg, lower):
    zero = jnp.zeros_like(qg)
    return jnp.concatenate([jnp.where(lower, qg, zero), jnp.where(lower, zero, qg)], axis=0)


def _na_kernel(q_ref, k_ref, v_ref, kc_ref, vc_ref, bias_ref, o_ref, *, rows):
    nt = (((1,), (1,)), ((), ()))
    lower = lax.broadcasted_iota(jnp.int32, (GRID_W, LANES), 1) < HEAD_DIM
    cols = [slice(g * LANES, (g + 1) * LANES) for g in range(NA_HEADS // 2)]
    jobs = []
    for rr in range(_NA_ROWS_PER_STEP):
        r = pl.program_id(1) * _NA_ROWS_PER_STEP + rr
        off = _na_row_offset(r, rows)
        band = pl.ds(pl.multiple_of((r - off) * GRID_W, GRID_W), NA_WIN_H * GRID_W)
        jobs += [(slice(rr * GRID_W, (rr + 1) * GRID_W), off, band, g) for g in range(NA_HEADS // 2)]
    qs = [_head_pair_rows(q_ref[0, qr, cols[g]], lower) for qr, _, _, g in jobs]
    s_loc = [lax.dot_general(q, k_ref[0, band, cols[g]], nt, preferred_element_type=F32) + bias_ref[off, g]
             for q, (_, off, band, g) in zip(qs, jobs)]
    s_ctx = [lax.dot_general(q, kc_ref[0, :, cols[g]], nt, preferred_element_type=F32)
             for q, (_, _, _, g) in zip(qs, jobs)]
    p_loc, p_ctx, den = [], [], []
    for sl, sc in zip(s_loc, s_ctx):
        m = jnp.maximum(jnp.max(sl, axis=-1, keepdims=True), jnp.max(sc, axis=-1, keepdims=True))
        p_loc.append(jnp.exp(sl - m))
        p_ctx.append(jnp.exp(sc - m))
        den.append(jnp.sum(p_loc[-1], axis=-1, keepdims=True) + jnp.sum(p_ctx[-1], axis=-1, keepdims=True))
    outs = []
    for (_, _, band, g), pl_, pc, d in zip(jobs, p_loc, p_ctx, den):
        o = (jnp.dot(pl_.astype(BF16), v_ref[0, band, cols[g]], preferred_element_type=F32)
             + jnp.dot(pc.astype(BF16), vc_ref[0, :, cols[g]], preferred_element_type=F32)) / d
        outs.append(jnp.where(lower, o[:GRID_W], o[GRID_W:]).astype(o_ref.dtype))
    n = NA_HEADS // 2
    for rr in range(_NA_ROWS_PER_STEP):
        o_ref[0, rr * GRID_W:(rr + 1) * GRID_W, :] = jnp.concatenate(outs[rr * n:(rr + 1) * n], axis=-1)


def na_bias_table(rel_bias):
    cols = jnp.arange(GRID_W, dtype=jnp.int32)
    c0 = jnp.clip(cols - NA_WIN_W // 2, 0, GRID_W - NA_WIN_W)
    in_win = (cols[None, :] >= c0[:, None]) & (cols[None, :] < c0[:, None] + NA_WIN_W)
    col_idx = jnp.clip(cols[None, :] - cols[:, None] + (NA_WIN_W - 1), 0, 2 * NA_WIN_W - 2)
    off = jnp.arange(NA_WIN_H, dtype=jnp.int32)
    row_idx = off[None, :] - off[:, None] + (NA_WIN_H - 1)
    t = rel_bias[:, row_idx]
    t = t[..., col_idx]
    t = jnp.where(in_win[None, None, None], t, MASK_VALUE)
    return t.transpose(1, 0, 3, 2, 4).reshape(NA_WIN_H, NA_HEADS // 2, 2 * GRID_W, NA_WIN_H * GRID_W)


def neighbourhood_attention(q, k, v, kc, vc, bias):
    B, T, W = q.shape
    C = kc.shape[1]
    rows = T // GRID_W
    rq = _NA_ROWS_PER_STEP * GRID_W
    assert rows >= NA_WIN_H and rows % _NA_ROWS_PER_STEP == 0 and W == NA_HEADS * HEAD_DIM
    return pl.pallas_call(
        functools.partial(_na_kernel, rows=rows),
        grid=(B, rows // _NA_ROWS_PER_STEP),
        in_specs=[pl.BlockSpec((1, rq, W), lambda b, r: (b, r, 0)),
                  pl.BlockSpec((1, T, W), lambda b, r: (b, 0, 0)),
                  pl.BlockSpec((1, T, W), lambda b, r: (b, 0, 0)),
                  pl.BlockSpec((1, C, W), lambda b, r: (b, 0, 0)),
                  pl.BlockSpec((1, C, W), lambda b, r: (b, 0, 0)),
                  pl.BlockSpec(bias.shape, lambda b, r: (0, 0, 0, 0))],
        out_specs=pl.BlockSpec((1, rq, W), lambda b, r: (b, r, 0)),
        out_shape=jax.ShapeDtypeStruct((B, T, W), BF16),
        compiler_params=_params("parallel", "arbitrary"),
        name="neighbourhood_attention",
    )(q, k, v, kc, vc, bias)


def _pair_attn_kernel(*refs, Gq, Gk, bq, n_sets):
    q_ref = refs[0]
    kv_refs = refs[1:1 + 2 * n_sets]
    o_ref = refs[1 + 2 * n_sets]
    nt = (((1,), (1,)), ((), ()))
    sets = range(n_sets)
    lower = lax.broadcasted_iota(jnp.int32, (bq, LANES), 1) < HEAD_DIM
    col = lambda g: slice(g * LANES, (g + 1) * LANES)
    pairs = [_head_pair_rows(q_ref[0, :, col(g)], lower) for g in range(Gq)]
    if Gk == 1:
        jobs = [(jnp.concatenate([p[:bq] for p in pairs], axis=0), col(0)),
                (jnp.concatenate([p[bq:] for p in pairs], axis=0), col(0))]
    else:
        jobs = [(pairs[g], col(g)) for g in range(Gq)]
    ss = [[lax.dot_general(rows, kv_refs[2 * i][0, :, kc], nt, preferred_element_type=F32) for i in sets]
          for rows, kc in jobs]
    ps, dens = [], []
    for s in ss:
        m = functools.reduce(jnp.maximum, [jnp.max(x, axis=-1, keepdims=True) for x in s])
        ps.append([jnp.exp(x - m) for x in s])
        dens.append(functools.reduce(jnp.add, [jnp.sum(p, axis=-1, keepdims=True) for p in ps[-1]]))
    os_ = []
    for (rows, kc), p, den in zip(jobs, ps, dens):
        o = functools.reduce(jnp.add, [jnp.dot(p[i].astype(BF16), kv_refs[2 * i + 1][0, :, kc],
                                               preferred_element_type=F32) for i in sets])
        os_.append(o / den)
    outs = []
    for g in range(Gq):
        if Gk == 1:
            first, second = os_[0][g * bq:(g + 1) * bq], os_[1][g * bq:(g + 1) * bq]
        else:
            first, second = os_[g][:bq], os_[g][bq:]
        outs.append(jnp.where(lower, first, second).astype(o_ref.dtype))
    o_ref[0] = jnp.concatenate(outs, axis=-1)


def pair_attention(q, kv_sets):
    B, L, Wq = q.shape
    Wk = kv_sets[0][0].shape[2]
    Gq, Gk = Wq // LANES, Wk // LANES
    assert Gk in (1, Gq)
    bq = min(256, L)
    in_specs = [pl.BlockSpec((1, bq, Wq), lambda b, i: (b, i, 0))]
    args = [q]
    for k, v in kv_sets:
        for t in (k, v):
            in_specs.append(pl.BlockSpec((1, t.shape[1], Wk), lambda b, i: (b, 0, 0)))
            args.append(t)
    return pl.pallas_call(
        functools.partial(_pair_attn_kernel, Gq=Gq, Gk=Gk, bq=bq, n_sets=len(kv_sets)),
        grid=(B, L // bq),
        in_specs=in_specs,
        out_specs=pl.BlockSpec((1, bq, Wq), lambda b, i: (b, i, 0)),
        out_shape=jax.ShapeDtypeStruct((B, L, Wq), BF16),
        compiler_params=_params("parallel", "arbitrary"),
        name="pair_attention",
    )(*args)


def _proj_residual_kernel(a_ref, b_ref, g_ref, wa_ref, wb_ref, wg_ref, x_ref, gate_ref, o_ref):
    y = (jnp.dot(a_ref[0], wa_ref[...], preferred_element_type=F32)
         + jnp.dot(b_ref[0], wb_ref[...], preferred_element_type=F32)
         + jnp.dot(g_ref[0], wg_ref[...], preferred_element_type=F32))
    o_ref[0] = x_ref[0] + gate_ref[0] * y


def proj_residual(parts, w, x, gate):
    B, L, D = x.shape
    tm = min(512, L)
    gate_map = (lambda b, i: (b, 0, 0)) if gate.shape[0] == B else (lambda b, i: (0, 0, 0))
    widths = [p.shape[2] for p in parts]
    assert sum(widths) == w.shape[0]
    starts = [sum(widths[:i]) for i in range(len(parts))]
    ws = [w[s:s + k] for s, k in zip(starts, widths)]
    return pl.pallas_call(
        _proj_residual_kernel,
        grid=(B, L // tm),
        in_specs=[pl.BlockSpec((1, tm, k), lambda b, i: (b, i, 0)) for k in widths]
        + [pl.BlockSpec((k, D), lambda b, i: (0, 0)) for k in widths]
        + [pl.BlockSpec((1, tm, D), lambda b, i: (b, i, 0)), pl.BlockSpec((1, 1, D), gate_map)],
        out_specs=pl.BlockSpec((1, tm, D), lambda b, i: (b, i, 0)),
        out_shape=jax.ShapeDtypeStruct((B, L, D), F32),
        compiler_params=_params("parallel", "parallel"),
        name="proj_residual",
    )(*parts, *ws, x, gate)


def _top_rows(s, k, order=None, payload=None):
    if order is None:
        order = lax.broadcasted_iota(jnp.int32, s.shape, 0).astype(F32)
    vals, picks = [], []
    for _ in range(k):
        m = jnp.max(s, axis=0, keepdims=True)
        first = jnp.min(jnp.where(s == m, order, jnp.inf), axis=0, keepdims=True)
        hit = order == first
        vals.append(m)
        if payload is None:
            picks.append(first)
        else:
            picks.append(jnp.max(jnp.where(hit, payload, -1.0), axis=0, keepdims=True))
        s = jnp.where(hit, -jnp.inf, s)
    return jnp.concatenate(vals, axis=0), jnp.concatenate(picks, axis=0)


def _pair_candidates(s0, i0, s1, i1):
    K = PEER_TOPK
    t = s0.shape[1]
    sub = lax.broadcasted_iota(jnp.int32, (SUBLANES, t), 0)
    scores, order, ids = [], [], []

    def add(a_rows, b_rows, a_of_row, b_of_row):
        sa, ia = a_rows
        sb, ib = b_rows
        ok = (a_of_row + 1) * (b_of_row + 1) <= K
        scores.append(jnp.where(ok, sa + sb, -jnp.inf))
        order.append((a_of_row * K + b_of_row).astype(F32))
        ids.append(ia * float(PEER_N_KEYS) + ib)

    row = lambda x, r: (x[0][r:r + 1], x[1][r:r + 1])
    rows = lambda x, r: (x[0][r:r + SUBLANES], x[1][r:r + SUBLANES])
    A, Bv = (s0, i0), (s1, i1)
    add(row(A, 0), rows(Bv, 0), jnp.zeros_like(sub), sub)
    add(row(A, 0), rows(Bv, SUBLANES), jnp.zeros_like(sub), sub + SUBLANES)
    for a in range(1, 4):
        add(row(A, a), rows(Bv, 0), jnp.full_like(sub, a), sub)
    for b in range(3):
        dup = sub < 4
        sa, ia = rows(A, 0)
        add((jnp.where(dup, -jnp.inf, sa), ia), row(Bv, b), sub, jnp.full_like(sub, b))
    add(rows(A, SUBLANES), row(Bv, 0), sub + SUBLANES, jnp.zeros_like(sub))
    cat = lambda xs: jnp.concatenate(xs, axis=0)
    return cat(scores), cat(order), cat(ids)


def _peer_topk_kernel(q_ref, keys_ref, offs_ref, g_ref):
    K = PEER_TOPK
    assert K == 2 * SUBLANES
    nt = (((1,), (1,)), ((), ()))
    words, gates = [], []
    for h in range(PEER_HEADS):
        tops = []
        for p in range(2):
            col = (2 * h + p) * PEER_D_KEY
            qhp = q_ref[:, col:col + PEER_D_KEY]
            s = lax.dot_general(keys_ref[h, p], qhp, nt, preferred_element_type=F32)
            tops.append(_top_rows(s, K))
        (s0, i0), (s1, i1) = tops
        cand_s, cand_order, cand_i = _pair_candidates(s0, i0, s1, i1)
        best_s, best_i = _top_rows(cand_s, K, order=cand_order, payload=cand_i)
        e = jnp.exp(best_s - jnp.max(best_s, axis=0, keepdims=True))
        gates.append(e / jnp.sum(e, axis=0, keepdims=True))
        off = best_i.astype(jnp.int32) * _HALF
        lo = jnp.concatenate([off[0:_HALF], off[SUBLANES:SUBLANES + _HALF]], axis=0)
        hi = jnp.concatenate([off[_HALF:SUBLANES], off[SUBLANES + _HALF:K]], axis=0)
        words.append(lo | (hi << 16))
    g_ref[...] = jnp.concatenate(gates, axis=0).T
    words.append(jnp.zeros((PEER_PAIRS // 2, q_ref.shape[0]), jnp.int32))
    offs_ref[...] = jnp.concatenate(words, axis=0).T[:, :PEER_PAIRS // 2]


def peer_topk(q, keys):
    N = q.shape[0]
    tm = LANES
    return pl.pallas_call(
        _peer_topk_kernel,
        grid=(N // tm,),
        in_specs=[pl.BlockSpec((tm, q.shape[1]), lambda i: (i, 0)),
                  pl.BlockSpec(keys.shape, lambda i: (0, 0, 0, 0))],
        out_specs=[pl.BlockSpec((tm, PEER_PAIRS // 2), lambda i: (i, 0)),
                   pl.BlockSpec((tm, PEER_PAIRS), lambda i: (i, 0))],
        out_shape=[jax.ShapeDtypeStruct((N, PEER_PAIRS // 2), jnp.int32),
                   jax.ShapeDtypeStruct((N, PEER_PAIRS), F32)],
        compiler_params=_params("parallel"),
        name="peer_topk",
    )(q, keys)


_PEER_TOKENS = 128
_GROUPS = PEER_PAIRS // SUBLANES


def _pack_table_kernel(w_ref, o_ref):
    te = w_ref.shape[0]
    bits = lambda x: pltpu.bitcast(x.astype(BF16).astype(F32), jnp.uint32)
    for i in range(_HALF):
        lo = bits(w_ref[:, (2 * i) * LANES:(2 * i + 1) * LANES]) >> 16
        hi = bits(w_ref[:, (2 * i + 1) * LANES:(2 * i + 2) * LANES])
        o_ref[pl.ds(i, te, stride=_HALF), :] = lo | hi


def pack_expert_table(w, layer):
    _, E, D = w.shape
    assert D == SUBLANES * LANES
    te = 256
    return pl.pallas_call(
        _pack_table_kernel,
        grid=(E // te,),
        in_specs=[pl.BlockSpec((None, te, D), lambda i: (layer, i, 0))],
        out_specs=pl.BlockSpec((te * _HALF, LANES), lambda i: (i, 0)),
        out_shape=jax.ShapeDtypeStruct((E * _HALF, LANES), jnp.uint32),
        compiler_params=_params("parallel"),
        name="pack_expert_table",
    )(w)


def _pair_position(word, half):
    return SUBLANES * (word // _HALF) + _HALF * half + word % _HALF


def _expert_rows(tab_ref, word):
    starts = (word & 0xFFFF, lax.shift_right_logical(word, 16))
    return [pltpu.bitcast(tab_ref[pl.ds(pl.multiple_of(s, _HALF), _HALF), :], BF16).astype(F32) for s in starts]


def _token_row(ref, t):
    row = ref[pl.ds(t, 1), :]
    return jnp.concatenate([row[:, s * LANES:(s + 1) * LANES] for s in range(SUBLANES)], axis=0)


def _merge_pair(a, b, shift, first):
    if shift == _HALF:
        return jnp.where(first, a, b) + pltpu.roll(jnp.where(first, b, a), shift, axis=0)
    bs = pltpu.roll(b, shift, axis=0)
    return jnp.where(first, a, bs) + pltpu.roll(jnp.where(first, bs, a), SUBLANES - shift, axis=0)


_MERGE_ORDER = (0, 4, 2, 6, 1, 5, 3, 7)


def _merge8(ps):
    sub = lax.broadcasted_iota(jnp.int32, (SUBLANES, LANES), 0)
    shift = _HALF
    while len(ps) > 1:
        first = (sub % (2 * shift)) < shift
        ps = [_merge_pair(ps[2 * i], ps[2 * i + 1], shift, first) for i in range(len(ps) // 2)]
        shift //= 2
    return ps[0]


def _peer_act_kernel(idx_ref, h_ref, g_ref, tab_ref, o_ref, part_ref):
    tn = h_ref.shape[0]

    def products(t, slot):
        x = _token_row(h_ref, t)
        for gi in range(_GROUPS):
            rows = []
            for w in range(_HALF):
                rows += _expert_rows(tab_ref, idx_ref[t, gi * _HALF + w])
            prods = [rows[2 * (j % _HALF) + j // _HALF] * x for j in _MERGE_ORDER]
            part_ref[slot, pl.ds(gi * SUBLANES, SUBLANES), :] = _merge8(prods)

    def reduce(t, slot):
        o_ref[pl.ds(t, 1), :] = jnp.sum(part_ref[slot].T, axis=0, keepdims=True)

    part_ref[...] = jnp.zeros(part_ref.shape, F32)

    def two_tokens(i, carry):
        t = 2 * i
        reduce(jnp.maximum(t - 2, 0), 0)
        reduce(jnp.maximum(t - 1, 0), 1)
        products(t, 0)
        products(t + 1, 1)
        return carry

    lax.fori_loop(0, tn // 2, two_tokens, 0)
    reduce(tn - 2, 0)
    reduce(tn - 1, 1)
    o_ref[...] = g_ref[...] * jax.nn.gelu(o_ref[...], approximate=True)


def peer_act(offs, h, g, table):
    N = offs.shape[0]
    tn = _PEER_TOKENS
    return pl.pallas_call(
        _peer_act_kernel,
        grid=(N // tn,),
        in_specs=[pl.BlockSpec((tn, PEER_PAIRS // 2), lambda i: (i, 0), memory_space=pltpu.SMEM),
                  pl.BlockSpec((tn, SUBLANES * LANES), lambda i: (i, 0)),
                  pl.BlockSpec((tn, PEER_PAIRS), lambda i: (i, 0)),
                  pl.BlockSpec(memory_space=pltpu.VMEM)],
        out_specs=pl.BlockSpec((tn, PEER_PAIRS), lambda i: (i, 0)),
        out_shape=jax.ShapeDtypeStruct((N, PEER_PAIRS), F32),
        scratch_shapes=[pltpu.VMEM((2, PEER_PAIRS, LANES), F32)],
        compiler_params=_params("arbitrary"),
        name="peer_act",
    )(offs, h, g, table)


_MIX_CHAINS = 4


def _peer_mix_kernel(idx_ref, coef_ref, tab_ref, x_ref, gate_ref, o_ref, cb_ref):
    tn = x_ref.shape[0]

    def spread(t):
        row = coef_ref[pl.ds(t, 1), :]
        return jnp.broadcast_to(row, (PEER_PAIRS, PEER_PAIRS)).T

    cb_ref[...] = spread(0)

    def token(t, carry):
        accs = [None] * _MIX_CHAINS
        for w in range(PEER_PAIRS // 2):
            for k, row in enumerate(_expert_rows(tab_ref, idx_ref[t, w])):
                j = _pair_position(w, k)
                term = jnp.broadcast_to(cb_ref[pl.ds(j, 1), :], (SUBLANES, LANES)) * row
                a = j % _MIX_CHAINS
                accs[a] = term if accs[a] is None else accs[a] + term
        y = (accs[0] + accs[1]) + (accs[2] + accs[3])
        y_row = jnp.concatenate([y[s:s + 1] for s in range(SUBLANES)], axis=1)
        o_ref[pl.ds(t, 1), :] = x_ref[pl.ds(t, 1), :] + gate_ref[0] * y_row
        cb_ref[...] = spread(jnp.minimum(t + 1, tn - 1))
        return carry

    lax.fori_loop(0, tn, token, 0)


def peer_mix(offs, coef, table, x, gate, tokens_per_batch):
    N, D = x.shape
    tn = _PEER_TOKENS
    assert tokens_per_batch % tn == 0
    if gate.shape[0] == 1:
        gate_map = lambda i: (0, 0, 0)
    else:
        gate_map = lambda i: ((i * tn) // tokens_per_batch, 0, 0)
    return pl.pallas_call(
        _peer_mix_kernel,
        grid=(N // tn,),
        in_specs=[pl.BlockSpec((tn, PEER_PAIRS // 2), lambda i: (i, 0), memory_space=pltpu.SMEM),
                  pl.BlockSpec((tn, PEER_PAIRS), lambda i: (i, 0)),
                  pl.BlockSpec(memory_space=pltpu.VMEM),
                  pl.BlockSpec((tn, D), lambda i: (i, 0)),
                  pl.BlockSpec((1, 1, D), gate_map)],
        out_specs=pl.BlockSpec((tn, D), lambda i: (i, 0)),
        out_shape=jax.ShapeDtypeStruct((N, D), F32),
        scratch_shapes=[pltpu.VMEM((PEER_PAIRS, LANES), F32)],
        compiler_params=_params("arbitrary"),
        name="peer_mix",
    )(offs, coef, table, x, gate)


def _rmsnorm_kernel(x_ref, g_ref, o_ref):
    x = x_ref[0]
    o_ref[0] = x * lax.rsqrt(jnp.mean(x * x, axis=-1, keepdims=True) + EPS) * g_ref[...]


def rmsnorm(x, g):
    B, L, D = x.shape
    tm = min(512, L)
    return pl.pallas_call(
        _rmsnorm_kernel,
        grid=(B, L // tm),
        in_specs=[pl.BlockSpec((1, tm, D), lambda b, i: (b, i, 0)),
                  pl.BlockSpec((1, D), lambda b, i: (0, 0))],
        out_specs=pl.BlockSpec((1, tm, D), lambda b, i: (b, i, 0)),
        out_shape=jax.ShapeDtypeStruct((B, L, D), F32),
        compiler_params=_params("parallel", "parallel"),
        name="final_rmsnorm",
    )(x, g.reshape(1, D))


def _rope_tables(T):
    t = jnp.arange(T, dtype=jnp.int32)
    row = (t // GRID_W).astype(F32)
    col = (t % GRID_W).astype(F32)
    n_freq = HEAD_DIM // 4
    inv_freq = ROPE_THETA ** (-jnp.arange(n_freq, dtype=F32) / n_freq)
    ang = jnp.concatenate([row[:, None] * inv_freq, col[:, None] * inv_freq], axis=-1)
    cos, sin = jnp.cos(ang), jnp.sin(ang)
    return jnp.tile(cos, (1, 4)), jnp.tile(jnp.concatenate([-sin, sin], axis=-1), (1, 2))


def _peer_ffn_residual(x, g2, sc2, sh2, gate, w_q, keys, u_tab, v_tab):
    B, L, D = x.shape
    N = B * L
    q, h = norm_mod_proj(x, g2, sc2, sh2, w_q, (("bf16", w_q.shape[1], 1.0),), emit_h=True)
    offs, gates = peer_topk(q.reshape(N, -1), keys)
    coef = peer_act(offs, h.reshape(N, D), gates, u_tab)
    out = peer_mix(offs, coef, v_tab, x.reshape(N, D), gate, L)
    return out.reshape(B, L, D)


_Q_SCALE = HEAD_DIM ** -0.5
_NA_W = NA_HEADS * HEAD_DIM
_GQA_W = GQA_HEADS * HEAD_DIM
_GQA_KV_W = GQA_KV_HEADS * HEAD_DIM
_IN_OUTS = (("f32", 2 * CONV_CH), ("bf16", _NA_W, _Q_SCALE), ("norm", _GQA_W // LANES, _Q_SCALE, 0),
            ("bf16", _NA_W, 1.0), ("bf16", _NA_W, 1.0), ("norm", _GQA_KV_W // LANES, 1.0, 1),
            ("bf16", _GQA_KV_W, 1.0))
_OFF_G_Q = 2 * CONV_CH + _NA_W
_OFF_G_OUT = CONV_CH + _NA_W
_GQA_PAIR_ORDER = tuple(h for g in range(GQA_HEADS // GQA_KV_HEADS) for h in (g, GQA_HEADS // GQA_KV_HEADS + g))
assert GQA_KV_HEADS == 2 and 2 * HEAD_DIM == LANES


def _reorder_heads(w, start, axis):
    idx = jnp.concatenate([jnp.arange(HEAD_DIM) + start + h * HEAD_DIM for h in _GQA_PAIR_ORDER])
    full = jnp.arange(w.shape[axis]).at[start:start + _GQA_W].set(idx)
    return jnp.take(w, full, axis=axis)


def kernel(x, c, ctx, c_ctx, norm1_g, norm2_g, w_ada, b_ada, w_in, conv_w, conv_b, conv_ln_g, conv_ln_b,
           na_rel_bias, gqa_q_norm, gqa_k_norm, w_out, peer_w_q, peer_keys, peer_u, peer_v, final_norm_g):
    B, T, D = x.shape
    depth = w_in.shape[0]
    rope = _rope_tables(T)
    ada_rows = -(-(B + 1) // SUBLANES) * SUBLANES
    ada_in = jnp.zeros((ada_rows, D), F32).at[:B].set(c).at[B].set(c_ctx)

    for l in range(depth):
        last = l == depth - 1
        mod = ada_mod(ada_in, w_ada[l], b_ada[l])
        sh1, sc1, g1, sh2, sc2, g2 = [m.reshape(B, 1, D) for m in jnp.split(mod[:B], 6, axis=-1)]
        csh1, csc1, cg1, csh2, csc2, cg2 = [m.reshape(1, 1, D) for m in jnp.split(mod[B:B + 1], 6, axis=-1)]
        w_in_b = _reorder_heads(w_in[l], _OFF_G_Q, 1).astype(BF16)
        w_out_b = _reorder_heads(w_out[l], _OFF_G_OUT, 0).astype(BF16)
        w_q_b = peer_w_q[l].astype(BF16)
        keys_b = peer_keys[l].astype(BF16)
        u_tab = pack_expert_table(peer_u, l)
        v_tab = pack_expert_table(peer_v, l)
        qk_gains = jnp.stack([jnp.tile(gqa_q_norm[l], 2), jnp.tile(gqa_k_norm[l], 2)])

        za, na_q, g_q, na_k, na_v, g_k, g_v = norm_mod_proj(x, norm1_g[l], sc1, sh1, w_in_b, _IN_OUTS,
                                                            gains=qk_gains, rope=rope)
        zca, cna_q, cg_q, cna_k, cna_v, cg_k, cg_v = norm_mod_proj(ctx, norm1_g[l], csc1, csh1, w_in_b, _IN_OUTS,
                                                                   gains=qk_gains)
        a = conformer_conv(za, conv_w[l], conv_b[l], conv_ln_g[l], conv_ln_b[l])
        bm = neighbourhood_attention(na_q, na_k, na_v, cna_k, cna_v, na_bias_table(na_rel_bias[l]))
        gm = pair_attention(g_q, [(g_k, g_v), (cg_k, cg_v)])
        x = proj_residual((a, bm, gm), w_out_b, x, g1)

        x = _peer_ffn_residual(x, norm2_g[l], sc2, sh2, g2, w_q_b, keys_b, u_tab, v_tab)

        if not last:
            ac = conformer_conv(zca, conv_w[l], conv_b[l], conv_ln_g[l], conv_ln_b[l])
            bc = pair_attention(cna_q, [(cna_k, cna_v)])
            gc = pair_attention(cg_q, [(cg_k, cg_v)])
            ctx = proj_residual((ac, bc, gc), w_out_b, ctx, cg1)
            ctx = _peer_ffn_residual(ctx, norm2_g[l], csc2, csh2, cg2, w_q_b, keys_b, u_tab, v_tab)
    return rmsnorm(x, final_norm_g)
```

```python
import functools

import jax
import jax.numpy as jnp
from jax import lax
from jax.experimental import pallas as pl
from jax.experimental.pallas import tpu as pltpu

F32 = jnp.float32
BF16 = jnp.bfloat16

GRID_W = 64
HEAD_DIM = 64
CONV_CH = 256
CONV_WIDTH = 31
NA_HEADS = 6
NA_WIN_H = 8
NA_WIN_W = 16
GQA_HEADS = 6
GQA_KV_HEADS = 2
ROPE_THETA = 10000.0
PEER_HEADS = 8
PEER_N_KEYS = 128
PEER_D_KEY = 128
PEER_TOPK = 16
EPS = 1e-6

PEER_PAIRS = PEER_HEADS * PEER_TOPK

LANES = 128
SUBLANES = 8
_HALF = SUBLANES // 2
assert PEER_N_KEYS * PEER_N_KEYS * _HALF <= 1 << 16
VMEM_LIMIT = 48 * 1024 * 1024
MASK_VALUE = -1e30


def _params(*sem):
    return pltpu.CompilerParams(dimension_semantics=sem, vmem_limit_bytes=VMEM_LIMIT)


def _ada_kernel(a_ref, w_ref, b_ref, o_ref):
    a = a_ref[...]
    s = a * jax.nn.sigmoid(a)
    o_ref[...] = jnp.dot(s.astype(BF16), w_ref[...].astype(BF16), preferred_element_type=F32) + b_ref[...]


def ada_mod(a, w, b):
    R, D = a.shape
    N = w.shape[1]
    tn = 1024
    return pl.pallas_call(
        _ada_kernel,
        grid=(N // tn,),
        in_specs=[pl.BlockSpec((R, D), lambda j: (0, 0)),
                  pl.BlockSpec((D, tn), lambda j: (0, j)),
                  pl.BlockSpec((1, tn), lambda j: (0, j))],
        out_specs=pl.BlockSpec((R, tn), lambda j: (0, j)),
        out_shape=jax.ShapeDtypeStruct((R, N), F32),
        compiler_params=_params("parallel"),
        name="ada_mod",
    )(a, w, b.reshape(1, N))


def _pair_norm_rope(x, gain, rope_cos, rope_sin):
    lane = lax.broadcasted_iota(jnp.int32, x.shape, 1)
    lower = lane < HEAD_DIM
    sq = x * x
    ms_lo = jnp.sum(jnp.where(lower, sq, 0.0), axis=-1, keepdims=True) * (1.0 / HEAD_DIM)
    ms_hi = jnp.sum(jnp.where(lower, 0.0, sq), axis=-1, keepdims=True) * (1.0 / HEAD_DIM)
    y = x * jnp.where(lower, lax.rsqrt(ms_lo + EPS), lax.rsqrt(ms_hi + EPS)) * gain
    if rope_cos is None:
        return y
    half = HEAD_DIM // 2
    first = (lane % HEAD_DIM) < half
    partner = jnp.where(first, pltpu.roll(y, LANES - half, axis=1), pltpu.roll(y, half, axis=1))
    return y * rope_cos + partner * rope_sin


def _norm_mod_proj_kernel(*refs, outs, emit_h, has_gains, rope):
    x_ref, g_ref, sc_ref, sh_ref, w_ref = refs[:5]
    n_in = 5
    gains_ref = cos = sin = None
    if has_gains:
        gains_ref = refs[n_in]
        n_in += 1
    if rope:
        cos, sin = refs[n_in][...], refs[n_in + 1][...]
        n_in += 2
    out_refs = refs[n_in:]
    x = x_ref[0]
    y = x * lax.rsqrt(jnp.mean(x * x, axis=-1, keepdims=True) + EPS) * g_ref[...]
    h = y * (1.0 + sc_ref[0]) + sh_ref[0]
    z = jnp.dot(h.astype(BF16), w_ref[...], preferred_element_type=F32)
    off = 0
    for o_ref, spec in zip(out_refs, outs):
        if spec[0] == "f32":
            o_ref[0] = z[:, off:off + spec[1]]
            off += spec[1]
        elif spec[0] == "bf16":
            o_ref[0] = (z[:, off:off + spec[1]] * spec[2]).astype(BF16)
            off += spec[1]
        else:
            groups = []
            for _ in range(spec[1]):
                v = _pair_norm_rope(z[:, off:off + LANES], gains_ref[spec[3]:spec[3] + 1, :], cos, sin)
                groups.append((v * spec[2]).astype(BF16))
                off += LANES
            o_ref[0] = groups[0] if len(groups) == 1 else jnp.concatenate(groups, axis=-1)
    if emit_h:
        out_refs[len(outs)][0] = h


def _out_width(spec):
    return spec[1] * LANES if spec[0] == "norm" else spec[1]


def norm_mod_proj(x, g, sc, sh, w, outs, emit_h=False, gains=None, rope=None):
    B, L, D = x.shape
    N = w.shape[1]
    assert sum(_out_width(o) for o in outs) == N
    tm = min(256, L)
    per_batch = sc.shape[0] == B
    mod_map = (lambda b, i: (b, 0, 0)) if per_batch else (lambda b, i: (0, 0, 0))
    in_specs = [pl.BlockSpec((1, tm, D), lambda b, i: (b, i, 0)),
                pl.BlockSpec((1, D), lambda b, i: (0, 0)),
                pl.BlockSpec((1, 1, D), mod_map),
                pl.BlockSpec((1, 1, D), mod_map),
                pl.BlockSpec((D, N), lambda b, i: (0, 0))]
    args = [x, g.reshape(1, D), sc, sh, w]
    if gains is not None:
        in_specs.append(pl.BlockSpec(gains.shape, lambda b, i: (0, 0)))
        args.append(gains)
    if rope is not None:
        for tab in rope:
            in_specs.append(pl.BlockSpec((tm, LANES), lambda b, i: (i, 0)))
            args.append(tab)
    out_shape = [jax.ShapeDtypeStruct((B, L, _out_width(o)), F32 if o[0] == "f32" else BF16) for o in outs]
    out_specs = [pl.BlockSpec((1, tm, _out_width(o)), lambda b, i: (b, i, 0)) for o in outs]
    if emit_h:
        out_shape.append(jax.ShapeDtypeStruct((B, L, D), F32))
        out_specs.append(pl.BlockSpec((1, tm, D), lambda b, i: (b, i, 0)))
    return pl.pallas_call(
        functools.partial(_norm_mod_proj_kernel, outs=outs, emit_h=emit_h, has_gains=gains is not None,
                          rope=rope is not None),
        grid=(B, L // tm),
        in_specs=in_specs,
        out_specs=out_specs,
        out_shape=out_shape,
        compiler_params=_params("parallel", "parallel"),
        name="norm_mod_proj",
    )(*args)


_CONV_PAD = 16


def _conv_kernel(za_ref, w_ref, b_ref, lg_ref, lb_ref, o_ref, upad_ref, *, L, tc):
    c = pl.program_id(1)

    @pl.when(c == 0)
    def _():
        val = za_ref[0, :, :CONV_CH]
        gate = za_ref[0, :, CONV_CH:]
        zeros = jnp.zeros((_CONV_PAD, CONV_CH), F32)
        upad_ref[pl.ds(0, _CONV_PAD), :] = zeros
        upad_ref[pl.ds(_CONV_PAD + L, _CONV_PAD), :] = zeros
        upad_ref[pl.ds(_CONV_PAD, L), :] = val * jax.nn.sigmoid(gate)

    start = pl.multiple_of(c * tc, SUBLANES)
    win = upad_ref[pl.ds(start, tc + 2 * _CONV_PAD), :]
    acc = jnp.zeros((tc, CONV_CH), F32)
    first = _CONV_PAD - CONV_WIDTH // 2
    rows = tc + 2 * _CONV_PAD
    shifted = [win if c == 0 else pltpu.roll(win, rows - c, axis=0) for c in range(SUBLANES)]
    for j in range(CONV_WIDTH):
        a, c = divmod(first + j, SUBLANES)
        acc = acc + shifted[c][a * SUBLANES:a * SUBLANES + tc, :] * w_ref[j:j + 1, :]
    y = acc + b_ref[...]
    mu = jnp.mean(y, axis=-1, keepdims=True)
    d = y - mu
    var = jnp.mean(d * d, axis=-1, keepdims=True)
    yn = d * lax.rsqrt(var + EPS) * lg_ref[...] + lb_ref[...]
    o_ref[0] = (yn * jax.nn.sigmoid(yn)).astype(o_ref.dtype)


def conformer_conv(za, w_dw, b_dw, ln_g, ln_b):
    B, L, _ = za.shape
    tc = min(256, L)
    vec = lambda v: v.reshape(1, CONV_CH)
    return pl.pallas_call(
        functools.partial(_conv_kernel, L=L, tc=tc),
        grid=(B, L // tc),
        in_specs=[pl.BlockSpec((1, L, 2 * CONV_CH), lambda b, c: (b, 0, 0)),
                  pl.BlockSpec((CONV_WIDTH, CONV_CH), lambda b, c: (0, 0)),
                  pl.BlockSpec((1, CONV_CH), lambda b, c: (0, 0)),
                  pl.BlockSpec((1, CONV_CH), lambda b, c: (0, 0)),
                  pl.BlockSpec((1, CONV_CH), lambda b, c: (0, 0))],
        out_specs=pl.BlockSpec((1, tc, CONV_CH), lambda b, c: (b, c, 0)),
        out_shape=jax.ShapeDtypeStruct((B, L, CONV_CH), BF16),
        scratch_shapes=[pltpu.VMEM((L + 2 * _CONV_PAD, CONV_CH), F32)],
        compiler_params=_params("parallel", "arbitrary"),
        name="conformer_conv",
    )(za, w_dw, vec(b_dw), vec(ln_g), vec(ln_b))


_NA_ROWS_PER_STEP = 2


def _na_row_offset(r, rows):
    return r - jnp.clip(r - NA_WIN_H // 2, 0, rows - NA_WIN_H)


def _head_pair_rows(qg, lower):
    zero = jnp.zeros_like(qg)
    return jnp.concatenate([jnp.where(lower, qg, zero), jnp.where(lower, zero, qg)], axis=0)


def _na_kernel(q_ref, k_ref, v_ref, kc_ref, vc_ref, bias_ref, o_ref, *, rows):
    nt = (((1,), (1,)), ((), ()))
    lower = lax.broadcasted_iota(jnp.int32, (GRID_W, LANES), 1) < HEAD_DIM
    cols = [slice(g * LANES, (g + 1) * LANES) for g in range(NA_HEADS // 2)]
    jobs = []
    for rr in range(_NA_ROWS_PER_STEP):
        r = pl.program_id(1) * _NA_ROWS_PER_STEP + rr
        off = _na_row_offset(r, rows)
        band = pl.ds(pl.multiple_of((r - off) * GRID_W, GRID_W), NA_WIN_H * GRID_W)
        jobs += [(slice(rr * GRID_W, (rr + 1) * GRID_W), off, band, g) for g in range(NA_HEADS // 2)]
    qs = [_head_pair_rows(q_ref[0, qr, cols[g]], lower) for qr, _, _, g in jobs]
    s_loc = [lax.dot_general(q, k_ref[0, band, cols[g]], nt, preferred_element_type=F32) + bias_ref[off, g]
             for q, (_, off, band, g) in zip(qs, jobs)]
    s_ctx = [lax.dot_general(q, kc_ref[0, :, cols[g]], nt, preferred_element_type=F32)
             for q, (_, _, _, g) in zip(qs, jobs)]
    p_loc, p_ctx, den = [], [], []
    for sl, sc in zip(s_loc, s_ctx):
        m = jnp.maximum(jnp.max(sl, axis=-1, keepdims=True), jnp.max(sc, axis=-1, keepdims=True))
        p_loc.append(jnp.exp(sl - m))
        p_ctx.append(jnp.exp(sc - m))
        den.append(jnp.sum(p_loc[-1], axis=-1, keepdims=True) + jnp.sum(p_ctx[-1], axis=-1, keepdims=True))
    outs = []
    for (_, _, band, g), pl_, pc, d in zip(jobs, p_loc, p_ctx, den):
        o = (jnp.dot(pl_.astype(BF16), v_ref[0, band, cols[g]], preferred_element_type=F32)
             + jnp.dot(pc.astype(BF16), vc_ref[0, :, cols[g]], preferred_element_type=F32)) / d
        outs.append(jnp.where(lower, o[:GRID_W], o[GRID_W:]).astype(o_ref.dtype))
    n = NA_HEADS // 2
    for rr in range(_NA_ROWS_PER_STEP):
        o_ref[0, rr * GRID_W:(rr + 1) * GRID_W, :] = jnp.concatenate(outs[rr * n:(rr + 1) * n], axis=-1)


def na_bias_table(rel_bias):
    cols = jnp.arange(GRID_W, dtype=jnp.int32)
    c0 = jnp.clip(cols - NA_WIN_W // 2, 0, GRID_W - NA_WIN_W)
    in_win = (cols[None, :] >= c0[:, None]) & (cols[None, :] < c0[:, None] + NA_WIN_W)
    col_idx = jnp.clip(cols[None, :] - cols[:, None] + (NA_WIN_W - 1), 0, 2 * NA_WIN_W - 2)
    off = jnp.arange(NA_WIN_H, dtype=jnp.int32)
    row_idx = off[None, :] - off[:, None] + (NA_WIN_H - 1)
    t = rel_bias[:, row_idx]
    t = t[..., col_idx]
    t = jnp.where(in_win[None, None, None], t, MASK_VALUE)
    return t.transpose(1, 0, 3, 2, 4).reshape(NA_WIN_H, NA_HEADS // 2, 2 * GRID_W, NA_WIN_H * GRID_W)


def neighbourhood_attention(q, k, v, kc, vc, bias):
    B, T, W = q.shape
    C = kc.shape[1]
    rows = T // GRID_W
    rq = _NA_ROWS_PER_STEP * GRID_W
    assert rows >= NA_WIN_H and rows % _NA_ROWS_PER_STEP == 0 and W == NA_HEADS * HEAD_DIM
    return pl.pallas_call(
        functools.partial(_na_kernel, rows=rows),
        grid=(B, rows // _NA_ROWS_PER_STEP),
        in_specs=[pl.BlockSpec((1, rq, W), lambda b, r: (b, r, 0)),
                  pl.BlockSpec((1, T, W), lambda b, r: (b, 0, 0)),
                  pl.BlockSpec((1, T, W), lambda b, r: (b, 0, 0)),
                  pl.BlockSpec((1, C, W), lambda b, r: (b, 0, 0)),
                  pl.BlockSpec((1, C, W), lambda b, r: (b, 0, 0)),
                  pl.BlockSpec(bias.shape, lambda b, r: (0, 0, 0, 0))],
        out_specs=pl.BlockSpec((1, rq, W), lambda b, r: (b, r, 0)),
        out_shape=jax.ShapeDtypeStruct((B, T, W), BF16),
        compiler_params=_params("parallel", "arbitrary"),
        name="neighbourhood_attention",
    )(q, k, v, kc, vc, bias)


def _pair_attn_kernel(*refs, Gq, Gk, bq, n_sets):
    q_ref = refs[0]
    kv_refs = refs[1:1 + 2 * n_sets]
    o_ref = refs[1 + 2 * n_sets]
    nt = (((1,), (1,)), ((), ()))
    sets = range(n_sets)
    lower = lax.broadcasted_iota(jnp.int32, (bq, LANES), 1) < HEAD_DIM
    col = lambda g: slice(g * LANES, (g + 1) * LANES)
    pairs = [_head_pair_rows(q_ref[0, :, col(g)], lower) for g in range(Gq)]
    if Gk == 1:
        jobs = [(jnp.concatenate([p[:bq] for p in pairs], axis=0), col(0)),
                (jnp.concatenate([p[bq:] for p in pairs], axis=0), col(0))]
    else:
        jobs = [(pairs[g], col(g)) for g in range(Gq)]
    ss = [[lax.dot_general(rows, kv_refs[2 * i][0, :, kc], nt, preferred_element_type=F32) for i in sets]
          for rows, kc in jobs]
    ps, dens = [], []
    for s in ss:
        m = functools.reduce(jnp.maximum, [jnp.max(x, axis=-1, keepdims=True) for x in s])
        ps.append([jnp.exp(x - m) for x in s])
        dens.append(functools.reduce(jnp.add, [jnp.sum(p, axis=-1, keepdims=True) for p in ps[-1]]))
    os_ = []
    for (rows, kc), p, den in zip(jobs, ps, dens):
        o = functools.reduce(jnp.add, [jnp.dot(p[i].astype(BF16), kv_refs[2 * i + 1][0, :, kc],
                                               preferred_element_type=F32) for i in sets])
        os_.append(o / den)
    outs = []
    for g in range(Gq):
        if Gk == 1:
            first, second = os_[0][g * bq:(g + 1) * bq], os_[1][g * bq:(g + 1) * bq]
        else:
            first, second = os_[g][:bq], os_[g][bq:]
        outs.append(jnp.where(lower, first, second).astype(o_ref.dtype))
    o_ref[0] = jnp.concatenate(outs, axis=-1)


def pair_attention(q, kv_sets):
    B, L, Wq = q.shape
    Wk = kv_sets[0][0].shape[2]
    Gq, Gk = Wq // LANES, Wk // LANES
    assert Gk in (1, Gq)
    bq = min(256, L)
    in_specs = [pl.BlockSpec((1, bq, Wq), lambda b, i: (b, i, 0))]
    args = [q]
    for k, v in kv_sets:
        for t in (k, v):
            in_specs.append(pl.BlockSpec((1, t.shape[1], Wk), lambda b, i: (b, 0, 0)))
            args.append(t)
    return pl.pallas_call(
        functools.partial(_pair_attn_kernel, Gq=Gq, Gk=Gk, bq=bq, n_sets=len(kv_sets)),
        grid=(B, L // bq),
        in_specs=in_specs,
        out_specs=pl.BlockSpec((1, bq, Wq), lambda b, i: (b, i, 0)),
        out_shape=jax.ShapeDtypeStruct((B, L, Wq), BF16),
        compiler_params=_params("parallel", "arbitrary"),
        name="pair_attention",
    )(*args)


def _proj_residual_kernel(a_ref, b_ref, g_ref, wa_ref, wb_ref, wg_ref, x_ref, gate_ref, o_ref):
    y = (jnp.dot(a_ref[0], wa_ref[...], preferred_element_type=F32)
         + jnp.dot(b_ref[0], wb_ref[...], preferred_element_type=F32)
         + jnp.dot(g_ref[0], wg_ref[...], preferred_element_type=F32))
    o_ref[0] = x_ref[0] + gate_ref[0] * y


def proj_residual(parts, w, x, gate):
    B, L, D = x.shape
    tm = min(512, L)
    gate_map = (lambda b, i: (b, 0, 0)) if gate.shape[0] == B else (lambda b, i: (0, 0, 0))
    widths = [p.shape[2] for p in parts]
    assert sum(widths) == w.shape[0]
    starts = [sum(widths[:i]) for i in range(len(parts))]
    ws = [w[s:s + k] for s, k in zip(starts, widths)]
    return pl.pallas_call(
        _proj_residual_kernel,
        grid=(B, L // tm),
        in_specs=[pl.BlockSpec((1, tm, k), lambda b, i: (b, i, 0)) for k in widths]
        + [pl.BlockSpec((k, D), lambda b, i: (0, 0)) for k in widths]
        + [pl.BlockSpec((1, tm, D), lambda b, i: (b, i, 0)), pl.BlockSpec((1, 1, D), gate_map)],
        out_specs=pl.BlockSpec((1, tm, D), lambda b, i: (b, i, 0)),
        out_shape=jax.ShapeDtypeStruct((B, L, D), F32),
        compiler_params=_params("parallel", "parallel"),
        name="proj_residual",
    )(*parts, *ws, x, gate)


def _top_rows(s, k, order=None, payload=None):
    if order is None:
        return _top_rows_by_row(s, k)
    vals, picks = [], []
    for _ in range(k):
        m = jnp.max(s, axis=0, keepdims=True)
        first = jnp.min(jnp.where(s == m, order, jnp.inf), axis=0, keepdims=True)
        hit = order == first
        vals.append(m)
        picks.append(jnp.max(jnp.where(hit, payload, -1.0), axis=0, keepdims=True))
        s = jnp.where(hit, -jnp.inf, s)
    return jnp.concatenate(vals, axis=0), jnp.concatenate(picks, axis=0)


def _top_rows_by_row(s, k):
    n, t = s.shape
    sub = lax.broadcasted_iota(jnp.int32, (SUBLANES, t), 0).astype(F32)
    pieces = [s[i:i + SUBLANES] for i in range(0, n, SUBLANES)]
    rows = [sub + float(i) for i in range(0, n, SUBLANES)]
    vals, picks = [], []
    for _ in range(k):
        level = list(zip(pieces, rows))
        while len(level) > 1:
            nxt = []
            for (va, ia), (vb, ib) in zip(level[0::2], level[1::2]):
                nxt.append((jnp.maximum(va, vb), jnp.where(va >= vb, ia, ib)))
            level = nxt
        v, i = level[0]
        m = jnp.max(v, axis=0, keepdims=True)
        first = jnp.min(jnp.where(v == m, i, jnp.inf), axis=0, keepdims=True)
        vals.append(m)
        picks.append(first)
        pieces = [jnp.where(r == first, -jnp.inf, p) for p, r in zip(pieces, rows)]
    return jnp.concatenate(vals, axis=0), jnp.concatenate(picks, axis=0)


def _pair_candidates(s0, i0, s1, i1):
    K = PEER_TOPK
    t = s0.shape[1]
    sub = lax.broadcasted_iota(jnp.int32, (SUBLANES, t), 0)
    scores, order, ids = [], [], []

    def add(a_rows, b_rows, a_of_row, b_of_row):
        sa, ia = a_rows
        sb, ib = b_rows
        ok = (a_of_row + 1) * (b_of_row + 1) <= K
        scores.append(jnp.where(ok, sa + sb, -jnp.inf))
        order.append((a_of_row * K + b_of_row).astype(F32))
        ids.append(ia * float(PEER_N_KEYS) + ib)

    row = lambda x, r: (x[0][r:r + 1], x[1][r:r + 1])
    rows = lambda x, r: (x[0][r:r + SUBLANES], x[1][r:r + SUBLANES])
    A, Bv = (s0, i0), (s1, i1)
    add(row(A, 0), rows(Bv, 0), jnp.zeros_like(sub), sub)
    add(row(A, 0), rows(Bv, SUBLANES), jnp.zeros_like(sub), sub + SUBLANES)
    for a in range(1, 4):
        add(row(A, a), rows(Bv, 0), jnp.full_like(sub, a), sub)
    for b in range(3):
        dup = sub < 4
        sa, ia = rows(A, 0)
        add((jnp.where(dup, -jnp.inf, sa), ia), row(Bv, b), sub, jnp.full_like(sub, b))
    add(rows(A, SUBLANES), row(Bv, 0), sub + SUBLANES, jnp.zeros_like(sub))
    cat = lambda xs: jnp.concatenate(xs, axis=0)
    return cat(scores), cat(order), cat(ids)


def _peer_topk_kernel(q_ref, keys_ref, offs_ref, g_ref):
    K = PEER_TOPK
    assert K == 2 * SUBLANES
    nt = (((1,), (1,)), ((), ()))
    words, gates = [], []
    for h in range(PEER_HEADS):
        tops = []
        for p in range(2):
            col = (2 * h + p) * PEER_D_KEY
            qhp = q_ref[:, col:col + PEER_D_KEY]
            s = lax.dot_general(keys_ref[h, p], qhp, nt, preferred_element_type=F32)
            tops.append(_top_rows(s, K))
        (s0, i0), (s1, i1) = tops
        cand_s, cand_order, cand_i = _pair_candidates(s0, i0, s1, i1)
        best_s, best_i = _top_rows(cand_s, K, order=cand_order, payload=cand_i)
        e = jnp.exp(best_s - jnp.max(best_s, axis=0, keepdims=True))
        gates.append(e / jnp.sum(e, axis=0, keepdims=True))
        off = best_i.astype(jnp.int32) * _HALF
        lo = jnp.concatenate([off[0:_HALF], off[SUBLANES:SUBLANES + _HALF]], axis=0)
        hi = jnp.concatenate([off[_HALF:SUBLANES], off[SUBLANES + _HALF:K]], axis=0)
        words.append(lo | (hi << 16))
    g_ref[...] = jnp.concatenate(gates, axis=0).T
    words.append(jnp.zeros((PEER_PAIRS // 2, q_ref.shape[0]), jnp.int32))
    offs_ref[...] = jnp.concatenate(words, axis=0).T[:, :PEER_PAIRS // 2]


def peer_topk(q, keys):
    N = q.shape[0]
    tm = LANES
    return pl.pallas_call(
        _peer_topk_kernel,
        grid=(N // tm,),
        in_specs=[pl.BlockSpec((tm, q.shape[1]), lambda i: (i, 0)),
                  pl.BlockSpec(keys.shape, lambda i: (0, 0, 0, 0))],
        out_specs=[pl.BlockSpec((tm, PEER_PAIRS // 2), lambda i: (i, 0)),
                   pl.BlockSpec((tm, PEER_PAIRS), lambda i: (i, 0))],
        out_shape=[jax.ShapeDtypeStruct((N, PEER_PAIRS // 2), jnp.int32),
                   jax.ShapeDtypeStruct((N, PEER_PAIRS), F32)],
        compiler_params=_params("parallel"),
        name="peer_topk",
    )(q, keys)


_PEER_TOKENS = 128
_GROUPS = PEER_PAIRS // SUBLANES


def _pack_table_kernel(w_ref, o_ref):
    te = w_ref.shape[0]
    bits = lambda x: pltpu.bitcast(x.astype(BF16).astype(F32), jnp.uint32)
    for i in range(_HALF):
        lo = bits(w_ref[:, (2 * i) * LANES:(2 * i + 1) * LANES]) >> 16
        hi = bits(w_ref[:, (2 * i + 1) * LANES:(2 * i + 2) * LANES])
        o_ref[pl.ds(i, te, stride=_HALF), :] = lo | hi


def pack_expert_table(w, layer):
    _, E, D = w.shape
    assert D == SUBLANES * LANES
    te = 256
    return pl.pallas_call(
        _pack_table_kernel,
        grid=(E // te,),
        in_specs=[pl.BlockSpec((None, te, D), lambda i: (layer, i, 0))],
        out_specs=pl.BlockSpec((te * _HALF, LANES), lambda i: (i, 0)),
        out_shape=jax.ShapeDtypeStruct((E * _HALF, LANES), jnp.uint32),
        compiler_params=_params("parallel"),
        name="pack_expert_table",
    )(w)


def _pair_position(word, half):
    return SUBLANES * (word // _HALF) + _HALF * half + word % _HALF


def _expert_rows(tab_ref, word):
    starts = (word & 0xFFFF, lax.shift_right_logical(word, 16))
    return [pltpu.bitcast(tab_ref[pl.ds(pl.multiple_of(s, _HALF), _HALF), :], BF16).astype(F32) for s in starts]


def _token_row(ref, t):
    row = ref[pl.ds(t, 1), :]
    return jnp.concatenate([row[:, s * LANES:(s + 1) * LANES] for s in range(SUBLANES)], axis=0)


def _merge_pair(a, b, shift, first):
    if shift == _HALF:
        return jnp.where(first, a, b) + pltpu.roll(jnp.where(first, b, a), shift, axis=0)
    bs = pltpu.roll(b, shift, axis=0)
    return jnp.where(first, a, bs) + pltpu.roll(jnp.where(first, bs, a), SUBLANES - shift, axis=0)


_MERGE_ORDER = (0, 4, 2, 6, 1, 5, 3, 7)


def _merge8(ps):
    sub = lax.broadcasted_iota(jnp.int32, (SUBLANES, LANES), 0)
    shift = _HALF
    while len(ps) > 1:
        first = (sub % (2 * shift)) < shift
        ps = [_merge_pair(ps[2 * i], ps[2 * i + 1], shift, first) for i in range(len(ps) // 2)]
        shift //= 2
    return ps[0]


def _peer_act_kernel(idx_ref, h_ref, g_ref, tab_ref, o_ref, part_ref):
    tn = h_ref.shape[0]

    def products(t, slot):
        x = _token_row(h_ref, t)
        for gi in range(_GROUPS):
            rows = []
            for w in range(_HALF):
                rows += _expert_rows(tab_ref, idx_ref[t, gi * _HALF + w])
            prods = [rows[2 * (j % _HALF) + j // _HALF] * x for j in _MERGE_ORDER]
            part_ref[slot, pl.ds(gi * SUBLANES, SUBLANES), :] = _merge8(prods)

    def reduce(t, slot):
        o_ref[pl.ds(t, 1), :] = jnp.sum(part_ref[slot].T, axis=0, keepdims=True)

    part_ref[...] = jnp.zeros(part_ref.shape, F32)

    def two_tokens(i, carry):
        t = 2 * i
        reduce(jnp.maximum(t - 2, 0), 0)
        reduce(jnp.maximum(t - 1, 0), 1)
        products(t, 0)
        products(t + 1, 1)
        return carry

    lax.fori_loop(0, tn // 2, two_tokens, 0)
    reduce(tn - 2, 0)
    reduce(tn - 1, 1)
    o_ref[...] = g_ref[...] * jax.nn.gelu(o_ref[...], approximate=True)


def peer_act(offs, h, g, table):
    N = offs.shape[0]
    tn = _PEER_TOKENS
    return pl.pallas_call(
        _peer_act_kernel,
        grid=(N // tn,),
        in_specs=[pl.BlockSpec((tn, PEER_PAIRS // 2), lambda i: (i, 0), memory_space=pltpu.SMEM),
                  pl.BlockSpec((tn, SUBLANES * LANES), lambda i: (i, 0)),
                  pl.BlockSpec((tn, PEER_PAIRS), lambda i: (i, 0)),
                  pl.BlockSpec(memory_space=pltpu.VMEM)],
        out_specs=pl.BlockSpec((tn, PEER_PAIRS), lambda i: (i, 0)),
        out_shape=jax.ShapeDtypeStruct((N, PEER_PAIRS), F32),
        scratch_shapes=[pltpu.VMEM((2, PEER_PAIRS, LANES), F32)],
        compiler_params=_params("arbitrary"),
        name="peer_act",
    )(offs, h, g, table)


_MIX_CHAINS = 4


def _peer_mix_kernel(idx_ref, coef_ref, tab_ref, x_ref, gate_ref, o_ref, cb_ref):
    tn = x_ref.shape[0]

    def spread(t):
        row = coef_ref[pl.ds(t, 1), :]
        return jnp.broadcast_to(row, (PEER_PAIRS, PEER_PAIRS)).T

    cb_ref[...] = spread(0)

    def token(t, carry):
        accs = [None] * _MIX_CHAINS
        for w in range(PEER_PAIRS // 2):
            for k, row in enumerate(_expert_rows(tab_ref, idx_ref[t, w])):
                j = _pair_position(w, k)
                term = jnp.broadcast_to(cb_ref[pl.ds(j, 1), :], (SUBLANES, LANES)) * row
                a = j % _MIX_CHAINS
                accs[a] = term if accs[a] is None else accs[a] + term
        y = (accs[0] + accs[1]) + (accs[2] + accs[3])
        y_row = jnp.concatenate([y[s:s + 1] for s in range(SUBLANES)], axis=1)
        o_ref[pl.ds(t, 1), :] = x_ref[pl.ds(t, 1), :] + gate_ref[0] * y_row
        cb_ref[...] = spread(jnp.minimum(t + 1, tn - 1))
        return carry

    lax.fori_loop(0, tn, token, 0)


def peer_mix(offs, coef, table, x, gate, tokens_per_batch):
    N, D = x.shape
    tn = _PEER_TOKENS
    assert tokens_per_batch % tn == 0
    if gate.shape[0] == 1:
        gate_map = lambda i: (0, 0, 0)
    else:
        gate_map = lambda i: ((i * tn) // tokens_per_batch, 0, 0)
    return pl.pallas_call(
        _peer_mix_kernel,
        grid=(N // tn,),
        in_specs=[pl.BlockSpec((tn, PEER_PAIRS // 2), lambda i: (i, 0), memory_space=pltpu.SMEM),
                  pl.BlockSpec((tn, PEER_PAIRS), lambda i: (i, 0)),
                  pl.BlockSpec(memory_space=pltpu.VMEM),
                  pl.BlockSpec((tn, D), lambda i: (i, 0)),
                  pl.BlockSpec((1, 1, D), gate_map)],
        out_specs=pl.BlockSpec((tn, D), lambda i: (i, 0)),
        out_shape=jax.ShapeDtypeStruct((N, D), F32),
        scratch_shapes=[pltpu.VMEM((PEER_PAIRS, LANES), F32)],
        compiler_params=_params("arbitrary"),
        name="peer_mix",
    )(offs, coef, table, x, gate)


def _rmsnorm_kernel(x_ref, g_ref, o_ref):
    x = x_ref[0]
    o_ref[0] = x * lax.rsqrt(jnp.mean(x * x, axis=-1, keepdims=True) + EPS) * g_ref[...]


def rmsnorm(x, g):
    B, L, D = x.shape
    tm = min(512, L)
    return pl.pallas_call(
        _rmsnorm_kernel,
        grid=(B, L // tm),
        in_specs=[pl.BlockSpec((1, tm, D), lambda b, i: (b, i, 0)),
                  pl.BlockSpec((1, D), lambda b, i: (0, 0))],
        out_specs=pl.BlockSpec((1, tm, D), lambda b, i: (b, i, 0)),
        out_shape=jax.ShapeDtypeStruct((B, L, D), F32),
        compiler_params=_params("parallel", "parallel"),
        name="final_rmsnorm",
    )(x, g.reshape(1, D))


def _rope_tables(T):
    t = jnp.arange(T, dtype=jnp.int32)
    row = (t // GRID_W).astype(F32)
    col = (t % GRID_W).astype(F32)
    n_freq = HEAD_DIM // 4
    inv_freq = ROPE_THETA ** (-jnp.arange(n_freq, dtype=F32) / n_freq)
    ang = jnp.concatenate([row[:, None] * inv_freq, col[:, None] * inv_freq], axis=-1)
    cos, sin = jnp.cos(ang), jnp.sin(ang)
    return jnp.tile(cos, (1, 4)), jnp.tile(jnp.concatenate([-sin, sin], axis=-1), (1, 2))


def _peer_ffn_residual(x, g2, sc2, sh2, gate, w_q, keys, u_tab, v_tab):
    B, L, D = x.shape
    N = B * L
    q, h = norm_mod_proj(x, g2, sc2, sh2, w_q, (("bf16", w_q.shape[1], 1.0),), emit_h=True)
    offs, gates = peer_topk(q.reshape(N, -1), keys)
    coef = peer_act(offs, h.reshape(N, D), gates, u_tab)
    out = peer_mix(offs, coef, v_tab, x.reshape(N, D), gate, L)
    return out.reshape(B, L, D)


_Q_SCALE = HEAD_DIM ** -0.5
_NA_W = NA_HEADS * HEAD_DIM
_GQA_W = GQA_HEADS * HEAD_DIM
_GQA_KV_W = GQA_KV_HEADS * HEAD_DIM
_IN_OUTS = (("f32", 2 * CONV_CH), ("bf16", _NA_W, _Q_SCALE), ("norm", _GQA_W // LANES, _Q_SCALE, 0),
            ("bf16", _NA_W, 1.0), ("bf16", _NA_W, 1.0), ("norm", _GQA_KV_W // LANES, 1.0, 1),
            ("bf16", _GQA_KV_W, 1.0))
_OFF_G_Q = 2 * CONV_CH + _NA_W
_OFF_G_OUT = CONV_CH + _NA_W
_GQA_PAIR_ORDER = tuple(h for g in range(GQA_HEADS // GQA_KV_HEADS) for h in (g, GQA_HEADS // GQA_KV_HEADS + g))
assert GQA_KV_HEADS == 2 and 2 * HEAD_DIM == LANES


def _reorder_heads(w, start, axis):
    idx = jnp.concatenate([jnp.arange(HEAD_DIM) + start + h * HEAD_DIM for h in _GQA_PAIR_ORDER])
    full = jnp.arange(w.shape[axis]).at[start:start + _GQA_W].set(idx)
    return jnp.take(w, full, axis=axis)


def kernel(x, c, ctx, c_ctx, norm1_g, norm2_g, w_ada, b_ada, w_in, conv_w, conv_b, conv_ln_g, conv_ln_b,
           na_rel_bias, gqa_q_norm, gqa_k_norm, w_out, peer_w_q, peer_keys, peer_u, peer_v, final_norm_g):
    B, T, D = x.shape
    depth = w_in.shape[0]
    rope = _rope_tables(T)
    ada_rows = -(-(B + 1) // SUBLANES) * SUBLANES
    ada_in = jnp.zeros((ada_rows, D), F32).at[:B].set(c).at[B].set(c_ctx)

    for l in range(depth):
        last = l == depth - 1
        mod = ada_mod(ada_in, w_ada[l], b_ada[l])
        sh1, sc1, g1, sh2, sc2, g2 = [m.reshape(B, 1, D) for m in jnp.split(mod[:B], 6, axis=-1)]
        csh1, csc1, cg1, csh2, csc2, cg2 = [m.reshape(1, 1, D) for m in jnp.split(mod[B:B + 1], 6, axis=-1)]
        w_in_b = _reorder_heads(w_in[l], _OFF_G_Q, 1).astype(BF16)
        w_out_b = _reorder_heads(w_out[l], _OFF_G_OUT, 0).astype(BF16)
        w_q_b = peer_w_q[l].astype(BF16)
        keys_b = peer_keys[l].astype(BF16)
        u_tab = pack_expert_table(peer_u, l)
        v_tab = pack_expert_table(peer_v, l)
        qk_gains = jnp.stack([jnp.tile(gqa_q_norm[l], 2), jnp.tile(gqa_k_norm[l], 2)])

        za, na_q, g_q, na_k, na_v, g_k, g_v = norm_mod_proj(x, norm1_g[l], sc1, sh1, w_in_b, _IN_OUTS,
                                                            gains=qk_gains, rope=rope)
        zca, cna_q, cg_q, cna_k, cna_v, cg_k, cg_v = norm_mod_proj(ctx, norm1_g[l], csc1, csh1, w_in_b, _IN_OUTS,
                                                                   gains=qk_gains)
        a = conformer_conv(za, conv_w[l], conv_b[l], conv_ln_g[l], conv_ln_b[l])
        bm = neighbourhood_attention(na_q, na_k, na_v, cna_k, cna_v, na_bias_table(na_rel_bias[l]))
        gm = pair_attention(g_q, [(g_k, g_v), (cg_k, cg_v)])
        x = proj_residual((a, bm, gm), w_out_b, x, g1)

        x = _peer_ffn_residual(x, norm2_g[l], sc2, sh2, g2, w_q_b, keys_b, u_tab, v_tab)

        if not last:
            ac = conformer_conv(zca, conv_w[l], conv_b[l], conv_ln_g[l], conv_ln_b[l])
            bc = pair_attention(cna_q, [(cna_k, cna_v)])
            gc = pair_attention(cg_q, [(cg_k, cg_v)])
            ctx = proj_residual((ac, bc, gc), w_out_b, ctx, cg1)
            ctx = _peer_ffn_residual(ctx, norm2_g[l], csc2, csh2, cg2, w_q_b, keys_b, u_tab, v_tab)
    return rmsnorm(x, final_norm_g)
```

```python
import functools

import jax
import jax.numpy as jnp
from jax import lax
from jax.experimental import pallas as pl
from jax.experimental.pallas import tpu as pltpu

F32 = jnp.float32
BF16 = jnp.bfloat16

GRID_W = 64
HEAD_DIM = 64
CONV_CH = 256
CONV_WIDTH = 31
NA_HEADS = 6
NA_WIN_H = 8
NA_WIN_W = 16
GQA_HEADS = 6
GQA_KV_HEADS = 2
ROPE_THETA = 10000.0
PEER_HEADS = 8
PEER_N_KEYS = 128
PEER_D_KEY = 128
PEER_TOPK = 16
EPS = 1e-6

PEER_PAIRS = PEER_HEADS * PEER_TOPK

LANES = 128
SUBLANES = 8
_HALF = SUBLANES // 2
assert PEER_N_KEYS * PEER_N_KEYS * _HALF <= 1 << 16
VMEM_LIMIT = 48 * 1024 * 1024
MASK_VALUE = -1e30


def _params(*sem):
    return pltpu.CompilerParams(dimension_semantics=sem, vmem_limit_bytes=VMEM_LIMIT)


def _ada_kernel(a_ref, w_ref, b_ref, o_ref):
    a = a_ref[...]
    s = a * jax.nn.sigmoid(a)
    o_ref[...] = jnp.dot(s.astype(BF16), w_ref[...].astype(BF16), preferred_element_type=F32) + b_ref[...]


def ada_mod(a, w, b):
    R, D = a.shape
    N = w.shape[1]
    tn = 1024
    return pl.pallas_call(
        _ada_kernel,
        grid=(N // tn,),
        in_specs=[pl.BlockSpec((R, D), lambda j: (0, 0)),
                  pl.BlockSpec((D, tn), lambda j: (0, j)),
                  pl.BlockSpec((1, tn), lambda j: (0, j))],
        out_specs=pl.BlockSpec((R, tn), lambda j: (0, j)),
        out_shape=jax.ShapeDtypeStruct((R, N), F32),
        compiler_params=_params("parallel"),
        name="ada_mod",
    )(a, w, b.reshape(1, N))


def _pair_norm_rope(x, gain, rope_cos, rope_sin):
    lane = lax.broadcasted_iota(jnp.int32, x.shape, 1)
    lower = lane < HEAD_DIM
    sq = x * x
    ms_lo = jnp.sum(jnp.where(lower, sq, 0.0), axis=-1, keepdims=True) * (1.0 / HEAD_DIM)
    ms_hi = jnp.sum(jnp.where(lower, 0.0, sq), axis=-1, keepdims=True) * (1.0 / HEAD_DIM)
    y = x * jnp.where(lower, lax.rsqrt(ms_lo + EPS), lax.rsqrt(ms_hi + EPS)) * gain
    if rope_cos is None:
        return y
    half = HEAD_DIM // 2
    first = (lane % HEAD_DIM) < half
    partner = jnp.where(first, pltpu.roll(y, LANES - half, axis=1), pltpu.roll(y, half, axis=1))
    return y * rope_cos + partner * rope_sin


def _norm_mod_proj_kernel(*refs, outs, emit_h, has_gains, rope):
    x_ref, g_ref, sc_ref, sh_ref, w_ref = refs[:5]
    n_in = 5
    gains_ref = cos = sin = None
    if has_gains:
        gains_ref = refs[n_in]
        n_in += 1
    if rope:
        cos, sin = refs[n_in][...], refs[n_in + 1][...]
        n_in += 2
    out_refs = refs[n_in:]
    x = x_ref[0]
    y = x * lax.rsqrt(jnp.mean(x * x, axis=-1, keepdims=True) + EPS) * g_ref[...]
    h = y * (1.0 + sc_ref[0]) + sh_ref[0]
    z = jnp.dot(h.astype(BF16), w_ref[...], preferred_element_type=F32)
    off = 0
    for o_ref, spec in zip(out_refs, outs):
        if spec[0] == "f32":
            o_ref[0] = z[:, off:off + spec[1]]
            off += spec[1]
        elif spec[0] == "bf16":
            o_ref[0] = (z[:, off:off + spec[1]] * spec[2]).astype(BF16)
            off += spec[1]
        else:
            groups = []
            for _ in range(spec[1]):
                v = _pair_norm_rope(z[:, off:off + LANES], gains_ref[spec[3]:spec[3] + 1, :], cos, sin)
                groups.append((v * spec[2]).astype(BF16))
                off += LANES
            o_ref[0] = groups[0] if len(groups) == 1 else jnp.concatenate(groups, axis=-1)
    if emit_h:
        out_refs[len(outs)][0] = h


def _out_width(spec):
    return spec[1] * LANES if spec[0] == "norm" else spec[1]


def norm_mod_proj(x, g, sc, sh, w, outs, emit_h=False, gains=None, rope=None):
    B, L, D = x.shape
    N = w.shape[1]
    assert sum(_out_width(o) for o in outs) == N
    tm = min(256, L)
    per_batch = sc.shape[0] == B
    mod_map = (lambda b, i: (b, 0, 0)) if per_batch else (lambda b, i: (0, 0, 0))
    in_specs = [pl.BlockSpec((1, tm, D), lambda b, i: (b, i, 0)),
                pl.BlockSpec((1, D), lambda b, i: (0, 0)),
                pl.BlockSpec((1, 1, D), mod_map),
                pl.BlockSpec((1, 1, D), mod_map),
                pl.BlockSpec((D, N), lambda b, i: (0, 0))]
    args = [x, g.reshape(1, D), sc, sh, w]
    if gains is not None:
        in_specs.append(pl.BlockSpec(gains.shape, lambda b, i: (0, 0)))
        args.append(gains)
    if rope is not None:
        for tab in rope:
            in_specs.append(pl.BlockSpec((tm, LANES), lambda b, i: (i, 0)))
            args.append(tab)
    out_shape = [jax.ShapeDtypeStruct((B, L, _out_width(o)), F32 if o[0] == "f32" else BF16) for o in outs]
    out_specs = [pl.BlockSpec((1, tm, _out_width(o)), lambda b, i: (b, i, 0)) for o in outs]
    if emit_h:
        out_shape.append(jax.ShapeDtypeStruct((B, L, D), F32))
        out_specs.append(pl.BlockSpec((1, tm, D), lambda b, i: (b, i, 0)))
    return pl.pallas_call(
        functools.partial(_norm_mod_proj_kernel, outs=outs, emit_h=emit_h, has_gains=gains is not None,
                          rope=rope is not None),
        grid=(B, L // tm),
        in_specs=in_specs,
        out_specs=out_specs,
        out_shape=out_shape,
        compiler_params=_params("parallel", "parallel"),
        name="norm_mod_proj",
    )(*args)


_CONV_PAD = 16


def _conv_kernel(za_ref, w_ref, b_ref, lg_ref, lb_ref, o_ref, upad_ref, *, L, tc):
    c = pl.program_id(1)

    @pl.when(c == 0)
    def _():
        val = za_ref[0, :, :CONV_CH]
        gate = za_ref[0, :, CONV_CH:]
        zeros = jnp.zeros((_CONV_PAD, CONV_CH), F32)
        upad_ref[pl.ds(0, _CONV_PAD), :] = zeros
        upad_ref[pl.ds(_CONV_PAD + L, _CONV_PAD), :] = zeros
        upad_ref[pl.ds(_CONV_PAD, L), :] = val * jax.nn.sigmoid(gate)

    start = pl.multiple_of(c * tc, SUBLANES)
    win = upad_ref[pl.ds(start, tc + 2 * _CONV_PAD), :]
    acc = jnp.zeros((tc, CONV_CH), F32)
    first = _CONV_PAD - CONV_WIDTH // 2
    rows = tc + 2 * _CONV_PAD
    shifted = [win if c == 0 else pltpu.roll(win, rows - c, axis=0) for c in range(SUBLANES)]
    for j in range(CONV_WIDTH):
        a, c = divmod(first + j, SUBLANES)
        acc = acc + shifted[c][a * SUBLANES:a * SUBLANES + tc, :] * w_ref[j:j + 1, :]
    y = acc + b_ref[...]
    mu = jnp.mean(y, axis=-1, keepdims=True)
    d = y - mu
    var = jnp.mean(d * d, axis=-1, keepdims=True)
    yn = d * lax.rsqrt(var + EPS) * lg_ref[...] + lb_ref[...]
    o_ref[0] = (yn * jax.nn.sigmoid(yn)).astype(o_ref.dtype)


def conformer_conv(za, w_dw, b_dw, ln_g, ln_b):
    B, L, _ = za.shape
    tc = min(256, L)
    vec = lambda v: v.reshape(1, CONV_CH)
    return pl.pallas_call(
        functools.partial(_conv_kernel, L=L, tc=tc),
        grid=(B, L // tc),
        in_specs=[pl.BlockSpec((1, L, 2 * CONV_CH), lambda b, c: (b, 0, 0)),
                  pl.BlockSpec((CONV_WIDTH, CONV_CH), lambda b, c: (0, 0)),
                  pl.BlockSpec((1, CONV_CH), lambda b, c: (0, 0)),
                  pl.BlockSpec((1, CONV_CH), lambda b, c: (0, 0)),
                  pl.BlockSpec((1, CONV_CH), lambda b, c: (0, 0))],
        out_specs=pl.BlockSpec((1, tc, CONV_CH), lambda b, c: (b, c, 0)),
        out_shape=jax.ShapeDtypeStruct((B, L, CONV_CH), BF16),
        scratch_shapes=[pltpu.VMEM((L + 2 * _CONV_PAD, CONV_CH), F32)],
        compiler_params=_params("parallel", "arbitrary"),
        name="conformer_conv",
    )(za, w_dw, vec(b_dw), vec(ln_g), vec(ln_b))


_NA_ROWS_PER_STEP = 4


def _na_row_offset(r, rows):
    return r - jnp.clip(r - NA_WIN_H // 2, 0, rows - NA_WIN_H)


def _head_pair_rows(qg, lower):
    zero = jnp.zeros_like(qg)
    return jnp.concatenate([jnp.where(lower, qg, zero), jnp.where(lower, zero, qg)], axis=0)


def _na_kernel(q_ref, k_ref, v_ref, kc_ref, vc_ref, bias_ref, o_ref, *, rows):
    nt = (((1,), (1,)), ((), ()))
    lower = lax.broadcasted_iota(jnp.int32, (GRID_W, LANES), 1) < HEAD_DIM
    cols = [slice(g * LANES, (g + 1) * LANES) for g in range(NA_HEADS // 2)]
    jobs = []
    for rr in range(_NA_ROWS_PER_STEP):
        r = pl.program_id(1) * _NA_ROWS_PER_STEP + rr
        off = _na_row_offset(r, rows)
        band = pl.ds(pl.multiple_of((r - off) * GRID_W, GRID_W), NA_WIN_H * GRID_W)
        jobs += [(slice(rr * GRID_W, (rr + 1) * GRID_W), off, band, g) for g in range(NA_HEADS // 2)]
    qs = [_head_pair_rows(q_ref[0, qr, cols[g]], lower) for qr, _, _, g in jobs]
    s_loc = [lax.dot_general(q, k_ref[0, band, cols[g]], nt, preferred_element_type=F32) + bias_ref[off, g]
             for q, (_, off, band, g) in zip(qs, jobs)]
    s_ctx = [lax.dot_general(q, kc_ref[0, :, cols[g]], nt, preferred_element_type=F32)
             for q, (_, _, _, g) in zip(qs, jobs)]
    p_loc, p_ctx, den = [], [], []
    for sl, sc in zip(s_loc, s_ctx):
        m = jnp.maximum(jnp.max(sl, axis=-1, keepdims=True), jnp.max(sc, axis=-1, keepdims=True))
        p_loc.append(jnp.exp(sl - m))
        p_ctx.append(jnp.exp(sc - m))
        den.append(jnp.sum(p_loc[-1], axis=-1, keepdims=True) + jnp.sum(p_ctx[-1], axis=-1, keepdims=True))
    outs = []
    for (_, _, band, g), pl_, pc, d in zip(jobs, p_loc, p_ctx, den):
        o = (jnp.dot(pl_.astype(BF16), v_ref[0, band, cols[g]], preferred_element_type=F32)
             + jnp.dot(pc.astype(BF16), vc_ref[0, :, cols[g]], preferred_element_type=F32)) / d
        outs.append(jnp.where(lower, o[:GRID_W], o[GRID_W:]).astype(o_ref.dtype))
    n = NA_HEADS // 2
    for rr in range(_NA_ROWS_PER_STEP):
        o_ref[0, rr * GRID_W:(rr + 1) * GRID_W, :] = jnp.concatenate(outs[rr * n:(rr + 1) * n], axis=-1)


def na_bias_table(rel_bias):
    cols = jnp.arange(GRID_W, dtype=jnp.int32)
    c0 = jnp.clip(cols - NA_WIN_W // 2, 0, GRID_W - NA_WIN_W)
    in_win = (cols[None, :] >= c0[:, None]) & (cols[None, :] < c0[:, None] + NA_WIN_W)
    col_idx = jnp.clip(cols[None, :] - cols[:, None] + (NA_WIN_W - 1), 0, 2 * NA_WIN_W - 2)
    off = jnp.arange(NA_WIN_H, dtype=jnp.int32)
    row_idx = off[None, :] - off[:, None] + (NA_WIN_H - 1)
    t = rel_bias[:, row_idx]
    t = t[..., col_idx]
    t = jnp.where(in_win[None, None, None], t, MASK_VALUE)
    return t.transpose(1, 0, 3, 2, 4).reshape(NA_WIN_H, NA_HEADS // 2, 2 * GRID_W, NA_WIN_H * GRID_W)


def neighbourhood_attention(q, k, v, kc, vc, bias):
    B, T, W = q.shape
    C = kc.shape[1]
    rows = T // GRID_W
    rq = _NA_ROWS_PER_STEP * GRID_W
    assert rows >= NA_WIN_H and rows % _NA_ROWS_PER_STEP == 0 and W == NA_HEADS * HEAD_DIM
    return pl.pallas_call(
        functools.partial(_na_kernel, rows=rows),
        grid=(B, rows // _NA_ROWS_PER_STEP),
        in_specs=[pl.BlockSpec((1, rq, W), lambda b, r: (b, r, 0)),
                  pl.BlockSpec((1, T, W), lambda b, r: (b, 0, 0)),
                  pl.BlockSpec((1, T, W), lambda b, r: (b, 0, 0)),
                  pl.BlockSpec((1, C, W), lambda b, r: (b, 0, 0)),
                  pl.BlockSpec((1, C, W), lambda b, r: (b, 0, 0)),
                  pl.BlockSpec(bias.shape, lambda b, r: (0, 0, 0, 0))],
        out_specs=pl.BlockSpec((1, rq, W), lambda b, r: (b, r, 0)),
        out_shape=jax.ShapeDtypeStruct((B, T, W), BF16),
        compiler_params=_params("parallel", "arbitrary"),
        name="neighbourhood_attention",
    )(q, k, v, kc, vc, bias)


def _pair_attn_kernel(*refs, Gq, Gk, bq, n_sets):
    q_ref = refs[0]
    kv_refs = refs[1:1 + 2 * n_sets]
    o_ref = refs[1 + 2 * n_sets]
    nt = (((1,), (1,)), ((), ()))
    sets = range(n_sets)
    lower = lax.broadcasted_iota(jnp.int32, (bq, LANES), 1) < HEAD_DIM
    col = lambda g: slice(g * LANES, (g + 1) * LANES)
    pairs = [_head_pair_rows(q_ref[0, :, col(g)], lower) for g in range(Gq)]
    if Gk == 1:
        jobs = [(jnp.concatenate([p[:bq] for p in pairs], axis=0), col(0)),
                (jnp.concatenate([p[bq:] for p in pairs], axis=0), col(0))]
    else:
        jobs = [(pairs[g], col(g)) for g in range(Gq)]
    ss = [[lax.dot_general(rows, kv_refs[2 * i][0, :, kc], nt, preferred_element_type=F32) for i in sets]
          for rows, kc in jobs]
    ps, dens = [], []
    for s in ss:
        m = functools.reduce(jnp.maximum, [jnp.max(x, axis=-1, keepdims=True) for x in s])
        ps.append([jnp.exp(x - m) for x in s])
        dens.append(functools.reduce(jnp.add, [jnp.sum(p, axis=-1, keepdims=True) for p in ps[-1]]))
    os_ = []
    for (rows, kc), p, den in zip(jobs, ps, dens):
        o = functools.reduce(jnp.add, [jnp.dot(p[i].astype(BF16), kv_refs[2 * i + 1][0, :, kc],
                                               preferred_element_type=F32) for i in sets])
        os_.append(o / den)
    outs = []
    for g in range(Gq):
        if Gk == 1:
            first, second = os_[0][g * bq:(g + 1) * bq], os_[1][g * bq:(g + 1) * bq]
        else:
            first, second = os_[g][:bq], os_[g][bq:]
        outs.append(jnp.where(lower, first, second).astype(o_ref.dtype))
    o_ref[0] = jnp.concatenate(outs, axis=-1)


def pair_attention(q, kv_sets):
    B, L, Wq = q.shape
    Wk = kv_sets[0][0].shape[2]
    Gq, Gk = Wq // LANES, Wk // LANES
    assert Gk in (1, Gq)
    bq = min(256, L)
    in_specs = [pl.BlockSpec((1, bq, Wq), lambda b, i: (b, i, 0))]
    args = [q]
    for k, v in kv_sets:
        for t in (k, v):
            in_specs.append(pl.BlockSpec((1, t.shape[1], Wk), lambda b, i: (b, 0, 0)))
            args.append(t)
    return pl.pallas_call(
        functools.partial(_pair_attn_kernel, Gq=Gq, Gk=Gk, bq=bq, n_sets=len(kv_sets)),
        grid=(B, L // bq),
        in_specs=in_specs,
        out_specs=pl.BlockSpec((1, bq, Wq), lambda b, i: (b, i, 0)),
        out_shape=jax.ShapeDtypeStruct((B, L, Wq), BF16),
        compiler_params=_params("parallel", "arbitrary"),
        name="pair_attention",
    )(*args)


def _proj_residual_kernel(a_ref, b_ref, g_ref, wa_ref, wb_ref, wg_ref, x_ref, gate_ref, o_ref):
    y = (jnp.dot(a_ref[0], wa_ref[...], preferred_element_type=F32)
         + jnp.dot(b_ref[0], wb_ref[...], preferred_element_type=F32)
         + jnp.dot(g_ref[0], wg_ref[...], preferred_element_type=F32))
    o_ref[0] = x_ref[0] + gate_ref[0] * y


def proj_residual(parts, w, x, gate):
    B, L, D = x.shape
    tm = min(512, L)
    gate_map = (lambda b, i: (b, 0, 0)) if gate.shape[0] == B else (lambda b, i: (0, 0, 0))
    widths = [p.shape[2] for p in parts]
    assert sum(widths) == w.shape[0]
    starts = [sum(widths[:i]) for i in range(len(parts))]
    ws = [w[s:s + k] for s, k in zip(starts, widths)]
    return pl.pallas_call(
        _proj_residual_kernel,
        grid=(B, L // tm),
        in_specs=[pl.BlockSpec((1, tm, k), lambda b, i: (b, i, 0)) for k in widths]
        + [pl.BlockSpec((k, D), lambda b, i: (0, 0)) for k in widths]
        + [pl.BlockSpec((1, tm, D), lambda b, i: (b, i, 0)), pl.BlockSpec((1, 1, D), gate_map)],
        out_specs=pl.BlockSpec((1, tm, D), lambda b, i: (b, i, 0)),
        out_shape=jax.ShapeDtypeStruct((B, L, D), F32),
        compiler_params=_params("parallel", "parallel"),
        name="proj_residual",
    )(*parts, *ws, x, gate)


def _top_rows(s, k, order=None, payload=None):
    if order is None:
        return _top_rows_by_row(s, k)
    n = s.shape[0]
    cut = lambda x: [x[i:i + SUBLANES] for i in range(0, n, SUBLANES)]
    pieces, orders, payloads = cut(s), cut(order), cut(payload)
    vals, picks = [], []
    for _ in range(k):
        level = list(zip(pieces, orders, payloads))
        while len(level) > 1:
            nxt = [(jnp.maximum(va, vb), jnp.where(va >= vb, oa, ob), jnp.where(va >= vb, pa, pb))
                   for (va, oa, pa), (vb, ob, pb) in zip(level[0::2], level[1::2])]
            if len(level) % 2:
                nxt.append(level[-1])
            level = nxt
        v, o, p = level[0]
        m = jnp.max(v, axis=0, keepdims=True)
        first = jnp.min(jnp.where(v == m, o, jnp.inf), axis=0, keepdims=True)
        vals.append(m)
        picks.append(jnp.max(jnp.where(o == first, p, -1.0), axis=0, keepdims=True))
        pieces = [jnp.where(oi == first, -jnp.inf, pi) for pi, oi in zip(pieces, orders)]
    return jnp.concatenate(vals, axis=0), jnp.concatenate(picks, axis=0)


def _top_rows_by_row(s, k):
    n, t = s.shape
    sub = lax.broadcasted_iota(jnp.int32, (SUBLANES, t), 0).astype(F32)
    pieces = [s[i:i + SUBLANES] for i in range(0, n, SUBLANES)]
    rows = [sub + float(i) for i in range(0, n, SUBLANES)]
    vals, picks = [], []
    for _ in range(k):
        level = list(zip(pieces, rows))
        while len(level) > 1:
            nxt = []
            for (va, ia), (vb, ib) in zip(level[0::2], level[1::2]):
                nxt.append((jnp.maximum(va, vb), jnp.where(va >= vb, ia, ib)))
            level = nxt
        v, i = level[0]
        m = jnp.max(v, axis=0, keepdims=True)
        first = jnp.min(jnp.where(v == m, i, jnp.inf), axis=0, keepdims=True)
        vals.append(m)
        picks.append(first)
        pieces = [jnp.where(r == first, -jnp.inf, p) for p, r in zip(pieces, rows)]
    return jnp.concatenate(vals, axis=0), jnp.concatenate(picks, axis=0)


def _pair_candidates(s0, i0, s1, i1):
    K = PEER_TOPK
    t = s0.shape[1]
    sub = lax.broadcasted_iota(jnp.int32, (SUBLANES, t), 0)
    scores, order, ids = [], [], []

    def add(a_rows, b_rows, a_of_row, b_of_row):
        sa, ia = a_rows
        sb, ib = b_rows
        ok = (a_of_row + 1) * (b_of_row + 1) <= K
        scores.append(jnp.where(ok, sa + sb, -jnp.inf))
        order.append((a_of_row * K + b_of_row).astype(F32))
        ids.append(ia * float(PEER_N_KEYS) + ib)

    row = lambda x, r: (x[0][r:r + 1], x[1][r:r + 1])
    rows = lambda x, r: (x[0][r:r + SUBLANES], x[1][r:r + SUBLANES])
    A, Bv = (s0, i0), (s1, i1)
    add(row(A, 0), rows(Bv, 0), jnp.zeros_like(sub), sub)
    add(row(A, 0), rows(Bv, SUBLANES), jnp.zeros_like(sub), sub + SUBLANES)
    for a in range(1, 4):
        add(row(A, a), rows(Bv, 0), jnp.full_like(sub, a), sub)
    for b in range(3):
        dup = sub < 4
        sa, ia = rows(A, 0)
        add((jnp.where(dup, -jnp.inf, sa), ia), row(Bv, b), sub, jnp.full_like(sub, b))
    add(rows(A, SUBLANES), row(Bv, 0), sub + SUBLANES, jnp.zeros_like(sub))
    cat = lambda xs: jnp.concatenate(xs, axis=0)
    return cat(scores), cat(order), cat(ids)


def _peer_topk_kernel(q_ref, keys_ref, offs_ref, g_ref):
    K = PEER_TOPK
    assert K == 2 * SUBLANES
    nt = (((1,), (1,)), ((), ()))
    words, gates = [], []
    for h in range(PEER_HEADS):
        tops = []
        for p in range(2):
            col = (2 * h + p) * PEER_D_KEY
            qhp = q_ref[:, col:col + PEER_D_KEY]
            s = lax.dot_general(keys_ref[h, p], qhp, nt, preferred_element_type=F32)
            tops.append(_top_rows(s, K))
        (s0, i0), (s1, i1) = tops
        cand_s, cand_order, cand_i = _pair_candidates(s0, i0, s1, i1)
        best_s, best_i = _top_rows(cand_s, K, order=cand_order, payload=cand_i)
        e = jnp.exp(best_s - jnp.max(best_s, axis=0, keepdims=True))
        gates.append(e / jnp.sum(e, axis=0, keepdims=True))
        off = best_i.astype(jnp.int32) * _HALF
        lo = jnp.concatenate([off[0:_HALF], off[SUBLANES:SUBLANES + _HALF]], axis=0)
        hi = jnp.concatenate([off[_HALF:SUBLANES], off[SUBLANES + _HALF:K]], axis=0)
        words.append(lo | (hi << 16))
    g_ref[...] = jnp.concatenate(gates, axis=0).T
    words.append(jnp.zeros((PEER_PAIRS // 2, q_ref.shape[0]), jnp.int32))
    offs_ref[...] = jnp.concatenate(words, axis=0).T[:, :PEER_PAIRS // 2]


def peer_topk(q, keys):
    N = q.shape[0]
    tm = LANES
    return pl.pallas_call(
        _peer_topk_kernel,
        grid=(N // tm,),
        in_specs=[pl.BlockSpec((tm, q.shape[1]), lambda i: (i, 0)),
                  pl.BlockSpec(keys.shape, lambda i: (0, 0, 0, 0))],
        out_specs=[pl.BlockSpec((tm, PEER_PAIRS // 2), lambda i: (i, 0)),
                   pl.BlockSpec((tm, PEER_PAIRS), lambda i: (i, 0))],
        out_shape=[jax.ShapeDtypeStruct((N, PEER_PAIRS // 2), jnp.int32),
                   jax.ShapeDtypeStruct((N, PEER_PAIRS), F32)],
        compiler_params=_params("parallel"),
        name="peer_topk",
    )(q, keys)


_PEER_TOKENS = 128
_GROUPS = PEER_PAIRS // SUBLANES


def _pack_table_kernel(w_ref, o_ref):
    te = w_ref.shape[0]
    bits = lambda x: pltpu.bitcast(x.astype(BF16).astype(F32), jnp.uint32)
    for i in range(_HALF):
        lo = bits(w_ref[:, (2 * i) * LANES:(2 * i + 1) * LANES]) >> 16
        hi = bits(w_ref[:, (2 * i + 1) * LANES:(2 * i + 2) * LANES])
        o_ref[pl.ds(i, te, stride=_HALF), :] = lo | hi


def pack_expert_table(w, layer):
    _, E, D = w.shape
    assert D == SUBLANES * LANES
    te = 256
    return pl.pallas_call(
        _pack_table_kernel,
        grid=(E // te,),
        in_specs=[pl.BlockSpec((None, te, D), lambda i: (layer, i, 0))],
        out_specs=pl.BlockSpec((te * _HALF, LANES), lambda i: (i, 0)),
        out_shape=jax.ShapeDtypeStruct((E * _HALF, LANES), jnp.uint32),
        compiler_params=_params("parallel"),
        name="pack_expert_table",
    )(w)


def _pair_position(word, half):
    return SUBLANES * (word // _HALF) + _HALF * half + word % _HALF


def _expert_rows(tab_ref, word):
    starts = (word & 0xFFFF, lax.shift_right_logical(word, 16))
    return [pltpu.bitcast(tab_ref[pl.ds(pl.multiple_of(s, _HALF), _HALF), :], BF16).astype(F32) for s in starts]


def _token_row(ref, t):
    row = ref[pl.ds(t, 1), :]
    return jnp.concatenate([row[:, s * LANES:(s + 1) * LANES] for s in range(SUBLANES)], axis=0)


def _merge_pair(a, b, shift, first):
    if shift == _HALF:
        return jnp.where(first, a, b) + pltpu.roll(jnp.where(first, b, a), shift, axis=0)
    bs = pltpu.roll(b, shift, axis=0)
    return jnp.where(first, a, bs) + pltpu.roll(jnp.where(first, bs, a), SUBLANES - shift, axis=0)


_MERGE_ORDER = (0, 4, 2, 6, 1, 5, 3, 7)


def _merge8(ps):
    sub = lax.broadcasted_iota(jnp.int32, (SUBLANES, LANES), 0)
    shift = _HALF
    while len(ps) > 1:
        first = (sub % (2 * shift)) < shift
        ps = [_merge_pair(ps[2 * i], ps[2 * i + 1], shift, first) for i in range(len(ps) // 2)]
        shift //= 2
    return ps[0]


def _peer_act_kernel(idx_ref, h_ref, g_ref, tab_ref, o_ref, part_ref):
    tn = h_ref.shape[0]

    def products(t, slot):
        x = _token_row(h_ref, t)
        for gi in range(_GROUPS):
            rows = []
            for w in range(_HALF):
                rows += _expert_rows(tab_ref, idx_ref[t, gi * _HALF + w])
            prods = [rows[2 * (j % _HALF) + j // _HALF] * x for j in _MERGE_ORDER]
            part_ref[slot, pl.ds(gi * SUBLANES, SUBLANES), :] = _merge8(prods)

    def reduce(t, slot):
        o_ref[pl.ds(t, 1), :] = jnp.sum(part_ref[slot].T, axis=0, keepdims=True)

    part_ref[...] = jnp.zeros(part_ref.shape, F32)

    def two_tokens(i, carry):
        t = 2 * i
        reduce(jnp.maximum(t - 2, 0), 0)
        reduce(jnp.maximum(t - 1, 0), 1)
        products(t, 0)
        products(t + 1, 1)
        return carry

    lax.fori_loop(0, tn // 2, two_tokens, 0)
    reduce(tn - 2, 0)
    reduce(tn - 1, 1)
    o_ref[...] = g_ref[...] * jax.nn.gelu(o_ref[...], approximate=True)


def peer_act(offs, h, g, table):
    N = offs.shape[0]
    tn = _PEER_TOKENS
    return pl.pallas_call(
        _peer_act_kernel,
        grid=(N // tn,),
        in_specs=[pl.BlockSpec((tn, PEER_PAIRS // 2), lambda i: (i, 0), memory_space=pltpu.SMEM),
                  pl.BlockSpec((tn, SUBLANES * LANES), lambda i: (i, 0)),
                  pl.BlockSpec((tn, PEER_PAIRS), lambda i: (i, 0)),
                  pl.BlockSpec(memory_space=pltpu.VMEM)],
        out_specs=pl.BlockSpec((tn, PEER_PAIRS), lambda i: (i, 0)),
        out_shape=jax.ShapeDtypeStruct((N, PEER_PAIRS), F32),
        scratch_shapes=[pltpu.VMEM((2, PEER_PAIRS, LANES), F32)],
        compiler_params=_params("arbitrary"),
        name="peer_act",
    )(offs, h, g, table)


_MIX_CHAINS = 4


def _peer_mix_kernel(idx_ref, coef_ref, tab_ref, x_ref, gate_ref, o_ref, cb_ref):
    tn = x_ref.shape[0]

    def spread(t):
        row = coef_ref[pl.ds(t, 1), :]
        return jnp.broadcast_to(row, (PEER_PAIRS, PEER_PAIRS)).T

    cb_ref[...] = spread(0)

    def token(t, carry):
        accs = [None] * _MIX_CHAINS
        for w in range(PEER_PAIRS // 2):
            for k, row in enumerate(_expert_rows(tab_ref, idx_ref[t, w])):
                j = _pair_position(w, k)
                term = jnp.broadcast_to(cb_ref[pl.ds(j, 1), :], (SUBLANES, LANES)) * row
                a = j % _MIX_CHAINS
                accs[a] = term if accs[a] is None else accs[a] + term
        y = (accs[0] + accs[1]) + (accs[2] + accs[3])
        y_row = jnp.concatenate([y[s:s + 1] for s in range(SUBLANES)], axis=1)
        o_ref[pl.ds(t, 1), :] = x_ref[pl.ds(t, 1), :] + gate_ref[0] * y_row
        cb_ref[...] = spread(jnp.minimum(t + 1, tn - 1))
        return carry

    lax.fori_loop(0, tn, token, 0)


def peer_mix(offs, coef, table, x, gate, tokens_per_batch):
    N, D = x.shape
    tn = _PEER_TOKENS
    assert tokens_per_batch % tn == 0
    if gate.shape[0] == 1:
        gate_map = lambda i: (0, 0, 0)
    else:
        gate_map = lambda i: ((i * tn) // tokens_per_batch, 0, 0)
    return pl.pallas_call(
        _peer_mix_kernel,
        grid=(N // tn,),
        in_specs=[pl.BlockSpec((tn, PEER_PAIRS // 2), lambda i: (i, 0), memory_space=pltpu.SMEM),
                  pl.BlockSpec((tn, PEER_PAIRS), lambda i: (i, 0)),
                  pl.BlockSpec(memory_space=pltpu.VMEM),
                  pl.BlockSpec((tn, D), lambda i: (i, 0)),
                  pl.BlockSpec((1, 1, D), gate_map)],
        out_specs=pl.BlockSpec((tn, D), lambda i: (i, 0)),
        out_shape=jax.ShapeDtypeStruct((N, D), F32),
        scratch_shapes=[pltpu.VMEM((PEER_PAIRS, LANES), F32)],
        compiler_params=_params("arbitrary"),
        name="peer_mix",
    )(offs, coef, table, x, gate)


def _rmsnorm_kernel(x_ref, g_ref, o_ref):
    x = x_ref[0]
    o_ref[0] = x * lax.rsqrt(jnp.mean(x * x, axis=-1, keepdims=True) + EPS) * g_ref[...]


def rmsnorm(x, g):
    B, L, D = x.shape
    tm = min(512, L)
    return pl.pallas_call(
        _rmsnorm_kernel,
        grid=(B, L // tm),
        in_specs=[pl.BlockSpec((1, tm, D), lambda b, i: (b, i, 0)),
                  pl.BlockSpec((1, D), lambda b, i: (0, 0))],
        out_specs=pl.BlockSpec((1, tm, D), lambda b, i: (b, i, 0)),
        out_shape=jax.ShapeDtypeStruct((B, L, D), F32),
        compiler_params=_params("parallel", "parallel"),
        name="final_rmsnorm",
    )(x, g.reshape(1, D))


def _rope_tables(T):
    t = jnp.arange(T, dtype=jnp.int32)
    row = (t // GRID_W).astype(F32)
    col = (t % GRID_W).astype(F32)
    n_freq = HEAD_DIM // 4
    inv_freq = ROPE_THETA ** (-jnp.arange(n_freq, dtype=F32) / n_freq)
    ang = jnp.concatenate([row[:, None] * inv_freq, col[:, None] * inv_freq], axis=-1)
    cos, sin = jnp.cos(ang), jnp.sin(ang)
    return jnp.tile(cos, (1, 4)), jnp.tile(jnp.concatenate([-sin, sin], axis=-1), (1, 2))


def _peer_ffn_residual(x, g2, sc2, sh2, gate, w_q, keys, u_tab, v_tab):
    B, L, D = x.shape
    N = B * L
    q, h = norm_mod_proj(x, g2, sc2, sh2, w_q, (("bf16", w_q.shape[1], 1.0),), emit_h=True)
    offs, gates = peer_topk(q.reshape(N, -1), keys)
    coef = peer_act(offs, h.reshape(N, D), gates, u_tab)
    out = peer_mix(offs, coef, v_tab, x.reshape(N, D), gate, L)
    return out.reshape(B, L, D)


_Q_SCALE = HEAD_DIM ** -0.5
_NA_W = NA_HEADS * HEAD_DIM
_GQA_W = GQA_HEADS * HEAD_DIM
_GQA_KV_W = GQA_KV_HEADS * HEAD_DIM
_IN_OUTS = (("f32", 2 * CONV_CH), ("bf16", _NA_W, _Q_SCALE), ("norm", _GQA_W // LANES, _Q_SCALE, 0),
            ("bf16", _NA_W, 1.0), ("bf16", _NA_W, 1.0), ("norm", _GQA_KV_W // LANES, 1.0, 1),
            ("bf16", _GQA_KV_W, 1.0))
_OFF_G_Q = 2 * CONV_CH + _NA_W
_OFF_G_OUT = CONV_CH + _NA_W
_GQA_PAIR_ORDER = tuple(h for g in range(GQA_HEADS // GQA_KV_HEADS) for h in (g, GQA_HEADS // GQA_KV_HEADS + g))
assert GQA_KV_HEADS == 2 and 2 * HEAD_DIM == LANES


def _reorder_heads(w, start, axis):
    idx = jnp.concatenate([jnp.arange(HEAD_DIM) + start + h * HEAD_DIM for h in _GQA_PAIR_ORDER])
    full = jnp.arange(w.shape[axis]).at[start:start + _GQA_W].set(idx)
    return jnp.take(w, full, axis=axis)


def kernel(x, c, ctx, c_ctx, norm1_g, norm2_g, w_ada, b_ada, w_in, conv_w, conv_b, conv_ln_g, conv_ln_b,
           na_rel_bias, gqa_q_norm, gqa_k_norm, w_out, peer_w_q, peer_keys, peer_u, peer_v, final_norm_g):
    B, T, D = x.shape
    depth = w_in.shape[0]
    rope = _rope_tables(T)
    ada_rows = -(-(B + 1) // SUBLANES) * SUBLANES
    ada_in = jnp.zeros((ada_rows, D), F32).at[:B].set(c).at[B].set(c_ctx)

    for l in range(depth):
        last = l == depth - 1
        mod = ada_mod(ada_in, w_ada[l], b_ada[l])
        sh1, sc1, g1, sh2, sc2, g2 = [m.reshape(B, 1, D) for m in jnp.split(mod[:B], 6, axis=-1)]
        csh1, csc1, cg1, csh2, csc2, cg2 = [m.reshape(1, 1, D) for m in jnp.split(mod[B:B + 1], 6, axis=-1)]
        w_in_b = _reorder_heads(w_in[l], _OFF_G_Q, 1).astype(BF16)
        w_out_b = _reorder_heads(w_out[l], _OFF_G_OUT, 0).astype(BF16)
        w_q_b = peer_w_q[l].astype(BF16)
        keys_b = peer_keys[l].astype(BF16)
        u_tab = pack_expert_table(peer_u, l)
        v_tab = pack_expert_table(peer_v, l)
        qk_gains = jnp.stack([jnp.tile(gqa_q_norm[l], 2), jnp.tile(gqa_k_norm[l], 2)])

        za, na_q, g_q, na_k, na_v, g_k, g_v = norm_mod_proj(x, norm1_g[l], sc1, sh1, w_in_b, _IN_OUTS,
                                                            gains=qk_gains, rope=rope)
        zca, cna_q, cg_q, cna_k, cna_v, cg_k, cg_v = norm_mod_proj(ctx, norm1_g[l], csc1, csh1, w_in_b, _IN_OUTS,
                                                                   gains=qk_gains)
        a = conformer_conv(za, conv_w[l], conv_b[l], conv_ln_g[l], conv_ln_b[l])
        bm = neighbourhood_attention(na_q, na_k, na_v, cna_k, cna_v, na_bias_table(na_rel_bias[l]))
        gm = pair_attention(g_q, [(g_k, g_v), (cg_k, cg_v)])
        x = proj_residual((a, bm, gm), w_out_b, x, g1)

        x = _peer_ffn_residual(x, norm2_g[l], sc2, sh2, g2, w_q_b, keys_b, u_tab, v_tab)

        if not last:
            ac = conformer_conv(zca, conv_w[l], conv_b[l], conv_ln_g[l], conv_ln_b[l])
            bc = pair_attention(cna_q, [(cna_k, cna_v)])
            gc = pair_attention(cg_q, [(cg_k, cg_v)])
            ctx = proj_residual((ac, bc, gc), w_out_b, ctx, cg1)
            ctx = _peer_ffn_residual(ctx, norm2_g[l], csc2, csh2, cg2, w_q_b, keys_b, u_tab, v_tab)
    return rmsnorm(x, final_norm_g)
```

```python
import functools

import jax
import jax.numpy as jnp
from jax import lax
from jax.experimental import pallas as pl
from jax.experimental.pallas import tpu as pltpu

F32 = jnp.float32
BF16 = jnp.bfloat16

GRID_W = 64
HEAD_DIM = 64
CONV_CH = 256
CONV_WIDTH = 31
NA_HEADS = 6
NA_WIN_H = 8
NA_WIN_W = 16
GQA_HEADS = 6
GQA_KV_HEADS = 2
ROPE_THETA = 10000.0
PEER_HEADS = 8
PEER_N_KEYS = 128
PEER_D_KEY = 128
PEER_TOPK = 16
EPS = 1e-6

PEER_PAIRS = PEER_HEADS * PEER_TOPK

LANES = 128
SUBLANES = 8
_HALF = SUBLANES // 2
assert PEER_N_KEYS * PEER_N_KEYS * _HALF <= 1 << 16
VMEM_LIMIT = 48 * 1024 * 1024
MASK_VALUE = -1e30


def _params(*sem):
    return pltpu.CompilerParams(dimension_semantics=sem, vmem_limit_bytes=VMEM_LIMIT)


def _ada_kernel(a_ref, w_ref, b_ref, o_ref):
    a = a_ref[...]
    s = a * jax.nn.sigmoid(a)
    o_ref[...] = jnp.dot(s.astype(BF16), w_ref[...].astype(BF16), preferred_element_type=F32) + b_ref[...]


def ada_mod(a, w, b):
    R, D = a.shape
    N = w.shape[1]
    tn = 1024
    return pl.pallas_call(
        _ada_kernel,
        grid=(N // tn,),
        in_specs=[pl.BlockSpec((R, D), lambda j: (0, 0)),
                  pl.BlockSpec((D, tn), lambda j: (0, j)),
                  pl.BlockSpec((1, tn), lambda j: (0, j))],
        out_specs=pl.BlockSpec((R, tn), lambda j: (0, j)),
        out_shape=jax.ShapeDtypeStruct((R, N), F32),
        compiler_params=_params("parallel"),
        name="ada_mod",
    )(a, w, b.reshape(1, N))


def _pair_norm_rope(x, gain, rope_cos, rope_sin):
    lane = lax.broadcasted_iota(jnp.int32, x.shape, 1)
    lower = lane < HEAD_DIM
    sq = x * x
    ms_lo = jnp.sum(jnp.where(lower, sq, 0.0), axis=-1, keepdims=True) * (1.0 / HEAD_DIM)
    ms_hi = jnp.sum(jnp.where(lower, 0.0, sq), axis=-1, keepdims=True) * (1.0 / HEAD_DIM)
    y = x * jnp.where(lower, lax.rsqrt(ms_lo + EPS), lax.rsqrt(ms_hi + EPS)) * gain
    if rope_cos is None:
        return y
    half = HEAD_DIM // 2
    first = (lane % HEAD_DIM) < half
    partner = jnp.where(first, pltpu.roll(y, LANES - half, axis=1), pltpu.roll(y, half, axis=1))
    return y * rope_cos + partner * rope_sin


def _norm_mod_proj_kernel(*refs, outs, emit_h, has_gains, rope):
    x_ref, g_ref, sc_ref, sh_ref, w_ref = refs[:5]
    n_in = 5
    gains_ref = cos = sin = None
    if has_gains:
        gains_ref = refs[n_in]
        n_in += 1
    if rope:
        cos, sin = refs[n_in][...], refs[n_in + 1][...]
        n_in += 2
    out_refs = refs[n_in:]
    x = x_ref[0]
    y = x * lax.rsqrt(jnp.mean(x * x, axis=-1, keepdims=True) + EPS) * g_ref[...]
    h = y * (1.0 + sc_ref[0]) + sh_ref[0]
    z = jnp.dot(h.astype(BF16), w_ref[...], preferred_element_type=F32)
    off = 0
    for o_ref, spec in zip(out_refs, outs):
        if spec[0] == "f32":
            o_ref[0] = z[:, off:off + spec[1]]
            off += spec[1]
        elif spec[0] == "bf16":
            o_ref[0] = (z[:, off:off + spec[1]] * spec[2]).astype(BF16)
            off += spec[1]
        else:
            groups = []
            for _ in range(spec[1]):
                v = _pair_norm_rope(z[:, off:off + LANES], gains_ref[spec[3]:spec[3] + 1, :], cos, sin)
                groups.append((v * spec[2]).astype(BF16))
                off += LANES
            o_ref[0] = groups[0] if len(groups) == 1 else jnp.concatenate(groups, axis=-1)
    if emit_h:
        out_refs[len(outs)][0] = h


def _out_width(spec):
    return spec[1] * LANES if spec[0] == "norm" else spec[1]


def norm_mod_proj(x, g, sc, sh, w, outs, emit_h=False, gains=None, rope=None):
    B, L, D = x.shape
    N = w.shape[1]
    assert sum(_out_width(o) for o in outs) == N
    tm = min(256, L)
    per_batch = sc.shape[0] == B
    mod_map = (lambda b, i: (b, 0, 0)) if per_batch else (lambda b, i: (0, 0, 0))
    in_specs = [pl.BlockSpec((1, tm, D), lambda b, i: (b, i, 0)),
                pl.BlockSpec((1, D), lambda b, i: (0, 0)),
                pl.BlockSpec((1, 1, D), mod_map),
                pl.BlockSpec((1, 1, D), mod_map),
                pl.BlockSpec((D, N), lambda b, i: (0, 0))]
    args = [x, g.reshape(1, D), sc, sh, w]
    if gains is not None:
        in_specs.append(pl.BlockSpec(gains.shape, lambda b, i: (0, 0)))
        args.append(gains)
    if rope is not None:
        for tab in rope:
            in_specs.append(pl.BlockSpec((tm, LANES), lambda b, i: (i, 0)))
            args.append(tab)
    out_shape = [jax.ShapeDtypeStruct((B, L, _out_width(o)), F32 if o[0] == "f32" else BF16) for o in outs]
    out_specs = [pl.BlockSpec((1, tm, _out_width(o)), lambda b, i: (b, i, 0)) for o in outs]
    if emit_h:
        out_shape.append(jax.ShapeDtypeStruct((B, L, D), F32))
        out_specs.append(pl.BlockSpec((1, tm, D), lambda b, i: (b, i, 0)))
    return pl.pallas_call(
        functools.partial(_norm_mod_proj_kernel, outs=outs, emit_h=emit_h, has_gains=gains is not None,
                          rope=rope is not None),
        grid=(B, L // tm),
        in_specs=in_specs,
        out_specs=out_specs,
        out_shape=out_shape,
        compiler_params=_params("parallel", "parallel"),
        name="norm_mod_proj",
    )(*args)


_CONV_PAD = 16


def _conv_kernel(za_ref, w_ref, b_ref, lg_ref, lb_ref, o_ref, upad_ref, *, L, tc):
    c = pl.program_id(1)

    @pl.when(c == 0)
    def _():
        val = za_ref[0, :, :CONV_CH]
        gate = za_ref[0, :, CONV_CH:]
        zeros = jnp.zeros((_CONV_PAD, CONV_CH), F32)
        upad_ref[pl.ds(0, _CONV_PAD), :] = zeros
        upad_ref[pl.ds(_CONV_PAD + L, _CONV_PAD), :] = zeros
        upad_ref[pl.ds(_CONV_PAD, L), :] = val * jax.nn.sigmoid(gate)

    start = pl.multiple_of(c * tc, SUBLANES)
    win = upad_ref[pl.ds(start, tc + 2 * _CONV_PAD), :]
    acc = jnp.zeros((tc, CONV_CH), F32)
    first = _CONV_PAD - CONV_WIDTH // 2
    rows = tc + 2 * _CONV_PAD
    shifted = [win if c == 0 else pltpu.roll(win, rows - c, axis=0) for c in range(SUBLANES)]
    for j in range(CONV_WIDTH):
        a, c = divmod(first + j, SUBLANES)
        acc = acc + shifted[c][a * SUBLANES:a * SUBLANES + tc, :] * w_ref[j:j + 1, :]
    y = acc + b_ref[...]
    mu = jnp.mean(y, axis=-1, keepdims=True)
    d = y - mu
    var = jnp.mean(d * d, axis=-1, keepdims=True)
    yn = d * lax.rsqrt(var + EPS) * lg_ref[...] + lb_ref[...]
    o_ref[0] = (yn * jax.nn.sigmoid(yn)).astype(o_ref.dtype)


def conformer_conv(za, w_dw, b_dw, ln_g, ln_b):
    B, L, _ = za.shape
    tc = min(256, L)
    vec = lambda v: v.reshape(1, CONV_CH)
    return pl.pallas_call(
        functools.partial(_conv_kernel, L=L, tc=tc),
        grid=(B, L // tc),
        in_specs=[pl.BlockSpec((1, L, 2 * CONV_CH), lambda b, c: (b, 0, 0)),
                  pl.BlockSpec((CONV_WIDTH, CONV_CH), lambda b, c: (0, 0)),
                  pl.BlockSpec((1, CONV_CH), lambda b, c: (0, 0)),
                  pl.BlockSpec((1, CONV_CH), lambda b, c: (0, 0)),
                  pl.BlockSpec((1, CONV_CH), lambda b, c: (0, 0))],
        out_specs=pl.BlockSpec((1, tc, CONV_CH), lambda b, c: (b, c, 0)),
        out_shape=jax.ShapeDtypeStruct((B, L, CONV_CH), BF16),
        scratch_shapes=[pltpu.VMEM((L + 2 * _CONV_PAD, CONV_CH), F32)],
        compiler_params=_params("parallel", "arbitrary"),
        name="conformer_conv",
    )(za, w_dw, vec(b_dw), vec(ln_g), vec(ln_b))


_NA_ROWS_PER_STEP = 4


def _na_row_offset(r, rows):
    return r - jnp.clip(r - NA_WIN_H // 2, 0, rows - NA_WIN_H)


def _head_pair_rows(qg, lower):
    zero = jnp.zeros_like(qg)
    return jnp.concatenate([jnp.where(lower, qg, zero), jnp.where(lower, zero, qg)], axis=0)


def _na_kernel(q_ref, k_ref, v_ref, kc_ref, vc_ref, bias_ref, o_ref, *, rows):
    nt = (((1,), (1,)), ((), ()))
    lower = lax.broadcasted_iota(jnp.int32, (GRID_W, LANES), 1) < HEAD_DIM
    cols = [slice(g * LANES, (g + 1) * LANES) for g in range(NA_HEADS // 2)]
    jobs = []
    for rr in range(_NA_ROWS_PER_STEP):
        r = pl.program_id(1) * _NA_ROWS_PER_STEP + rr
        off = _na_row_offset(r, rows)
        band = pl.ds(pl.multiple_of((r - off) * GRID_W, GRID_W), NA_WIN_H * GRID_W)
        jobs += [(slice(rr * GRID_W, (rr + 1) * GRID_W), off, band, g) for g in range(NA_HEADS // 2)]
    qs = [_head_pair_rows(q_ref[0, qr, cols[g]], lower) for qr, _, _, g in jobs]
    s_loc = [lax.dot_general(q, k_ref[0, band, cols[g]], nt, preferred_element_type=F32) + bias_ref[off, g]
             for q, (_, off, band, g) in zip(qs, jobs)]
    s_ctx = [lax.dot_general(q, kc_ref[0, :, cols[g]], nt, preferred_element_type=F32)
             for q, (_, _, _, g) in zip(qs, jobs)]
    p_loc, p_ctx, den = [], [], []
    for sl, sc in zip(s_loc, s_ctx):
        m = jnp.maximum(jnp.max(sl, axis=-1, keepdims=True), jnp.max(sc, axis=-1, keepdims=True))
        p_loc.append(jnp.exp(sl - m))
        p_ctx.append(jnp.exp(sc - m))
        den.append(jnp.sum(p_loc[-1], axis=-1, keepdims=True) + jnp.sum(p_ctx[-1], axis=-1, keepdims=True))
    outs = []
    for (_, _, band, g), pl_, pc, d in zip(jobs, p_loc, p_ctx, den):
        o = (jnp.dot(pl_.astype(BF16), v_ref[0, band, cols[g]], preferred_element_type=F32)
             + jnp.dot(pc.astype(BF16), vc_ref[0, :, cols[g]], preferred_element_type=F32)) / d
        outs.append(jnp.where(lower, o[:GRID_W], o[GRID_W:]).astype(o_ref.dtype))
    n = NA_HEADS // 2
    for rr in range(_NA_ROWS_PER_STEP):
        o_ref[0, rr * GRID_W:(rr + 1) * GRID_W, :] = jnp.concatenate(outs[rr * n:(rr + 1) * n], axis=-1)


def na_bias_table(rel_bias):
    cols = jnp.arange(GRID_W, dtype=jnp.int32)
    c0 = jnp.clip(cols - NA_WIN_W // 2, 0, GRID_W - NA_WIN_W)
    in_win = (cols[None, :] >= c0[:, None]) & (cols[None, :] < c0[:, None] + NA_WIN_W)
    col_idx = jnp.clip(cols[None, :] - cols[:, None] + (NA_WIN_W - 1), 0, 2 * NA_WIN_W - 2)
    off = jnp.arange(NA_WIN_H, dtype=jnp.int32)
    row_idx = off[None, :] - off[:, None] + (NA_WIN_H - 1)
    t = rel_bias[:, row_idx]
    t = t[..., col_idx]
    t = jnp.where(in_win[None, None, None], t, MASK_VALUE)
    return t.transpose(1, 0, 3, 2, 4).reshape(NA_WIN_H, NA_HEADS // 2, 2 * GRID_W, NA_WIN_H * GRID_W)


def neighbourhood_attention(q, k, v, kc, vc, bias):
    B, T, W = q.shape
    C = kc.shape[1]
    rows = T // GRID_W
    rq = _NA_ROWS_PER_STEP * GRID_W
    assert rows >= NA_WIN_H and rows % _NA_ROWS_PER_STEP == 0 and W == NA_HEADS * HEAD_DIM
    return pl.pallas_call(
        functools.partial(_na_kernel, rows=rows),
        grid=(B, rows // _NA_ROWS_PER_STEP),
        in_specs=[pl.BlockSpec((1, rq, W), lambda b, r: (b, r, 0)),
                  pl.BlockSpec((1, T, W), lambda b, r: (b, 0, 0)),
                  pl.BlockSpec((1, T, W), lambda b, r: (b, 0, 0)),
                  pl.BlockSpec((1, C, W), lambda b, r: (b, 0, 0)),
                  pl.BlockSpec((1, C, W), lambda b, r: (b, 0, 0)),
                  pl.BlockSpec(bias.shape, lambda b, r: (0, 0, 0, 0))],
        out_specs=pl.BlockSpec((1, rq, W), lambda b, r: (b, r, 0)),
        out_shape=jax.ShapeDtypeStruct((B, T, W), BF16),
        compiler_params=_params("parallel", "arbitrary"),
        name="neighbourhood_attention",
    )(q, k, v, kc, vc, bias)


def _pair_attn_kernel(*refs, Gq, Gk, bq, n_sets):
    q_ref = refs[0]
    kv_refs = refs[1:1 + 2 * n_sets]
    o_ref = refs[1 + 2 * n_sets]
    nt = (((1,), (1,)), ((), ()))
    sets = range(n_sets)
    lower = lax.broadcasted_iota(jnp.int32, (bq, LANES), 1) < HEAD_DIM
    col = lambda g: slice(g * LANES, (g + 1) * LANES)
    pairs = [_head_pair_rows(q_ref[0, :, col(g)], lower) for g in range(Gq)]
    if Gk == 1:
        jobs = [(jnp.concatenate([p[:bq] for p in pairs], axis=0), col(0)),
                (jnp.concatenate([p[bq:] for p in pairs], axis=0), col(0))]
    else:
        jobs = [(pairs[g], col(g)) for g in range(Gq)]
    ss = [[lax.dot_general(rows, kv_refs[2 * i][0, :, kc], nt, preferred_element_type=F32) for i in sets]
          for rows, kc in jobs]
    ps, dens = [], []
    for s in ss:
        m = functools.reduce(jnp.maximum, [jnp.max(x, axis=-1, keepdims=True) for x in s])
        ps.append([jnp.exp(x - m) for x in s])
        dens.append(functools.reduce(jnp.add, [jnp.sum(p, axis=-1, keepdims=True) for p in ps[-1]]))
    os_ = []
    for (rows, kc), p, den in zip(jobs, ps, dens):
        o = functools.reduce(jnp.add, [jnp.dot(p[i].astype(BF16), kv_refs[2 * i + 1][0, :, kc],
                                               preferred_element_type=F32) for i in sets])
        os_.append(o / den)
    outs = []
    for g in range(Gq):
        if Gk == 1:
            first, second = os_[0][g * bq:(g + 1) * bq], os_[1][g * bq:(g + 1) * bq]
        else:
            first, second = os_[g][:bq], os_[g][bq:]
        outs.append(jnp.where(lower, first, second).astype(o_ref.dtype))
    o_ref[0] = jnp.concatenate(outs, axis=-1)


def pair_attention(q, kv_sets):
    B, L, Wq = q.shape
    Wk = kv_sets[0][0].shape[2]
    Gq, Gk = Wq // LANES, Wk // LANES
    assert Gk in (1, Gq)
    bq = min(256, L)
    in_specs = [pl.BlockSpec((1, bq, Wq), lambda b, i: (b, i, 0))]
    args = [q]
    for k, v in kv_sets:
        for t in (k, v):
            in_specs.append(pl.BlockSpec((1, t.shape[1], Wk), lambda b, i: (b, 0, 0)))
            args.append(t)
    return pl.pallas_call(
        functools.partial(_pair_attn_kernel, Gq=Gq, Gk=Gk, bq=bq, n_sets=len(kv_sets)),
        grid=(B, L // bq),
        in_specs=in_specs,
        out_specs=pl.BlockSpec((1, bq, Wq), lambda b, i: (b, i, 0)),
        out_shape=jax.ShapeDtypeStruct((B, L, Wq), BF16),
        compiler_params=_params("parallel", "arbitrary"),
        name="pair_attention",
    )(*args)


def _proj_residual_kernel(a_ref, b_ref, g_ref, wa_ref, wb_ref, wg_ref, x_ref, gate_ref, o_ref):
    y = (jnp.dot(a_ref[0], wa_ref[...], preferred_element_type=F32)
         + jnp.dot(b_ref[0], wb_ref[...], preferred_element_type=F32)
         + jnp.dot(g_ref[0], wg_ref[...], preferred_element_type=F32))
    o_ref[0] = x_ref[0] + gate_ref[0] * y


def proj_residual(parts, w, x, gate):
    B, L, D = x.shape
    tm = min(512, L)
    gate_map = (lambda b, i: (b, 0, 0)) if gate.shape[0] == B else (lambda b, i: (0, 0, 0))
    widths = [p.shape[2] for p in parts]
    assert sum(widths) == w.shape[0]
    starts = [sum(widths[:i]) for i in range(len(parts))]
    ws = [w[s:s + k] for s, k in zip(starts, widths)]
    return pl.pallas_call(
        _proj_residual_kernel,
        grid=(B, L // tm),
        in_specs=[pl.BlockSpec((1, tm, k), lambda b, i: (b, i, 0)) for k in widths]
        + [pl.BlockSpec((k, D), lambda b, i: (0, 0)) for k in widths]
        + [pl.BlockSpec((1, tm, D), lambda b, i: (b, i, 0)), pl.BlockSpec((1, 1, D), gate_map)],
        out_specs=pl.BlockSpec((1, tm, D), lambda b, i: (b, i, 0)),
        out_shape=jax.ShapeDtypeStruct((B, L, D), F32),
        compiler_params=_params("parallel", "parallel"),
        name="proj_residual",
    )(*parts, *ws, x, gate)


def _top_rows(s, k, order=None, payload=None):
    if order is None:
        return _top_rows_by_row(s, k)
    n = s.shape[0]
    cut = lambda x: [x[i:i + SUBLANES] for i in range(0, n, SUBLANES)]
    pieces, orders, payloads = cut(s), cut(order), cut(payload)
    vals, picks = [], []
    for _ in range(k):
        level = list(zip(pieces, orders, payloads))
        while len(level) > 1:
            nxt = [(jnp.maximum(va, vb), jnp.where(va >= vb, oa, ob), jnp.where(va >= vb, pa, pb))
                   for (va, oa, pa), (vb, ob, pb) in zip(level[0::2], level[1::2])]
            if len(level) % 2:
                nxt.append(level[-1])
            level = nxt
        v, o, p = level[0]
        m = jnp.max(v, axis=0, keepdims=True)
        first = jnp.min(jnp.where(v == m, o, jnp.inf), axis=0, keepdims=True)
        vals.append(m)
        picks.append(jnp.max(jnp.where(o == first, p, -1.0), axis=0, keepdims=True))
        pieces = [jnp.where(oi == first, -jnp.inf, pi) for pi, oi in zip(pieces, orders)]
    return jnp.concatenate(vals, axis=0), jnp.concatenate(picks, axis=0)


def _top_rows_by_row(s, k):
    n, t = s.shape
    sub = lax.broadcasted_iota(jnp.int32, (SUBLANES, t), 0).astype(F32)
    pieces = [s[i:i + SUBLANES] for i in range(0, n, SUBLANES)]
    rows = [sub + float(i) for i in range(0, n, SUBLANES)]
    vals, picks = [], []
    for _ in range(k):
        level = list(zip(pieces, rows))
        while len(level) > 1:
            nxt = []
            for (va, ia), (vb, ib) in zip(level[0::2], level[1::2]):
                nxt.append((jnp.maximum(va, vb), jnp.where(va >= vb, ia, ib)))
            level = nxt
        v, i = level[0]
        m = jnp.max(v, axis=0, keepdims=True)
        first = jnp.min(jnp.where(v == m, i, jnp.inf), axis=0, keepdims=True)
        vals.append(m)
        picks.append(first)
        pieces = [jnp.where(r == first, -jnp.inf, p) for p, r in zip(pieces, rows)]
    return jnp.concatenate(vals, axis=0), jnp.concatenate(picks, axis=0)


def _pair_candidates(s0, i0, s1, i1):
    K = PEER_TOPK
    t = s0.shape[1]
    sub = lax.broadcasted_iota(jnp.int32, (SUBLANES, t), 0)
    scores, order, ids = [], [], []

    def add(a_rows, b_rows, a_of_row, b_of_row):
        sa, ia = a_rows
        sb, ib = b_rows
        ok = (a_of_row + 1) * (b_of_row + 1) <= K
        scores.append(jnp.where(ok, sa + sb, -jnp.inf))
        order.append((a_of_row * K + b_of_row).astype(F32))
        ids.append(ia * float(PEER_N_KEYS) + ib)

    row = lambda x, r: (x[0][r:r + 1], x[1][r:r + 1])
    rows = lambda x, r: (x[0][r:r + SUBLANES], x[1][r:r + SUBLANES])
    A, Bv = (s0, i0), (s1, i1)
    add(row(A, 0), rows(Bv, 0), jnp.zeros_like(sub), sub)
    add(row(A, 0), rows(Bv, SUBLANES), jnp.zeros_like(sub), sub + SUBLANES)
    for a in range(1, 4):
        add(row(A, a), rows(Bv, 0), jnp.full_like(sub, a), sub)
    for b in range(3):
        dup = sub < 4
        sa, ia = rows(A, 0)
        add((jnp.where(dup, -jnp.inf, sa), ia), row(Bv, b), sub, jnp.full_like(sub, b))
    add(rows(A, SUBLANES), row(Bv, 0), sub + SUBLANES, jnp.zeros_like(sub))
    cat = lambda xs: jnp.concatenate(xs, axis=0)
    return cat(scores), cat(order), cat(ids)


def _peer_topk_kernel(q_ref, keys_ref, offs_ref, g_ref):
    K = PEER_TOPK
    assert K == 2 * SUBLANES
    nt = (((1,), (1,)), ((), ()))
    words, gates = [], []
    for h in range(PEER_HEADS):
        tops = []
        for p in range(2):
            col = (2 * h + p) * PEER_D_KEY
            qhp = q_ref[:, col:col + PEER_D_KEY]
            s = lax.dot_general(keys_ref[h, p], qhp, nt, preferred_element_type=F32)
            tops.append(_top_rows(s, K))
        (s0, i0), (s1, i1) = tops
        cand_s, cand_order, cand_i = _pair_candidates(s0, i0, s1, i1)
        best_s, best_i = _top_rows(cand_s, K, order=cand_order, payload=cand_i)
        e = jnp.exp(best_s - jnp.max(best_s, axis=0, keepdims=True))
        gates.append(e / jnp.sum(e, axis=0, keepdims=True))
        off = best_i.astype(jnp.int32) * _HALF
        lo = jnp.concatenate([off[0:_HALF], off[SUBLANES:SUBLANES + _HALF]], axis=0)
        hi = jnp.concatenate([off[_HALF:SUBLANES], off[SUBLANES + _HALF:K]], axis=0)
        words.append(lo | (hi << 16))
    g_ref[...] = jnp.concatenate(gates, axis=0).T
    words.append(jnp.zeros((PEER_PAIRS // 2, q_ref.shape[0]), jnp.int32))
    offs_ref[...] = jnp.concatenate(words, axis=0).T[:, :PEER_PAIRS // 2]


def peer_topk(q, keys):
    N = q.shape[0]
    tm = LANES
    return pl.pallas_call(
        _peer_topk_kernel,
        grid=(N // tm,),
        in_specs=[pl.BlockSpec((tm, q.shape[1]), lambda i: (i, 0)),
                  pl.BlockSpec(keys.shape, lambda i: (0, 0, 0, 0))],
        out_specs=[pl.BlockSpec((tm, PEER_PAIRS // 2), lambda i: (i, 0)),
                   pl.BlockSpec((tm, PEER_PAIRS), lambda i: (i, 0))],
        out_shape=[jax.ShapeDtypeStruct((N, PEER_PAIRS // 2), jnp.int32),
                   jax.ShapeDtypeStruct((N, PEER_PAIRS), F32)],
        compiler_params=_params("parallel"),
        name="peer_topk",
    )(q, keys)


_PEER_TOKENS = 128
_ACT_TOKENS_PER_TRIP = 4
_GROUPS = PEER_PAIRS // SUBLANES


def _pack_table_kernel(w_ref, o_ref):
    te = w_ref.shape[0]
    bits = lambda x: pltpu.bitcast(x.astype(BF16).astype(F32), jnp.uint32)
    for i in range(_HALF):
        lo = bits(w_ref[:, (2 * i) * LANES:(2 * i + 1) * LANES]) >> 16
        hi = bits(w_ref[:, (2 * i + 1) * LANES:(2 * i + 2) * LANES])
        o_ref[pl.ds(i, te, stride=_HALF), :] = lo | hi


def pack_expert_table(w, layer):
    _, E, D = w.shape
    assert D == SUBLANES * LANES
    te = 256
    return pl.pallas_call(
        _pack_table_kernel,
        grid=(E // te,),
        in_specs=[pl.BlockSpec((None, te, D), lambda i: (layer, i, 0))],
        out_specs=pl.BlockSpec((te * _HALF, LANES), lambda i: (i, 0)),
        out_shape=jax.ShapeDtypeStruct((E * _HALF, LANES), jnp.uint32),
        compiler_params=_params("parallel"),
        name="pack_expert_table",
    )(w)


def _pair_position(word, half):
    return SUBLANES * (word // _HALF) + _HALF * half + word % _HALF


def _expert_rows(tab_ref, word):
    starts = (word & 0xFFFF, lax.shift_right_logical(word, 16))
    return [pltpu.bitcast(tab_ref[pl.ds(pl.multiple_of(s, _HALF), _HALF), :], BF16).astype(F32) for s in starts]


def _token_row(ref, t):
    row = ref[pl.ds(t, 1), :]
    return jnp.concatenate([row[:, s * LANES:(s + 1) * LANES] for s in range(SUBLANES)], axis=0)


def _merge_pair(a, b, shift, first):
    if shift == _HALF:
        return jnp.where(first, a, b) + pltpu.roll(jnp.where(first, b, a), shift, axis=0)
    bs = pltpu.roll(b, shift, axis=0)
    return jnp.where(first, a, bs) + pltpu.roll(jnp.where(first, bs, a), SUBLANES - shift, axis=0)


_MERGE_ORDER = (0, 4, 2, 6, 1, 5, 3, 7)


def _merge8(ps):
    sub = lax.broadcasted_iota(jnp.int32, (SUBLANES, LANES), 0)
    shift = _HALF
    while len(ps) > 1:
        first = (sub % (2 * shift)) < shift
        ps = [_merge_pair(ps[2 * i], ps[2 * i + 1], shift, first) for i in range(len(ps) // 2)]
        shift //= 2
    return ps[0]


def _peer_act_kernel(idx_ref, h_ref, g_ref, tab_ref, o_ref, part_ref):
    tn = h_ref.shape[0]

    def products(t, slot):
        x = _token_row(h_ref, t)
        for gi in range(_GROUPS):
            rows = []
            for w in range(_HALF):
                rows += _expert_rows(tab_ref, idx_ref[t, gi * _HALF + w])
            prods = [rows[2 * (j % _HALF) + j // _HALF] * x for j in _MERGE_ORDER]
            part_ref[slot, pl.ds(gi * SUBLANES, SUBLANES), :] = _merge8(prods)

    def reduce(t, slot):
        o_ref[pl.ds(t, 1), :] = jnp.sum(part_ref[slot].T, axis=0, keepdims=True)

    part_ref[...] = jnp.zeros(part_ref.shape, F32)
    slots = range(_ACT_TOKENS_PER_TRIP)

    def trip(i, carry):
        t = _ACT_TOKENS_PER_TRIP * i
        for sl in slots:
            reduce(jnp.maximum(t - _ACT_TOKENS_PER_TRIP + sl, 0), sl)
        for sl in slots:
            products(t + sl, sl)
        return carry

    lax.fori_loop(0, tn // _ACT_TOKENS_PER_TRIP, trip, 0)
    for sl in slots:
        reduce(tn - _ACT_TOKENS_PER_TRIP + sl, sl)
    o_ref[...] = g_ref[...] * jax.nn.gelu(o_ref[...], approximate=True)


def peer_act(offs, h, g, table):
    N = offs.shape[0]
    tn = _PEER_TOKENS
    return pl.pallas_call(
        _peer_act_kernel,
        grid=(N // tn,),
        in_specs=[pl.BlockSpec((tn, PEER_PAIRS // 2), lambda i: (i, 0), memory_space=pltpu.SMEM),
                  pl.BlockSpec((tn, SUBLANES * LANES), lambda i: (i, 0)),
                  pl.BlockSpec((tn, PEER_PAIRS), lambda i: (i, 0)),
                  pl.BlockSpec(memory_space=pltpu.VMEM)],
        out_specs=pl.BlockSpec((tn, PEER_PAIRS), lambda i: (i, 0)),
        out_shape=jax.ShapeDtypeStruct((N, PEER_PAIRS), F32),
        scratch_shapes=[pltpu.VMEM((_ACT_TOKENS_PER_TRIP, PEER_PAIRS, LANES), F32)],
        compiler_params=_params("arbitrary"),
        name="peer_act",
    )(offs, h, g, table)


_MIX_CHAINS = 4


def _peer_mix_kernel(idx_ref, coef_ref, tab_ref, x_ref, gate_ref, o_ref, cb_ref):
    tn = x_ref.shape[0]

    def spread(t):
        row = coef_ref[pl.ds(t, 1), :]
        return jnp.broadcast_to(row, (PEER_PAIRS, PEER_PAIRS)).T

    cb_ref[...] = spread(0)

    def token(t, carry):
        accs = [None] * _MIX_CHAINS
        for w in range(PEER_PAIRS // 2):
            for k, row in enumerate(_expert_rows(tab_ref, idx_ref[t, w])):
                j = _pair_position(w, k)
                term = jnp.broadcast_to(cb_ref[pl.ds(j, 1), :], (SUBLANES, LANES)) * row
                a = j % _MIX_CHAINS
                accs[a] = term if accs[a] is None else accs[a] + term
        y = (accs[0] + accs[1]) + (accs[2] + accs[3])
        y_row = jnp.concatenate([y[s:s + 1] for s in range(SUBLANES)], axis=1)
        o_ref[pl.ds(t, 1), :] = x_ref[pl.ds(t, 1), :] + gate_ref[0] * y_row
        cb_ref[...] = spread(jnp.minimum(t + 1, tn - 1))
        return carry

    lax.fori_loop(0, tn, token, 0)


def peer_mix(offs, coef, table, x, gate, tokens_per_batch):
    N, D = x.shape
    tn = _PEER_TOKENS
    assert tokens_per_batch % tn == 0
    if gate.shape[0] == 1:
        gate_map = lambda i: (0, 0, 0)
    else:
        gate_map = lambda i: ((i * tn) // tokens_per_batch, 0, 0)
    return pl.pallas_call(
        _peer_mix_kernel,
        grid=(N // tn,),
        in_specs=[pl.BlockSpec((tn, PEER_PAIRS // 2), lambda i: (i, 0), memory_space=pltpu.SMEM),
                  pl.BlockSpec((tn, PEER_PAIRS), lambda i: (i, 0)),
                  pl.BlockSpec(memory_space=pltpu.VMEM),
                  pl.BlockSpec((tn, D), lambda i: (i, 0)),
                  pl.BlockSpec((1, 1, D), gate_map)],
        out_specs=pl.BlockSpec((tn, D), lambda i: (i, 0)),
        out_shape=jax.ShapeDtypeStruct((N, D), F32),
        scratch_shapes=[pltpu.VMEM((PEER_PAIRS, LANES), F32)],
        compiler_params=_params("arbitrary"),
        name="peer_mix",
    )(offs, coef, table, x, gate)


def _rmsnorm_kernel(x_ref, g_ref, o_ref):
    x = x_ref[0]
    o_ref[0] = x * lax.rsqrt(jnp.mean(x * x, axis=-1, keepdims=True) + EPS) * g_ref[...]


def rmsnorm(x, g):
    B, L, D = x.shape
    tm = min(512, L)
    return pl.pallas_call(
        _rmsnorm_kernel,
        grid=(B, L // tm),
        in_specs=[pl.BlockSpec((1, tm, D), lambda b, i: (b, i, 0)),
                  pl.BlockSpec((1, D), lambda b, i: (0, 0))],
        out_specs=pl.BlockSpec((1, tm, D), lambda b, i: (b, i, 0)),
        out_shape=jax.ShapeDtypeStruct((B, L, D), F32),
        compiler_params=_params("parallel", "parallel"),
        name="final_rmsnorm",
    )(x, g.reshape(1, D))


def _rope_tables(T):
    t = jnp.arange(T, dtype=jnp.int32)
    row = (t // GRID_W).astype(F32)
    col = (t % GRID_W).astype(F32)
    n_freq = HEAD_DIM // 4
    inv_freq = ROPE_THETA ** (-jnp.arange(n_freq, dtype=F32) / n_freq)
    ang = jnp.concatenate([row[:, None] * inv_freq, col[:, None] * inv_freq], axis=-1)
    cos, sin = jnp.cos(ang), jnp.sin(ang)
    return jnp.tile(cos, (1, 4)), jnp.tile(jnp.concatenate([-sin, sin], axis=-1), (1, 2))


def _peer_ffn_residual(x, g2, sc2, sh2, gate, w_q, keys, u_tab, v_tab):
    B, L, D = x.shape
    N = B * L
    q, h = norm_mod_proj(x, g2, sc2, sh2, w_q, (("bf16", w_q.shape[1], 1.0),), emit_h=True)
    offs, gates = peer_topk(q.reshape(N, -1), keys)
    coef = peer_act(offs, h.reshape(N, D), gates, u_tab)
    out = peer_mix(offs, coef, v_tab, x.reshape(N, D), gate, L)
    return out.reshape(B, L, D)


_Q_SCALE = HEAD_DIM ** -0.5
_NA_W = NA_HEADS * HEAD_DIM
_GQA_W = GQA_HEADS * HEAD_DIM
_GQA_KV_W = GQA_KV_HEADS * HEAD_DIM
_IN_OUTS = (("f32", 2 * CONV_CH), ("bf16", _NA_W, _Q_SCALE), ("norm", _GQA_W // LANES, _Q_SCALE, 0),
            ("bf16", _NA_W, 1.0), ("bf16", _NA_W, 1.0), ("norm", _GQA_KV_W // LANES, 1.0, 1),
            ("bf16", _GQA_KV_W, 1.0))
_OFF_G_Q = 2 * CONV_CH + _NA_W
_OFF_G_OUT = CONV_CH + _NA_W
_GQA_PAIR_ORDER = tuple(h for g in range(GQA_HEADS // GQA_KV_HEADS) for h in (g, GQA_HEADS // GQA_KV_HEADS + g))
assert GQA_KV_HEADS == 2 and 2 * HEAD_DIM == LANES


def _reorder_heads(w, start, axis):
    idx = jnp.concatenate([jnp.arange(HEAD_DIM) + start + h * HEAD_DIM for h in _GQA_PAIR_ORDER])
    full = jnp.arange(w.shape[axis]).at[start:start + _GQA_W].set(idx)
    return jnp.take(w, full, axis=axis)


def kernel(x, c, ctx, c_ctx, norm1_g, norm2_g, w_ada, b_ada, w_in, conv_w, conv_b, conv_ln_g, conv_ln_b,
           na_rel_bias, gqa_q_norm, gqa_k_norm, w_out, peer_w_q, peer_keys, peer_u, peer_v, final_norm_g):
    B, T, D = x.shape
    depth = w_in.shape[0]
    rope = _rope_tables(T)
    ada_rows = -(-(B + 1) // SUBLANES) * SUBLANES
    ada_in = jnp.zeros((ada_rows, D), F32).at[:B].set(c).at[B].set(c_ctx)

    for l in range(depth):
        last = l == depth - 1
        mod = ada_mod(ada_in, w_ada[l], b_ada[l])
        sh1, sc1, g1, sh2, sc2, g2 = [m.reshape(B, 1, D) for m in jnp.split(mod[:B], 6, axis=-1)]
        csh1, csc1, cg1, csh2, csc2, cg2 = [m.reshape(1, 1, D) for m in jnp.split(mod[B:B + 1], 6, axis=-1)]
        w_in_b = _reorder_heads(w_in[l], _OFF_G_Q, 1).astype(BF16)
        w_out_b = _reorder_heads(w_out[l], _OFF_G_OUT, 0).astype(BF16)
        w_q_b = peer_w_q[l].astype(BF16)
        keys_b = peer_keys[l].astype(BF16)
        u_tab = pack_expert_table(peer_u, l)
        v_tab = pack_expert_table(peer_v, l)
        qk_gains = jnp.stack([jnp.tile(gqa_q_norm[l], 2), jnp.tile(gqa_k_norm[l], 2)])

        za, na_q, g_q, na_k, na_v, g_k, g_v = norm_mod_proj(x, norm1_g[l], sc1, sh1, w_in_b, _IN_OUTS,
                                                            gains=qk_gains, rope=rope)
        zca, cna_q, cg_q, cna_k, cna_v, cg_k, cg_v = norm_mod_proj(ctx, norm1_g[l], csc1, csh1, w_in_b, _IN_OUTS,
                                                                   gains=qk_gains)
        a = conformer_conv(za, conv_w[l], conv_b[l], conv_ln_g[l], conv_ln_b[l])
        bm = neighbourhood_attention(na_q, na_k, na_v, cna_k, cna_v, na_bias_table(na_rel_bias[l]))
        gm = pair_attention(g_q, [(g_k, g_v), (cg_k, cg_v)])
        x = proj_residual((a, bm, gm), w_out_b, x, g1)

        x = _peer_ffn_residual(x, norm2_g[l], sc2, sh2, g2, w_q_b, keys_b, u_tab, v_tab)

        if not last:
            ac = conformer_conv(zca, conv_w[l], conv_b[l], conv_ln_g[l], conv_ln_b[l])
            bc = pair_attention(cna_q, [(cna_k, cna_v)])
            gc = pair_attention(cg_q, [(cg_k, cg_v)])
            ctx = proj_residual((ac, bc, gc), w_out_b, ctx, cg1)
            ctx = _peer_ffn_residual(ctx, norm2_g[l], csc2, csh2, cg2, w_q_b, keys_b, u_tab, v_tab)
    return rmsnorm(x, final_norm_g)
```

```python
import functools

import jax
import jax.numpy as jnp
from jax import lax
from jax.experimental import pallas as pl
from jax.experimental.pallas import tpu as pltpu

F32 = jnp.float32
BF16 = jnp.bfloat16

GRID_W = 64
HEAD_DIM = 64
CONV_CH = 256
CONV_WIDTH = 31
NA_HEADS = 6
NA_WIN_H = 8
NA_WIN_W = 16
GQA_HEADS = 6
GQA_KV_HEADS = 2
ROPE_THETA = 10000.0
PEER_HEADS = 8
PEER_N_KEYS = 128
PEER_D_KEY = 128
PEER_TOPK = 16
EPS = 1e-6

PEER_PAIRS = PEER_HEADS * PEER_TOPK

LANES = 128
SUBLANES = 8
_HALF = SUBLANES // 2
assert PEER_N_KEYS * PEER_N_KEYS * _HALF <= 1 << 16
VMEM_LIMIT = 48 * 1024 * 1024
MASK_VALUE = -1e30


def _params(*sem):
    return pltpu.CompilerParams(dimension_semantics=sem, vmem_limit_bytes=VMEM_LIMIT)


def _ada_kernel(a_ref, w_ref, b_ref, o_ref):
    a = a_ref[...]
    s = a * jax.nn.sigmoid(a)
    o_ref[...] = jnp.dot(s.astype(BF16), w_ref[...].astype(BF16), preferred_element_type=F32) + b_ref[...]


def ada_mod(a, w, b):
    R, D = a.shape
    N = w.shape[1]
    tn = 1024
    return pl.pallas_call(
        _ada_kernel,
        grid=(N // tn,),
        in_specs=[pl.BlockSpec((R, D), lambda j: (0, 0)),
                  pl.BlockSpec((D, tn), lambda j: (0, j)),
                  pl.BlockSpec((1, tn), lambda j: (0, j))],
        out_specs=pl.BlockSpec((R, tn), lambda j: (0, j)),
        out_shape=jax.ShapeDtypeStruct((R, N), F32),
        compiler_params=_params("parallel"),
        name="ada_mod",
    )(a, w, b.reshape(1, N))


def _pair_norm_rope(x, gain, rope_cos, rope_sin):
    lane = lax.broadcasted_iota(jnp.int32, x.shape, 1)
    lower = lane < HEAD_DIM
    sq = x * x
    ms_lo = jnp.sum(jnp.where(lower, sq, 0.0), axis=-1, keepdims=True) * (1.0 / HEAD_DIM)
    ms_hi = jnp.sum(jnp.where(lower, 0.0, sq), axis=-1, keepdims=True) * (1.0 / HEAD_DIM)
    y = x * jnp.where(lower, lax.rsqrt(ms_lo + EPS), lax.rsqrt(ms_hi + EPS)) * gain
    if rope_cos is None:
        return y
    half = HEAD_DIM // 2
    first = (lane % HEAD_DIM) < half
    partner = jnp.where(first, pltpu.roll(y, LANES - half, axis=1), pltpu.roll(y, half, axis=1))
    return y * rope_cos + partner * rope_sin


def _norm_mod_proj_kernel(*refs, outs, emit_h, has_gains, rope):
    x_ref, g_ref, sc_ref, sh_ref, w_ref = refs[:5]
    n_in = 5
    gains_ref = cos = sin = None
    if has_gains:
        gains_ref = refs[n_in]
        n_in += 1
    if rope:
        cos, sin = refs[n_in][...], refs[n_in + 1][...]
        n_in += 2
    out_refs = refs[n_in:]
    x = x_ref[0]
    y = x * lax.rsqrt(jnp.mean(x * x, axis=-1, keepdims=True) + EPS) * g_ref[...]
    h = y * (1.0 + sc_ref[0]) + sh_ref[0]
    z = jnp.dot(h.astype(BF16), w_ref[...], preferred_element_type=F32)
    off = 0
    for o_ref, spec in zip(out_refs, outs):
        if spec[0] == "f32":
            o_ref[0] = z[:, off:off + spec[1]]
            off += spec[1]
        elif spec[0] == "bf16":
            o_ref[0] = (z[:, off:off + spec[1]] * spec[2]).astype(BF16)
            off += spec[1]
        else:
            groups = []
            for _ in range(spec[1]):
                v = _pair_norm_rope(z[:, off:off + LANES], gains_ref[spec[3]:spec[3] + 1, :], cos, sin)
                groups.append((v * spec[2]).astype(BF16))
                off += LANES
            o_ref[0] = groups[0] if len(groups) == 1 else jnp.concatenate(groups, axis=-1)
    if emit_h:
        out_refs[len(outs)][0] = h


def _out_width(spec):
    return spec[1] * LANES if spec[0] == "norm" else spec[1]


def norm_mod_proj(x, g, sc, sh, w, outs, emit_h=False, gains=None, rope=None):
    B, L, D = x.shape
    N = w.shape[1]
    assert sum(_out_width(o) for o in outs) == N
    tm = min(512, L)
    per_batch = sc.shape[0] == B
    mod_map = (lambda b, i: (b, 0, 0)) if per_batch else (lambda b, i: (0, 0, 0))
    in_specs = [pl.BlockSpec((1, tm, D), lambda b, i: (b, i, 0)),
                pl.BlockSpec((1, D), lambda b, i: (0, 0)),
                pl.BlockSpec((1, 1, D), mod_map),
                pl.BlockSpec((1, 1, D), mod_map),
                pl.BlockSpec((D, N), lambda b, i: (0, 0))]
    args = [x, g.reshape(1, D), sc, sh, w]
    if gains is not None:
        in_specs.append(pl.BlockSpec(gains.shape, lambda b, i: (0, 0)))
        args.append(gains)
    if rope is not None:
        for tab in rope:
            in_specs.append(pl.BlockSpec((tm, LANES), lambda b, i: (i, 0)))
            args.append(tab)
    out_shape = [jax.ShapeDtypeStruct((B, L, _out_width(o)), F32 if o[0] == "f32" else BF16) for o in outs]
    out_specs = [pl.BlockSpec((1, tm, _out_width(o)), lambda b, i: (b, i, 0)) for o in outs]
    if emit_h:
        out_shape.append(jax.ShapeDtypeStruct((B, L, D), F32))
        out_specs.append(pl.BlockSpec((1, tm, D), lambda b, i: (b, i, 0)))
    return pl.pallas_call(
        functools.partial(_norm_mod_proj_kernel, outs=outs, emit_h=emit_h, has_gains=gains is not None,
                          rope=rope is not None),
        grid=(B, L // tm),
        in_specs=in_specs,
        out_specs=out_specs,
        out_shape=out_shape,
        compiler_params=_params("parallel", "parallel"),
        name="norm_mod_proj",
    )(*args)


_CONV_PAD = 16


def _conv_kernel(za_ref, w_ref, b_ref, lg_ref, lb_ref, o_ref, upad_ref, *, L, tc):
    c = pl.program_id(1)

    @pl.when(c == 0)
    def _():
        val = za_ref[0, :, :CONV_CH]
        gate = za_ref[0, :, CONV_CH:]
        zeros = jnp.zeros((_CONV_PAD, CONV_CH), F32)
        upad_ref[pl.ds(0, _CONV_PAD), :] = zeros
        upad_ref[pl.ds(_CONV_PAD + L, _CONV_PAD), :] = zeros
        upad_ref[pl.ds(_CONV_PAD, L), :] = val * jax.nn.sigmoid(gate)

    start = pl.multiple_of(c * tc, SUBLANES)
    win = upad_ref[pl.ds(start, tc + 2 * _CONV_PAD), :]
    acc = jnp.zeros((tc, CONV_CH), F32)
    first = _CONV_PAD - CONV_WIDTH // 2
    rows = tc + 2 * _CONV_PAD
    shifted = [win if c == 0 else pltpu.roll(win, rows - c, axis=0) for c in range(SUBLANES)]
    for j in range(CONV_WIDTH):
        a, c = divmod(first + j, SUBLANES)
        acc = acc + shifted[c][a * SUBLANES:a * SUBLANES + tc, :] * w_ref[j:j + 1, :]
    y = acc + b_ref[...]
    mu = jnp.mean(y, axis=-1, keepdims=True)
    d = y - mu
    var = jnp.mean(d * d, axis=-1, keepdims=True)
    yn = d * lax.rsqrt(var + EPS) * lg_ref[...] + lb_ref[...]
    o_ref[0] = (yn * jax.nn.sigmoid(yn)).astype(o_ref.dtype)


def conformer_conv(za, w_dw, b_dw, ln_g, ln_b):
    B, L, _ = za.shape
    tc = min(256, L)
    vec = lambda v: v.reshape(1, CONV_CH)
    return pl.pallas_call(
        functools.partial(_conv_kernel, L=L, tc=tc),
        grid=(B, L // tc),
        in_specs=[pl.BlockSpec((1, L, 2 * CONV_CH), lambda b, c: (b, 0, 0)),
                  pl.BlockSpec((CONV_WIDTH, CONV_CH), lambda b, c: (0, 0)),
                  pl.BlockSpec((1, CONV_CH), lambda b, c: (0, 0)),
                  pl.BlockSpec((1, CONV_CH), lambda b, c: (0, 0)),
                  pl.BlockSpec((1, CONV_CH), lambda b, c: (0, 0))],
        out_specs=pl.BlockSpec((1, tc, CONV_CH), lambda b, c: (b, c, 0)),
        out_shape=jax.ShapeDtypeStruct((B, L, CONV_CH), BF16),
        scratch_shapes=[pltpu.VMEM((L + 2 * _CONV_PAD, CONV_CH), F32)],
        compiler_params=_params("parallel", "arbitrary"),
        name="conformer_conv",
    )(za, w_dw, vec(b_dw), vec(ln_g), vec(ln_b))


_NA_ROWS_PER_STEP = 4


def _na_row_offset(r, rows):
    return r - jnp.clip(r - NA_WIN_H // 2, 0, rows - NA_WIN_H)


def _head_pair_rows(qg, lower):
    zero = jnp.zeros_like(qg)
    return jnp.concatenate([jnp.where(lower, qg, zero), jnp.where(lower, zero, qg)], axis=0)


def _na_kernel(q_ref, k_ref, v_ref, kc_ref, vc_ref, bias_ref, o_ref, *, rows):
    nt = (((1,), (1,)), ((), ()))
    lower = lax.broadcasted_iota(jnp.int32, (GRID_W, LANES), 1) < HEAD_DIM
    cols = [slice(g * LANES, (g + 1) * LANES) for g in range(NA_HEADS // 2)]
    jobs = []
    for rr in range(_NA_ROWS_PER_STEP):
        r = pl.program_id(1) * _NA_ROWS_PER_STEP + rr
        off = _na_row_offset(r, rows)
        band = pl.ds(pl.multiple_of((r - off) * GRID_W, GRID_W), NA_WIN_H * GRID_W)
        jobs += [(slice(rr * GRID_W, (rr + 1) * GRID_W), off, band, g) for g in range(NA_HEADS // 2)]
    qs = [_head_pair_rows(q_ref[0, qr, cols[g]], lower) for qr, _, _, g in jobs]
    s_loc = [lax.dot_general(q, k_ref[0, band, cols[g]], nt, preferred_element_type=F32) + bias_ref[off, g]
             for q, (_, off, band, g) in zip(qs, jobs)]
    s_ctx = [lax.dot_general(q, kc_ref[0, :, cols[g]], nt, preferred_element_type=F32)
             for q, (_, _, _, g) in zip(qs, jobs)]
    p_loc, p_ctx, den = [], [], []
    for sl, sc in zip(s_loc, s_ctx):
        m = jnp.maximum(jnp.max(sl, axis=-1, keepdims=True), jnp.max(sc, axis=-1, keepdims=True))
        p_loc.append(jnp.exp(sl - m))
        p_ctx.append(jnp.exp(sc - m))
        den.append(jnp.sum(p_loc[-1], axis=-1, keepdims=True) + jnp.sum(p_ctx[-1], axis=-1, keepdims=True))
    outs = []
    for (_, _, band, g), pl_, pc, d in zip(jobs, p_loc, p_ctx, den):
        o = (jnp.dot(pl_.astype(BF16), v_ref[0, band, cols[g]], preferred_element_type=F32)
             + jnp.dot(pc.astype(BF16), vc_ref[0, :, cols[g]], preferred_element_type=F32)) / d
        outs.append(jnp.where(lower, o[:GRID_W], o[GRID_W:]).astype(o_ref.dtype))
    n = NA_HEADS // 2
    for rr in range(_NA_ROWS_PER_STEP):
        o_ref[0, rr * GRID_W:(rr + 1) * GRID_W, :] = jnp.concatenate(outs[rr * n:(rr + 1) * n], axis=-1)


def na_bias_table(rel_bias):
    cols = jnp.arange(GRID_W, dtype=jnp.int32)
    c0 = jnp.clip(cols - NA_WIN_W // 2, 0, GRID_W - NA_WIN_W)
    in_win = (cols[None, :] >= c0[:, None]) & (cols[None, :] < c0[:, None] + NA_WIN_W)
    col_idx = jnp.clip(cols[None, :] - cols[:, None] + (NA_WIN_W - 1), 0, 2 * NA_WIN_W - 2)
    off = jnp.arange(NA_WIN_H, dtype=jnp.int32)
    row_idx = off[None, :] - off[:, None] + (NA_WIN_H - 1)
    t = rel_bias[:, row_idx]
    t = t[..., col_idx]
    t = jnp.where(in_win[None, None, None], t, MASK_VALUE)
    return t.transpose(1, 0, 3, 2, 4).reshape(NA_WIN_H, NA_HEADS // 2, 2 * GRID_W, NA_WIN_H * GRID_W)


def neighbourhood_attention(q, k, v, kc, vc, bias):
    B, T, W = q.shape
    C = kc.shape[1]
    rows = T // GRID_W
    rq = _NA_ROWS_PER_STEP * GRID_W
    assert rows >= NA_WIN_H and rows % _NA_ROWS_PER_STEP == 0 and W == NA_HEADS * HEAD_DIM
    return pl.pallas_call(
        functools.partial(_na_kernel, rows=rows),
        grid=(B, rows // _NA_ROWS_PER_STEP),
        in_specs=[pl.BlockSpec((1, rq, W), lambda b, r: (b, r, 0)),
                  pl.BlockSpec((1, T, W), lambda b, r: (b, 0, 0)),
                  pl.BlockSpec((1, T, W), lambda b, r: (b, 0, 0)),
                  pl.BlockSpec((1, C, W), lambda b, r: (b, 0, 0)),
                  pl.BlockSpec((1, C, W), lambda b, r: (b, 0, 0)),
                  pl.BlockSpec(bias.shape, lambda b, r: (0, 0, 0, 0))],
        out_specs=pl.BlockSpec((1, rq, W), lambda b, r: (b, r, 0)),
        out_shape=jax.ShapeDtypeStruct((B, T, W), BF16),
        compiler_params=_params("parallel", "arbitrary"),
        name="neighbourhood_attention",
    )(q, k, v, kc, vc, bias)


def _pair_attn_kernel(*refs, Gq, Gk, bq, n_sets):
    q_ref = refs[0]
    kv_refs = refs[1:1 + 2 * n_sets]
    o_ref = refs[1 + 2 * n_sets]
    nt = (((1,), (1,)), ((), ()))
    sets = range(n_sets)
    lower = lax.broadcasted_iota(jnp.int32, (bq, LANES), 1) < HEAD_DIM
    col = lambda g: slice(g * LANES, (g + 1) * LANES)
    pairs = [_head_pair_rows(q_ref[0, :, col(g)], lower) for g in range(Gq)]
    if Gk == 1:
        jobs = [(jnp.concatenate([p[:bq] for p in pairs], axis=0), col(0)),
                (jnp.concatenate([p[bq:] for p in pairs], axis=0), col(0))]
    else:
        jobs = [(pairs[g], col(g)) for g in range(Gq)]
    ss = [[lax.dot_general(rows, kv_refs[2 * i][0, :, kc], nt, preferred_element_type=F32) for i in sets]
          for rows, kc in jobs]
    ps, dens = [], []
    for s in ss:
        m = functools.reduce(jnp.maximum, [jnp.max(x, axis=-1, keepdims=True) for x in s])
        ps.append([jnp.exp(x - m) for x in s])
        dens.append(functools.reduce(jnp.add, [jnp.sum(p, axis=-1, keepdims=True) for p in ps[-1]]))
    os_ = []
    for (rows, kc), p, den in zip(jobs, ps, dens):
        o = functools.reduce(jnp.add, [jnp.dot(p[i].astype(BF16), kv_refs[2 * i + 1][0, :, kc],
                                               preferred_element_type=F32) for i in sets])
        os_.append(o / den)
    outs = []
    for g in range(Gq):
        if Gk == 1:
            first, second = os_[0][g * bq:(g + 1) * bq], os_[1][g * bq:(g + 1) * bq]
        else:
            first, second = os_[g][:bq], os_[g][bq:]
        outs.append(jnp.where(lower, first, second).astype(o_ref.dtype))
    o_ref[0] = jnp.concatenate(outs, axis=-1)


def pair_attention(q, kv_sets):
    B, L, Wq = q.shape
    Wk = kv_sets[0][0].shape[2]
    Gq, Gk = Wq // LANES, Wk // LANES
    assert Gk in (1, Gq)
    bq = min(256, L)
    in_specs = [pl.BlockSpec((1, bq, Wq), lambda b, i: (b, i, 0))]
    args = [q]
    for k, v in kv_sets:
        for t in (k, v):
            in_specs.append(pl.BlockSpec((1, t.shape[1], Wk), lambda b, i: (b, 0, 0)))
            args.append(t)
    return pl.pallas_call(
        functools.partial(_pair_attn_kernel, Gq=Gq, Gk=Gk, bq=bq, n_sets=len(kv_sets)),
        grid=(B, L // bq),
        in_specs=in_specs,
        out_specs=pl.BlockSpec((1, bq, Wq), lambda b, i: (b, i, 0)),
        out_shape=jax.ShapeDtypeStruct((B, L, Wq), BF16),
        compiler_params=_params("parallel", "arbitrary"),
        name="pair_attention",
    )(*args)


def _proj_residual_kernel(a_ref, b_ref, g_ref, wa_ref, wb_ref, wg_ref, x_ref, gate_ref, o_ref):
    y = (jnp.dot(a_ref[0], wa_ref[...], preferred_element_type=F32)
         + jnp.dot(b_ref[0], wb_ref[...], preferred_element_type=F32)
         + jnp.dot(g_ref[0], wg_ref[...], preferred_element_type=F32))
    o_ref[0] = x_ref[0] + gate_ref[0] * y


def proj_residual(parts, w, x, gate):
    B, L, D = x.shape
    tm = min(512, L)
    gate_map = (lambda b, i: (b, 0, 0)) if gate.shape[0] == B else (lambda b, i: (0, 0, 0))
    widths = [p.shape[2] for p in parts]
    assert sum(widths) == w.shape[0]
    starts = [sum(widths[:i]) for i in range(len(parts))]
    ws = [w[s:s + k] for s, k in zip(starts, widths)]
    return pl.pallas_call(
        _proj_residual_kernel,
        grid=(B, L // tm),
        in_specs=[pl.BlockSpec((1, tm, k), lambda b, i: (b, i, 0)) for k in widths]
        + [pl.BlockSpec((k, D), lambda b, i: (0, 0)) for k in widths]
        + [pl.BlockSpec((1, tm, D), lambda b, i: (b, i, 0)), pl.BlockSpec((1, 1, D), gate_map)],
        out_specs=pl.BlockSpec((1, tm, D), lambda b, i: (b, i, 0)),
        out_shape=jax.ShapeDtypeStruct((B, L, D), F32),
        compiler_params=_params("parallel", "parallel"),
        name="proj_residual",
    )(*parts, *ws, x, gate)


def _top_rows(s, k, order=None, payload=None):
    if order is None:
        return _top_rows_by_row(s, k)
    n = s.shape[0]
    cut = lambda x: [x[i:i + SUBLANES] for i in range(0, n, SUBLANES)]
    pieces, orders, payloads = cut(s), cut(order), cut(payload)
    vals, picks = [], []
    for _ in range(k):
        level = list(zip(pieces, orders, payloads))
        while len(level) > 1:
            nxt = [(jnp.maximum(va, vb), jnp.where(va >= vb, oa, ob), jnp.where(va >= vb, pa, pb))
                   for (va, oa, pa), (vb, ob, pb) in zip(level[0::2], level[1::2])]
            if len(level) % 2:
                nxt.append(level[-1])
            level = nxt
        v, o, p = level[0]
        m = jnp.max(v, axis=0, keepdims=True)
        first = jnp.min(jnp.where(v == m, o, jnp.inf), axis=0, keepdims=True)
        vals.append(m)
        picks.append(jnp.max(jnp.where(o == first, p, -1.0), axis=0, keepdims=True))
        pieces = [jnp.where(oi == first, -jnp.inf, pi) for pi, oi in zip(pieces, orders)]
    return jnp.concatenate(vals, axis=0), jnp.concatenate(picks, axis=0)


def _top_rows_by_row(s, k):
    n, t = s.shape
    sub = lax.broadcasted_iota(jnp.int32, (SUBLANES, t), 0).astype(F32)
    pieces = [s[i:i + SUBLANES] for i in range(0, n, SUBLANES)]
    rows = [sub + float(i) for i in range(0, n, SUBLANES)]
    vals, picks = [], []
    for _ in range(k):
        level = list(zip(pieces, rows))
        while len(level) > 1:
            nxt = []
            for (va, ia), (vb, ib) in zip(level[0::2], level[1::2]):
                nxt.append((jnp.maximum(va, vb), jnp.where(va >= vb, ia, ib)))
            level = nxt
        v, i = level[0]
        m = jnp.max(v, axis=0, keepdims=True)
        first = jnp.min(jnp.where(v == m, i, jnp.inf), axis=0, keepdims=True)
        vals.append(m)
        picks.append(first)
        pieces = [jnp.where(r == first, -jnp.inf, p) for p, r in zip(pieces, rows)]
    return jnp.concatenate(vals, axis=0), jnp.concatenate(picks, axis=0)


def _pair_candidates(s0, i0, s1, i1):
    K = PEER_TOPK
    t = s0.shape[1]
    sub = lax.broadcasted_iota(jnp.int32, (SUBLANES, t), 0)
    scores, order, ids = [], [], []

    def add(a_rows, b_rows, a_of_row, b_of_row):
        sa, ia = a_rows
        sb, ib = b_rows
        ok = (a_of_row + 1) * (b_of_row + 1) <= K
        scores.append(jnp.where(ok, sa + sb, -jnp.inf))
        order.append((a_of_row * K + b_of_row).astype(F32))
        ids.append(ia * float(PEER_N_KEYS) + ib)

    row = lambda x, r: (x[0][r:r + 1], x[1][r:r + 1])
    rows = lambda x, r: (x[0][r:r + SUBLANES], x[1][r:r + SUBLANES])
    A, Bv = (s0, i0), (s1, i1)
    add(row(A, 0), rows(Bv, 0), jnp.zeros_like(sub), sub)
    add(row(A, 0), rows(Bv, SUBLANES), jnp.zeros_like(sub), sub + SUBLANES)
    for a in range(1, 4):
        add(row(A, a), rows(Bv, 0), jnp.full_like(sub, a), sub)
    for b in range(3):
        dup = sub < 4
        sa, ia = rows(A, 0)
        add((jnp.where(dup, -jnp.inf, sa), ia), row(Bv, b), sub, jnp.full_like(sub, b))
    add(rows(A, SUBLANES), row(Bv, 0), sub + SUBLANES, jnp.zeros_like(sub))
    cat = lambda xs: jnp.concatenate(xs, axis=0)
    return cat(scores), cat(order), cat(ids)


def _peer_topk_kernel(q_ref, keys_ref, offs_ref, g_ref):
    K = PEER_TOPK
    assert K == 2 * SUBLANES
    nt = (((1,), (1,)), ((), ()))
    words, gates = [], []
    for h in range(PEER_HEADS):
        tops = []
        for p in range(2):
            col = (2 * h + p) * PEER_D_KEY
            qhp = q_ref[:, col:col + PEER_D_KEY]
            s = lax.dot_general(keys_ref[h, p], qhp, nt, preferred_element_type=F32)
            tops.append(_top_rows(s, K))
        (s0, i0), (s1, i1) = tops
        cand_s, cand_order, cand_i = _pair_candidates(s0, i0, s1, i1)
        best_s, best_i = _top_rows(cand_s, K, order=cand_order, payload=cand_i)
        e = jnp.exp(best_s - jnp.max(best_s, axis=0, keepdims=True))
        gates.append(e / jnp.sum(e, axis=0, keepdims=True))
        off = best_i.astype(jnp.int32) * _HALF
        lo = jnp.concatenate([off[0:_HALF], off[SUBLANES:SUBLANES + _HALF]], axis=0)
        hi = jnp.concatenate([off[_HALF:SUBLANES], off[SUBLANES + _HALF:K]], axis=0)
        words.append(lo | (hi << 16))
    g_ref[...] = jnp.concatenate(gates, axis=0).T
    words.append(jnp.zeros((PEER_PAIRS // 2, q_ref.shape[0]), jnp.int32))
    offs_ref[...] = jnp.concatenate(words, axis=0).T[:, :PEER_PAIRS // 2]


def peer_topk(q, keys):
    N = q.shape[0]
    tm = LANES
    return pl.pallas_call(
        _peer_topk_kernel,
        grid=(N // tm,),
        in_specs=[pl.BlockSpec((tm, q.shape[1]), lambda i: (i, 0)),
                  pl.BlockSpec(keys.shape, lambda i: (0, 0, 0, 0))],
        out_specs=[pl.BlockSpec((tm, PEER_PAIRS // 2), lambda i: (i, 0)),
                   pl.BlockSpec((tm, PEER_PAIRS), lambda i: (i, 0))],
        out_shape=[jax.ShapeDtypeStruct((N, PEER_PAIRS // 2), jnp.int32),
                   jax.ShapeDtypeStruct((N, PEER_PAIRS), F32)],
        compiler_params=_params("parallel"),
        name="peer_topk",
    )(q, keys)


_PEER_TOKENS = 128
_ACT_TOKENS_PER_TRIP = 4
_GROUPS = PEER_PAIRS // SUBLANES


def _pack_table_kernel(w_ref, o_ref):
    te = w_ref.shape[0]
    bits = lambda x: pltpu.bitcast(x.astype(BF16).astype(F32), jnp.uint32)
    for i in range(_HALF):
        lo = bits(w_ref[:, (2 * i) * LANES:(2 * i + 1) * LANES]) >> 16
        hi = bits(w_ref[:, (2 * i + 1) * LANES:(2 * i + 2) * LANES])
        o_ref[pl.ds(i, te, stride=_HALF), :] = lo | hi


def pack_expert_table(w, layer):
    _, E, D = w.shape
    assert D == SUBLANES * LANES
    te = 256
    return pl.pallas_call(
        _pack_table_kernel,
        grid=(E // te,),
        in_specs=[pl.BlockSpec((None, te, D), lambda i: (layer, i, 0))],
        out_specs=pl.BlockSpec((te * _HALF, LANES), lambda i: (i, 0)),
        out_shape=jax.ShapeDtypeStruct((E * _HALF, LANES), jnp.uint32),
        compiler_params=_params("parallel"),
        name="pack_expert_table",
    )(w)


def _pair_position(word, half):
    return SUBLANES * (word // _HALF) + _HALF * half + word % _HALF


def _expert_rows(tab_ref, word):
    starts = (word & 0xFFFF, lax.shift_right_logical(word, 16))
    return [pltpu.bitcast(tab_ref[pl.ds(pl.multiple_of(s, _HALF), _HALF), :], BF16).astype(F32) for s in starts]


def _token_row(ref, t):
    row = ref[pl.ds(t, 1), :]
    return jnp.concatenate([row[:, s * LANES:(s + 1) * LANES] for s in range(SUBLANES)], axis=0)


def _merge_pair(a, b, shift, first):
    if shift == _HALF:
        return jnp.where(first, a, b) + pltpu.roll(jnp.where(first, b, a), shift, axis=0)
    bs = pltpu.roll(b, shift, axis=0)
    return jnp.where(first, a, bs) + pltpu.roll(jnp.where(first, bs, a), SUBLANES - shift, axis=0)


_MERGE_ORDER = (0, 4, 2, 6, 1, 5, 3, 7)


def _merge8(ps):
    sub = lax.broadcasted_iota(jnp.int32, (SUBLANES, LANES), 0)
    shift = _HALF
    while len(ps) > 1:
        first = (sub % (2 * shift)) < shift
        ps = [_merge_pair(ps[2 * i], ps[2 * i + 1], shift, first) for i in range(len(ps) // 2)]
        shift //= 2
    return ps[0]


def _peer_act_kernel(idx_ref, h_ref, g_ref, tab_ref, o_ref, part_ref):
    tn = h_ref.shape[0]

    def products(t, slot):
        x = _token_row(h_ref, t)
        for gi in range(_GROUPS):
            rows = []
            for w in range(_HALF):
                rows += _expert_rows(tab_ref, idx_ref[t, gi * _HALF + w])
            prods = [rows[2 * (j % _HALF) + j // _HALF] * x for j in _MERGE_ORDER]
            part_ref[slot, pl.ds(gi * SUBLANES, SUBLANES), :] = _merge8(prods)

    def reduce(t, slot):
        o_ref[pl.ds(t, 1), :] = jnp.sum(part_ref[slot].T, axis=0, keepdims=True)

    part_ref[...] = jnp.zeros(part_ref.shape, F32)
    slots = range(_ACT_TOKENS_PER_TRIP)

    def trip(i, carry):
        t = _ACT_TOKENS_PER_TRIP * i
        for sl in slots:
            reduce(jnp.maximum(t - _ACT_TOKENS_PER_TRIP + sl, 0), sl)
        for sl in slots:
            products(t + sl, sl)
        return carry

    lax.fori_loop(0, tn // _ACT_TOKENS_PER_TRIP, trip, 0)
    for sl in slots:
        reduce(tn - _ACT_TOKENS_PER_TRIP + sl, sl)
    o_ref[...] = g_ref[...] * jax.nn.gelu(o_ref[...], approximate=True)


def peer_act(offs, h, g, table):
    N = offs.shape[0]
    tn = _PEER_TOKENS
    return pl.pallas_call(
        _peer_act_kernel,
        grid=(N // tn,),
        in_specs=[pl.BlockSpec((tn, PEER_PAIRS // 2), lambda i: (i, 0), memory_space=pltpu.SMEM),
                  pl.BlockSpec((tn, SUBLANES * LANES), lambda i: (i, 0)),
                  pl.BlockSpec((tn, PEER_PAIRS), lambda i: (i, 0)),
                  pl.BlockSpec(memory_space=pltpu.VMEM)],
        out_specs=pl.BlockSpec((tn, PEER_PAIRS), lambda i: (i, 0)),
        out_shape=jax.ShapeDtypeStruct((N, PEER_PAIRS), F32),
        scratch_shapes=[pltpu.VMEM((_ACT_TOKENS_PER_TRIP, PEER_PAIRS, LANES), F32)],
        compiler_params=_params("arbitrary"),
        name="peer_act",
    )(offs, h, g, table)


_MIX_CHAINS = 4


def _peer_mix_kernel(idx_ref, coef_ref, tab_ref, x_ref, gate_ref, o_ref, cb_ref):
    tn = x_ref.shape[0]

    def spread(t):
        row = coef_ref[pl.ds(t, 1), :]
        return jnp.broadcast_to(row, (PEER_PAIRS, PEER_PAIRS)).T

    cb_ref[...] = spread(0)

    def token(t, carry):
        accs = [None] * _MIX_CHAINS
        for w in range(PEER_PAIRS // 2):
            for k, row in enumerate(_expert_rows(tab_ref, idx_ref[t, w])):
                j = _pair_position(w, k)
                term = jnp.broadcast_to(cb_ref[pl.ds(j, 1), :], (SUBLANES, LANES)) * row
                a = j % _MIX_CHAINS
                accs[a] = term if accs[a] is None else accs[a] + term
        y = (accs[0] + accs[1]) + (accs[2] + accs[3])
        y_row = jnp.concatenate([y[s:s + 1] for s in range(SUBLANES)], axis=1)
        o_ref[pl.ds(t, 1), :] = x_ref[pl.ds(t, 1), :] + gate_ref[0] * y_row
        cb_ref[...] = spread(jnp.minimum(t + 1, tn - 1))
        return carry

    lax.fori_loop(0, tn, token, 0)


def peer_mix(offs, coef, table, x, gate, tokens_per_batch):
    N, D = x.shape
    tn = _PEER_TOKENS
    assert tokens_per_batch % tn == 0
    if gate.shape[0] == 1:
        gate_map = lambda i: (0, 0, 0)
    else:
        gate_map = lambda i: ((i * tn) // tokens_per_batch, 0, 0)
    return pl.pallas_call(
        _peer_mix_kernel,
        grid=(N // tn,),
        in_specs=[pl.BlockSpec((tn, PEER_PAIRS // 2), lambda i: (i, 0), memory_space=pltpu.SMEM),
                  pl.BlockSpec((tn, PEER_PAIRS), lambda i: (i, 0)),
                  pl.BlockSpec(memory_space=pltpu.VMEM),
                  pl.BlockSpec((tn, D), lambda i: (i, 0)),
                  pl.BlockSpec((1, 1, D), gate_map)],
        out_specs=pl.BlockSpec((tn, D), lambda i: (i, 0)),
        out_shape=jax.ShapeDtypeStruct((N, D), F32),
        scratch_shapes=[pltpu.VMEM((PEER_PAIRS, LANES), F32)],
        compiler_params=_params("arbitrary"),
        name="peer_mix",
    )(offs, coef, table, x, gate)


def _rmsnorm_kernel(x_ref, g_ref, o_ref):
    x = x_ref[0]
    o_ref[0] = x * lax.rsqrt(jnp.mean(x * x, axis=-1, keepdims=True) + EPS) * g_ref[...]


def rmsnorm(x, g):
    B, L, D = x.shape
    tm = min(512, L)
    return pl.pallas_call(
        _rmsnorm_kernel,
        grid=(B, L // tm),
        in_specs=[pl.BlockSpec((1, tm, D), lambda b, i: (b, i, 0)),
                  pl.BlockSpec((1, D), lambda b, i: (0, 0))],
        out_specs=pl.BlockSpec((1, tm, D), lambda b, i: (b, i, 0)),
        out_shape=jax.ShapeDtypeStruct((B, L, D), F32),
        compiler_params=_params("parallel", "parallel"),
        name="final_rmsnorm",
    )(x, g.reshape(1, D))


def _rope_tables(T):
    t = jnp.arange(T, dtype=jnp.int32)
    row = (t // GRID_W).astype(F32)
    col = (t % GRID_W).astype(F32)
    n_freq = HEAD_DIM // 4
    inv_freq = ROPE_THETA ** (-jnp.arange(n_freq, dtype=F32) / n_freq)
    ang = jnp.concatenate([row[:, None] * inv_freq, col[:, None] * inv_freq], axis=-1)
    cos, sin = jnp.cos(ang), jnp.sin(ang)
    return jnp.tile(cos, (1, 4)), jnp.tile(jnp.concatenate([-sin, sin], axis=-1), (1, 2))


def _peer_ffn_residual(x, g2, sc2, sh2, gate, w_q, keys, u_tab, v_tab):
    B, L, D = x.shape
    N = B * L
    q, h = norm_mod_proj(x, g2, sc2, sh2, w_q, (("bf16", w_q.shape[1], 1.0),), emit_h=True)
    offs, gates = peer_topk(q.reshape(N, -1), keys)
    coef = peer_act(offs, h.reshape(N, D), gates, u_tab)
    out = peer_mix(offs, coef, v_tab, x.reshape(N, D), gate, L)
    return out.reshape(B, L, D)


_Q_SCALE = HEAD_DIM ** -0.5
_NA_W = NA_HEADS * HEAD_DIM
_GQA_W = GQA_HEADS * HEAD_DIM
_GQA_KV_W = GQA_KV_HEADS * HEAD_DIM
_IN_OUTS = (("f32", 2 * CONV_CH), ("bf16", _NA_W, _Q_SCALE), ("norm", _GQA_W // LANES, _Q_SCALE, 0),
            ("bf16", _NA_W, 1.0), ("bf16", _NA_W, 1.0), ("norm", _GQA_KV_W // LANES, 1.0, 1),
            ("bf16", _GQA_KV_W, 1.0))
_OFF_G_Q = 2 * CONV_CH + _NA_W
_OFF_G_OUT = CONV_CH + _NA_W
_GQA_PAIR_ORDER = tuple(h for g in range(GQA_HEADS // GQA_KV_HEADS) for h in (g, GQA_HEADS // GQA_KV_HEADS + g))
assert GQA_KV_HEADS == 2 and 2 * HEAD_DIM == LANES


def _reorder_heads(w, start, axis):
    idx = jnp.concatenate([jnp.arange(HEAD_DIM) + start + h * HEAD_DIM for h in _GQA_PAIR_ORDER])
    full = jnp.arange(w.shape[axis]).at[start:start + _GQA_W].set(idx)
    return jnp.take(w, full, axis=axis)


def kernel(x, c, ctx, c_ctx, norm1_g, norm2_g, w_ada, b_ada, w_in, conv_w, conv_b, conv_ln_g, conv_ln_b,
           na_rel_bias, gqa_q_norm, gqa_k_norm, w_out, peer_w_q, peer_keys, peer_u, peer_v, final_norm_g):
    B, T, D = x.shape
    depth = w_in.shape[0]
    rope = _rope_tables(T)
    ada_rows = -(-(B + 1) // SUBLANES) * SUBLANES
    ada_in = jnp.zeros((ada_rows, D), F32).at[:B].set(c).at[B].set(c_ctx)

    for l in range(depth):
        last = l == depth - 1
        mod = ada_mod(ada_in, w_ada[l], b_ada[l])
        sh1, sc1, g1, sh2, sc2, g2 = [m.reshape(B, 1, D) for m in jnp.split(mod[:B], 6, axis=-1)]
        csh1, csc1, cg1, csh2, csc2, cg2 = [m.reshape(1, 1, D) for m in jnp.split(mod[B:B + 1], 6, axis=-1)]
        w_in_b = _reorder_heads(w_in[l], _OFF_G_Q, 1).astype(BF16)
        w_out_b = _reorder_heads(w_out[l], _OFF_G_OUT, 0).astype(BF16)
        w_q_b = peer_w_q[l].astype(BF16)
        keys_b = peer_keys[l].astype(BF16)
        u_tab = pack_expert_table(peer_u, l)
        v_tab = pack_expert_table(peer_v, l)
        qk_gains = jnp.stack([jnp.tile(gqa_q_norm[l], 2), jnp.tile(gqa_k_norm[l], 2)])

        za, na_q, g_q, na_k, na_v, g_k, g_v = norm_mod_proj(x, norm1_g[l], sc1, sh1, w_in_b, _IN_OUTS,
                                                            gains=qk_gains, rope=rope)
        zca, cna_q, cg_q, cna_k, cna_v, cg_k, cg_v = norm_mod_proj(ctx, norm1_g[l], csc1, csh1, w_in_b, _IN_OUTS,
                                                                   gains=qk_gains)
        a = conformer_conv(za, conv_w[l], conv_b[l], conv_ln_g[l], conv_ln_b[l])
        bm = neighbourhood_attention(na_q, na_k, na_v, cna_k, cna_v, na_bias_table(na_rel_bias[l]))
        gm = pair_attention(g_q, [(g_k, g_v), (cg_k, cg_v)])
        x = proj_residual((a, bm, gm), w_out_b, x, g1)

        x = _peer_ffn_residual(x, norm2_g[l], sc2, sh2, g2, w_q_b, keys_b, u_tab, v_tab)

        if not last:
            ac = conformer_conv(zca, conv_w[l], conv_b[l], conv_ln_g[l], conv_ln_b[l])
            bc = pair_attention(cna_q, [(cna_k, cna_v)])
            gc = pair_attention(cg_q, [(cg_k, cg_v)])
            ctx = proj_residual((ac, bc, gc), w_out_b, ctx, cg1)
            ctx = _peer_ffn_residual(ctx, norm2_g[l], csc2, csh2, cg2, w_q_b, keys_b, u_tab, v_tab)
    return rmsnorm(x, final_norm_g)
```

```python
import functools

import jax
import jax.numpy as jnp
from jax import lax
from jax.experimental import pallas as pl
from jax.experimental.pallas import tpu as pltpu

F32 = jnp.float32
BF16 = jnp.bfloat16

GRID_W = 64
HEAD_DIM = 64
CONV_CH = 256
CONV_WIDTH = 31
NA_HEADS = 6
NA_WIN_H = 8
NA_WIN_W = 16
GQA_HEADS = 6
GQA_KV_HEADS = 2
ROPE_THETA = 10000.0
PEER_HEADS = 8
PEER_N_KEYS = 128
PEER_D_KEY = 128
PEER_TOPK = 16
EPS = 1e-6

PEER_PAIRS = PEER_HEADS * PEER_TOPK

LANES = 128
SUBLANES = 8
_HALF = SUBLANES // 2
assert PEER_N_KEYS * PEER_N_KEYS * _HALF <= 1 << 16
VMEM_LIMIT = 48 * 1024 * 1024
MASK_VALUE = -1e30


def _params(*sem):
    return pltpu.CompilerParams(dimension_semantics=sem, vmem_limit_bytes=VMEM_LIMIT)


def _ada_kernel(a_ref, w_ref, b_ref, o_ref):
    a = a_ref[...]
    s = a * jax.nn.sigmoid(a)
    o_ref[...] = jnp.dot(s.astype(BF16), w_ref[...].astype(BF16), preferred_element_type=F32) + b_ref[...]


def ada_mod(a, w, b):
    R, D = a.shape
    N = w.shape[1]
    tn = 1024
    return pl.pallas_call(
        _ada_kernel,
        grid=(N // tn,),
        in_specs=[pl.BlockSpec((R, D), lambda j: (0, 0)),
                  pl.BlockSpec((D, tn), lambda j: (0, j)),
                  pl.BlockSpec((1, tn), lambda j: (0, j))],
        out_specs=pl.BlockSpec((R, tn), lambda j: (0, j)),
        out_shape=jax.ShapeDtypeStruct((R, N), F32),
        compiler_params=_params("parallel"),
        name="ada_mod",
    )(a, w, b.reshape(1, N))


def _pair_norm_rope(x, gain, rope_cos, rope_sin):
    lane = lax.broadcasted_iota(jnp.int32, x.shape, 1)
    lower = lane < HEAD_DIM
    sq = x * x
    ms_lo = jnp.sum(jnp.where(lower, sq, 0.0), axis=-1, keepdims=True) * (1.0 / HEAD_DIM)
    ms_hi = jnp.sum(jnp.where(lower, 0.0, sq), axis=-1, keepdims=True) * (1.0 / HEAD_DIM)
    y = x * jnp.where(lower, lax.rsqrt(ms_lo + EPS), lax.rsqrt(ms_hi + EPS)) * gain
    if rope_cos is None:
        return y
    half = HEAD_DIM // 2
    first = (lane % HEAD_DIM) < half
    partner = jnp.where(first, pltpu.roll(y, LANES - half, axis=1), pltpu.roll(y, half, axis=1))
    return y * rope_cos + partner * rope_sin


def _norm_mod_proj_kernel(*refs, outs, emit_h, has_gains, rope):
    x_ref, g_ref, sc_ref, sh_ref, w_ref = refs[:5]
    n_in = 5
    gains_ref = cos = sin = None
    if has_gains:
        gains_ref = refs[n_in]
        n_in += 1
    if rope:
        cos, sin = refs[n_in][...], refs[n_in + 1][...]
        n_in += 2
    out_refs = refs[n_in:]
    x = x_ref[0]
    y = x * lax.rsqrt(jnp.mean(x * x, axis=-1, keepdims=True) + EPS) * g_ref[...]
    h = y * (1.0 + sc_ref[0]) + sh_ref[0]
    z = jnp.dot(h.astype(BF16), w_ref[...], preferred_element_type=F32)
    off = 0
    for o_ref, spec in zip(out_refs, outs):
        if spec[0] == "f32":
            o_ref[0] = z[:, off:off + spec[1]]
            off += spec[1]
        elif spec[0] == "bf16":
            o_ref[0] = (z[:, off:off + spec[1]] * spec[2]).astype(BF16)
            off += spec[1]
        else:
            groups = []
            for _ in range(spec[1]):
                v = _pair_norm_rope(z[:, off:off + LANES], gains_ref[spec[3]:spec[3] + 1, :], cos, sin)
                groups.append((v * spec[2]).astype(BF16))
                off += LANES
            o_ref[0] = groups[0] if len(groups) == 1 else jnp.concatenate(groups, axis=-1)
    if emit_h:
        out_refs[len(outs)][0] = h


def _out_width(spec):
    return spec[1] * LANES if spec[0] == "norm" else spec[1]


def norm_mod_proj(x, g, sc, sh, w, outs, emit_h=False, gains=None, rope=None):
    B, L, D = x.shape
    N = w.shape[1]
    assert sum(_out_width(o) for o in outs) == N
    tm = min(512, L)
    per_batch = sc.shape[0] == B
    mod_map = (lambda b, i: (b, 0, 0)) if per_batch else (lambda b, i: (0, 0, 0))
    in_specs = [pl.BlockSpec((1, tm, D), lambda b, i: (b, i, 0)),
                pl.BlockSpec((1, D), lambda b, i: (0, 0)),
                pl.BlockSpec((1, 1, D), mod_map),
                pl.BlockSpec((1, 1, D), mod_map),
                pl.BlockSpec((D, N), lambda b, i: (0, 0))]
    args = [x, g.reshape(1, D), sc, sh, w]
    if gains is not None:
        in_specs.append(pl.BlockSpec(gains.shape, lambda b, i: (0, 0)))
        args.append(gains)
    if rope is not None:
        for tab in rope:
            in_specs.append(pl.BlockSpec((tm, LANES), lambda b, i: (i, 0)))
            args.append(tab)
    out_shape = [jax.ShapeDtypeStruct((B, L, _out_width(o)), F32 if o[0] == "f32" else BF16) for o in outs]
    out_specs = [pl.BlockSpec((1, tm, _out_width(o)), lambda b, i: (b, i, 0)) for o in outs]
    if emit_h:
        out_shape.append(jax.ShapeDtypeStruct((B, L, D), F32))
        out_specs.append(pl.BlockSpec((1, tm, D), lambda b, i: (b, i, 0)))
    return pl.pallas_call(
        functools.partial(_norm_mod_proj_kernel, outs=outs, emit_h=emit_h, has_gains=gains is not None,
                          rope=rope is not None),
        grid=(B, L // tm),
        in_specs=in_specs,
        out_specs=out_specs,
        out_shape=out_shape,
        compiler_params=_params("parallel", "parallel"),
        name="norm_mod_proj",
    )(*args)


_CONV_PAD = 16


def _conv_kernel(za_ref, w_ref, b_ref, lg_ref, lb_ref, o_ref, upad_ref, *, L, tc):
    c = pl.program_id(1)

    @pl.when(c == 0)
    def _():
        val = za_ref[0, :, :CONV_CH]
        gate = za_ref[0, :, CONV_CH:]
        zeros = jnp.zeros((_CONV_PAD, CONV_CH), F32)
        upad_ref[pl.ds(0, _CONV_PAD), :] = zeros
        upad_ref[pl.ds(_CONV_PAD + L, _CONV_PAD), :] = zeros
        upad_ref[pl.ds(_CONV_PAD, L), :] = val * jax.nn.sigmoid(gate)

    start = pl.multiple_of(c * tc, SUBLANES)
    win = upad_ref[pl.ds(start, tc + 2 * _CONV_PAD), :]
    acc = jnp.zeros((tc, CONV_CH), F32)
    first = _CONV_PAD - CONV_WIDTH // 2
    rows = tc + 2 * _CONV_PAD
    shifted = [win if c == 0 else pltpu.roll(win, rows - c, axis=0) for c in range(SUBLANES)]
    for j in range(CONV_WIDTH):
        a, c = divmod(first + j, SUBLANES)
        acc = acc + shifted[c][a * SUBLANES:a * SUBLANES + tc, :] * w_ref[j:j + 1, :]
    y = acc + b_ref[...]
    mu = jnp.mean(y, axis=-1, keepdims=True)
    d = y - mu
    var = jnp.mean(d * d, axis=-1, keepdims=True)
    yn = d * lax.rsqrt(var + EPS) * lg_ref[...] + lb_ref[...]
    o_ref[0] = (yn * jax.nn.sigmoid(yn)).astype(o_ref.dtype)


def conformer_conv(za, w_dw, b_dw, ln_g, ln_b):
    B, L, _ = za.shape
    tc = min(256, L)
    vec = lambda v: v.reshape(1, CONV_CH)
    return pl.pallas_call(
        functools.partial(_conv_kernel, L=L, tc=tc),
        grid=(B, L // tc),
        in_specs=[pl.BlockSpec((1, L, 2 * CONV_CH), lambda b, c: (b, 0, 0)),
                  pl.BlockSpec((CONV_WIDTH, CONV_CH), lambda b, c: (0, 0)),
                  pl.BlockSpec((1, CONV_CH), lambda b, c: (0, 0)),
                  pl.BlockSpec((1, CONV_CH), lambda b, c: (0, 0)),
                  pl.BlockSpec((1, CONV_CH), lambda b, c: (0, 0))],
        out_specs=pl.BlockSpec((1, tc, CONV_CH), lambda b, c: (b, c, 0)),
        out_shape=jax.ShapeDtypeStruct((B, L, CONV_CH), BF16),
        scratch_shapes=[pltpu.VMEM((L + 2 * _CONV_PAD, CONV_CH), F32)],
        compiler_params=_params("parallel", "arbitrary"),
        name="conformer_conv",
    )(za, w_dw, vec(b_dw), vec(ln_g), vec(ln_b))


_NA_ROWS_PER_STEP = 4


def _na_row_offset(r, rows):
    return r - jnp.clip(r - NA_WIN_H // 2, 0, rows - NA_WIN_H)


def _head_pair_rows(qg, lower):
    zero = jnp.zeros_like(qg)
    return jnp.concatenate([jnp.where(lower, qg, zero), jnp.where(lower, zero, qg)], axis=0)


def _na_kernel(q_ref, k_ref, v_ref, kc_ref, vc_ref, bias_ref, o_ref, *, rows):
    nt = (((1,), (1,)), ((), ()))
    lower = lax.broadcasted_iota(jnp.int32, (GRID_W, LANES), 1) < HEAD_DIM
    cols = [slice(g * LANES, (g + 1) * LANES) for g in range(NA_HEADS // 2)]
    jobs = []
    for rr in range(_NA_ROWS_PER_STEP):
        r = pl.program_id(1) * _NA_ROWS_PER_STEP + rr
        off = _na_row_offset(r, rows)
        band = pl.ds(pl.multiple_of((r - off) * GRID_W, GRID_W), NA_WIN_H * GRID_W)
        jobs += [(slice(rr * GRID_W, (rr + 1) * GRID_W), off, band, g) for g in range(NA_HEADS // 2)]
    qs = [_head_pair_rows(q_ref[0, qr, cols[g]], lower) for qr, _, _, g in jobs]
    s_loc = [lax.dot_general(q, k_ref[0, band, cols[g]], nt, preferred_element_type=F32) + bias_ref[off, g]
             for q, (_, off, band, g) in zip(qs, jobs)]
    s_ctx = [lax.dot_general(q, kc_ref[0, :, cols[g]], nt, preferred_element_type=F32)
             for q, (_, _, _, g) in zip(qs, jobs)]
    p_loc, p_ctx, den = [], [], []
    for sl, sc in zip(s_loc, s_ctx):
        m = jnp.maximum(jnp.max(sl, axis=-1, keepdims=True), jnp.max(sc, axis=-1, keepdims=True))
        p_loc.append(jnp.exp(sl - m))
        p_ctx.append(jnp.exp(sc - m))
        den.append(jnp.sum(p_loc[-1], axis=-1, keepdims=True) + jnp.sum(p_ctx[-1], axis=-1, keepdims=True))
    outs = []
    for (_, _, band, g), pl_, pc, d in zip(jobs, p_loc, p_ctx, den):
        o = (jnp.dot(pl_.astype(BF16), v_ref[0, band, cols[g]], preferred_element_type=F32)
             + jnp.dot(pc.astype(BF16), vc_ref[0, :, cols[g]], preferred_element_type=F32)) / d
        outs.append(jnp.where(lower, o[:GRID_W], o[GRID_W:]).astype(o_ref.dtype))
    n = NA_HEADS // 2
    for rr in range(_NA_ROWS_PER_STEP):
        o_ref[0, rr * GRID_W:(rr + 1) * GRID_W, :] = jnp.concatenate(outs[rr * n:(rr + 1) * n], axis=-1)


def na_bias_table(rel_bias):
    cols = jnp.arange(GRID_W, dtype=jnp.int32)
    c0 = jnp.clip(cols - NA_WIN_W // 2, 0, GRID_W - NA_WIN_W)
    in_win = (cols[None, :] >= c0[:, None]) & (cols[None, :] < c0[:, None] + NA_WIN_W)
    col_idx = jnp.clip(cols[None, :] - cols[:, None] + (NA_WIN_W - 1), 0, 2 * NA_WIN_W - 2)
    off = jnp.arange(NA_WIN_H, dtype=jnp.int32)
    row_idx = off[None, :] - off[:, None] + (NA_WIN_H - 1)
    t = rel_bias[:, row_idx]
    t = t[..., col_idx]
    t = jnp.where(in_win[None, None, None], t, MASK_VALUE)
    return t.transpose(1, 0, 3, 2, 4).reshape(NA_WIN_H, NA_HEADS // 2, 2 * GRID_W, NA_WIN_H * GRID_W)


def neighbourhood_attention(q, k, v, kc, vc, bias):
    B, T, W = q.shape
    C = kc.shape[1]
    rows = T // GRID_W
    rq = _NA_ROWS_PER_STEP * GRID_W
    assert rows >= NA_WIN_H and rows % _NA_ROWS_PER_STEP == 0 and W == NA_HEADS * HEAD_DIM
    return pl.pallas_call(
        functools.partial(_na_kernel, rows=rows),
        grid=(B, rows // _NA_ROWS_PER_STEP),
        in_specs=[pl.BlockSpec((1, rq, W), lambda b, r: (b, r, 0)),
                  pl.BlockSpec((1, T, W), lambda b, r: (b, 0, 0)),
                  pl.BlockSpec((1, T, W), lambda b, r: (b, 0, 0)),
                  pl.BlockSpec((1, C, W), lambda b, r: (b, 0, 0)),
                  pl.BlockSpec((1, C, W), lambda b, r: (b, 0, 0)),
                  pl.BlockSpec(bias.shape, lambda b, r: (0, 0, 0, 0))],
        out_specs=pl.BlockSpec((1, rq, W), lambda b, r: (b, r, 0)),
        out_shape=jax.ShapeDtypeStruct((B, T, W), BF16),
        compiler_params=_params("parallel", "arbitrary"),
        name="neighbourhood_attention",
    )(q, k, v, kc, vc, bias)


def _pair_attn_kernel(*refs, Gq, Gk, bq, n_sets):
    q_ref = refs[0]
    kv_refs = refs[1:1 + 2 * n_sets]
    o_ref = refs[1 + 2 * n_sets]
    nt = (((1,), (1,)), ((), ()))
    sets = range(n_sets)
    lower = lax.broadcasted_iota(jnp.int32, (bq, LANES), 1) < HEAD_DIM
    col = lambda g: slice(g * LANES, (g + 1) * LANES)
    pairs = [_head_pair_rows(q_ref[0, :, col(g)], lower) for g in range(Gq)]
    if Gk == 1:
        jobs = [(jnp.concatenate([p[:bq] for p in pairs], axis=0), col(0)),
                (jnp.concatenate([p[bq:] for p in pairs], axis=0), col(0))]
    else:
        jobs = [(pairs[g], col(g)) for g in range(Gq)]
    ss = [[lax.dot_general(rows, kv_refs[2 * i][0, :, kc], nt, preferred_element_type=F32) for i in sets]
          for rows, kc in jobs]
    ps, dens = [], []
    for s in ss:
        m = functools.reduce(jnp.maximum, [jnp.max(x, axis=-1, keepdims=True) for x in s])
        ps.append([jnp.exp(x - m) for x in s])
        dens.append(functools.reduce(jnp.add, [jnp.sum(p, axis=-1, keepdims=True) for p in ps[-1]]))
    os_ = []
    for (rows, kc), p, den in zip(jobs, ps, dens):
        o = functools.reduce(jnp.add, [jnp.dot(p[i].astype(BF16), kv_refs[2 * i + 1][0, :, kc],
                                               preferred_element_type=F32) for i in sets])
        os_.append(o / den)
    outs = []
    for g in range(Gq):
        if Gk == 1:
            first, second = os_[0][g * bq:(g + 1) * bq], os_[1][g * bq:(g + 1) * bq]
        else:
            first, second = os_[g][:bq], os_[g][bq:]
        outs.append(jnp.where(lower, first, second).astype(o_ref.dtype))
    o_ref[0] = jnp.concatenate(outs, axis=-1)


def pair_attention(q, kv_sets):
    B, L, Wq = q.shape
    Wk = kv_sets[0][0].shape[2]
    Gq, Gk = Wq // LANES, Wk // LANES
    assert Gk in (1, Gq)
    bq = min(256, L)
    in_specs = [pl.BlockSpec((1, bq, Wq), lambda b, i: (b, i, 0))]
    args = [q]
    for k, v in kv_sets:
        for t in (k, v):
            in_specs.append(pl.BlockSpec((1, t.shape[1], Wk), lambda b, i: (b, 0, 0)))
            args.append(t)
    return pl.pallas_call(
        functools.partial(_pair_attn_kernel, Gq=Gq, Gk=Gk, bq=bq, n_sets=len(kv_sets)),
        grid=(B, L // bq),
        in_specs=in_specs,
        out_specs=pl.BlockSpec((1, bq, Wq), lambda b, i: (b, i, 0)),
        out_shape=jax.ShapeDtypeStruct((B, L, Wq), BF16),
        compiler_params=_params("parallel", "arbitrary"),
        name="pair_attention",
    )(*args)


def _proj_residual_query_kernel(a_ref, b_ref, g_ref, wa_ref, wb_ref, wg_ref, x_ref, gate_ref,
                                n2_ref, sc_ref, sh_ref, wq_ref, xo_ref, q_ref, h_ref):
    y = (jnp.dot(a_ref[0], wa_ref[...], preferred_element_type=F32)
         + jnp.dot(b_ref[0], wb_ref[...], preferred_element_type=F32)
         + jnp.dot(g_ref[0], wg_ref[...], preferred_element_type=F32))
    x = x_ref[0] + gate_ref[0] * y
    xo_ref[0] = x
    n = x * lax.rsqrt(jnp.mean(x * x, axis=-1, keepdims=True) + EPS) * n2_ref[...]
    h = n * (1.0 + sc_ref[0]) + sh_ref[0]
    h_ref[0] = h
    q_ref[0] = jnp.dot(h.astype(BF16), wq_ref[...], preferred_element_type=F32).astype(BF16)


def proj_residual_query(parts, w, x, gate, n2, sc2, sh2, w_q):
    B, L, D = x.shape
    Nq = w_q.shape[1]
    tm = min(512, L)
    mod_map = (lambda b, i: (b, 0, 0)) if gate.shape[0] == B else (lambda b, i: (0, 0, 0))
    widths = [p.shape[2] for p in parts]
    assert sum(widths) == w.shape[0]
    starts = [sum(widths[:i]) for i in range(len(parts))]
    ws = [w[s:s + k] for s, k in zip(starts, widths)]
    row = lambda n: pl.BlockSpec((1, tm, n), lambda b, i: (b, i, 0))
    return pl.pallas_call(
        _proj_residual_query_kernel,
        grid=(B, L // tm),
        in_specs=[row(k) for k in widths]
        + [pl.BlockSpec((k, D), lambda b, i: (0, 0)) for k in widths]
        + [row(D), pl.BlockSpec((1, 1, D), mod_map), pl.BlockSpec((1, D), lambda b, i: (0, 0)),
           pl.BlockSpec((1, 1, D), mod_map), pl.BlockSpec((1, 1, D), mod_map),
           pl.BlockSpec((D, Nq), lambda b, i: (0, 0))],
        out_specs=[row(D), row(Nq), row(D)],
        out_shape=[jax.ShapeDtypeStruct((B, L, D), F32), jax.ShapeDtypeStruct((B, L, Nq), BF16),
                   jax.ShapeDtypeStruct((B, L, D), F32)],
        compiler_params=_params("parallel", "parallel"),
        name="proj_residual_query",
    )(*parts, *ws, x, gate, n2.reshape(1, D), sc2, sh2, w_q)


def _top_rows(s, k, order=None, payload=None):
    if order is None:
        return _top_rows_by_row(s, k)
    n = s.shape[0]
    cut = lambda x: [x[i:i + SUBLANES] for i in range(0, n, SUBLANES)]
    pieces, orders, payloads = cut(s), cut(order), cut(payload)
    vals, picks = [], []
    for _ in range(k):
        level = list(zip(pieces, orders, payloads))
        while len(level) > 1:
            nxt = [(jnp.maximum(va, vb), jnp.where(va >= vb, oa, ob), jnp.where(va >= vb, pa, pb))
                   for (va, oa, pa), (vb, ob, pb) in zip(level[0::2], level[1::2])]
            if len(level) % 2:
                nxt.append(level[-1])
            level = nxt
        v, o, p = level[0]
        m = jnp.max(v, axis=0, keepdims=True)
        first = jnp.min(jnp.where(v == m, o, jnp.inf), axis=0, keepdims=True)
        vals.append(m)
        picks.append(jnp.max(jnp.where(o == first, p, -1.0), axis=0, keepdims=True))
        pieces = [jnp.where(oi == first, -jnp.inf, pi) for pi, oi in zip(pieces, orders)]
    return jnp.concatenate(vals, axis=0), jnp.concatenate(picks, axis=0)


def _top_rows_by_row(s, k):
    n, t = s.shape
    sub = lax.broadcasted_iota(jnp.int32, (SUBLANES, t), 0).astype(F32)
    pieces = [s[i:i + SUBLANES] for i in range(0, n, SUBLANES)]
    rows = [sub + float(i) for i in range(0, n, SUBLANES)]
    vals, picks = [], []
    for _ in range(k):
        level = list(zip(pieces, rows))
        while len(level) > 1:
            nxt = []
            for (va, ia), (vb, ib) in zip(level[0::2], level[1::2]):
                nxt.append((jnp.maximum(va, vb), jnp.where(va >= vb, ia, ib)))
            level = nxt
        v, i = level[0]
        m = jnp.max(v, axis=0, keepdims=True)
        first = jnp.min(jnp.where(v == m, i, jnp.inf), axis=0, keepdims=True)
        vals.append(m)
        picks.append(first)
        pieces = [jnp.where(r == first, -jnp.inf, p) for p, r in zip(pieces, rows)]
    return jnp.concatenate(vals, axis=0), jnp.concatenate(picks, axis=0)


def _pair_candidates(s0, i0, s1, i1):
    K = PEER_TOPK
    t = s0.shape[1]
    sub = lax.broadcasted_iota(jnp.int32, (SUBLANES, t), 0)
    scores, order, ids = [], [], []

    def add(a_rows, b_rows, a_of_row, b_of_row):
        sa, ia = a_rows
        sb, ib = b_rows
        ok = (a_of_row + 1) * (b_of_row + 1) <= K
        scores.append(jnp.where(ok, sa + sb, -jnp.inf))
        order.append((a_of_row * K + b_of_row).astype(F32))
        ids.append(ia * float(PEER_N_KEYS) + ib)

    row = lambda x, r: (x[0][r:r + 1], x[1][r:r + 1])
    rows = lambda x, r: (x[0][r:r + SUBLANES], x[1][r:r + SUBLANES])
    A, Bv = (s0, i0), (s1, i1)
    add(row(A, 0), rows(Bv, 0), jnp.zeros_like(sub), sub)
    add(row(A, 0), rows(Bv, SUBLANES), jnp.zeros_like(sub), sub + SUBLANES)
    for a in range(1, 4):
        add(row(A, a), rows(Bv, 0), jnp.full_like(sub, a), sub)
    for b in range(3):
        dup = sub < 4
        sa, ia = rows(A, 0)
        add((jnp.where(dup, -jnp.inf, sa), ia), row(Bv, b), sub, jnp.full_like(sub, b))
    add(rows(A, SUBLANES), row(Bv, 0), sub + SUBLANES, jnp.zeros_like(sub))
    cat = lambda xs: jnp.concatenate(xs, axis=0)
    return cat(scores), cat(order), cat(ids)


def _peer_topk_kernel(q_ref, keys_ref, offs_ref, g_ref):
    K = PEER_TOPK
    assert K == 2 * SUBLANES
    nt = (((1,), (1,)), ((), ()))
    words, gates = [], []
    for h in range(PEER_HEADS):
        tops = []
        for p in range(2):
            col = (2 * h + p) * PEER_D_KEY
            qhp = q_ref[:, col:col + PEER_D_KEY]
            s = lax.dot_general(keys_ref[h, p], qhp, nt, preferred_element_type=F32)
            tops.append(_top_rows(s, K))
        (s0, i0), (s1, i1) = tops
        cand_s, cand_order, cand_i = _pair_candidates(s0, i0, s1, i1)
        best_s, best_i = _top_rows(cand_s, K, order=cand_order, payload=cand_i)
        e = jnp.exp(best_s - jnp.max(best_s, axis=0, keepdims=True))
        gates.append(e / jnp.sum(e, axis=0, keepdims=True))
        off = best_i.astype(jnp.int32) * _HALF
        lo = jnp.concatenate([off[0:_HALF], off[SUBLANES:SUBLANES + _HALF]], axis=0)
        hi = jnp.concatenate([off[_HALF:SUBLANES], off[SUBLANES + _HALF:K]], axis=0)
        words.append(lo | (hi << 16))
    g_ref[...] = jnp.concatenate(gates, axis=0).T
    words.append(jnp.zeros((PEER_PAIRS // 2, q_ref.shape[0]), jnp.int32))
    offs_ref[...] = jnp.concatenate(words, axis=0).T[:, :PEER_PAIRS // 2]


def peer_topk(q, keys):
    N = q.shape[0]
    tm = LANES
    return pl.pallas_call(
        _peer_topk_kernel,
        grid=(N // tm,),
        in_specs=[pl.BlockSpec((tm, q.shape[1]), lambda i: (i, 0)),
                  pl.BlockSpec(keys.shape, lambda i: (0, 0, 0, 0))],
        out_specs=[pl.BlockSpec((tm, PEER_PAIRS // 2), lambda i: (i, 0)),
                   pl.BlockSpec((tm, PEER_PAIRS), lambda i: (i, 0))],
        out_shape=[jax.ShapeDtypeStruct((N, PEER_PAIRS // 2), jnp.int32),
                   jax.ShapeDtypeStruct((N, PEER_PAIRS), F32)],
        compiler_params=_params("parallel"),
        name="peer_topk",
    )(q, keys)


_PEER_TOKENS = 128
_ACT_TOKENS_PER_TRIP = 4
_GROUPS = PEER_PAIRS // SUBLANES


def _pack_table_kernel(w_ref, o_ref):
    te = w_ref.shape[0]
    bits = lambda x: pltpu.bitcast(x.astype(BF16).astype(F32), jnp.uint32)
    for i in range(_HALF):
        lo = bits(w_ref[:, (2 * i) * LANES:(2 * i + 1) * LANES]) >> 16
        hi = bits(w_ref[:, (2 * i + 1) * LANES:(2 * i + 2) * LANES])
        o_ref[pl.ds(i, te, stride=_HALF), :] = lo | hi


def pack_expert_table(w, layer):
    _, E, D = w.shape
    assert D == SUBLANES * LANES
    te = 256
    return pl.pallas_call(
        _pack_table_kernel,
        grid=(E // te,),
        in_specs=[pl.BlockSpec((None, te, D), lambda i: (layer, i, 0))],
        out_specs=pl.BlockSpec((te * _HALF, LANES), lambda i: (i, 0)),
        out_shape=jax.ShapeDtypeStruct((E * _HALF, LANES), jnp.uint32),
        compiler_params=_params("parallel"),
        name="pack_expert_table",
    )(w)


def _pair_position(word, half):
    return SUBLANES * (word // _HALF) + _HALF * half + word % _HALF


def _expert_rows(tab_ref, word):
    starts = (word & 0xFFFF, lax.shift_right_logical(word, 16))
    return [pltpu.bitcast(tab_ref[pl.ds(pl.multiple_of(s, _HALF), _HALF), :], BF16).astype(F32) for s in starts]


def _token_row(ref, t):
    row = ref[pl.ds(t, 1), :]
    return jnp.concatenate([row[:, s * LANES:(s + 1) * LANES] for s in range(SUBLANES)], axis=0)


def _merge_pair(a, b, shift, first):
    if shift == _HALF:
        return jnp.where(first, a, b) + pltpu.roll(jnp.where(first, b, a), shift, axis=0)
    bs = pltpu.roll(b, shift, axis=0)
    return jnp.where(first, a, bs) + pltpu.roll(jnp.where(first, bs, a), SUBLANES - shift, axis=0)


_MERGE_ORDER = (0, 4, 2, 6, 1, 5, 3, 7)


def _merge8(ps):
    sub = lax.broadcasted_iota(jnp.int32, (SUBLANES, LANES), 0)
    shift = _HALF
    while len(ps) > 1:
        first = (sub % (2 * shift)) < shift
        ps = [_merge_pair(ps[2 * i], ps[2 * i + 1], shift, first) for i in range(len(ps) // 2)]
        shift //= 2
    return ps[0]


def _peer_act_kernel(idx_ref, h_ref, g_ref, tab_ref, o_ref, part_ref):
    tn = h_ref.shape[0]

    def products(t, slot):
        x = _token_row(h_ref, t)
        for gi in range(_GROUPS):
            rows = []
            for w in range(_HALF):
                rows += _expert_rows(tab_ref, idx_ref[t, gi * _HALF + w])
            prods = [rows[2 * (j % _HALF) + j // _HALF] * x for j in _MERGE_ORDER]
            part_ref[slot, pl.ds(gi * SUBLANES, SUBLANES), :] = _merge8(prods)

    def reduce(t, slot):
        o_ref[pl.ds(t, 1), :] = jnp.sum(part_ref[slot].T, axis=0, keepdims=True)

    part_ref[...] = jnp.zeros(part_ref.shape, F32)
    slots = range(_ACT_TOKENS_PER_TRIP)

    def trip(i, carry):
        t = _ACT_TOKENS_PER_TRIP * i
        for sl in slots:
            reduce(jnp.maximum(t - _ACT_TOKENS_PER_TRIP + sl, 0), sl)
        for sl in slots:
            products(t + sl, sl)
        return carry

    lax.fori_loop(0, tn // _ACT_TOKENS_PER_TRIP, trip, 0)
    for sl in slots:
        reduce(tn - _ACT_TOKENS_PER_TRIP + sl, sl)
    o_ref[...] = g_ref[...] * jax.nn.gelu(o_ref[...], approximate=True)


def peer_act(offs, h, g, table):
    N = offs.shape[0]
    tn = _PEER_TOKENS
    return pl.pallas_call(
        _peer_act_kernel,
        grid=(N // tn,),
        in_specs=[pl.BlockSpec((tn, PEER_PAIRS // 2), lambda i: (i, 0), memory_space=pltpu.SMEM),
                  pl.BlockSpec((tn, SUBLANES * LANES), lambda i: (i, 0)),
                  pl.BlockSpec((tn, PEER_PAIRS), lambda i: (i, 0)),
                  pl.BlockSpec(memory_space=pltpu.VMEM)],
        out_specs=pl.BlockSpec((tn, PEER_PAIRS), lambda i: (i, 0)),
        out_shape=jax.ShapeDtypeStruct((N, PEER_PAIRS), F32),
        scratch_shapes=[pltpu.VMEM((_ACT_TOKENS_PER_TRIP, PEER_PAIRS, LANES), F32)],
        compiler_params=_params("arbitrary"),
        name="peer_act",
    )(offs, h, g, table)


_MIX_CHAINS = 4


def _peer_mix_kernel(idx_ref, coef_ref, tab_ref, x_ref, gate_ref, o_ref, cb_ref):
    tn = x_ref.shape[0]

    def spread(t):
        row = coef_ref[pl.ds(t, 1), :]
        return jnp.broadcast_to(row, (PEER_PAIRS, PEER_PAIRS)).T

    cb_ref[...] = spread(0)

    def token(t, carry):
        accs = [None] * _MIX_CHAINS
        for w in range(PEER_PAIRS // 2):
            for k, row in enumerate(_expert_rows(tab_ref, idx_ref[t, w])):
                j = _pair_position(w, k)
                term = jnp.broadcast_to(cb_ref[pl.ds(j, 1), :], (SUBLANES, LANES)) * row
                a = j % _MIX_CHAINS
                accs[a] = term if accs[a] is None else accs[a] + term
        y = (accs[0] + accs[1]) + (accs[2] + accs[3])
        y_row = jnp.concatenate([y[s:s + 1] for s in range(SUBLANES)], axis=1)
        o_ref[pl.ds(t, 1), :] = x_ref[pl.ds(t, 1), :] + gate_ref[0] * y_row
        cb_ref[...] = spread(jnp.minimum(t + 1, tn - 1))
        return carry

    lax.fori_loop(0, tn, token, 0)


def peer_mix(offs, coef, table, x, gate, tokens_per_batch):
    N, D = x.shape
    tn = _PEER_TOKENS
    assert tokens_per_batch % tn == 0
    if gate.shape[0] == 1:
        gate_map = lambda i: (0, 0, 0)
    else:
        gate_map = lambda i: ((i * tn) // tokens_per_batch, 0, 0)
    return pl.pallas_call(
        _peer_mix_kernel,
        grid=(N // tn,),
        in_specs=[pl.BlockSpec((tn, PEER_PAIRS // 2), lambda i: (i, 0), memory_space=pltpu.SMEM),
                  pl.BlockSpec((tn, PEER_PAIRS), lambda i: (i, 0)),
                  pl.BlockSpec(memory_space=pltpu.VMEM),
                  pl.BlockSpec((tn, D), lambda i: (i, 0)),
                  pl.BlockSpec((1, 1, D), gate_map)],
        out_specs=pl.BlockSpec((tn, D), lambda i: (i, 0)),
        out_shape=jax.ShapeDtypeStruct((N, D), F32),
        scratch_shapes=[pltpu.VMEM((PEER_PAIRS, LANES), F32)],
        compiler_params=_params("arbitrary"),
        name="peer_mix",
    )(offs, coef, table, x, gate)


def _rmsnorm_kernel(x_ref, g_ref, o_ref):
    x = x_ref[0]
    o_ref[0] = x * lax.rsqrt(jnp.mean(x * x, axis=-1, keepdims=True) + EPS) * g_ref[...]


def rmsnorm(x, g):
    B, L, D = x.shape
    tm = min(512, L)
    return pl.pallas_call(
        _rmsnorm_kernel,
        grid=(B, L // tm),
        in_specs=[pl.BlockSpec((1, tm, D), lambda b, i: (b, i, 0)),
                  pl.BlockSpec((1, D), lambda b, i: (0, 0))],
        out_specs=pl.BlockSpec((1, tm, D), lambda b, i: (b, i, 0)),
        out_shape=jax.ShapeDtypeStruct((B, L, D), F32),
        compiler_params=_params("parallel", "parallel"),
        name="final_rmsnorm",
    )(x, g.reshape(1, D))


def _rope_tables(T):
    t = jnp.arange(T, dtype=jnp.int32)
    row = (t // GRID_W).astype(F32)
    col = (t % GRID_W).astype(F32)
    n_freq = HEAD_DIM // 4
    inv_freq = ROPE_THETA ** (-jnp.arange(n_freq, dtype=F32) / n_freq)
    ang = jnp.concatenate([row[:, None] * inv_freq, col[:, None] * inv_freq], axis=-1)
    cos, sin = jnp.cos(ang), jnp.sin(ang)
    return jnp.tile(cos, (1, 4)), jnp.tile(jnp.concatenate([-sin, sin], axis=-1), (1, 2))


def _peer_ffn_residual(x, q, h, gate, keys, u_tab, v_tab):
    B, L, D = x.shape
    N = B * L
    offs, gates = peer_topk(q.reshape(N, -1), keys)
    coef = peer_act(offs, h.reshape(N, D), gates, u_tab)
    out = peer_mix(offs, coef, v_tab, x.reshape(N, D), gate, L)
    return out.reshape(B, L, D)


_Q_SCALE = HEAD_DIM ** -0.5
_NA_W = NA_HEADS * HEAD_DIM
_GQA_W = GQA_HEADS * HEAD_DIM
_GQA_KV_W = GQA_KV_HEADS * HEAD_DIM
_IN_OUTS = (("f32", 2 * CONV_CH), ("bf16", _NA_W, _Q_SCALE), ("norm", _GQA_W // LANES, _Q_SCALE, 0),
            ("bf16", _NA_W, 1.0), ("bf16", _NA_W, 1.0), ("norm", _GQA_KV_W // LANES, 1.0, 1),
            ("bf16", _GQA_KV_W, 1.0))
_OFF_G_Q = 2 * CONV_CH + _NA_W
_OFF_G_OUT = CONV_CH + _NA_W
_GQA_PAIR_ORDER = tuple(h for g in range(GQA_HEADS // GQA_KV_HEADS) for h in (g, GQA_HEADS // GQA_KV_HEADS + g))
assert GQA_KV_HEADS == 2 and 2 * HEAD_DIM == LANES


def _reorder_heads(w, start, axis):
    idx = jnp.concatenate([jnp.arange(HEAD_DIM) + start + h * HEAD_DIM for h in _GQA_PAIR_ORDER])
    full = jnp.arange(w.shape[axis]).at[start:start + _GQA_W].set(idx)
    return jnp.take(w, full, axis=axis)


def kernel(x, c, ctx, c_ctx, norm1_g, norm2_g, w_ada, b_ada, w_in, conv_w, conv_b, conv_ln_g, conv_ln_b,
           na_rel_bias, gqa_q_norm, gqa_k_norm, w_out, peer_w_q, peer_keys, peer_u, peer_v, final_norm_g):
    B, T, D = x.shape
    depth = w_in.shape[0]
    rope = _rope_tables(T)
    ada_rows = -(-(B + 1) // SUBLANES) * SUBLANES
    ada_in = jnp.zeros((ada_rows, D), F32).at[:B].set(c).at[B].set(c_ctx)

    for l in range(depth):
        last = l == depth - 1
        mod = ada_mod(ada_in, w_ada[l], b_ada[l])
        sh1, sc1, g1, sh2, sc2, g2 = [m.reshape(B, 1, D) for m in jnp.split(mod[:B], 6, axis=-1)]
        csh1, csc1, cg1, csh2, csc2, cg2 = [m.reshape(1, 1, D) for m in jnp.split(mod[B:B + 1], 6, axis=-1)]
        w_in_b = _reorder_heads(w_in[l], _OFF_G_Q, 1).astype(BF16)
        w_out_b = _reorder_heads(w_out[l], _OFF_G_OUT, 0).astype(BF16)
        w_q_b = peer_w_q[l].astype(BF16)
        keys_b = peer_keys[l].astype(BF16)
        u_tab = pack_expert_table(peer_u, l)
        v_tab = pack_expert_table(peer_v, l)
        qk_gains = jnp.stack([jnp.tile(gqa_q_norm[l], 2), jnp.tile(gqa_k_norm[l], 2)])

        za, na_q, g_q, na_k, na_v, g_k, g_v = norm_mod_proj(x, norm1_g[l], sc1, sh1, w_in_b, _IN_OUTS,
                                                            gains=qk_gains, rope=rope)
        zca, cna_q, cg_q, cna_k, cna_v, cg_k, cg_v = norm_mod_proj(ctx, norm1_g[l], csc1, csh1, w_in_b, _IN_OUTS,
                                                                   gains=qk_gains)
        a = conformer_conv(za, conv_w[l], conv_b[l], conv_ln_g[l], conv_ln_b[l])
        bm = neighbourhood_attention(na_q, na_k, na_v, cna_k, cna_v, na_bias_table(na_rel_bias[l]))
        gm = pair_attention(g_q, [(g_k, g_v), (cg_k, cg_v)])
        x, q, h = proj_residual_query((a, bm, gm), w_out_b, x, g1, norm2_g[l], sc2, sh2, w_q_b)

        x = _peer_ffn_residual(x, q, h, g2, keys_b, u_tab, v_tab)

        if not last:
            ac = conformer_conv(zca, conv_w[l], conv_b[l], conv_ln_g[l], conv_ln_b[l])
            bc = pair_attention(cna_q, [(cna_k, cna_v)])
            gc = pair_attention(cg_q, [(cg_k, cg_v)])
            ctx, qc, hc = proj_residual_query((ac, bc, gc), w_out_b, ctx, cg1, norm2_g[l], csc2, csh2, w_q_b)
            ctx = _peer_ffn_residual(ctx, qc, hc, cg2, keys_b, u_tab, v_tab)
    return rmsnorm(x, final_norm_g)
```
